```python
import math
import jax, jax.numpy as jnp
from jax import lax
import numpy as np

D_MODEL = 1024
BATCH = 1
SEQ = 16384
DEPTH = 2

N_RET_HEADS = 4
RET_QK_DIM = D_MODEL // N_RET_HEADS
RET_V_DIM = 2 * RET_QK_DIM
RET_QK_TOTAL = N_RET_HEADS * RET_QK_DIM
RET_V_TOTAL = N_RET_HEADS * RET_V_DIM
RET_IN_WIDTH = 2 * RET_QK_TOTAL + 2 * RET_V_TOTAL
RET_CHUNK = 128
ROPE_BASE = 10000.0
POOL_WINDOWS = (2, 4, 8, 16)
POOL_GROUPS = len(POOL_WINDOWS)
POOL_GROUP_WIDTH = D_MODEL // POOL_GROUPS
D_FF = ((8 * D_MODEL // 3 + 255) // 256) * 256
N_EXPERTS = 8
TOP_K = 2
D_FF_EXPERT = 7 * D_MODEL // 2
MOE_BLOCK = 256
N_MOD = 6
EPS = 1e-6
N_A = (DEPTH + 1) // 2
N_B = DEPTH // 2

kernel_name = 'hybrid_retention_pool_moe_adaln'


def _rmsnorm(x, gain):
    xf = x.astype(jnp.float32)
    y = xf * lax.rsqrt(jnp.mean(xf * xf, axis=-1, keepdims=True) + EPS)
    return (y * gain.astype(jnp.float32)).astype(x.dtype)


def _modulate(h, shift, scale):
    return h * (1.0 + scale) + shift


def _rotary(t, pos):
    half = t.shape[-1] // 2
    inv = ROPE_BASE ** (-jnp.arange(half, dtype=jnp.float32) / half)
    ang = pos.astype(jnp.float32)[:, None] * inv[None, :]
    cos = jnp.cos(ang)[None, :, None, :]
    sin = jnp.sin(ang)[None, :, None, :]
    t1, t2 = t[..., :half], t[..., half:]
    return jnp.concatenate([t1 * cos - t2 * sin, t1 * sin + t2 * cos], axis=-1)


def _retention(h, w_in, gn_gain, w_out):
    bsz, seq, _ = h.shape
    H, DK, DV, C = N_RET_HEADS, RET_QK_DIM, RET_V_DIM, RET_CHUNK
    n_chunks = seq // C
    proj = h @ w_in
    q, k, v, g = jnp.split(proj, [RET_QK_TOTAL, 2 * RET_QK_TOTAL, 2 * RET_QK_TOTAL + RET_V_TOTAL], axis=-1)
    pos = jnp.arange(seq)
    q = _rotary(q.reshape(bsz, seq, H, DK).astype(jnp.float32), pos)
    k = _rotary(k.reshape(bsz, seq, H, DK).astype(jnp.float32), pos) * (DK ** -0.5)
    v = v.reshape(bsz, seq, H, DV).astype(jnp.float32)

    def to_chunks(t):
        return t.reshape(bsz, n_chunks, C, H, t.shape[-1]).transpose(0, 3, 1, 2, 4)

    qc, kc, vc = to_chunks(q), to_chunks(k), to_chunks(v)
    log_gamma = jnp.log1p(-jnp.exp2(-5.0 - jnp.arange(H, dtype=jnp.float32)))
    idx = jnp.arange(C, dtype=jnp.float32)
    diff = idx[:, None] - idx[None, :]
    d_intra = jnp.where(diff >= 0, jnp.exp(log_gamma[:, None, None] * jnp.maximum(diff, 0.0)), 0.0)
    scores = jnp.einsum('bhncd,bhnmd->bhncm', qc, kc) * d_intra[None, :, None]
    intra = jnp.einsum('bhncm,bhnme->bhnce', scores, vc)
    q_dec = jnp.exp(log_gamma[:, None] * (idx + 1.0))
    k_dec = jnp.exp(log_gamma[:, None] * (C - 1.0 - idx))
    chunk_dec = jnp.exp(log_gamma * C)
    qs = (qc * q_dec[None, :, None, :, None]).transpose(2, 0, 1, 3, 4)
    ks = (kc * k_dec[None, :, None, :, None]).transpose(2, 0, 1, 3, 4)
    vs = vc.transpose(2, 0, 1, 3, 4)

    def step(state, inp):
        qn, kn, vn = inp
        out = jnp.einsum('bhcd,bhde->bhce', qn, state)
        state = state * chunk_dec[None, :, None, None] + jnp.einsum('bhcd,bhce->bhde', kn, vn)
        return state, out

    state0 = jnp.zeros((bsz, H, DK, DV), jnp.float32)
    _, cross = lax.scan(step, state0, (qs, ks, vs))
    o = intra + cross.transpose(1, 2, 0, 3, 4)
    o = o.transpose(0, 2, 3, 1, 4).reshape(bsz, seq, H, DV)
    mu = jnp.mean(o, axis=-1, keepdims=True)
    var = jnp.mean(jnp.square(o - mu), axis=-1, keepdims=True)
    o = ((o - mu) * lax.rsqrt(var + EPS)).reshape(bsz, seq, RET_V_TOTAL) * gn_gain.astype(jnp.float32)
    y = jax.nn.silu(g.astype(jnp.float32)) * o
    return y.astype(h.dtype) @ w_out


def _pool_mixer(h, w_pool, b_pool, scale):
    bsz, seq, dm = h.shape
    hf = h.astype(jnp.float32).reshape(bsz, seq, POOL_GROUPS, POOL_GROUP_WIDTH)
    t1 = jnp.arange(1, seq + 1, dtype=jnp.float32)
    outs = []
    for gi, w in enumerate(POOL_WINDOWS):
        xg = hf[:, :, gi]
        cs = jnp.cumsum(jnp.pad(xg, ((0, 0), (w, 0), (0, 0))), axis=1)
        win_sum = cs[:, w:] - cs[:, :seq]
        cnt = jnp.minimum(t1, float(w))
        outs.append(win_sum / cnt[None, :, None] - xg)
    pooled = jnp.stack(outs, axis=2)
    y = jnp.einsum('bsgc,gcd->bsgd', pooled, w_pool.astype(jnp.float32)) + b_pool.astype(jnp.float32)
    return (y.reshape(bsz, seq, dm) * scale.astype(jnp.float32)).astype(h.dtype)


def _swiglu(h, w_gate, w_up, w_down):
    return (jax.nn.silu(h @ w_gate) * (h @ w_up)) @ w_down


def _moe_swiglu(h, w_router, w_gate, w_up, w_down):
    bsz, seq, dm = h.shape
    T = bsz * seq
    xt = h.reshape(T, dm)
    logits = (xt @ w_router).astype(jnp.float32)
    top_logits, top_idx = lax.top_k(logits, TOP_K)
    gates = jax.nn.softmax(top_logits, axis=-1)
    e_flat = top_idx.reshape(-1)
    tok_flat = jnp.repeat(jnp.arange(T, dtype=jnp.int32), TOP_K)
    g_flat = gates.reshape(-1)
    order = jnp.argsort(e_flat)
    e_sorted, tok_sorted, g_sorted = e_flat[order], tok_flat[order], g_flat[order]
    counts = jnp.zeros((N_EXPERTS,), jnp.int32).at[e_flat].add(1)
    padded = ((counts + MOE_BLOCK - 1) // MOE_BLOCK) * MOE_BLOCK
    start_unp = jnp.cumsum(counts) - counts
    pad_end = jnp.cumsum(padded)
    start_pad = pad_end - padded
    n_assign = T * TOP_K
    rank = jnp.arange(n_assign, dtype=jnp.int32) - start_unp[e_sorted]
    dest = start_pad[e_sorted] + rank
    cap = ((n_assign + MOE_BLOCK - 1) // MOE_BLOCK) * MOE_BLOCK + N_EXPERTS * MOE_BLOCK
    n_blocks = cap // MOE_BLOCK
    buf_tok = jnp.full((cap,), T, jnp.int32).at[dest].set(tok_sorted)
    buf_gate = jnp.zeros((cap,), jnp.float32).at[dest].set(g_sorted)
    block_start = jnp.arange(n_blocks, dtype=jnp.int32) * MOE_BLOCK
    block_e = jnp.minimum(jnp.sum(block_start[:, None] >= pad_end[None, :], axis=1), N_EXPERTS - 1)
    x_pad = jnp.concatenate([xt, jnp.zeros((1, dm), xt.dtype)], axis=0)
    xb = x_pad[buf_tok].reshape(n_blocks, MOE_BLOCK, dm)

    def expert_block(args):
        xblk, e = args
        return _swiglu(xblk, w_gate[e], w_up[e], w_down[e])

    yb = lax.map(expert_block, (xb, block_e)).reshape(cap, dm)
    yb = (yb.astype(jnp.float32) * buf_gate[:, None]).astype(h.dtype)
    y = jnp.zeros((T + 1, dm), h.dtype).at[buf_tok].add(yb)
    return y[:T].reshape(bsz, seq, dm)


def setup_inputs(seed: int = 0) -> dict:
    key = jax.random.key(seed)
    ks = jax.random.split(key, 24)
    f32 = jnp.float32
    nrm = lambda k, shape, s: jax.random.normal(k, shape, f32) * s
    D = D_MODEL
    return {
        'x': nrm(ks[0], (BATCH, SEQ, D), 1.0),
        'c': nrm(ks[1], (BATCH, D), 1.0),
        'ada_w': nrm(ks[2], (DEPTH, D, N_MOD * D), 0.5 * D ** -0.5),
        'ada_b': nrm(ks[3], (DEPTH, N_MOD * D), 0.01),
        'norm_gain': 1.0 + nrm(ks[4], (DEPTH, 2, D), 0.05),
        'ret_w_in': nrm(ks[5], (N_A, D, RET_IN_WIDTH), D ** -0.5),
        'ret_gn_gain': 1.0 + nrm(ks[6], (N_A, RET_V_TOTAL), 0.05),
        'ret_w_out': nrm(ks[7], (N_A, RET_V_TOTAL, D), RET_V_TOTAL ** -0.5),
        'ffn_w_gate': nrm(ks[8], (N_A, D, D_FF), D ** -0.5),
        'ffn_w_up': nrm(ks[9], (N_A, D, D_FF), D ** -0.5),
        'ffn_w_down': nrm(ks[10], (N_A, D_FF, D), D_FF ** -0.5),
        'pool_w': nrm(ks[11], (N_B, POOL_GROUPS, POOL_GROUP_WIDTH, POOL_GROUP_WIDTH), POOL_GROUP_WIDTH ** -0.5),
        'pool_b': nrm(ks[12], (N_B, POOL_GROUPS, POOL_GROUP_WIDTH), 0.01),
        'pool_scale': 1.0 + nrm(ks[13], (N_B, D), 0.1),
        'moe_router': nrm(ks[14], (N_B, D, N_EXPERTS), D ** -0.5),
        'moe_w_gate': nrm(ks[15], (N_B, N_EXPERTS, D, D_FF_EXPERT), D ** -0.5),
        'moe_w_up': nrm(ks[16], (N_B, N_EXPERTS, D, D_FF_EXPERT), D ** -0.5),
        'moe_w_down': nrm(ks[17], (N_B, N_EXPERTS, D_FF_EXPERT, D), D_FF_EXPERT ** -0.5),
        'final_norm_gain': 1.0 + nrm(ks[18], (D,), 0.05),
    }


def reference(x, c, ada_w, ada_b, norm_gain, ret_w_in, ret_gn_gain, ret_w_out,
              ffn_w_gate, ffn_w_up, ffn_w_down, pool_w, pool_b, pool_scale,
              moe_router, moe_w_gate, moe_w_up, moe_w_down, final_norm_gain):
    c_act = jax.nn.silu(c)
    for i in range(DEPTH):
        j = i // 2
        mod = (c_act @ ada_w[i] + ada_b[i])[:, None, :]
        sh1, sc1, g1, sh2, sc2, g2 = jnp.split(mod, N_MOD, axis=-1)
        h = _modulate(_rmsnorm(x, norm_gain[i, 0]), sh1, sc1)
        if i % 2 == 0:
            m = _retention(h, ret_w_in[j], ret_gn_gain[j], ret_w_out[j])
        else:
            m = _pool_mixer(h, pool_w[j], pool_b[j], pool_scale[j])
        x = x + g1 * m
        h = _modulate(_rmsnorm(x, norm_gain[i, 1]), sh2, sc2)
        if i % 2 == 0:
            f = _swiglu(h, ffn_w_gate[j], ffn_w_up[j], ffn_w_down[j])
        else:
            f = _moe_swiglu(h, moe_router[j], moe_w_gate[j], moe_w_up[j], moe_w_down[j])
        x = x + g2 * f
    return _rmsnorm(x, final_norm_gain)
```

```python
import functools

import jax
import jax.numpy as jnp
from jax import lax
from jax.experimental import pallas as pl
from jax.experimental.pallas import tpu as pltpu

F32 = jnp.float32
BF16 = jnp.bfloat16

EPS = 1e-6
N_HEADS = 4
RET_CHUNK = 128
ROPE_BASE = 10000.0
POOL_WINDOWS = (2, 4, 8, 16)
POOL_HALO = 16
N_EXPERTS = 8
N_MOD = 6

VMEM_LIMIT_BYTES = 56 * 1024 * 1024

ROW_TILE = 1024
RET_ROWS = 512
POOL_ROWS = 256
RANK_ROWS = 512
MOE_BLOCK = 512
MOE_FF_TILE = 512
GATHER_CHUNK = 512
COMBINE_ROWS = 128
LANES = 128


def _silu(v):
    return v / (1.0 + jnp.exp(-v))


def _norm_mod(x, gain, scale1p, shift):
    ms = jnp.mean(x * x, axis=-1, keepdims=True)
    return (x * lax.rsqrt(ms + EPS) * gain) * scale1p + shift


def _params(*sem):
    return pltpu.CompilerParams(dimension_semantics=sem, vmem_limit_bytes=VMEM_LIMIT_BYTES)


def _resident(shape):
    nd = len(shape)
    return pl.BlockSpec(shape, lambda *_: (0,) * nd, pipeline_mode=pl.Buffered(1))


def _mod_kernel(c_ref, w_ref, b_ref, o_ref):
    ca = _silu(c_ref[...])
    o_ref[0] = jnp.sum(ca * w_ref[0], axis=0, keepdims=True) + b_ref[0]


def _modulation(c, ada_w, ada_b):
    depth, d, n = ada_w.shape
    tn = 1024
    return pl.pallas_call(
        _mod_kernel,
        grid=(depth, n // tn),
        in_specs=[
            pl.BlockSpec((d, 1), lambda i, j: (0, 0)),
            pl.BlockSpec((1, d, tn), lambda i, j: (i, 0, j)),
            pl.BlockSpec((1, 1, tn), lambda i, j: (i, 0, j)),
        ],
        out_specs=pl.BlockSpec((1, 1, tn), lambda i, j: (i, 0, j)),
        out_shape=jax.ShapeDtypeStruct((depth, 1, n), F32),
        compiler_params=_params("arbitrary", "arbitrary"),
        name="mod",
    )(c.reshape(d, 1), ada_w, ada_b.reshape(depth, 1, n))


def _proj_kernel(x_ref, gain_ref, sc_ref, sh_ref, w_ref, cos_ref, sin_ref, kdec_ref, o_ref, h_ref,
                 *, n_qk_tiles, n_v_tiles, dk):
    j = pl.program_id(1)

    @pl.when(j == 0)
    def _():
        h_ref[...] = _norm_mod(x_ref[...], gain_ref[...], sc_ref[...], sh_ref[...]).astype(BF16)

    acc = jnp.dot(h_ref[...], w_ref[...], preferred_element_type=F32)
    half = dk // 2
    heads_per_tile = acc.shape[1] // dk

    def rotary(scale_ref):
        cos = cos_ref[...]
        sin = sin_ref[...]
        for h in range(heads_per_tile):
            t1 = acc[:, h * dk:h * dk + half]
            t2 = acc[:, h * dk + half:(h + 1) * dk]
            o1 = t1 * cos - t2 * sin
            o2 = t1 * sin + t2 * cos
            if scale_ref is not None:
                s = scale_ref[:, h * half:(h + 1) * half]
                o1 = o1 * s
                o2 = o2 * s
            o_ref[:, h * dk:h * dk + half] = o1.astype(BF16)
            o_ref[:, h * dk + half:(h + 1) * dk] = o2.astype(BF16)

    @pl.when(j < n_qk_tiles)
    def _():
        rotary(None)

    @pl.when((j >= n_qk_tiles) & (j < 2 * n_qk_tiles))
    def _():
        rotary(kdec_ref)

    @pl.when((j >= 2 * n_qk_tiles) & (j < 2 * n_qk_tiles + n_v_tiles))
    def _():
        o_ref[...] = acc.astype(BF16)

    @pl.when(j >= 2 * n_qk_tiles + n_v_tiles)
    def _():
        o_ref[...] = _silu(acc).astype(BF16)


def _ret_proj(x, gain, sc1p, sh, w_in, cos, sin, kdec, qk_total, v_total):
    t, d = x.shape
    n = w_in.shape[1]
    tm, tn = ROW_TILE, 1024
    dk = qk_total // N_HEADS
    kern = functools.partial(_proj_kernel, n_qk_tiles=qk_total // tn, n_v_tiles=v_total // tn, dk=dk)
    vec = pl.BlockSpec((1, d), lambda i, j: (0, 0))
    return pl.pallas_call(
        kern,
        grid=(t // tm, n // tn),
        in_specs=[
            pl.BlockSpec((tm, d), lambda i, j: (i, 0)),
            vec, vec, vec,
            pl.BlockSpec((d, tn), lambda i, j: (0, j)),
            pl.BlockSpec((tm, dk // 2), lambda i, j: (i, 0)),
            pl.BlockSpec((tm, dk // 2), lambda i, j: (i, 0)),
            pl.BlockSpec((tm, N_HEADS * (dk // 2)), lambda i, j: (0, 0)),
        ],
        out_specs=pl.BlockSpec((tm, tn), lambda i, j: (i, j)),
        out_shape=jax.ShapeDtypeStruct((t, n), BF16),
        scratch_shapes=[pltpu.VMEM((tm, d), BF16)],
        compiler_params=_params("arbitrary", "arbitrary"),
        name="proj",
    )(x, gain, sc1p, sh, w_in, cos, sin, kdec)


def _ret_kernel(cdec_ref, q_ref, k_ref, v_ref, sg_ref, dmask_ref, qdec_ref, gn_ref, y_ref, state_ref,
                *, dk, dv):
    @pl.when(pl.program_id(0) == 0)
    def _():
        state_ref[...] = jnp.zeros_like(state_ref)

    rows_per_step = q_ref.shape[0]
    for n in range(rows_per_step // RET_CHUNK):
        rows = slice(n * RET_CHUNK, (n + 1) * RET_CHUNK)
        for h in range(N_HEADS):
            qcols = slice(h * dk, (h + 1) * dk)
            vcols = slice(h * dv, (h + 1) * dv)
            q = q_ref[rows, qcols]
            ks = k_ref[rows, qcols]
            v = v_ref[rows, vcols]
            s = lax.dot_general(q, ks, (((1,), (1,)), ((), ())), preferred_element_type=F32)
            a = (s * dmask_ref[h]).astype(BF16)
            intra = jnp.dot(a, v, preferred_element_type=F32)
            st = state_ref[h]
            cross = jnp.dot(q, st.astype(BF16), preferred_element_type=F32)
            o = intra + cross * qdec_ref[h]
            kv = lax.dot_general(ks, v, (((0,), (0,)), ((), ())), preferred_element_type=F32)
            state_ref[h] = st * cdec_ref[h] + kv
            mu = jnp.mean(o, axis=-1, keepdims=True)
            dlt = o - mu
            var = jnp.mean(dlt * dlt, axis=-1, keepdims=True)
            on = dlt * lax.rsqrt(var + EPS)
            y = on * gn_ref[:, vcols] * sg_ref[rows, vcols].astype(F32)
            y_ref[rows, vcols] = y.astype(BF16)


def _retention_core(proj, cdec, dmask, qdec, gn_gain, qk_total, v_total):
    t = proj.shape[0]
    r = RET_ROWS
    dk = qk_total // N_HEADS
    dv = v_total // N_HEADS
    assert v_total == 2 * qk_total
    kern = functools.partial(_ret_kernel, dk=dk, dv=dv)
    return pl.pallas_call(
        kern,
        grid=(t // r,),
        in_specs=[
            pl.BlockSpec(memory_space=pltpu.SMEM),
            pl.BlockSpec((r, qk_total), lambda i: (i, 0)),
            pl.BlockSpec((r, qk_total), lambda i: (i, 1)),
            pl.BlockSpec((r, v_total), lambda i: (i, 1)),
            pl.BlockSpec((r, v_total), lambda i: (i, 2)),
            pl.BlockSpec((N_HEADS, RET_CHUNK, RET_CHUNK), lambda i: (0, 0, 0)),
            pl.BlockSpec((N_HEADS, RET_CHUNK, dv), lambda i: (0, 0, 0)),
            pl.BlockSpec((1, v_total), lambda i: (0, 0)),
        ],
        out_specs=pl.BlockSpec((r, v_total), lambda i: (i, 0)),
        out_shape=jax.ShapeDtypeStruct((t, v_total), BF16),
        scratch_shapes=[pltpu.VMEM((N_HEADS, dk, dv), F32)],
        compiler_params=_params("arbitrary"),
        name="ret",
    )(cdec, proj, proj, proj, proj, dmask, qdec, gn_gain)


def _out_kernel(y_ref, w_ref, x_ref, g_ref, o_ref):
    m = jnp.dot(y_ref[...], w_ref[...], preferred_element_type=F32)
    o_ref[...] = x_ref[...] + g_ref[...] * m


def _ret_out(y, w_out, x, g1):
    t, d = x.shape
    kdim = y.shape[1]
    tm = ROW_TILE
    return pl.pallas_call(
        _out_kernel,
        grid=(t // tm,),
        in_specs=[
            pl.BlockSpec((tm, kdim), lambda i: (i, 0)),
            _resident((kdim, d)),
            pl.BlockSpec((tm, d), lambda i: (i, 0)),
            pl.BlockSpec((1, d), lambda i: (0, 0)),
        ],
        out_specs=pl.BlockSpec((tm, d), lambda i: (i, 0)),
        out_shape=jax.ShapeDtypeStruct((t, d), F32),
        compiler_params=_params("arbitrary"),
        name="out",
    )(y, w_out, x, g1)


def _ffn_kernel(x_ref, gain_ref, sc_ref, sh_ref, g_ref, wg_ref, wu_ref, wd_ref, o_ref, *, ff_tile):
    x = x_ref[...]
    h = _norm_mod(x, gain_ref[...], sc_ref[...], sh_ref[...]).astype(BF16)
    d_ff = wg_ref.shape[1]
    acc = jnp.zeros(x.shape, F32)
    for lo in range(0, d_ff, ff_tile):
        hi = min(lo + ff_tile, d_ff)
        a = jnp.dot(h, wg_ref[:, lo:hi], preferred_element_type=F32)
        b = jnp.dot(h, wu_ref[:, lo:hi], preferred_element_type=F32)
        act = (_silu(a) * b).astype(BF16)
        acc = acc + jnp.dot(act, wd_ref[lo:hi, :], preferred_element_type=F32)
    o_ref[...] = x + g_ref[...] * acc


def _dense_ffn(x, gain, sc1p, sh, g2, wg, wu, wd):
    t, d = x.shape
    d_ff = wg.shape[1]
    tm = ROW_TILE
    vec = pl.BlockSpec((1, d), lambda i: (0, 0))
    return pl.pallas_call(
        functools.partial(_ffn_kernel, ff_tile=512),
        grid=(t // tm,),
        in_specs=[
            pl.BlockSpec((tm, d), lambda i: (i, 0)),
            vec, vec, vec, vec,
            _resident((d, d_ff)), _resident((d, d_ff)), _resident((d_ff, d)),
        ],
        out_specs=pl.BlockSpec((tm, d), lambda i: (i, 0)),
        out_shape=jax.ShapeDtypeStruct((t, d), F32),
        compiler_params=_params("arbitrary"),
        name="ffn",
    )(x, gain, sc1p, sh, g2, wg, wu, wd)


def _pool_kernel(x_ref, gain_ref, sc_ref, sh_ref, g_ref, band_ref, wp_ref, bp_ref, ps_ref, o_ref, hbuf_ref):
    i = pl.program_id(0)
    tm = x_ref.shape[0]
    gw = wp_ref.shape[1]

    @pl.when(i == 0)
    def _():
        hbuf_ref[0:POOL_HALO, :] = jnp.zeros((POOL_HALO, hbuf_ref.shape[1]), F32)

    @pl.when(i > 0)
    def _():
        hbuf_ref[0:POOL_HALO, :] = hbuf_ref[tm:tm + POOL_HALO, :]

    x = x_ref[...]
    h = _norm_mod(x, gain_ref[...], sc_ref[...], sh_ref[...])
    hbuf_ref[POOL_HALO:, :] = h
    hb = hbuf_ref[...]
    hb_hi = hb.astype(BF16)
    hb_lo = (hb - hb_hi.astype(F32)).astype(BF16)
    t1 = (lax.broadcasted_iota(jnp.int32, (tm, 1), 0) + (i * tm + 1)).astype(F32)
    for g, w in enumerate(POOL_WINDOWS):
        cols = slice(g * gw, (g + 1) * gw)
        band = band_ref[g]
        win = (jnp.dot(band, hb_hi[:, cols], preferred_element_type=F32)
               + jnp.dot(band, hb_lo[:, cols], preferred_element_type=F32))
        pooled = win / jnp.minimum(t1, float(w)) - h[:, cols]
        y = jnp.dot(pooled.astype(BF16), wp_ref[g], preferred_element_type=F32) + bp_ref[:, cols]
        o_ref[:, cols] = x[:, cols] + g_ref[:, cols] * (y * ps_ref[:, cols])


def _pool_mixer(x, gain, sc1p, sh, g1, band, wp, bp, ps):
    t, d = x.shape
    tm = POOL_ROWS
    vec = pl.BlockSpec((1, d), lambda i: (0, 0))
    return pl.pallas_call(
        _pool_kernel,
        grid=(t // tm,),
        in_specs=[
            pl.BlockSpec((tm, d), lambda i: (i, 0)),
            vec, vec, vec, vec,
            pl.BlockSpec(band.shape, lambda i: (0, 0, 0)),
            pl.BlockSpec(wp.shape, lambda i: (0, 0, 0)),
            vec, vec,
        ],
        out_specs=pl.BlockSpec((tm, d), lambda i: (i, 0)),
        out_shape=jax.ShapeDtypeStruct((t, d), F32),
        scratch_shapes=[pltpu.VMEM((POOL_HALO + tm, d), F32)],
        compiler_params=_params("arbitrary"),
        name="pool",
    )(x, gain, sc1p, sh, g1, band, wp, bp, ps)


def _router_kernel(x_ref, gain_ref, sc_ref, sh_ref, whi_ref, wlo_ref, h_ref, gates_ref, sel_ref):
    h = _norm_mod(x_ref[...], gain_ref[...], sc_ref[...], sh_ref[...])
    h_hi = h.astype(BF16)
    h_ref[...] = h_hi
    h_lo = (h - h_hi.astype(F32)).astype(BF16)
    w_hi = whi_ref[...]
    logits = (jnp.dot(h_hi, w_hi, preferred_element_type=F32)
              + jnp.dot(h_lo, w_hi, preferred_element_type=F32)
              + jnp.dot(h_hi, wlo_ref[...], preferred_element_type=F32))
    lane = lax.broadcasted_iota(jnp.int32, logits.shape, 1).astype(F32)
    neg = jnp.float32(-jnp.inf)
    lg = jnp.where(lane < float(N_EXPERTS), logits, neg)
    m0 = jnp.max(lg, axis=-1, keepdims=True)
    i0 = jnp.min(jnp.where(lg == m0, lane, float(LANES)), axis=-1, keepdims=True)
    lg1 = jnp.where(lane == i0, neg, lg)
    m1 = jnp.max(lg1, axis=-1, keepdims=True)
    i1 = jnp.min(jnp.where(lg1 == m1, lane, float(LANES)), axis=-1, keepdims=True)
    e1 = jnp.exp(m1 - m0)
    den = 1.0 + e1
    is0 = lane == i0
    is1 = lane == i1
    gates_ref[...] = jnp.where(is0, 1.0 / den, jnp.where(is1, e1 / den, 0.0))
    sel_ref[...] = jnp.where(is0 | is1, 1.0, 0.0).astype(BF16)


def _router(x, gain, sc1p, sh, w_hi, w_lo):
    t, d = x.shape
    tm = ROW_TILE
    vec = pl.BlockSpec((1, d), lambda i: (0, 0))
    wspec = pl.BlockSpec((d, LANES), lambda i: (0, 0))
    return pl.pallas_call(
        _router_kernel,
        grid=(t // tm,),
        in_specs=[pl.BlockSpec((tm, d), lambda i: (i, 0)), vec, vec, vec, wspec, wspec],
        out_specs=[
            pl.BlockSpec((tm, d), lambda i: (i, 0)),
            pl.BlockSpec((tm, LANES), lambda i: (i, 0)),
            pl.BlockSpec((tm, LANES), lambda i: (i, 0)),
        ],
        out_shape=[
            jax.ShapeDtypeStruct((t, d), BF16),
            jax.ShapeDtypeStruct((t, LANES), F32),
            jax.ShapeDtypeStruct((t, LANES), BF16),
        ],
        compiler_params=_params("arbitrary"),
        name="router",
    )(x, gain, sc1p, sh, w_hi, w_lo)


def _rank_kernel(sel_ref, tri_ref, rank_ref, tot_ref, carry_ref):
    @pl.when(pl.program_id(0) == 0)
    def _():
        carry_ref[...] = jnp.zeros_like(carry_ref)

    s = sel_ref[...]
    cum = jnp.dot(tri_ref[...], s, preferred_element_type=F32)
    carry = carry_ref[...]
    rank_ref[...] = cum - s.astype(F32) + carry
    tot = carry + cum[cum.shape[0] - 1:, :]
    carry_ref[...] = tot
    tot_ref[0] = tot


def _ranks(sel, tri):
    t = sel.shape[0]
    tm = RANK_ROWS
    return pl.pallas_call(
        _rank_kernel,
        grid=(t // tm,),
        in_specs=[pl.BlockSpec((tm, LANES), lambda i: (i, 0)), pl.BlockSpec((tm, tm), lambda i: (0, 0))],
        out_specs=[pl.BlockSpec((tm, LANES), lambda i: (i, 0)), pl.BlockSpec((1, 1, LANES), lambda i: (i, 0, 0))],
        out_shape=[jax.ShapeDtypeStruct((t, LANES), F32), jax.ShapeDtypeStruct((t // tm, 1, LANES), F32)],
        scratch_shapes=[pltpu.VMEM((1, LANES), F32)],
        compiler_params=_params("arbitrary"),
        name="rank",
    )(sel, tri)


def _experts_kernel(be_ref, nv_ref, clo_ref, chi_ref, h_ref, post_ref, wg_ref, wu_ref, wd_ref, yb_ref,
                    xg_ref, acc_ref):
    b = pl.program_id(0)
    f = pl.program_id(1)
    nf = pl.num_programs(1)
    nv = nv_ref[b]
    rows_per_block = xg_ref.shape[0]

    @pl.when((f == 0) & (nv > 0))
    def _():
        e = be_ref[b]
        acc_ref[...] = jnp.zeros_like(acc_ref)
        row_pos = lax.broadcasted_iota(jnp.int32, (rows_per_block, GATHER_CHUNK), 0) + b * rows_per_block

        def body(c, carry):
            off = pl.multiple_of(c * GATHER_CHUNK, GATHER_CHUNK)
            p = post_ref[pl.ds(e, 1), pl.ds(off, GATHER_CHUNK)]
            onehot = jnp.where(row_pos == p, 1.0, 0.0).astype(BF16)
            acc_ref[...] += jnp.dot(onehot, h_ref[pl.ds(off, GATHER_CHUNK), :], preferred_element_type=F32)
            return carry

        lax.fori_loop(clo_ref[b], chi_ref[b], body, 0)
        xg_ref[...] = acc_ref[...].astype(BF16)
        acc_ref[...] = jnp.zeros_like(acc_ref)

    @pl.when(nv > 0)
    def _():
        x = xg_ref[...]
        a = jnp.dot(x, wg_ref[0], preferred_element_type=F32)
        u = jnp.dot(x, wu_ref[0], preferred_element_type=F32)
        act = (_silu(a) * u).astype(BF16)
        acc_ref[...] += jnp.dot(act, wd_ref[0], preferred_element_type=F32)

    @pl.when((f == nf - 1) & (nv > 0))
    def _():
        yb_ref[...] = acc_ref[...].astype(BF16)

    @pl.when((f == nf - 1) & (nv == 0))
    def _():
        yb_ref[...] = jnp.zeros_like(yb_ref)


def _experts(be, nv, clo, chi, h, post, wg, wu, wd):
    t, d = h.shape
    n_blocks = be.shape[0]
    d_ff = wg.shape[2]
    tf = MOE_FF_TILE
    nf = d_ff // tf
    bsz = MOE_BLOCK

    def ff_idx(f, nvr, b):
        return jnp.where(nvr[b] > 0, f, nf - 1)

    grid_spec = pltpu.PrefetchScalarGridSpec(
        num_scalar_prefetch=4,
        grid=(n_blocks, nf),
        in_specs=[
            pl.BlockSpec((t, d), lambda b, f, *_: (0, 0), pipeline_mode=pl.Buffered(1)),
            pl.BlockSpec(post.shape, lambda b, f, *_: (0, 0), pipeline_mode=pl.Buffered(1)),
            pl.BlockSpec((1, d, tf), lambda b, f, be_r, nv_r, *_: (be_r[b], 0, ff_idx(f, nv_r, b))),
            pl.BlockSpec((1, d, tf), lambda b, f, be_r, nv_r, *_: (be_r[b], 0, ff_idx(f, nv_r, b))),
            pl.BlockSpec((1, tf, d), lambda b, f, be_r, nv_r, *_: (be_r[b], ff_idx(f, nv_r, b), 0)),
        ],
        out_specs=pl.BlockSpec((bsz, d), lambda b, f, *_: (b, 0)),
        scratch_shapes=[pltpu.VMEM((bsz, d), BF16), pltpu.VMEM((bsz, d), F32)],
    )
    return pl.pallas_call(
        _experts_kernel,
        grid_spec=grid_spec,
        out_shape=jax.ShapeDtypeStruct((n_blocks * bsz, d), BF16),
        compiler_params=_params("arbitrary", "arbitrary"),
        name="experts",
    )(be, nv, clo, chi, h, post, wg, wu, wd)


def _combine_kernel(win_ref, x_ref, pos_ref, gates_ref, g_ref, fg_ref, *refs):
    yb_refs, o_ref = refs[:-1], refs[-1]
    i = pl.program_id(0)
    tm = x_ref.shape[0]
    lane = lax.broadcasted_iota(jnp.int32, (tm, 2 * COMBINE_ROWS), 1)
    acc = jnp.zeros(x_ref.shape, F32)
    for e in range(N_EXPERTS):
        base = win_ref[i * N_EXPERTS + e] * COMBINE_ROWS
        rel = pos_ref[:, e:e + 1] - base
        onehot = jnp.where(lane == rel, 1.0, 0.0).astype(BF16)
        window = jnp.concatenate([yb_refs[2 * e][...], yb_refs[2 * e + 1][...]], axis=0)
        picked = jnp.dot(onehot, window, preferred_element_type=F32)
        acc = acc + gates_ref[:, e:e + 1] * picked
    xo = x_ref[...] + g_ref[...] * acc
    ms = jnp.mean(xo * xo, axis=-1, keepdims=True)
    o_ref[...] = xo * lax.rsqrt(ms + EPS) * fg_ref[...]


def _combine(win, x, pos, gates, g2, fgain, yb):
    t, d = x.shape
    tm = COMBINE_ROWS
    n_win = yb.shape[0] // COMBINE_ROWS
    vec = pl.BlockSpec((1, d), lambda i, w: (0, 0))

    def first(e):
        return lambda i, w: (jnp.minimum(w[i * N_EXPERTS + e], n_win - 1), 0)

    def second(e):
        return lambda i, w: (jnp.minimum(w[i * N_EXPERTS + e] + 1, n_win - 1), 0)

    yb_specs = []
    for e in range(N_EXPERTS):
        yb_specs.append(pl.BlockSpec((COMBINE_ROWS, d), first(e)))
        yb_specs.append(pl.BlockSpec((COMBINE_ROWS, d), second(e)))
    grid_spec = pltpu.PrefetchScalarGridSpec(
        num_scalar_prefetch=1,
        grid=(t // tm,),
        in_specs=[
            pl.BlockSpec((tm, d), lambda i, w: (i, 0)),
            pl.BlockSpec((tm, N_EXPERTS), lambda i, w: (i, 0)),
            pl.BlockSpec((tm, LANES), lambda i, w: (i, 0)),
            vec, vec,
        ] + yb_specs,
        out_specs=pl.BlockSpec((tm, d), lambda i, w: (i, 0)),
    )
    return pl.pallas_call(
        _combine_kernel,
        grid_spec=grid_spec,
        out_shape=jax.ShapeDtypeStruct((t, d), F32),
        compiler_params=_params("arbitrary"),
        name="combine",
    )(win, x, pos, gates, g2, fgain, *([yb] * (2 * N_EXPERTS)))


def _routing_tables(rank, tot, sel):
    t = rank.shape[0]
    bsz = MOE_BLOCK
    n_blocks = (2 * t) // bsz + N_EXPERTS
    cum = tot[:, 0, :N_EXPERTS].astype(jnp.int32)
    counts = cum[-1]
    padded = ((counts + bsz - 1) // bsz) * bsz
    pad_end = jnp.cumsum(padded)
    start_pad = pad_end - padded
    selected = sel[:, :N_EXPERTS] > 0
    pos = jnp.where(selected, start_pad[None, :] + rank[:, :N_EXPERTS].astype(jnp.int32), -1)

    bstart = jnp.arange(n_blocks, dtype=jnp.int32) * bsz
    be = jnp.minimum(jnp.sum(bstart[:, None] >= pad_end[None, :], axis=1), N_EXPERTS - 1).astype(jnp.int32)
    r0 = bstart - start_pad[be]
    nv = jnp.clip(counts[be] - r0, 0, bsz).astype(jnp.int32)
    chunk_end = cum[:, be].T
    chunk_start = jnp.concatenate([jnp.zeros((1, N_EXPERTS), jnp.int32), cum[:-1]], axis=0)[:, be].T
    clo = jnp.sum(chunk_end <= r0[:, None], axis=1).astype(jnp.int32)
    chi = jnp.sum(chunk_start < (r0 + nv)[:, None], axis=1).astype(jnp.int32)

    big = jnp.int32(2 ** 30)
    first = jnp.min(jnp.where(pos >= 0, pos, big).reshape(t // COMBINE_ROWS, COMBINE_ROWS, N_EXPERTS), axis=1)
    win = jnp.where(first == big, 0, first // COMBINE_ROWS).astype(jnp.int32).reshape(-1)
    return pos, be, nv, clo, chi, win


def _rotary_tables(seq, dk):
    half = dk // 2
    inv = ROPE_BASE ** (-jnp.arange(half, dtype=F32) / half)
    ang = jnp.arange(seq).astype(F32)[:, None] * inv[None, :]
    return jnp.cos(ang), jnp.sin(ang)


def _decay_tables(dk, dv):
    c = RET_CHUNK
    log_gamma = jnp.log1p(-jnp.exp2(-5.0 - jnp.arange(N_HEADS, dtype=F32)))
    idx = jnp.arange(c, dtype=F32)
    row_dec = jnp.exp(log_gamma[:, None] * (idx + 1.0 - c))
    causal = (idx[:, None] >= idx[None, :]).astype(F32)
    dmask = row_dec[:, :, None] * causal[None]
    k_dec = jnp.exp(log_gamma[:, None] * (c - 1.0 - idx)) * (dk ** -0.5)
    q_dec = jnp.exp(log_gamma[:, None] * (idx + 1.0))
    chunk_dec = jnp.exp(log_gamma * c)
    half = dk // 2
    kdec_tile = jnp.tile(jnp.repeat(k_dec.T, half, axis=1), (ROW_TILE // c, 1))
    qdec_tile = jnp.broadcast_to(q_dec[:, :, None], (N_HEADS, c, dv))
    return dmask, kdec_tile, qdec_tile, chunk_dec


def _pool_band(tm):
    r = jnp.arange(tm)[:, None] + POOL_HALO
    j = jnp.arange(tm + POOL_HALO)[None, :]
    lag = r - j
    return jnp.stack([((lag >= 0) & (lag < w)) for w in POOL_WINDOWS]).astype(BF16)


def kernel(x, c, ada_w, ada_b, norm_gain, ret_w_in, ret_gn_gain, ret_w_out, ffn_w_gate, ffn_w_up, ffn_w_down,
           pool_w, pool_b, pool_scale, moe_router, moe_w_gate, moe_w_up, moe_w_down, final_norm_gain):
    bsz, seq, d = x.shape
    assert bsz == 1 and ada_w.shape[0] == 2
    xt = x.reshape(seq, d)
    qk_total = d
    v_total = ret_w_out.shape[1]
    dk = qk_total // N_HEADS
    dv = v_total // N_HEADS

    mod = _modulation(c, ada_w, ada_b)
    def mods(i):
        parts = [mod[i, :, k * d:(k + 1) * d] for k in range(N_MOD)]
        sh1, sc1, g1, sh2, sc2, g2 = parts
        return sh1, 1.0 + sc1, g1, sh2, 1.0 + sc2, g2

    sh1, sc1p, g1, sh2, sc2p, g2 = mods(0)
    cos, sin = _rotary_tables(seq, dk)
    dmask, kdec_tile, qdec_tile, chunk_dec = _decay_tables(dk, dv)
    proj = _ret_proj(xt, norm_gain[0, 0][None], sc1p, sh1, ret_w_in[0].astype(BF16), cos, sin, kdec_tile,
                     qk_total, v_total)
    y = _retention_core(proj, chunk_dec, dmask, qdec_tile, ret_gn_gain[0][None], qk_total, v_total)
    x1 = _ret_out(y, ret_w_out[0].astype(BF16), xt, g1)
    x2 = _dense_ffn(x1, norm_gain[0, 1][None], sc2p, sh2, g2,
                    ffn_w_gate[0].astype(BF16), ffn_w_up[0].astype(BF16), ffn_w_down[0].astype(BF16))

    sh1, sc1p, g1, sh2, sc2p, g2 = mods(1)
    x3 = _pool_mixer(x2, norm_gain[1, 0][None], sc1p, sh1, g1, _pool_band(POOL_ROWS),
                     pool_w[0].astype(BF16), pool_b[0].reshape(1, d), pool_scale[0][None])

    wr = jnp.pad(moe_router[0], ((0, 0), (0, LANES - N_EXPERTS)))
    wr_hi = wr.astype(BF16)
    wr_lo = (wr - wr_hi.astype(F32)).astype(BF16)
    h4, gates, sel = _router(x3, norm_gain[1, 1][None], sc2p, sh2, wr_hi, wr_lo)
    tri = (jnp.arange(RANK_ROWS)[:, None] >= jnp.arange(RANK_ROWS)[None, :]).astype(BF16)
    rank, tot = _ranks(sel, tri)
    pos, be, nv, clo, chi, win = _routing_tables(rank, tot, sel)
    yb = _experts(be, nv, clo, chi, h4, pos.T, moe_w_gate[0].astype(BF16), moe_w_up[0].astype(BF16),
                  moe_w_down[0].astype(BF16))
    out = _combine(win, x3, pos, gates, g2, final_norm_gain[None], yb)
    return out.reshape(bsz, seq, d)
```

```python
import functools

import jax
import jax.numpy as jnp
from jax import lax
from jax.experimental import pallas as pl
from jax.experimental.pallas import tpu as pltpu

F32 = jnp.float32
BF16 = jnp.bfloat16

EPS = 1e-6
N_HEADS = 4
RET_CHUNK = 128
ROPE_BASE = 10000.0
POOL_WINDOWS = (2, 4, 8, 16)
POOL_HALO = 16
N_EXPERTS = 8
N_MOD = 6

VMEM_LIMIT_BYTES = 56 * 1024 * 1024

ROW_TILE = 1024
RET_ROWS = 512
POOL_ROWS = 512
POOL_SUB_ROWS = 256
RANK_ROWS = 512
MOE_BLOCK = 1024
MOE_DOT_ROWS = 512
MOE_GATHER_ROWS = 256
MOE_FF_TILE = 256
GATHER_CHUNK = 512
COMBINE_ROWS = 128
LANES = 128


def _silu(v):
    return v / (1.0 + jnp.exp(-v))


def _norm_mod(x, gain, scale1p, shift):
    ms = jnp.mean(x * x, axis=-1, keepdims=True)
    return (x * lax.rsqrt(ms + EPS) * gain) * scale1p + shift


def _params(*sem):
    return pltpu.CompilerParams(dimension_semantics=sem, vmem_limit_bytes=VMEM_LIMIT_BYTES)


def _resident(shape):
    nd = len(shape)
    return pl.BlockSpec(shape, lambda *_: (0,) * nd, pipeline_mode=pl.Buffered(1))


def _mod_kernel(c_ref, w_ref, b_ref, o_ref):
    ca = _silu(c_ref[...])
    o_ref[0] = jnp.sum(ca * w_ref[0], axis=0, keepdims=True) + b_ref[0]


def _modulation(c, ada_w, ada_b):
    depth, d, n = ada_w.shape
    tn = 1024
    return pl.pallas_call(
        _mod_kernel,
        grid=(depth, n // tn),
        in_specs=[
            pl.BlockSpec((d, 1), lambda i, j: (0, 0)),
            pl.BlockSpec((1, d, tn), lambda i, j: (i, 0, j)),
            pl.BlockSpec((1, 1, tn), lambda i, j: (i, 0, j)),
        ],
        out_specs=pl.BlockSpec((1, 1, tn), lambda i, j: (i, 0, j)),
        out_shape=jax.ShapeDtypeStruct((depth, 1, n), F32),
        compiler_params=_params("arbitrary", "arbitrary"),
        name="mod",
    )(c.reshape(d, 1), ada_w, ada_b.reshape(depth, 1, n))


def _proj_kernel(x_ref, gain_ref, sc_ref, sh_ref, w_ref, cos_ref, sin_ref, kdec_ref, o_ref, h_ref,
                 *, n_qk_tiles, n_v_tiles, dk):
    j = pl.program_id(1)

    @pl.when(j == 0)
    def _():
        h_ref[...] = _norm_mod(x_ref[...], gain_ref[...], sc_ref[...], sh_ref[...]).astype(BF16)

    half = dk // 2
    n_chunks = w_ref.shape[1] // dk

    def chunk_dot(c):
        return jnp.dot(h_ref[...], w_ref[:, c * dk:(c + 1) * dk], preferred_element_type=F32)

    def rotary(scale_ref):
        cos = cos_ref[...]
        sin = sin_ref[...]
        for c in range(n_chunks):
            acc = chunk_dot(c)
            t1 = acc[:, :half]
            t2 = acc[:, half:]
            o1 = t1 * cos - t2 * sin
            o2 = t1 * sin + t2 * cos
            if scale_ref is not None:
                s = scale_ref[:, c * half:(c + 1) * half]
                o1 = o1 * s
                o2 = o2 * s
            o_ref[:, c * dk:c * dk + half] = o1.astype(BF16)
            o_ref[:, c * dk + half:(c + 1) * dk] = o2.astype(BF16)

    @pl.when(j < n_qk_tiles)
    def _():
        rotary(None)

    @pl.when((j >= n_qk_tiles) & (j < 2 * n_qk_tiles))
    def _():
        rotary(kdec_ref)

    @pl.when((j >= 2 * n_qk_tiles) & (j < 2 * n_qk_tiles + n_v_tiles))
    def _():
        for c in range(n_chunks):
            o_ref[:, c * dk:(c + 1) * dk] = chunk_dot(c).astype(BF16)

    @pl.when(j >= 2 * n_qk_tiles + n_v_tiles)
    def _():
        for c in range(n_chunks):
            o_ref[:, c * dk:(c + 1) * dk] = _silu(chunk_dot(c)).astype(BF16)


def _ret_proj(x, gain, sc1p, sh, w_in, cos, sin, kdec, qk_total, v_total):
    t, d = x.shape
    n = w_in.shape[1]
    tm, tn = ROW_TILE, 1024
    dk = qk_total // N_HEADS
    kern = functools.partial(_proj_kernel, n_qk_tiles=qk_total // tn, n_v_tiles=v_total // tn, dk=dk)
    vec = pl.BlockSpec((1, d), lambda i, j: (0, 0))
    return pl.pallas_call(
        kern,
        grid=(t // tm, n // tn),
        in_specs=[
            pl.BlockSpec((tm, d), lambda i, j: (i, 0)),
            vec, vec, vec,
            pl.BlockSpec((d, tn), lambda i, j: (0, j)),
            pl.BlockSpec((tm, dk // 2), lambda i, j: (i, 0)),
            pl.BlockSpec((tm, dk // 2), lambda i, j: (i, 0)),
            pl.BlockSpec((tm, N_HEADS * (dk // 2)), lambda i, j: (0, 0)),
        ],
        out_specs=pl.BlockSpec((tm, tn), lambda i, j: (i, j)),
        out_shape=jax.ShapeDtypeStruct((t, n), BF16),
        scratch_shapes=[pltpu.VMEM((tm, d), BF16)],
        compiler_params=_params("arbitrary", "arbitrary"),
        name="proj",
    )(x, gain, sc1p, sh, w_in, cos, sin, kdec)


def _ret_kernel(cdec_ref, q_ref, k_ref, v_ref, sg_ref, dmask_ref, qdec_ref, gn_ref, y_ref, state_ref,
                *, dk, dv):
    @pl.when(pl.program_id(0) == 0)
    def _():
        state_ref[...] = jnp.zeros_like(state_ref)

    rows_per_step = q_ref.shape[0]
    for n in range(rows_per_step // RET_CHUNK):
        rows = slice(n * RET_CHUNK, (n + 1) * RET_CHUNK)
        for h in range(N_HEADS):
            qcols = slice(h * dk, (h + 1) * dk)
            vcols = slice(h * dv, (h + 1) * dv)
            q = q_ref[rows, qcols]
            ks = k_ref[rows, qcols]
            v = v_ref[rows, vcols]
            s = lax.dot_general(q, ks, (((1,), (1,)), ((), ())), preferred_element_type=F32)
            a = (s * dmask_ref[h]).astype(BF16)
            intra = jnp.dot(a, v, preferred_element_type=F32)
            st = state_ref[h]
            cross = jnp.dot(q, st.astype(BF16), preferred_element_type=F32)
            o = intra + cross * qdec_ref[h]
            kv = lax.dot_general(ks, v, (((0,), (0,)), ((), ())), preferred_element_type=F32)
            state_ref[h] = st * cdec_ref[h] + kv
            mu = jnp.mean(o, axis=-1, keepdims=True)
            dlt = o - mu
            var = jnp.mean(dlt * dlt, axis=-1, keepdims=True)
            on = dlt * lax.rsqrt(var + EPS)
            y = on * gn_ref[:, vcols] * sg_ref[rows, vcols].astype(F32)
            y_ref[rows, vcols] = y.astype(BF16)


def _retention_core(proj, cdec, dmask, qdec, gn_gain, qk_total, v_total):
    t = proj.shape[0]
    r = RET_ROWS
    dk = qk_total // N_HEADS
    dv = v_total // N_HEADS
    assert v_total == 2 * qk_total
    kern = functools.partial(_ret_kernel, dk=dk, dv=dv)
    return pl.pallas_call(
        kern,
        grid=(t // r,),
        in_specs=[
            pl.BlockSpec(memory_space=pltpu.SMEM),
            pl.BlockSpec((r, qk_total), lambda i: (i, 0)),
            pl.BlockSpec((r, qk_total), lambda i: (i, 1)),
            pl.BlockSpec((r, v_total), lambda i: (i, 1)),
            pl.BlockSpec((r, v_total), lambda i: (i, 2)),
            pl.BlockSpec((N_HEADS, RET_CHUNK, RET_CHUNK), lambda i: (0, 0, 0)),
            pl.BlockSpec((N_HEADS, RET_CHUNK, dv), lambda i: (0, 0, 0)),
            pl.BlockSpec((1, v_total), lambda i: (0, 0)),
        ],
        out_specs=pl.BlockSpec((r, v_total), lambda i: (i, 0)),
        out_shape=jax.ShapeDtypeStruct((t, v_total), BF16),
        scratch_shapes=[pltpu.VMEM((N_HEADS, dk, dv), F32)],
        compiler_params=_params("arbitrary"),
        name="ret",
    )(cdec, proj, proj, proj, proj, dmask, qdec, gn_gain)


def _out_kernel(y_ref, w_ref, x_ref, g_ref, o_ref):
    m = jnp.dot(y_ref[...], w_ref[...], preferred_element_type=F32)
    o_ref[...] = x_ref[...] + g_ref[...] * m


def _ret_out(y, w_out, x, g1):
    t, d = x.shape
    kdim = y.shape[1]
    tm = ROW_TILE
    return pl.pallas_call(
        _out_kernel,
        grid=(t // tm,),
        in_specs=[
            pl.BlockSpec((tm, kdim), lambda i: (i, 0)),
            _resident((kdim, d)),
            pl.BlockSpec((tm, d), lambda i: (i, 0)),
            pl.BlockSpec((1, d), lambda i: (0, 0)),
        ],
        out_specs=pl.BlockSpec((tm, d), lambda i: (i, 0)),
        out_shape=jax.ShapeDtypeStruct((t, d), F32),
        compiler_params=_params("arbitrary"),
        name="out",
    )(y, w_out, x, g1)


def _ffn_kernel(x_ref, gain_ref, sc_ref, sh_ref, g_ref, wg_ref, wu_ref, wd_ref, o_ref, *, ff_tile):
    x = x_ref[...]
    h = _norm_mod(x, gain_ref[...], sc_ref[...], sh_ref[...]).astype(BF16)
    d_ff = wg_ref.shape[1]
    acc = jnp.zeros(x.shape, F32)
    for lo in range(0, d_ff, ff_tile):
        hi = min(lo + ff_tile, d_ff)
        a = jnp.dot(h, wg_ref[:, lo:hi], preferred_element_type=F32)
        b = jnp.dot(h, wu_ref[:, lo:hi], preferred_element_type=F32)
        act = (_silu(a) * b).astype(BF16)
        acc = acc + jnp.dot(act, wd_ref[lo:hi, :], preferred_element_type=F32)
    o_ref[...] = x + g_ref[...] * acc


def _dense_ffn(x, gain, sc1p, sh, g2, wg, wu, wd):
    t, d = x.shape
    d_ff = wg.shape[1]
    tm = ROW_TILE
    vec = pl.BlockSpec((1, d), lambda i: (0, 0))
    return pl.pallas_call(
        functools.partial(_ffn_kernel, ff_tile=512),
        grid=(t // tm,),
        in_specs=[
            pl.BlockSpec((tm, d), lambda i: (i, 0)),
            vec, vec, vec, vec,
            _resident((d, d_ff)), _resident((d, d_ff)), _resident((d_ff, d)),
        ],
        out_specs=pl.BlockSpec((tm, d), lambda i: (i, 0)),
        out_shape=jax.ShapeDtypeStruct((t, d), F32),
        compiler_params=_params("arbitrary"),
        name="ffn",
    )(x, gain, sc1p, sh, g2, wg, wu, wd)


def _pool_kernel(x_ref, gain_ref, sc_ref, sh_ref, g_ref, band_ref, bandh_ref, wp_ref, bp_ref, ps_ref, o_ref,
                 halo_ref):
    i = pl.program_id(0)
    tm = x_ref.shape[0]
    gw = wp_ref.shape[1]

    @pl.when(i == 0)
    def _():
        halo_ref[...] = jnp.zeros_like(halo_ref)

    def split(v):
        hi = v.astype(BF16)
        return hi, (v - hi.astype(F32)).astype(BF16)

    sub = band_ref.shape[1]
    for s in range(tm // sub):
        rows = slice(s * sub, (s + 1) * sub)
        x = x_ref[rows, :]
        h = _norm_mod(x, gain_ref[...], sc_ref[...], sh_ref[...])
        h_hi, h_lo = split(h)
        halo_hi, halo_lo = split(halo_ref[...])
        halo_ref[...] = h[sub - POOL_HALO:, :]
        t1 = (lax.broadcasted_iota(jnp.int32, (sub, 1), 0) + (i * tm + s * sub + 1)).astype(F32)
        for g, w in enumerate(POOL_WINDOWS):
            cols = slice(g * gw, (g + 1) * gw)
            band = band_ref[g]
            bandh = bandh_ref[g]
            win = (jnp.dot(band, h_hi[:, cols], preferred_element_type=F32)
                   + jnp.dot(band, h_lo[:, cols], preferred_element_type=F32))
            head = (jnp.dot(bandh, halo_hi[:, cols], preferred_element_type=F32)
                    + jnp.dot(bandh, halo_lo[:, cols], preferred_element_type=F32))
            win = jnp.concatenate([win[:POOL_HALO] + head, win[POOL_HALO:]], axis=0)
            pooled = win / jnp.minimum(t1, float(w)) - h[:, cols]
            y = jnp.dot(pooled.astype(BF16), wp_ref[g], preferred_element_type=F32) + bp_ref[:, cols]
            o_ref[rows, cols] = x[:, cols] + g_ref[:, cols] * (y * ps_ref[:, cols])


def _pool_mixer(x, gain, sc1p, sh, g1, band, bandh, wp, bp, ps):
    t, d = x.shape
    tm = POOL_ROWS
    vec = pl.BlockSpec((1, d), lambda i: (0, 0))
    return pl.pallas_call(
        _pool_kernel,
        grid=(t // tm,),
        in_specs=[
            pl.BlockSpec((tm, d), lambda i: (i, 0)),
            vec, vec, vec, vec,
            pl.BlockSpec(band.shape, lambda i: (0, 0, 0)),
            pl.BlockSpec(bandh.shape, lambda i: (0, 0, 0)),
            pl.BlockSpec(wp.shape, lambda i: (0, 0, 0)),
            vec, vec,
        ],
        out_specs=pl.BlockSpec((tm, d), lambda i: (i, 0)),
        out_shape=jax.ShapeDtypeStruct((t, d), F32),
        scratch_shapes=[pltpu.VMEM((POOL_HALO, d), F32)],
        compiler_params=_params("arbitrary"),
        name="pool",
    )(x, gain, sc1p, sh, g1, band, bandh, wp, bp, ps)


def _router_kernel(x_ref, gain_ref, sc_ref, sh_ref, whi_ref, wlo_ref, h_ref, gates_ref, sel_ref):
    h = _norm_mod(x_ref[...], gain_ref[...], sc_ref[...], sh_ref[...])
    h_hi = h.astype(BF16)
    h_ref[...] = h_hi
    h_lo = (h - h_hi.astype(F32)).astype(BF16)
    w_hi = whi_ref[...]
    logits = (jnp.dot(h_hi, w_hi, preferred_element_type=F32)
              + jnp.dot(h_lo, w_hi, preferred_element_type=F32)
              + jnp.dot(h_hi, wlo_ref[...], preferred_element_type=F32))
    lane = lax.broadcasted_iota(jnp.int32, logits.shape, 1).astype(F32)
    neg = jnp.float32(-jnp.inf)
    lg = jnp.where(lane < float(N_EXPERTS), logits, neg)
    m0 = jnp.max(lg, axis=-1, keepdims=True)
    i0 = jnp.min(jnp.where(lg == m0, lane, float(LANES)), axis=-1, keepdims=True)
    lg1 = jnp.where(lane == i0, neg, lg)
    m1 = jnp.max(lg1, axis=-1, keepdims=True)
    i1 = jnp.min(jnp.where(lg1 == m1, lane, float(LANES)), axis=-1, keepdims=True)
    e1 = jnp.exp(m1 - m0)
    den = 1.0 + e1
    is0 = lane == i0
    is1 = lane == i1
    gates_ref[...] = jnp.where(is0, 1.0 / den, jnp.where(is1, e1 / den, 0.0))
    sel_ref[...] = jnp.where(is0 | is1, 1.0, 0.0).astype(BF16)


def _router(x, gain, sc1p, sh, w_hi, w_lo):
    t, d = x.shape
    tm = ROW_TILE
    vec = pl.BlockSpec((1, d), lambda i: (0, 0))
    wspec = pl.BlockSpec((d, LANES), lambda i: (0, 0))
    return pl.pallas_call(
        _router_kernel,
        grid=(t // tm,),
        in_specs=[pl.BlockSpec((tm, d), lambda i: (i, 0)), vec, vec, vec, wspec, wspec],
        out_specs=[
            pl.BlockSpec((tm, d), lambda i: (i, 0)),
            pl.BlockSpec((tm, LANES), lambda i: (i, 0)),
            pl.BlockSpec((tm, LANES), lambda i: (i, 0)),
        ],
        out_shape=[
            jax.ShapeDtypeStruct((t, d), BF16),
            jax.ShapeDtypeStruct((t, LANES), F32),
            jax.ShapeDtypeStruct((t, LANES), BF16),
        ],
        compiler_params=_params("arbitrary"),
        name="router",
    )(x, gain, sc1p, sh, w_hi, w_lo)


def _rank_kernel(sel_ref, tri_ref, rank_ref, tot_ref, carry_ref):
    @pl.when(pl.program_id(0) == 0)
    def _():
        carry_ref[...] = jnp.zeros_like(carry_ref)

    s = sel_ref[...]
    cum = jnp.dot(tri_ref[...], s, preferred_element_type=F32)
    carry = carry_ref[...]
    rank_ref[...] = cum - s.astype(F32) + carry
    tot = carry + cum[cum.shape[0] - 1:, :]
    carry_ref[...] = tot
    tot_ref[0] = tot


def _ranks(sel, tri):
    t = sel.shape[0]
    tm = RANK_ROWS
    return pl.pallas_call(
        _rank_kernel,
        grid=(t // tm,),
        in_specs=[pl.BlockSpec((tm, LANES), lambda i: (i, 0)), pl.BlockSpec((tm, tm), lambda i: (0, 0))],
        out_specs=[pl.BlockSpec((tm, LANES), lambda i: (i, 0)), pl.BlockSpec((1, 1, LANES), lambda i: (i, 0, 0))],
        out_shape=[jax.ShapeDtypeStruct((t, LANES), F32), jax.ShapeDtypeStruct((t // tm, 1, LANES), F32)],
        scratch_shapes=[pltpu.VMEM((1, LANES), F32)],
        compiler_params=_params("arbitrary"),
        name="rank",
    )(sel, tri)


def _experts_kernel(be_ref, nv_ref, clo_ref, chi_ref, h_ref, post_ref, wg_ref, wu_ref, wd_ref, yb_ref,
                    xg_ref, acc_ref, wgb_ref, wub_ref, wdb_ref):
    b = pl.program_id(0)
    f = pl.program_id(1)
    nf = pl.num_programs(1)
    nv = nv_ref[b]
    block_rows = xg_ref.shape[0]
    n_gather = block_rows // MOE_GATHER_ROWS

    @pl.when((f == 0) & (nv > 0))
    def _():
        e = be_ref[b]
        for q in range(n_gather):
            rows = slice(q * MOE_GATHER_ROWS, (q + 1) * MOE_GATHER_ROWS)

            @pl.when(nv > q * MOE_GATHER_ROWS)
            def _():
                acc_ref[rows, :] = jnp.zeros((MOE_GATHER_ROWS, acc_ref.shape[1]), F32)
                row_pos = (lax.broadcasted_iota(jnp.int32, (MOE_GATHER_ROWS, GATHER_CHUNK), 0)
                           + (b * block_rows + q * MOE_GATHER_ROWS))

                def body(c, carry):
                    off = pl.multiple_of(c * GATHER_CHUNK, GATHER_CHUNK)
                    p = post_ref[pl.ds(e, 1), pl.ds(off, GATHER_CHUNK)]
                    onehot = jnp.where(row_pos == p, 1.0, 0.0).astype(BF16)
                    acc_ref[rows, :] += jnp.dot(onehot, h_ref[pl.ds(off, GATHER_CHUNK), :],
                                                preferred_element_type=F32)
                    return carry

                lax.fori_loop(clo_ref[b * n_gather + q], chi_ref[b * n_gather + q], body, 0)
                xg_ref[rows, :] = acc_ref[rows, :].astype(BF16)

            @pl.when(nv <= q * MOE_GATHER_ROWS)
            def _():
                xg_ref[rows, :] = jnp.zeros((MOE_GATHER_ROWS, xg_ref.shape[1]), BF16)

        acc_ref[...] = jnp.zeros_like(acc_ref)

    @pl.when(nv > 0)
    def _():
        wgb_ref[...] = wg_ref[0].astype(BF16)
        wub_ref[...] = wu_ref[0].astype(BF16)
        wdb_ref[...] = wd_ref[0].astype(BF16)
        for g in range(block_rows // MOE_DOT_ROWS):
            rows = slice(g * MOE_DOT_ROWS, (g + 1) * MOE_DOT_ROWS)

            @pl.when(nv > g * MOE_DOT_ROWS)
            def _():
                x = xg_ref[rows, :]
                a = jnp.dot(x, wgb_ref[...], preferred_element_type=F32)
                u = jnp.dot(x, wub_ref[...], preferred_element_type=F32)
                act = (_silu(a) * u).astype(BF16)
                acc_ref[rows, :] += jnp.dot(act, wdb_ref[...], preferred_element_type=F32)

    @pl.when(f == nf - 1)
    def _():
        for g in range(block_rows // MOE_DOT_ROWS):
            rows = slice(g * MOE_DOT_ROWS, (g + 1) * MOE_DOT_ROWS)

            @pl.when(nv > g * MOE_DOT_ROWS)
            def _():
                yb_ref[rows, :] = acc_ref[rows, :].astype(BF16)

            @pl.when(nv <= g * MOE_DOT_ROWS)
            def _():
                yb_ref[rows, :] = jnp.zeros((MOE_DOT_ROWS, yb_ref.shape[1]), BF16)


def _experts(be, nv, clo, chi, h, post, wg, wu, wd):
    t, d = h.shape
    n_blocks = be.shape[0]
    d_ff = wg.shape[2]
    tf = MOE_FF_TILE
    nf = d_ff // tf
    bsz = MOE_BLOCK

    def ff_idx(f, nvr, b):
        return jnp.where(nvr[b] > 0, f, nf - 1)

    grid_spec = pltpu.PrefetchScalarGridSpec(
        num_scalar_prefetch=4,
        grid=(n_blocks, nf),
        in_specs=[
            pl.BlockSpec((t, d), lambda b, f, *_: (0, 0), pipeline_mode=pl.Buffered(1)),
            pl.BlockSpec(post.shape, lambda b, f, *_: (0, 0), pipeline_mode=pl.Buffered(1)),
            pl.BlockSpec((1, d, tf), lambda b, f, be_r, nv_r, *_: (be_r[b], 0, ff_idx(f, nv_r, b))),
            pl.BlockSpec((1, d, tf), lambda b, f, be_r, nv_r, *_: (be_r[b], 0, ff_idx(f, nv_r, b))),
            pl.BlockSpec((1, tf, d), lambda b, f, be_r, nv_r, *_: (be_r[b], ff_idx(f, nv_r, b), 0)),
        ],
        out_specs=pl.BlockSpec((bsz, d), lambda b, f, *_: (b, 0)),
        scratch_shapes=[
            pltpu.VMEM((bsz, d), BF16), pltpu.VMEM((bsz, d), F32),
            pltpu.VMEM((d, tf), BF16), pltpu.VMEM((d, tf), BF16), pltpu.VMEM((tf, d), BF16),
        ],
    )
    return pl.pallas_call(
        _experts_kernel,
        grid_spec=grid_spec,
        out_shape=jax.ShapeDtypeStruct((n_blocks * bsz, d), BF16),
        compiler_params=_params("arbitrary", "arbitrary"),
        name="experts",
    )(be, nv, clo, chi, h, post, wg, wu, wd)


def _combine_kernel(win_ref, x_ref, pos_ref, gates_ref, g_ref, fg_ref, *refs):
    yb_refs, o_ref = refs[:-1], refs[-1]
    i = pl.program_id(0)
    tm = x_ref.shape[0]
    lane = lax.broadcasted_iota(jnp.int32, (tm, 2 * COMBINE_ROWS), 1)
    acc = jnp.zeros(x_ref.shape, F32)
    for e in range(N_EXPERTS):
        base = win_ref[i * N_EXPERTS + e] * COMBINE_ROWS
        rel = pos_ref[:, e:e + 1] - base
        onehot = jnp.where(lane == rel, 1.0, 0.0).astype(BF16)
        window = jnp.concatenate([yb_refs[2 * e][...], yb_refs[2 * e + 1][...]], axis=0)
        picked = jnp.dot(onehot, window, preferred_element_type=F32)
        acc = acc + gates_ref[:, e:e + 1] * picked
    xo = x_ref[...] + g_ref[...] * acc
    ms = jnp.mean(xo * xo, axis=-1, keepdims=True)
    o_ref[...] = xo * lax.rsqrt(ms + EPS) * fg_ref[...]


def _combine(win, x, pos, gates, g2, fgain, yb):
    t, d = x.shape
    tm = COMBINE_ROWS
    n_win = yb.shape[0] // COMBINE_ROWS
    vec = pl.BlockSpec((1, d), lambda i, w: (0, 0))

    def first(e):
        return lambda i, w: (jnp.minimum(w[i * N_EXPERTS + e], n_win - 1), 0)

    def second(e):
        return lambda i, w: (jnp.minimum(w[i * N_EXPERTS + e] + 1, n_win - 1), 0)

    yb_specs = []
    for e in range(N_EXPERTS):
        yb_specs.append(pl.BlockSpec((COMBINE_ROWS, d), first(e)))
        yb_specs.append(pl.BlockSpec((COMBINE_ROWS, d), second(e)))
    grid_spec = pltpu.PrefetchScalarGridSpec(
        num_scalar_prefetch=1,
        grid=(t // tm,),
        in_specs=[
            pl.BlockSpec((tm, d), lambda i, w: (i, 0)),
            pl.BlockSpec((tm, N_EXPERTS), lambda i, w: (i, 0)),
            pl.BlockSpec((tm, LANES), lambda i, w: (i, 0)),
            vec, vec,
        ] + yb_specs,
        out_specs=pl.BlockSpec((tm, d), lambda i, w: (i, 0)),
    )
    return pl.pallas_call(
        _combine_kernel,
        grid_spec=grid_spec,
        out_shape=jax.ShapeDtypeStruct((t, d), F32),
        compiler_params=_params("arbitrary"),
        name="combine",
    )(win, x, pos, gates, g2, fgain, *([yb] * (2 * N_EXPERTS)))


def _routing_tables(rank, tot, sel):
    t = rank.shape[0]
    bsz = MOE_BLOCK
    n_blocks = (2 * t) // bsz + N_EXPERTS
    cum = tot[:, 0, :N_EXPERTS].astype(jnp.int32)
    counts = cum[-1]
    padded = ((counts + bsz - 1) // bsz) * bsz
    pad_end = jnp.cumsum(padded)
    start_pad = pad_end - padded
    selected = sel[:, :N_EXPERTS] > 0
    pos = jnp.where(selected, start_pad[None, :] + rank[:, :N_EXPERTS].astype(jnp.int32), -1)

    bstart = jnp.arange(n_blocks, dtype=jnp.int32) * bsz
    be = jnp.minimum(jnp.sum(bstart[:, None] >= pad_end[None, :], axis=1), N_EXPERTS - 1).astype(jnp.int32)
    r0 = bstart - start_pad[be]
    nv = jnp.clip(counts[be] - r0, 0, bsz).astype(jnp.int32)
    n_gather = bsz // MOE_GATHER_ROWS
    goff = jnp.arange(n_gather, dtype=jnp.int32) * MOE_GATHER_ROWS
    r0g = (r0[:, None] + goff[None, :]).reshape(-1)
    ng = jnp.clip(nv[:, None] - goff[None, :], 0, MOE_GATHER_ROWS).reshape(-1)
    beg = jnp.repeat(be, n_gather)
    chunk_end = cum[:, beg].T
    chunk_start = jnp.concatenate([jnp.zeros((1, N_EXPERTS), jnp.int32), cum[:-1]], axis=0)[:, beg].T
    clo = jnp.sum(chunk_end <= r0g[:, None], axis=1).astype(jnp.int32)
    chi = jnp.sum(chunk_start < (r0g + ng)[:, None], axis=1).astype(jnp.int32)

    big = jnp.int32(2 ** 30)
    first = jnp.min(jnp.where(pos >= 0, pos, big).reshape(t // COMBINE_ROWS, COMBINE_ROWS, N_EXPERTS), axis=1)
    win = jnp.where(first == big, 0, first // COMBINE_ROWS).astype(jnp.int32).reshape(-1)
    return pos, be, nv, clo, chi, win


def _rotary_tables(seq, dk):
    half = dk // 2
    inv = ROPE_BASE ** (-jnp.arange(half, dtype=F32) / half)
    ang = jnp.arange(seq).astype(F32)[:, None] * inv[None, :]
    return jnp.cos(ang), jnp.sin(ang)


def _decay_tables(dk, dv):
    c = RET_CHUNK
    log_gamma = jnp.log1p(-jnp.exp2(-5.0 - jnp.arange(N_HEADS, dtype=F32)))
    idx = jnp.arange(c, dtype=F32)
    row_dec = jnp.exp(log_gamma[:, None] * (idx + 1.0 - c))
    causal = (idx[:, None] >= idx[None, :]).astype(F32)
    dmask = row_dec[:, :, None] * causal[None]
    k_dec = jnp.exp(log_gamma[:, None] * (c - 1.0 - idx)) * (dk ** -0.5)
    q_dec = jnp.exp(log_gamma[:, None] * (idx + 1.0))
    chunk_dec = jnp.exp(log_gamma * c)
    half = dk // 2
    kdec_tile = jnp.tile(jnp.repeat(k_dec.T, half, axis=1), (ROW_TILE // c, 1))
    qdec_tile = jnp.broadcast_to(q_dec[:, :, None], (N_HEADS, c, dv))
    return dmask, kdec_tile, qdec_tile, chunk_dec


def _pool_bands(tm):
    def bands(lag):
        return jnp.stack([((lag >= 0) & (lag < w)) for w in POOL_WINDOWS]).astype(BF16)
    main = bands(jnp.arange(tm)[:, None] - jnp.arange(tm)[None, :])
    head = bands(jnp.arange(POOL_HALO)[:, None] + POOL_HALO - jnp.arange(POOL_HALO)[None, :])
    return main, head


def kernel(x, c, ada_w, ada_b, norm_gain, ret_w_in, ret_gn_gain, ret_w_out, ffn_w_gate, ffn_w_up, ffn_w_down,
           pool_w, pool_b, pool_scale, moe_router, moe_w_gate, moe_w_up, moe_w_down, final_norm_gain):
    bsz, seq, d = x.shape
    assert bsz == 1 and ada_w.shape[0] == 2
    xt = x.reshape(seq, d)
    qk_total = d
    v_total = ret_w_out.shape[1]
    dk = qk_total // N_HEADS
    dv = v_total // N_HEADS

    mod = _modulation(c, ada_w, ada_b)
    def mods(i):
        parts = [mod[i, :, k * d:(k + 1) * d] for k in range(N_MOD)]
        sh1, sc1, g1, sh2, sc2, g2 = parts
        return sh1, 1.0 + sc1, g1, sh2, 1.0 + sc2, g2

    sh1, sc1p, g1, sh2, sc2p, g2 = mods(0)
    cos, sin = _rotary_tables(seq, dk)
    dmask, kdec_tile, qdec_tile, chunk_dec = _decay_tables(dk, dv)
    proj = _ret_proj(xt, norm_gain[0, 0][None], sc1p, sh1, ret_w_in[0].astype(BF16), cos, sin, kdec_tile,
                     qk_total, v_total)
    y = _retention_core(proj, chunk_dec, dmask, qdec_tile, ret_gn_gain[0][None], qk_total, v_total)
    x1 = _ret_out(y, ret_w_out[0].astype(BF16), xt, g1)
    x2 = _dense_ffn(x1, norm_gain[0, 1][None], sc2p, sh2, g2,
                    ffn_w_gate[0].astype(BF16), ffn_w_up[0].astype(BF16), ffn_w_down[0].astype(BF16))

    sh1, sc1p, g1, sh2, sc2p, g2 = mods(1)
    band, bandh = _pool_bands(POOL_SUB_ROWS)
    x3 = _pool_mixer(x2, norm_gain[1, 0][None], sc1p, sh1, g1, band, bandh,
                     pool_w[0].astype(BF16), pool_b[0].reshape(1, d), pool_scale[0][None])

    wr = jnp.pad(moe_router[0], ((0, 0), (0, LANES - N_EXPERTS)))
    wr_hi = wr.astype(BF16)
    wr_lo = (wr - wr_hi.astype(F32)).astype(BF16)
    h4, gates, sel = _router(x3, norm_gain[1, 1][None], sc2p, sh2, wr_hi, wr_lo)
    tri = (jnp.arange(RANK_ROWS)[:, None] >= jnp.arange(RANK_ROWS)[None, :]).astype(BF16)
    rank, tot = _ranks(sel, tri)
    pos, be, nv, clo, chi, win = _routing_tables(rank, tot, sel)
    yb = _experts(be, nv, clo, chi, h4, pos.T, moe_w_gate[0], moe_w_up[0], moe_w_down[0])
    out = _combine(win, x3, pos, gates, g2, final_norm_gain[None], yb)
    return out.reshape(bsz, seq, d)
```

```python
import functools

import jax
import jax.numpy as jnp
from jax import lax
from jax.experimental import pallas as pl
from jax.experimental.pallas import tpu as pltpu

F32 = jnp.float32
BF16 = jnp.bfloat16

EPS = 1e-6
N_HEADS = 4
RET_CHUNK = 256
ROPE_BASE = 10000.0
POOL_WINDOWS = (2, 4, 8, 16)
POOL_HALO = 16
N_EXPERTS = 8
N_MOD = 6

VMEM_LIMIT_BYTES = 56 * 1024 * 1024

ROW_TILE = 1024
RET_ROWS = 512
POOL_ROWS = 512
POOL_SUB_ROWS = 256
RANK_ROWS = 512
MOE_BLOCK = 2048
MOE_DOT_ROWS = 512
MOE_GATHER_UNIT = 1024
MOE_GATHER_ROWS = 256
MOE_FF_TILE = 512
GATHER_CHUNK = 512
COMBINE_TILE = 256
COMBINE_ROWS = 128
LANES = 128


def _silu(v):
    return v / (1.0 + jnp.exp(-v))


def _norm_mod(x, gain, scale1p, shift):
    ms = jnp.mean(x * x, axis=-1, keepdims=True)
    return (x * lax.rsqrt(ms + EPS) * gain) * scale1p + shift


def _params(*sem):
    return pltpu.CompilerParams(dimension_semantics=sem, vmem_limit_bytes=VMEM_LIMIT_BYTES)


def _resident(shape):
    nd = len(shape)
    return pl.BlockSpec(shape, lambda *_: (0,) * nd, pipeline_mode=pl.Buffered(1))


def _mod_kernel(c_ref, w_ref, b_ref, o_ref):
    ca = _silu(c_ref[...])
    o_ref[0] = jnp.sum(ca * w_ref[0], axis=0, keepdims=True) + b_ref[0]


def _modulation(c, ada_w, ada_b):
    depth, d, n = ada_w.shape
    tn = 1024
    return pl.pallas_call(
        _mod_kernel,
        grid=(depth, n // tn),
        in_specs=[
            pl.BlockSpec((d, 1), lambda i, j: (0, 0)),
            pl.BlockSpec((1, d, tn), lambda i, j: (i, 0, j)),
            pl.BlockSpec((1, 1, tn), lambda i, j: (i, 0, j)),
        ],
        out_specs=pl.BlockSpec((1, 1, tn), lambda i, j: (i, 0, j)),
        out_shape=jax.ShapeDtypeStruct((depth, 1, n), F32),
        compiler_params=_params("arbitrary", "arbitrary"),
        name="mod",
    )(c.reshape(d, 1), ada_w, ada_b.reshape(depth, 1, n))


def _proj_kernel(x_ref, gain_ref, sc_ref, sh_ref, w_ref, cos_ref, sin_ref, kdec_ref, o_ref, h_ref,
                 *, n_qk_tiles, n_v_tiles, dk):
    j = pl.program_id(1)

    @pl.when(j == 0)
    def _():
        h_ref[...] = _norm_mod(x_ref[...], gain_ref[...], sc_ref[...], sh_ref[...]).astype(BF16)

    half = dk // 2
    n_chunks = w_ref.shape[1] // dk

    def chunk_dot(c):
        return jnp.dot(h_ref[...], w_ref[:, c * dk:(c + 1) * dk], preferred_element_type=F32)

    def rotary(scale_ref):
        cos = cos_ref[...]
        sin = sin_ref[...]
        for c in range(n_chunks):
            acc = chunk_dot(c)
            t1 = acc[:, :half]
            t2 = acc[:, half:]
            o1 = t1 * cos - t2 * sin
            o2 = t1 * sin + t2 * cos
            if scale_ref is not None:
                s = scale_ref[:, c * half:(c + 1) * half]
                o1 = o1 * s
                o2 = o2 * s
            o_ref[:, c * dk:c * dk + half] = o1.astype(BF16)
            o_ref[:, c * dk + half:(c + 1) * dk] = o2.astype(BF16)

    @pl.when(j < n_qk_tiles)
    def _():
        rotary(None)

    @pl.when((j >= n_qk_tiles) & (j < 2 * n_qk_tiles))
    def _():
        rotary(kdec_ref)

    @pl.when((j >= 2 * n_qk_tiles) & (j < 2 * n_qk_tiles + n_v_tiles))
    def _():
        for c in range(n_chunks):
            o_ref[:, c * dk:(c + 1) * dk] = chunk_dot(c).astype(BF16)

    @pl.when(j >= 2 * n_qk_tiles + n_v_tiles)
    def _():
        for c in range(n_chunks):
            o_ref[:, c * dk:(c + 1) * dk] = _silu(chunk_dot(c)).astype(BF16)


def _ret_proj(x, gain, sc1p, sh, w_in, cos, sin, kdec, qk_total, v_total):
    t, d = x.shape
    n = w_in.shape[1]
    tm, tn = ROW_TILE, 1024
    dk = qk_total // N_HEADS
    kern = functools.partial(_proj_kernel, n_qk_tiles=qk_total // tn, n_v_tiles=v_total // tn, dk=dk)
    vec = pl.BlockSpec((1, d), lambda i, j: (0, 0))
    return pl.pallas_call(
        kern,
        grid=(t // tm, n // tn),
        in_specs=[
            pl.BlockSpec((tm, d), lambda i, j: (i, 0)),
            vec, vec, vec,
            pl.BlockSpec((d, tn), lambda i, j: (0, j)),
            pl.BlockSpec((tm, dk // 2), lambda i, j: (i, 0)),
            pl.BlockSpec((tm, dk // 2), lambda i, j: (i, 0)),
            pl.BlockSpec((tm, N_HEADS * (dk // 2)), lambda i, j: (0, 0)),
        ],
        out_specs=pl.BlockSpec((tm, tn), lambda i, j: (i, j)),
        out_shape=jax.ShapeDtypeStruct((t, n), BF16),
        scratch_shapes=[pltpu.VMEM((tm, d), BF16)],
        compiler_params=_params("arbitrary", "arbitrary"),
        name="proj",
    )(x, gain, sc1p, sh, w_in, cos, sin, kdec)


def _ret_kernel(cdec_ref, q_ref, k_ref, v_ref, sg_ref, dmask_ref, qdec_ref, gn_ref, y_ref, state_ref,
                *, dk, dv):
    @pl.when(pl.program_id(0) == 0)
    def _():
        state_ref[...] = jnp.zeros_like(state_ref)

    rows_per_step = q_ref.shape[0]
    for n in range(rows_per_step // RET_CHUNK):
        rows = slice(n * RET_CHUNK, (n + 1) * RET_CHUNK)
        for h in range(N_HEADS):
            qcols = slice(h * dk, (h + 1) * dk)
            vcols = slice(h * dv, (h + 1) * dv)
            q = q_ref[rows, qcols]
            ks = k_ref[rows, qcols]
            v = v_ref[rows, vcols]
            s = lax.dot_general(q, ks, (((1,), (1,)), ((), ())), preferred_element_type=F32)
            a = (s * dmask_ref[h]).astype(BF16)
            intra = jnp.dot(a, v, preferred_element_type=F32)
            st = state_ref[h]
            cross = jnp.dot(q, st.astype(BF16), preferred_element_type=F32)
            o = intra + cross * qdec_ref[h]
            kv = lax.dot_general(ks, v, (((0,), (0,)), ((), ())), preferred_element_type=F32)
            state_ref[h] = st * cdec_ref[h] + kv
            mu = jnp.mean(o, axis=-1, keepdims=True)
            dlt = o - mu
            var = jnp.mean(dlt * dlt, axis=-1, keepdims=True)
            on = dlt * lax.rsqrt(var + EPS)
            y = on * gn_ref[:, vcols] * sg_ref[rows, vcols].astype(F32)
            y_ref[rows, vcols] = y.astype(BF16)


def _retention_core(proj, cdec, dmask, qdec, gn_gain, qk_total, v_total):
    t = proj.shape[0]
    r = RET_ROWS
    dk = qk_total // N_HEADS
    dv = v_total // N_HEADS
    assert v_total == 2 * qk_total
    kern = functools.partial(_ret_kernel, dk=dk, dv=dv)
    return pl.pallas_call(
        kern,
        grid=(t // r,),
        in_specs=[
            pl.BlockSpec(memory_space=pltpu.SMEM),
            pl.BlockSpec((r, qk_total), lambda i: (i, 0)),
            pl.BlockSpec((r, qk_total), lambda i: (i, 1)),
            pl.BlockSpec((r, v_total), lambda i: (i, 1)),
            pl.BlockSpec((r, v_total), lambda i: (i, 2)),
            pl.BlockSpec((N_HEADS, RET_CHUNK, RET_CHUNK), lambda i: (0, 0, 0)),
            pl.BlockSpec((N_HEADS, RET_CHUNK, dv), lambda i: (0, 0, 0)),
            pl.BlockSpec((1, v_total), lambda i: (0, 0)),
        ],
        out_specs=pl.BlockSpec((r, v_total), lambda i: (i, 0)),
        out_shape=jax.ShapeDtypeStruct((t, v_total), BF16),
        scratch_shapes=[pltpu.VMEM((N_HEADS, dk, dv), F32)],
        compiler_params=_params("arbitrary"),
        name="ret",
    )(cdec, proj, proj, proj, proj, dmask, qdec, gn_gain)


def _out_kernel(y_ref, w_ref, x_ref, g_ref, o_ref):
    m = jnp.dot(y_ref[...], w_ref[...], preferred_element_type=F32)
    o_ref[...] = x_ref[...] + g_ref[...] * m


def _ret_out(y, w_out, x, g1):
    t, d = x.shape
    kdim = y.shape[1]
    tm = ROW_TILE
    return pl.pallas_call(
        _out_kernel,
        grid=(t // tm,),
        in_specs=[
            pl.BlockSpec((tm, kdim), lambda i: (i, 0)),
            _resident((kdim, d)),
            pl.BlockSpec((tm, d), lambda i: (i, 0)),
            pl.BlockSpec((1, d), lambda i: (0, 0)),
        ],
        out_specs=pl.BlockSpec((tm, d), lambda i: (i, 0)),
        out_shape=jax.ShapeDtypeStruct((t, d), F32),
        compiler_params=_params("arbitrary"),
        name="out",
    )(y, w_out, x, g1)


def _ffn_kernel(x_ref, gain_ref, sc_ref, sh_ref, g_ref, wg_ref, wu_ref, wd_ref, o_ref, *, ff_tile):
    x = x_ref[...]
    h = _norm_mod(x, gain_ref[...], sc_ref[...], sh_ref[...]).astype(BF16)
    d_ff = wg_ref.shape[1]
    acc = jnp.zeros(x.shape, F32)
    for lo in range(0, d_ff, ff_tile):
        hi = min(lo + ff_tile, d_ff)
        a = jnp.dot(h, wg_ref[:, lo:hi], preferred_element_type=F32)
        b = jnp.dot(h, wu_ref[:, lo:hi], preferred_element_type=F32)
        act = (_silu(a) * b).astype(BF16)
        acc = acc + jnp.dot(act, wd_ref[lo:hi, :], preferred_element_type=F32)
    o_ref[...] = x + g_ref[...] * acc


def _dense_ffn(x, gain, sc1p, sh, g2, wg, wu, wd):
    t, d = x.shape
    d_ff = wg.shape[1]
    tm = ROW_TILE
    vec = pl.BlockSpec((1, d), lambda i: (0, 0))
    return pl.pallas_call(
        functools.partial(_ffn_kernel, ff_tile=512),
        grid=(t // tm,),
        in_specs=[
            pl.BlockSpec((tm, d), lambda i: (i, 0)),
            vec, vec, vec, vec,
            _resident((d, d_ff)), _resident((d, d_ff)), _resident((d_ff, d)),
        ],
        out_specs=pl.BlockSpec((tm, d), lambda i: (i, 0)),
        out_shape=jax.ShapeDtypeStruct((t, d), F32),
        compiler_params=_params("arbitrary"),
        name="ffn",
    )(x, gain, sc1p, sh, g2, wg, wu, wd)


def _pool_kernel(x_ref, gain_ref, sc_ref, sh_ref, g_ref, band_ref, bandh_ref, wp_ref, bp_ref, ps_ref, o_ref,
                 halo_ref):
    i = pl.program_id(0)
    tm = x_ref.shape[0]
    gw = wp_ref.shape[1]

    @pl.when(i == 0)
    def _():
        halo_ref[...] = jnp.zeros_like(halo_ref)

    def split(v):
        hi = v.astype(BF16)
        return hi, (v - hi.astype(F32)).astype(BF16)

    sub = band_ref.shape[1]
    for s in range(tm // sub):
        rows = slice(s * sub, (s + 1) * sub)
        x = x_ref[rows, :]
        h = _norm_mod(x, gain_ref[...], sc_ref[...], sh_ref[...])
        h_hi, h_lo = split(h)
        halo_hi, halo_lo = split(halo_ref[...])
        halo_ref[...] = h[sub - POOL_HALO:, :]
        t1 = (lax.broadcasted_iota(jnp.int32, (sub, 1), 0) + (i * tm + s * sub + 1)).astype(F32)
        for g, w in enumerate(POOL_WINDOWS):
            cols = slice(g * gw, (g + 1) * gw)
            band = band_ref[g]
            bandh = bandh_ref[g]
            win = (jnp.dot(band, h_hi[:, cols], preferred_element_type=F32)
                   + jnp.dot(band, h_lo[:, cols], preferred_element_type=F32))
            head = (jnp.dot(bandh, halo_hi[:, cols], preferred_element_type=F32)
                    + jnp.dot(bandh, halo_lo[:, cols], preferred_element_type=F32))
            win = jnp.concatenate([win[:POOL_HALO] + head, win[POOL_HALO:]], axis=0)
            pooled = win / jnp.minimum(t1, float(w)) - h[:, cols]
            y = jnp.dot(pooled.astype(BF16), wp_ref[g], preferred_element_type=F32) + bp_ref[:, cols]
            o_ref[rows, cols] = x[:, cols] + g_ref[:, cols] * (y * ps_ref[:, cols])


def _pool_mixer(x, gain, sc1p, sh, g1, band, bandh, wp, bp, ps):
    t, d = x.shape
    tm = POOL_ROWS
    vec = pl.BlockSpec((1, d), lambda i: (0, 0))
    return pl.pallas_call(
        _pool_kernel,
        grid=(t // tm,),
        in_specs=[
            pl.BlockSpec((tm, d), lambda i: (i, 0)),
            vec, vec, vec, vec,
            pl.BlockSpec(band.shape, lambda i: (0, 0, 0)),
            pl.BlockSpec(bandh.shape, lambda i: (0, 0, 0)),
            pl.BlockSpec(wp.shape, lambda i: (0, 0, 0)),
            vec, vec,
        ],
        out_specs=pl.BlockSpec((tm, d), lambda i: (i, 0)),
        out_shape=jax.ShapeDtypeStruct((t, d), F32),
        scratch_shapes=[pltpu.VMEM((POOL_HALO, d), F32)],
        compiler_params=_params("arbitrary"),
        name="pool",
    )(x, gain, sc1p, sh, g1, band, bandh, wp, bp, ps)


def _router_kernel(x_ref, gain_ref, sc_ref, sh_ref, whi_ref, wlo_ref, h_ref, gates_ref, sel_ref):
    h = _norm_mod(x_ref[...], gain_ref[...], sc_ref[...], sh_ref[...])
    h_hi = h.astype(BF16)
    h_ref[...] = h_hi
    h_lo = (h - h_hi.astype(F32)).astype(BF16)
    w_hi = whi_ref[...]
    logits = (jnp.dot(h_hi, w_hi, preferred_element_type=F32)
              + jnp.dot(h_lo, w_hi, preferred_element_type=F32)
              + jnp.dot(h_hi, wlo_ref[...], preferred_element_type=F32))
    lane = lax.broadcasted_iota(jnp.int32, logits.shape, 1).astype(F32)
    neg = jnp.float32(-jnp.inf)
    lg = jnp.where(lane < float(N_EXPERTS), logits, neg)
    m0 = jnp.max(lg, axis=-1, keepdims=True)
    i0 = jnp.min(jnp.where(lg == m0, lane, float(LANES)), axis=-1, keepdims=True)
    lg1 = jnp.where(lane == i0, neg, lg)
    m1 = jnp.max(lg1, axis=-1, keepdims=True)
    i1 = jnp.min(jnp.where(lg1 == m1, lane, float(LANES)), axis=-1, keepdims=True)
    e1 = jnp.exp(m1 - m0)
    den = 1.0 + e1
    is0 = lane == i0
    is1 = lane == i1
    gates_ref[...] = jnp.where(is0, 1.0 / den, jnp.where(is1, e1 / den, 0.0))
    sel_ref[...] = jnp.where(is0 | is1, 1.0, 0.0).astype(BF16)


def _router(x, gain, sc1p, sh, w_hi, w_lo):
    t, d = x.shape
    tm = ROW_TILE
    vec = pl.BlockSpec((1, d), lambda i: (0, 0))
    wspec = pl.BlockSpec((d, LANES), lambda i: (0, 0))
    return pl.pallas_call(
        _router_kernel,
        grid=(t // tm,),
        in_specs=[pl.BlockSpec((tm, d), lambda i: (i, 0)), vec, vec, vec, wspec, wspec],
        out_specs=[
            pl.BlockSpec((tm, d), lambda i: (i, 0)),
            pl.BlockSpec((tm, LANES), lambda i: (i, 0)),
            pl.BlockSpec((tm, LANES), lambda i: (i, 0)),
        ],
        out_shape=[
            jax.ShapeDtypeStruct((t, d), BF16),
            jax.ShapeDtypeStruct((t, LANES), F32),
            jax.ShapeDtypeStruct((t, LANES), BF16),
        ],
        compiler_params=_params("arbitrary"),
        name="router",
    )(x, gain, sc1p, sh, w_hi, w_lo)


def _rank_kernel(sel_ref, tri_ref, rank_ref, tot_ref, carry_ref):
    @pl.when(pl.program_id(0) == 0)
    def _():
        carry_ref[...] = jnp.zeros_like(carry_ref)

    s = sel_ref[...]
    cum = jnp.dot(tri_ref[...], s, preferred_element_type=F32)
    carry = carry_ref[...]
    rank_ref[...] = cum - s.astype(F32) + carry
    tot = carry + cum[cum.shape[0] - 1:, :]
    carry_ref[...] = tot
    tot_ref[0] = tot


def _ranks(sel, tri):
    t = sel.shape[0]
    tm = RANK_ROWS
    return pl.pallas_call(
        _rank_kernel,
        grid=(t // tm,),
        in_specs=[pl.BlockSpec((tm, LANES), lambda i: (i, 0)), pl.BlockSpec((tm, tm), lambda i: (0, 0))],
        out_specs=[pl.BlockSpec((tm, LANES), lambda i: (i, 0)), pl.BlockSpec((1, 1, LANES), lambda i: (i, 0, 0))],
        out_shape=[jax.ShapeDtypeStruct((t, LANES), F32), jax.ShapeDtypeStruct((t // tm, 1, LANES), F32)],
        scratch_shapes=[pltpu.VMEM((1, LANES), F32)],
        compiler_params=_params("arbitrary"),
        name="rank",
    )(sel, tri)


def _gather_kernel(be_ref, ng_ref, clo_ref, chi_ref, h_ref, post_ref, xb_ref, acc_ref):
    u = pl.program_id(0)
    e = be_ref[u]
    unit_rows = xb_ref.shape[0]
    n_groups = unit_rows // MOE_GATHER_ROWS
    for q in range(n_groups):
        rows = slice(q * MOE_GATHER_ROWS, (q + 1) * MOE_GATHER_ROWS)
        gi = u * n_groups + q

        @pl.when(ng_ref[gi] > 0)
        def _():
            acc_ref[...] = jnp.zeros_like(acc_ref)
            row_pos = (lax.broadcasted_iota(jnp.int32, (MOE_GATHER_ROWS, GATHER_CHUNK), 0)
                       + (u * unit_rows + q * MOE_GATHER_ROWS))

            def body(c, carry):
                off = pl.multiple_of(c * GATHER_CHUNK, GATHER_CHUNK)
                p = post_ref[pl.ds(e, 1), pl.ds(off, GATHER_CHUNK)]
                onehot = jnp.where(row_pos == p, 1.0, 0.0).astype(BF16)
                acc_ref[...] += jnp.dot(onehot, h_ref[pl.ds(off, GATHER_CHUNK), :], preferred_element_type=F32)
                return carry

            lax.fori_loop(clo_ref[gi], chi_ref[gi], body, 0)
            xb_ref[rows, :] = acc_ref[...].astype(BF16)

        @pl.when(ng_ref[gi] == 0)
        def _():
            xb_ref[rows, :] = jnp.zeros((MOE_GATHER_ROWS, xb_ref.shape[1]), BF16)


def _gather(beu, ng, clo, chi, h, post):
    t, d = h.shape
    n_units = beu.shape[0]
    grid_spec = pltpu.PrefetchScalarGridSpec(
        num_scalar_prefetch=4,
        grid=(n_units,),
        in_specs=[
            pl.BlockSpec((t, d), lambda u, *_: (0, 0), pipeline_mode=pl.Buffered(1)),
            pl.BlockSpec(post.shape, lambda u, *_: (0, 0), pipeline_mode=pl.Buffered(1)),
        ],
        out_specs=pl.BlockSpec((MOE_GATHER_UNIT, d), lambda u, *_: (u, 0)),
        scratch_shapes=[pltpu.VMEM((MOE_GATHER_ROWS, d), F32)],
    )
    return pl.pallas_call(
        _gather_kernel,
        grid_spec=grid_spec,
        out_shape=jax.ShapeDtypeStruct((n_units * MOE_GATHER_UNIT, d), BF16),
        compiler_params=_params("arbitrary"),
        name="gather",
    )(beu, ng, clo, chi, h, post)


def _experts_kernel(be_ref, nv_ref, xb_ref, wg_ref, wu_ref, wd_ref, yb_ref, acc_ref, wgb_ref, wub_ref, wdb_ref):
    b = pl.program_id(0)
    f = pl.program_id(1)
    nf = pl.num_programs(1)
    nv = nv_ref[b]
    block_rows = xb_ref.shape[0]

    @pl.when(nv > 0)
    def _():
        wgb_ref[...] = wg_ref[0].astype(BF16)
        wub_ref[...] = wu_ref[0].astype(BF16)
        wdb_ref[...] = wd_ref[0].astype(BF16)
        for g in range(block_rows // MOE_DOT_ROWS):
            rows = slice(g * MOE_DOT_ROWS, (g + 1) * MOE_DOT_ROWS)

            @pl.when((nv > g * MOE_DOT_ROWS) & (f == 0))
            def _():
                acc_ref[rows, :] = jnp.zeros((MOE_DOT_ROWS, acc_ref.shape[1]), F32)

            @pl.when(nv > g * MOE_DOT_ROWS)
            def _():
                x = xb_ref[rows, :]
                a = jnp.dot(x, wgb_ref[...], preferred_element_type=F32)
                u = jnp.dot(x, wub_ref[...], preferred_element_type=F32)
                act = (_silu(a) * u).astype(BF16)
                acc_ref[rows, :] += jnp.dot(act, wdb_ref[...], preferred_element_type=F32)

    @pl.when(f == nf - 1)
    def _():
        for g in range(block_rows // MOE_DOT_ROWS):
            rows = slice(g * MOE_DOT_ROWS, (g + 1) * MOE_DOT_ROWS)

            @pl.when(nv > g * MOE_DOT_ROWS)
            def _():
                yb_ref[rows, :] = acc_ref[rows, :].astype(BF16)

            @pl.when(nv <= g * MOE_DOT_ROWS)
            def _():
                yb_ref[rows, :] = jnp.zeros((MOE_DOT_ROWS, yb_ref.shape[1]), BF16)


def _experts(be, nv, xb, wg, wu, wd):
    d = xb.shape[1]
    n_blocks = be.shape[0]
    d_ff = wg.shape[2]
    tf = MOE_FF_TILE
    nf = d_ff // tf
    bsz = MOE_BLOCK

    def ff_idx(f, nvr, b):
        return jnp.where(nvr[b] > 0, f, nf - 1)

    grid_spec = pltpu.PrefetchScalarGridSpec(
        num_scalar_prefetch=2,
        grid=(n_blocks, nf),
        in_specs=[
            pl.BlockSpec((bsz, d), lambda b, f, *_: (b, 0)),
            pl.BlockSpec((1, d, tf), lambda b, f, be_r, nv_r: (be_r[b], 0, ff_idx(f, nv_r, b))),
            pl.BlockSpec((1, d, tf), lambda b, f, be_r, nv_r: (be_r[b], 0, ff_idx(f, nv_r, b))),
            pl.BlockSpec((1, tf, d), lambda b, f, be_r, nv_r: (be_r[b], ff_idx(f, nv_r, b), 0)),
        ],
        out_specs=pl.BlockSpec((bsz, d), lambda b, f, *_: (b, 0)),
        scratch_shapes=[
            pltpu.VMEM((bsz, d), F32),
            pltpu.VMEM((d, tf), BF16), pltpu.VMEM((d, tf), BF16), pltpu.VMEM((tf, d), BF16),
        ],
    )
    return pl.pallas_call(
        _experts_kernel,
        grid_spec=grid_spec,
        out_shape=jax.ShapeDtypeStruct((n_blocks * bsz, d), BF16),
        compiler_params=_params("arbitrary", "arbitrary"),
        name="experts",
    )(be, nv, xb, wg, wu, wd)


def _combine_kernel(win_ref, x_ref, pos_ref, gates_ref, g_ref, fg_ref, *refs):
    yb_refs, o_ref = refs[:-1], refs[-1]
    i = pl.program_id(0)
    n_sub = x_ref.shape[0] // COMBINE_ROWS
    lane = lax.broadcasted_iota(jnp.int32, (COMBINE_ROWS, 2 * COMBINE_ROWS), 1)
    for s in range(n_sub):
        rows = slice(s * COMBINE_ROWS, (s + 1) * COMBINE_ROWS)
        acc = jnp.zeros((COMBINE_ROWS, x_ref.shape[1]), F32)
        for e in range(N_EXPERTS):
            k = (i * n_sub + s) * N_EXPERTS + e
            base = win_ref[k] * COMBINE_ROWS
            rel = pos_ref[rows, e:e + 1] - base
            onehot = jnp.where(lane == rel, 1.0, 0.0).astype(BF16)
            w0 = 2 * (s * N_EXPERTS + e)
            window = jnp.concatenate([yb_refs[w0][...], yb_refs[w0 + 1][...]], axis=0)
            picked = jnp.dot(onehot, window, preferred_element_type=F32)
            acc = acc + gates_ref[rows, e:e + 1] * picked
        xo = x_ref[rows, :] + g_ref[...] * acc
        ms = jnp.mean(xo * xo, axis=-1, keepdims=True)
        o_ref[rows, :] = xo * lax.rsqrt(ms + EPS) * fg_ref[...]


def _combine(win, x, pos, gates, g2, fgain, yb):
    t, d = x.shape
    tm = COMBINE_TILE
    n_sub = tm // COMBINE_ROWS
    n_win = yb.shape[0] // COMBINE_ROWS
    vec = pl.BlockSpec((1, d), lambda i, w: (0, 0))

    def first(s, e):
        return lambda i, w: (jnp.minimum(w[(i * n_sub + s) * N_EXPERTS + e], n_win - 1), 0)

    def second(s, e):
        return lambda i, w: (jnp.minimum(w[(i * n_sub + s) * N_EXPERTS + e] + 1, n_win - 1), 0)

    yb_specs = []
    for s in range(n_sub):
        for e in range(N_EXPERTS):
            yb_specs.append(pl.BlockSpec((COMBINE_ROWS, d), first(s, e)))
            yb_specs.append(pl.BlockSpec((COMBINE_ROWS, d), second(s, e)))
    grid_spec = pltpu.PrefetchScalarGridSpec(
        num_scalar_prefetch=1,
        grid=(t // tm,),
        in_specs=[
            pl.BlockSpec((tm, d), lambda i, w: (i, 0)),
            pl.BlockSpec((tm, N_EXPERTS), lambda i, w: (i, 0)),
            pl.BlockSpec((tm, LANES), lambda i, w: (i, 0)),
            vec, vec,
        ] + yb_specs,
        out_specs=pl.BlockSpec((tm, d), lambda i, w: (i, 0)),
    )
    return pl.pallas_call(
        _combine_kernel,
        grid_spec=grid_spec,
        out_shape=jax.ShapeDtypeStruct((t, d), F32),
        compiler_params=_params("arbitrary"),
        name="combine",
    )(win, x, pos, gates, g2, fgain, *([yb] * len(yb_specs)))


def _routing_tables(rank, tot, sel):
    t = rank.shape[0]
    bsz = MOE_BLOCK
    n_blocks = (2 * t) // bsz + N_EXPERTS
    cum = tot[:, 0, :N_EXPERTS].astype(jnp.int32)
    counts = cum[-1]
    padded = ((counts + bsz - 1) // bsz) * bsz
    pad_end = jnp.cumsum(padded)
    start_pad = pad_end - padded
    selected = sel[:, :N_EXPERTS] > 0
    pos = jnp.where(selected, start_pad[None, :] + rank[:, :N_EXPERTS].astype(jnp.int32), -1)

    bstart = jnp.arange(n_blocks, dtype=jnp.int32) * bsz
    be = jnp.minimum(jnp.sum(bstart[:, None] >= pad_end[None, :], axis=1), N_EXPERTS - 1).astype(jnp.int32)
    r0 = bstart - start_pad[be]
    nv = jnp.clip(counts[be] - r0, 0, bsz).astype(jnp.int32)
    n_gather = bsz // MOE_GATHER_ROWS
    goff = jnp.arange(n_gather, dtype=jnp.int32) * MOE_GATHER_ROWS
    r0g = (r0[:, None] + goff[None, :]).reshape(-1)
    ng = jnp.clip(nv[:, None] - goff[None, :], 0, MOE_GATHER_ROWS).reshape(-1)
    beg = jnp.repeat(be, n_gather)
    chunk_end = cum[:, beg].T
    chunk_start = jnp.concatenate([jnp.zeros((1, N_EXPERTS), jnp.int32), cum[:-1]], axis=0)[:, beg].T
    clo = jnp.sum(chunk_end <= r0g[:, None], axis=1).astype(jnp.int32)
    chi = jnp.sum(chunk_start < (r0g + ng)[:, None], axis=1).astype(jnp.int32)
    beu = jnp.repeat(be, bsz // MOE_GATHER_UNIT)

    big = jnp.int32(2 ** 30)
    first = jnp.min(jnp.where(pos >= 0, pos, big).reshape(t // COMBINE_ROWS, COMBINE_ROWS, N_EXPERTS), axis=1)
    win = jnp.where(first == big, 0, first // COMBINE_ROWS).astype(jnp.int32).reshape(-1)
    return pos, be, nv, beu, ng, clo, chi, win


def _rotary_tables(seq, dk):
    half = dk // 2
    inv = ROPE_BASE ** (-jnp.arange(half, dtype=F32) / half)
    ang = jnp.arange(seq).astype(F32)[:, None] * inv[None, :]
    return jnp.cos(ang), jnp.sin(ang)


def _decay_tables(dk, dv):
    c = RET_CHUNK
    log_gamma = jnp.log1p(-jnp.exp2(-5.0 - jnp.arange(N_HEADS, dtype=F32)))
    idx = jnp.arange(c, dtype=F32)
    row_dec = jnp.exp(log_gamma[:, None] * (idx + 1.0 - c))
    causal = (idx[:, None] >= idx[None, :]).astype(F32)
    dmask = row_dec[:, :, None] * causal[None]
    k_dec = jnp.exp(log_gamma[:, None] * (c - 1.0 - idx)) * (dk ** -0.5)
    q_dec = jnp.exp(log_gamma[:, None] * (idx + 1.0))
    chunk_dec = jnp.exp(log_gamma * c)
    half = dk // 2
    kdec_tile = jnp.tile(jnp.repeat(k_dec.T, half, axis=1), (ROW_TILE // c, 1))
    qdec_tile = jnp.broadcast_to(q_dec[:, :, None], (N_HEADS, c, dv))
    return dmask, kdec_tile, qdec_tile, chunk_dec


def _pool_bands(tm):
    def bands(lag):
        return jnp.stack([((lag >= 0) & (lag < w)) for w in POOL_WINDOWS]).astype(BF16)
    main = bands(jnp.arange(tm)[:, None] - jnp.arange(tm)[None, :])
    head = bands(jnp.arange(POOL_HALO)[:, None] + POOL_HALO - jnp.arange(POOL_HALO)[None, :])
    return main, head


def kernel(x, c, ada_w, ada_b, norm_gain, ret_w_in, ret_gn_gain, ret_w_out, ffn_w_gate, ffn_w_up, ffn_w_down,
           pool_w, pool_b, pool_scale, moe_router, moe_w_gate, moe_w_up, moe_w_down, final_norm_gain):
    bsz, seq, d = x.shape
    assert bsz == 1 and ada_w.shape[0] == 2
    xt = x.reshape(seq, d)
    qk_total = d
    v_total = ret_w_out.shape[1]
    dk = qk_total // N_HEADS
    dv = v_total // N_HEADS

    mod = _modulation(c, ada_w, ada_b)
    def mods(i):
        parts = [mod[i, :, k * d:(k + 1) * d] for k in range(N_MOD)]
        sh1, sc1, g1, sh2, sc2, g2 = parts
        return sh1, 1.0 + sc1, g1, sh2, 1.0 + sc2, g2

    sh1, sc1p, g1, sh2, sc2p, g2 = mods(0)
    cos, sin = _rotary_tables(seq, dk)
    dmask, kdec_tile, qdec_tile, chunk_dec = _decay_tables(dk, dv)
    proj = _ret_proj(xt, norm_gain[0, 0][None], sc1p, sh1, ret_w_in[0].astype(BF16), cos, sin, kdec_tile,
                     qk_total, v_total)
    y = _retention_core(proj, chunk_dec, dmask, qdec_tile, ret_gn_gain[0][None], qk_total, v_total)
    x1 = _ret_out(y, ret_w_out[0].astype(BF16), xt, g1)
    x2 = _dense_ffn(x1, norm_gain[0, 1][None], sc2p, sh2, g2,
                    ffn_w_gate[0].astype(BF16), ffn_w_up[0].astype(BF16), ffn_w_down[0].astype(BF16))

    sh1, sc1p, g1, sh2, sc2p, g2 = mods(1)
    band, bandh = _pool_bands(POOL_SUB_ROWS)
    x3 = _pool_mixer(x2, norm_gain[1, 0][None], sc1p, sh1, g1, band, bandh,
                     pool_w[0].astype(BF16), pool_b[0].reshape(1, d), pool_scale[0][None])

    wr = jnp.pad(moe_router[0], ((0, 0), (0, LANES - N_EXPERTS)))
    wr_hi = wr.astype(BF16)
    wr_lo = (wr - wr_hi.astype(F32)).astype(BF16)
    h4, gates, sel = _router(x3, norm_gain[1, 1][None], sc2p, sh2, wr_hi, wr_lo)
    tri = (jnp.arange(RANK_ROWS)[:, None] >= jnp.arange(RANK_ROWS)[None, :]).astype(BF16)
    rank, tot = _ranks(sel, tri)
    pos, be, nv, beu, ng, clo, chi, win = _routing_tables(rank, tot, sel)
    xb = _gather(beu, ng, clo, chi, h4, pos.T)
    yb = _experts(be, nv, xb, moe_w_gate[0], moe_w_up[0], moe_w_down[0])
    out = _combine(win, x3, pos, gates, g2, final_norm_gain[None], yb)
    return out.reshape(bsz, seq, d)
```

```python
import functools

import jax
import jax.numpy as jnp
from jax import lax
from jax.experimental import pallas as pl
from jax.experimental.pallas import tpu as pltpu

F32 = jnp.float32
BF16 = jnp.bfloat16

EPS = 1e-6
N_HEADS = 4
RET_CHUNK = 256
ROPE_BASE = 10000.0
ROT_BLOCK = 128
POOL_WINDOWS = (2, 4, 8, 16)
POOL_HALO = 16
N_EXPERTS = 8
N_MOD = 6

VMEM_LIMIT_BYTES = 56 * 1024 * 1024

ROW_TILE = 1024
RET_ROWS = 512
POOL_ROWS = 512
POOL_SUB_ROWS = 256
RANK_ROWS = 512
MOE_BLOCK = 2048
MOE_DOT_ROWS = 512
MOE_GATHER_UNIT = 1024
MOE_GATHER_ROWS = 256
MOE_FF_TILE = 512
GATHER_CHUNK = 512
COMBINE_TILE = 256
COMBINE_ROWS = 128
LANES = 128


def _silu(v):
    return v / (1.0 + jnp.exp(-v))


def _norm_mod(x, gain, scale1p, shift):
    ms = jnp.mean(x * x, axis=-1, keepdims=True)
    return (x * lax.rsqrt(ms + EPS) * gain) * scale1p + shift


def _params(*sem):
    return pltpu.CompilerParams(dimension_semantics=sem, vmem_limit_bytes=VMEM_LIMIT_BYTES)


def _resident(shape):
    nd = len(shape)
    return pl.BlockSpec(shape, lambda *_: (0,) * nd, pipeline_mode=pl.Buffered(1))


def _mod_kernel(c_ref, w_ref, b_ref, o_ref):
    ca = _silu(c_ref[...])
    o_ref[0] = jnp.sum(ca * w_ref[0], axis=0, keepdims=True) + b_ref[0]


def _modulation(c, ada_w, ada_b):
    depth, d, n = ada_w.shape
    tn = 1024
    return pl.pallas_call(
        _mod_kernel,
        grid=(depth, n // tn),
        in_specs=[
            pl.BlockSpec((d, 1), lambda i, j: (0, 0)),
            pl.BlockSpec((1, d, tn), lambda i, j: (i, 0, j)),
            pl.BlockSpec((1, 1, tn), lambda i, j: (i, 0, j)),
        ],
        out_specs=pl.BlockSpec((1, 1, tn), lambda i, j: (i, 0, j)),
        out_shape=jax.ShapeDtypeStruct((depth, 1, n), F32),
        compiler_params=_params("arbitrary", "arbitrary"),
        name="mod",
    )(c.reshape(d, 1), ada_w, ada_b.reshape(depth, 1, n))


def _proj_kernel(x_ref, gain_ref, sc_ref, sh_ref, w_ref, cosa_ref, sina_ref, cosb_ref, sinb_ref, kdec_ref, o_ref,
                 h_ref, cos_ref, sin_ref, *, n_qk_tiles, n_v_tiles, dk):
    j = pl.program_id(1)

    @pl.when(j == 0)
    def _():
        h_ref[...] = _norm_mod(x_ref[...], gain_ref[...], sc_ref[...], sh_ref[...]).astype(BF16)
        cb = cosb_ref[...]
        sb = sinb_ref[...]
        for a in range(cosa_ref.shape[0]):
            ca = cosa_ref[a:a + 1, :]
            sa = sina_ref[a:a + 1, :]
            cos_ref[a * ROT_BLOCK:(a + 1) * ROT_BLOCK, :] = ca * cb - sa * sb
            sin_ref[a * ROT_BLOCK:(a + 1) * ROT_BLOCK, :] = sa * cb + ca * sb

    half = dk // 2
    n_chunks = w_ref.shape[1] // dk

    def chunk_dot(c):
        return jnp.dot(h_ref[...], w_ref[:, c * dk:(c + 1) * dk], preferred_element_type=F32)

    def rotary(scale_ref):
        cos = cos_ref[...]
        sin = sin_ref[...]
        for c in range(n_chunks):
            acc = chunk_dot(c)
            t1 = acc[:, :half]
            t2 = acc[:, half:]
            o1 = t1 * cos - t2 * sin
            o2 = t1 * sin + t2 * cos
            if scale_ref is not None:
                s = scale_ref[:, c * half:(c + 1) * half]
                o1 = o1 * s
                o2 = o2 * s
            o_ref[:, c * dk:c * dk + half] = o1.astype(BF16)
            o_ref[:, c * dk + half:(c + 1) * dk] = o2.astype(BF16)

    @pl.when(j < n_qk_tiles)
    def _():
        rotary(None)

    @pl.when((j >= n_qk_tiles) & (j < 2 * n_qk_tiles))
    def _():
        rotary(kdec_ref)

    @pl.when((j >= 2 * n_qk_tiles) & (j < 2 * n_qk_tiles + n_v_tiles))
    def _():
        for c in range(n_chunks):
            o_ref[:, c * dk:(c + 1) * dk] = chunk_dot(c).astype(BF16)

    @pl.when(j >= 2 * n_qk_tiles + n_v_tiles)
    def _():
        for c in range(n_chunks):
            o_ref[:, c * dk:(c + 1) * dk] = _silu(chunk_dot(c)).astype(BF16)


def _ret_proj(x, gain, sc1p, sh, w_in, rot, kdec, qk_total, v_total):
    t, d = x.shape
    n = w_in.shape[1]
    tm, tn = ROW_TILE, 1024
    dk = qk_total // N_HEADS
    half = dk // 2
    cosa, sina, cosb, sinb = rot
    kern = functools.partial(_proj_kernel, n_qk_tiles=qk_total // tn, n_v_tiles=v_total // tn, dk=dk)
    vec = pl.BlockSpec((1, d), lambda i, j: (0, 0))
    rot_a = pl.BlockSpec((tm // ROT_BLOCK, half), lambda i, j: (i, 0))
    rot_b = pl.BlockSpec((ROT_BLOCK, half), lambda i, j: (0, 0))
    return pl.pallas_call(
        kern,
        grid=(t // tm, n // tn),
        in_specs=[
            pl.BlockSpec((tm, d), lambda i, j: (i, 0)),
            vec, vec, vec,
            pl.BlockSpec((d, tn), lambda i, j: (0, j)),
            rot_a, rot_a, rot_b, rot_b,
            pl.BlockSpec((tm, N_HEADS * half), lambda i, j: (0, 0)),
        ],
        out_specs=pl.BlockSpec((tm, tn), lambda i, j: (i, j)),
        out_shape=jax.ShapeDtypeStruct((t, n), BF16),
        scratch_shapes=[pltpu.VMEM((tm, d), BF16), pltpu.VMEM((tm, half), F32), pltpu.VMEM((tm, half), F32)],
        compiler_params=_params("arbitrary", "arbitrary"),
        name="proj",
    )(x, gain, sc1p, sh, w_in, cosa, sina, cosb, sinb, kdec)


def _ret_kernel(cdec_ref, q_ref, k_ref, v_ref, sg_ref, dmask_ref, qdec_ref, gn_ref, y_ref, state_ref,
                *, dk, dv):
    @pl.when(pl.program_id(0) == 0)
    def _():
        state_ref[...] = jnp.zeros_like(state_ref)

    rows_per_step = q_ref.shape[0]
    for n in range(rows_per_step // RET_CHUNK):
        rows = slice(n * RET_CHUNK, (n + 1) * RET_CHUNK)
        for h in range(N_HEADS):
            qcols = slice(h * dk, (h + 1) * dk)
            vcols = slice(h * dv, (h + 1) * dv)
            q = q_ref[rows, qcols]
            ks = k_ref[rows, qcols]
            v = v_ref[rows, vcols]
            s = lax.dot_general(q, ks, (((1,), (1,)), ((), ())), preferred_element_type=F32)
            a = (s * dmask_ref[h]).astype(BF16)
            intra = jnp.dot(a, v, preferred_element_type=F32)
            st = state_ref[h]
            cross = jnp.dot(q, st.astype(BF16), preferred_element_type=F32)
            o = intra + cross * qdec_ref[h]
            kv = lax.dot_general(ks, v, (((0,), (0,)), ((), ())), preferred_element_type=F32)
            state_ref[h] = st * cdec_ref[h] + kv
            mu = jnp.mean(o, axis=-1, keepdims=True)
            dlt = o - mu
            var = jnp.mean(dlt * dlt, axis=-1, keepdims=True)
            on = dlt * lax.rsqrt(var + EPS)
            y = on * gn_ref[:, vcols] * sg_ref[rows, vcols].astype(F32)
            y_ref[rows, vcols] = y.astype(BF16)


def _retention_core(proj, cdec, dmask, qdec, gn_gain, qk_total, v_total):
    t = proj.shape[0]
    r = RET_ROWS
    dk = qk_total // N_HEADS
    dv = v_total // N_HEADS
    assert v_total == 2 * qk_total
    kern = functools.partial(_ret_kernel, dk=dk, dv=dv)
    return pl.pallas_call(
        kern,
        grid=(t // r,),
        in_specs=[
            pl.BlockSpec(memory_space=pltpu.SMEM),
            pl.BlockSpec((r, qk_total), lambda i: (i, 0)),
            pl.BlockSpec((r, qk_total), lambda i: (i, 1)),
            pl.BlockSpec((r, v_total), lambda i: (i, 1)),
            pl.BlockSpec((r, v_total), lambda i: (i, 2)),
            pl.BlockSpec((N_HEADS, RET_CHUNK, RET_CHUNK), lambda i: (0, 0, 0)),
            pl.BlockSpec((N_HEADS, RET_CHUNK, dv), lambda i: (0, 0, 0)),
            pl.BlockSpec((1, v_total), lambda i: (0, 0)),
        ],
        out_specs=pl.BlockSpec((r, v_total), lambda i: (i, 0)),
        out_shape=jax.ShapeDtypeStruct((t, v_total), BF16),
        scratch_shapes=[pltpu.VMEM((N_HEADS, dk, dv), F32)],
        compiler_params=_params("arbitrary"),
        name="ret",
    )(cdec, proj, proj, proj, proj, dmask, qdec, gn_gain)


def _out_kernel(y_ref, w_ref, x_ref, g_ref, o_ref):
    m = jnp.dot(y_ref[...], w_ref[...], preferred_element_type=F32)
    o_ref[...] = x_ref[...] + g_ref[...] * m


def _ret_out(y, w_out, x, g1):
    t, d = x.shape
    kdim = y.shape[1]
    tm = ROW_TILE
    return pl.pallas_call(
        _out_kernel,
        grid=(t // tm,),
        in_specs=[
            pl.BlockSpec((tm, kdim), lambda i: (i, 0)),
            _resident((kdim, d)),
            pl.BlockSpec((tm, d), lambda i: (i, 0)),
            pl.BlockSpec((1, d), lambda i: (0, 0)),
        ],
        out_specs=pl.BlockSpec((tm, d), lambda i: (i, 0)),
        out_shape=jax.ShapeDtypeStruct((t, d), F32),
        compiler_params=_params("arbitrary"),
        name="out",
    )(y, w_out, x, g1)


def _ffn_kernel(x_ref, gain_ref, sc_ref, sh_ref, g_ref, wg_ref, wu_ref, wd_ref, o_ref, *, ff_tile):
    x = x_ref[...]
    h = _norm_mod(x, gain_ref[...], sc_ref[...], sh_ref[...]).astype(BF16)
    d_ff = wg_ref.shape[1]
    acc = jnp.zeros(x.shape, F32)
    for lo in range(0, d_ff, ff_tile):
        hi = min(lo + ff_tile, d_ff)
        a = jnp.dot(h, wg_ref[:, lo:hi], preferred_element_type=F32)
        b = jnp.dot(h, wu_ref[:, lo:hi], preferred_element_type=F32)
        act = (_silu(a) * b).astype(BF16)
        acc = acc + jnp.dot(act, wd_ref[lo:hi, :], preferred_element_type=F32)
    o_ref[...] = x + g_ref[...] * acc


def _dense_ffn(x, gain, sc1p, sh, g2, wg, wu, wd):
    t, d = x.shape
    d_ff = wg.shape[1]
    tm = ROW_TILE
    vec = pl.BlockSpec((1, d), lambda i: (0, 0))
    return pl.pallas_call(
        functools.partial(_ffn_kernel, ff_tile=512),
        grid=(t // tm,),
        in_specs=[
            pl.BlockSpec((tm, d), lambda i: (i, 0)),
            vec, vec, vec, vec,
            _resident((d, d_ff)), _resident((d, d_ff)), _resident((d_ff, d)),
        ],
        out_specs=pl.BlockSpec((tm, d), lambda i: (i, 0)),
        out_shape=jax.ShapeDtypeStruct((t, d), F32),
        compiler_params=_params("arbitrary"),
        name="ffn",
    )(x, gain, sc1p, sh, g2, wg, wu, wd)


def _pool_kernel(x_ref, gain_ref, sc_ref, sh_ref, g_ref, band_ref, bandh_ref, wp_ref, bp_ref, ps_ref, o_ref,
                 halo_ref):
    i = pl.program_id(0)
    tm = x_ref.shape[0]
    gw = wp_ref.shape[1]

    @pl.when(i == 0)
    def _():
        halo_ref[...] = jnp.zeros_like(halo_ref)

    def split(v):
        hi = v.astype(BF16)
        return hi, (v - hi.astype(F32)).astype(BF16)

    sub = band_ref.shape[1]
    for s in range(tm // sub):
        rows = slice(s * sub, (s + 1) * sub)
        x = x_ref[rows, :]
        h = _norm_mod(x, gain_ref[...], sc_ref[...], sh_ref[...])
        h_hi, h_lo = split(h)
        halo_hi, halo_lo = split(halo_ref[...])
        halo_ref[...] = h[sub - POOL_HALO:, :]
        t1 = (lax.broadcasted_iota(jnp.int32, (sub, 1), 0) + (i * tm + s * sub + 1)).astype(F32)
        for g, w in enumerate(POOL_WINDOWS):
            cols = slice(g * gw, (g + 1) * gw)
            band = band_ref[g]
            bandh = bandh_ref[g]
            win = (jnp.dot(band, h_hi[:, cols], preferred_element_type=F32)
                   + jnp.dot(band, h_lo[:, cols], preferred_element_type=F32))
            head = (jnp.dot(bandh, halo_hi[:, cols], preferred_element_type=F32)
                    + jnp.dot(bandh, halo_lo[:, cols], preferred_element_type=F32))
            win = jnp.concatenate([win[:POOL_HALO] + head, win[POOL_HALO:]], axis=0)
            pooled = win / jnp.minimum(t1, float(w)) - h[:, cols]
            y = jnp.dot(pooled.astype(BF16), wp_ref[g], preferred_element_type=F32) + bp_ref[:, cols]
            o_ref[rows, cols] = x[:, cols] + g_ref[:, cols] * (y * ps_ref[:, cols])


def _pool_mixer(x, gain, sc1p, sh, g1, band, bandh, wp, bp, ps):
    t, d = x.shape
    tm = POOL_ROWS
    vec = pl.BlockSpec((1, d), lambda i: (0, 0))
    return pl.pallas_call(
        _pool_kernel,
        grid=(t // tm,),
        in_specs=[
            pl.BlockSpec((tm, d), lambda i: (i, 0)),
            vec, vec, vec, vec,
            pl.BlockSpec(band.shape, lambda i: (0, 0, 0)),
            pl.BlockSpec(bandh.shape, lambda i: (0, 0, 0)),
            pl.BlockSpec(wp.shape, lambda i: (0, 0, 0)),
            vec, vec,
        ],
        out_specs=pl.BlockSpec((tm, d), lambda i: (i, 0)),
        out_shape=jax.ShapeDtypeStruct((t, d), F32),
        scratch_shapes=[pltpu.VMEM((POOL_HALO, d), F32)],
        compiler_params=_params("arbitrary"),
        name="pool",
    )(x, gain, sc1p, sh, g1, band, bandh, wp, bp, ps)


def _router_kernel(x_ref, gain_ref, sc_ref, sh_ref, whi_ref, wlo_ref, h_ref, gates_ref, sel_ref):
    h = _norm_mod(x_ref[...], gain_ref[...], sc_ref[...], sh_ref[...])
    h_hi = h.astype(BF16)
    h_ref[...] = h_hi
    h_lo = (h - h_hi.astype(F32)).astype(BF16)
    w_hi = whi_ref[...]
    logits = (jnp.dot(h_hi, w_hi, preferred_element_type=F32)
              + jnp.dot(h_lo, w_hi, preferred_element_type=F32)
              + jnp.dot(h_hi, wlo_ref[...], preferred_element_type=F32))
    lane = lax.broadcasted_iota(jnp.int32, logits.shape, 1).astype(F32)
    neg = jnp.float32(-jnp.inf)
    lg = jnp.where(lane < float(N_EXPERTS), logits, neg)
    m0 = jnp.max(lg, axis=-1, keepdims=True)
    i0 = jnp.min(jnp.where(lg == m0, lane, float(LANES)), axis=-1, keepdims=True)
    lg1 = jnp.where(lane == i0, neg, lg)
    m1 = jnp.max(lg1, axis=-1, keepdims=True)
    i1 = jnp.min(jnp.where(lg1 == m1, lane, float(LANES)), axis=-1, keepdims=True)
    e1 = jnp.exp(m1 - m0)
    den = 1.0 + e1
    is0 = lane == i0
    is1 = lane == i1
    gates_ref[...] = jnp.where(is0, 1.0 / den, jnp.where(is1, e1 / den, 0.0))
    sel_ref[...] = jnp.where(is0 | is1, 1.0, 0.0).astype(BF16)


def _router(x, gain, sc1p, sh, w_hi, w_lo):
    t, d = x.shape
    tm = ROW_TILE
    vec = pl.BlockSpec((1, d), lambda i: (0, 0))
    wspec = pl.BlockSpec((d, LANES), lambda i: (0, 0))
    return pl.pallas_call(
        _router_kernel,
        grid=(t // tm,),
        in_specs=[pl.BlockSpec((tm, d), lambda i: (i, 0)), vec, vec, vec, wspec, wspec],
        out_specs=[
            pl.BlockSpec((tm, d), lambda i: (i, 0)),
            pl.BlockSpec((tm, LANES), lambda i: (i, 0)),
            pl.BlockSpec((tm, LANES), lambda i: (i, 0)),
        ],
        out_shape=[
            jax.ShapeDtypeStruct((t, d), BF16),
            jax.ShapeDtypeStruct((t, LANES), F32),
            jax.ShapeDtypeStruct((t, LANES), BF16),
        ],
        compiler_params=_params("arbitrary"),
        name="router",
    )(x, gain, sc1p, sh, w_hi, w_lo)


def _rank_kernel(sel_ref, tri_ref, rank_ref, tot_ref, carry_ref):
    @pl.when(pl.program_id(0) == 0)
    def _():
        carry_ref[...] = jnp.zeros_like(carry_ref)

    s = sel_ref[...]
    cum = jnp.dot(tri_ref[...], s, preferred_element_type=F32)
    carry = carry_ref[...]
    rank_ref[...] = cum - s.astype(F32) + carry
    tot = carry + cum[cum.shape[0] - 1:, :]
    carry_ref[...] = tot
    tot_ref[0] = tot


def _ranks(sel, tri):
    t = sel.shape[0]
    tm = RANK_ROWS
    return pl.pallas_call(
        _rank_kernel,
        grid=(t // tm,),
        in_specs=[pl.BlockSpec((tm, LANES), lambda i: (i, 0)), pl.BlockSpec((tm, tm), lambda i: (0, 0))],
        out_specs=[pl.BlockSpec((tm, LANES), lambda i: (i, 0)), pl.BlockSpec((1, 1, LANES), lambda i: (i, 0, 0))],
        out_shape=[jax.ShapeDtypeStruct((t, LANES), F32), jax.ShapeDtypeStruct((t // tm, 1, LANES), F32)],
        scratch_shapes=[pltpu.VMEM((1, LANES), F32)],
        compiler_params=_params("arbitrary"),
        name="rank",
    )(sel, tri)


def _gather_kernel(be_ref, ng_ref, clo_ref, chi_ref, h_ref, post_ref, gatest_ref, xb_ref, gs_ref, acc_ref):
    u = pl.program_id(0)
    e = be_ref[u]
    unit_rows = xb_ref.shape[0]
    n_groups = unit_rows // MOE_GATHER_ROWS
    for q in range(n_groups):
        rows = slice(q * MOE_GATHER_ROWS, (q + 1) * MOE_GATHER_ROWS)
        gi = u * n_groups + q

        @pl.when(ng_ref[gi] > 0)
        def _():
            acc_ref[...] = jnp.zeros_like(acc_ref)
            row_pos = (lax.broadcasted_iota(jnp.int32, (MOE_GATHER_ROWS, GATHER_CHUNK), 0)
                       + (u * unit_rows + q * MOE_GATHER_ROWS))

            def body(c, gsum):
                off = pl.multiple_of(c * GATHER_CHUNK, GATHER_CHUNK)
                p = post_ref[pl.ds(e, 1), pl.ds(off, GATHER_CHUNK)]
                match = row_pos == p
                onehot = jnp.where(match, 1.0, 0.0).astype(BF16)
                acc_ref[...] += jnp.dot(onehot, h_ref[pl.ds(off, GATHER_CHUNK), :], preferred_element_type=F32)
                gate = gatest_ref[pl.ds(e, 1), pl.ds(off, GATHER_CHUNK)]
                return gsum + jnp.sum(jnp.where(match, gate, 0.0), axis=1, keepdims=True)

            gsum = lax.fori_loop(clo_ref[gi], chi_ref[gi], body, jnp.zeros((MOE_GATHER_ROWS, 1), F32))
            xb_ref[rows, :] = acc_ref[...].astype(BF16)
            gs_ref[rows, :] = gsum

        @pl.when(ng_ref[gi] == 0)
        def _():
            xb_ref[rows, :] = jnp.zeros((MOE_GATHER_ROWS, xb_ref.shape[1]), BF16)
            gs_ref[rows, :] = jnp.zeros((MOE_GATHER_ROWS, 1), F32)


def _gather(beu, ng, clo, chi, h, post, gatest):
    t, d = h.shape
    n_units = beu.shape[0]
    grid_spec = pltpu.PrefetchScalarGridSpec(
        num_scalar_prefetch=4,
        grid=(n_units,),
        in_specs=[
            pl.BlockSpec((t, d), lambda u, *_: (0, 0), pipeline_mode=pl.Buffered(1)),
            pl.BlockSpec(post.shape, lambda u, *_: (0, 0), pipeline_mode=pl.Buffered(1)),
            pl.BlockSpec(gatest.shape, lambda u, *_: (0, 0), pipeline_mode=pl.Buffered(1)),
        ],
        out_specs=[
            pl.BlockSpec((MOE_GATHER_UNIT, d), lambda u, *_: (u, 0)),
            pl.BlockSpec((MOE_GATHER_UNIT, 1), lambda u, *_: (u, 0)),
        ],
        scratch_shapes=[pltpu.VMEM((MOE_GATHER_ROWS, d), F32)],
    )
    return pl.pallas_call(
        _gather_kernel,
        grid_spec=grid_spec,
        out_shape=[
            jax.ShapeDtypeStruct((n_units * MOE_GATHER_UNIT, d), BF16),
            jax.ShapeDtypeStruct((n_units * MOE_GATHER_UNIT, 1), F32),
        ],
        compiler_params=_params("arbitrary"),
        name="gather",
    )(beu, ng, clo, chi, h, post, gatest)


def _experts_kernel(be_ref, nv_ref, xb_ref, gs_ref, wg_ref, wu_ref, wd_ref, yb_ref,
                    acc_ref, wgb_ref, wub_ref, wdb_ref):
    b = pl.program_id(0)
    f = pl.program_id(1)
    nf = pl.num_programs(1)
    nv = nv_ref[b]
    block_rows = xb_ref.shape[0]

    @pl.when((b == 0) & (f == 0))
    def _():
        acc_ref[...] = jnp.zeros_like(acc_ref)

    for g in range(block_rows // MOE_DOT_ROWS):
        rows = slice(g * MOE_DOT_ROWS, (g + 1) * MOE_DOT_ROWS)

        @pl.when(nv > g * MOE_DOT_ROWS)
        def _():
            if g == 0:
                wgb_ref[...] = wg_ref[0].astype(BF16)
                wub_ref[...] = wu_ref[0].astype(BF16)
                wdb_ref[...] = wd_ref[0].astype(BF16)
            x = xb_ref[rows, :]
            a = jnp.dot(x, wgb_ref[...], preferred_element_type=F32)
            u = jnp.dot(x, wub_ref[...], preferred_element_type=F32)
            act = (_silu(a) * u).astype(BF16)
            prev = jnp.where(f > 0, acc_ref[rows, :], 0.0)
            acc_ref[rows, :] = prev + jnp.dot(act, wdb_ref[...], preferred_element_type=F32)

    @pl.when(f == nf - 1)
    def _():
        for g in range(block_rows // MOE_DOT_ROWS):
            rows = slice(g * MOE_DOT_ROWS, (g + 1) * MOE_DOT_ROWS)

            @pl.when(nv > g * MOE_DOT_ROWS)
            def _():
                yb_ref[rows, :] = (acc_ref[rows, :] * gs_ref[rows, :]).astype(BF16)

            @pl.when(nv <= g * MOE_DOT_ROWS)
            def _():
                yb_ref[rows, :] = jnp.zeros((MOE_DOT_ROWS, yb_ref.shape[1]), BF16)


def _experts(be, nv, xb, gs, wg, wu, wd):
    d = xb.shape[1]
    n_blocks = be.shape[0]
    d_ff = wg.shape[2]
    tf = MOE_FF_TILE
    nf = d_ff // tf
    bsz = MOE_BLOCK

    def ff_idx(f, nvr, b):
        return jnp.where(nvr[b] > 0, f, nf - 1)

    grid_spec = pltpu.PrefetchScalarGridSpec(
        num_scalar_prefetch=2,
        grid=(n_blocks, nf),
        in_specs=[
            pl.BlockSpec((bsz, d), lambda b, f, *_: (b, 0)),
            pl.BlockSpec((bsz, 1), lambda b, f, *_: (b, 0)),
            pl.BlockSpec((1, d, tf), lambda b, f, be_r, nv_r: (be_r[b], 0, ff_idx(f, nv_r, b))),
            pl.BlockSpec((1, d, tf), lambda b, f, be_r, nv_r: (be_r[b], 0, ff_idx(f, nv_r, b))),
            pl.BlockSpec((1, tf, d), lambda b, f, be_r, nv_r: (be_r[b], ff_idx(f, nv_r, b), 0)),
        ],
        out_specs=pl.BlockSpec((bsz, d), lambda b, f, *_: (b, 0)),
        scratch_shapes=[
            pltpu.VMEM((bsz, d), F32),
            pltpu.VMEM((d, tf), BF16), pltpu.VMEM((d, tf), BF16), pltpu.VMEM((tf, d), BF16),
        ],
    )
    return pl.pallas_call(
        _experts_kernel,
        grid_spec=grid_spec,
        out_shape=jax.ShapeDtypeStruct((n_blocks * bsz, d), BF16),
        compiler_params=_params("arbitrary", "arbitrary"),
        name="experts",
    )(be, nv, xb, gs, wg, wu, wd)


def _combine_kernel(win_ref, x_ref, pos_ref, g_ref, fg_ref, *refs):
    yb_refs, o_ref = refs[:-1], refs[-1]
    i = pl.program_id(0)
    n_sub = x_ref.shape[0] // COMBINE_ROWS
    lane = lax.broadcasted_iota(jnp.int32, (COMBINE_ROWS, 2 * COMBINE_ROWS), 1)
    for s in range(n_sub):
        rows = slice(s * COMBINE_ROWS, (s + 1) * COMBINE_ROWS)
        onehots, windows = [], []
        for e in range(N_EXPERTS):
            k = (i * n_sub + s) * N_EXPERTS + e
            base = win_ref[k] * COMBINE_ROWS
            rel = pos_ref[rows, e:e + 1] - base
            onehots.append(jnp.where(lane == rel, 1.0, 0.0).astype(BF16))
            w0 = 2 * (s * N_EXPERTS + e)
            windows += [yb_refs[w0][...], yb_refs[w0 + 1][...]]
        acc = jnp.dot(jnp.concatenate(onehots, axis=1), jnp.concatenate(windows, axis=0),
                      preferred_element_type=F32)
        xo = x_ref[rows, :] + g_ref[...] * acc
        ms = jnp.mean(xo * xo, axis=-1, keepdims=True)
        o_ref[rows, :] = xo * lax.rsqrt(ms + EPS) * fg_ref[...]


def _combine(win, x, pos, g2, fgain, yb):
    t, d = x.shape
    tm = COMBINE_TILE
    n_sub = tm // COMBINE_ROWS
    n_win = yb.shape[0] // COMBINE_ROWS
    vec = pl.BlockSpec((1, d), lambda i, w: (0, 0))

    def first(s, e):
        return lambda i, w: (jnp.minimum(w[(i * n_sub + s) * N_EXPERTS + e], n_win - 1), 0)

    def second(s, e):
        return lambda i, w: (jnp.minimum(w[(i * n_sub + s) * N_EXPERTS + e] + 1, n_win - 1), 0)

    yb_specs = []
    for s in range(n_sub):
        for e in range(N_EXPERTS):
            yb_specs.append(pl.BlockSpec((COMBINE_ROWS, d), first(s, e)))
            yb_specs.append(pl.BlockSpec((COMBINE_ROWS, d), second(s, e)))
    grid_spec = pltpu.PrefetchScalarGridSpec(
        num_scalar_prefetch=1,
        grid=(t // tm,),
        in_specs=[
            pl.BlockSpec((tm, d), lambda i, w: (i, 0)),
            pl.BlockSpec((tm, N_EXPERTS), lambda i, w: (i, 0)),
            vec, vec,
        ] + yb_specs,
        out_specs=pl.BlockSpec((tm, d), lambda i, w: (i, 0)),
    )
    return pl.pallas_call(
        _combine_kernel,
        grid_spec=grid_spec,
        out_shape=jax.ShapeDtypeStruct((t, d), F32),
        compiler_params=_params("arbitrary"),
        name="combine",
    )(win, x, pos, g2, fgain, *([yb] * len(yb_specs)))


def _routing_tables(rank, tot, sel):
    t = rank.shape[0]
    bsz = MOE_BLOCK
    n_blocks = (2 * t) // bsz + N_EXPERTS
    cum = tot[:, 0, :N_EXPERTS].astype(jnp.int32)
    counts = cum[-1]
    padded = ((counts + bsz - 1) // bsz) * bsz
    pad_end = jnp.cumsum(padded)
    start_pad = pad_end - padded
    selected = sel[:, :N_EXPERTS] > 0
    pos = jnp.where(selected, start_pad[None, :] + rank[:, :N_EXPERTS].astype(jnp.int32), -1)

    bstart = jnp.arange(n_blocks, dtype=jnp.int32) * bsz
    be = jnp.minimum(jnp.sum(bstart[:, None] >= pad_end[None, :], axis=1), N_EXPERTS - 1).astype(jnp.int32)
    r0 = bstart - start_pad[be]
    nv = jnp.clip(counts[be] - r0, 0, bsz).astype(jnp.int32)
    n_gather = bsz // MOE_GATHER_ROWS
    goff = jnp.arange(n_gather, dtype=jnp.int32) * MOE_GATHER_ROWS
    r0g = (r0[:, None] + goff[None, :]).reshape(-1)
    ng = jnp.clip(nv[:, None] - goff[None, :], 0, MOE_GATHER_ROWS).reshape(-1)
    beg = jnp.repeat(be, n_gather)
    chunk_end = cum[:, beg].T
    chunk_start = jnp.concatenate([jnp.zeros((1, N_EXPERTS), jnp.int32), cum[:-1]], axis=0)[:, beg].T
    clo = jnp.sum(chunk_end <= r0g[:, None], axis=1).astype(jnp.int32)
    chi = jnp.sum(chunk_start < (r0g + ng)[:, None], axis=1).astype(jnp.int32)
    beu = jnp.repeat(be, bsz // MOE_GATHER_UNIT)

    big = jnp.int32(2 ** 30)
    first = jnp.min(jnp.where(pos >= 0, pos, big).reshape(t // COMBINE_ROWS, COMBINE_ROWS, N_EXPERTS), axis=1)
    win = jnp.where(first == big, 0, first // COMBINE_ROWS).astype(jnp.int32).reshape(-1)
    return pos, be, nv, beu, ng, clo, chi, win


def _rotary_tables(seq, dk):
    half = dk // 2
    inv = ROPE_BASE ** (-jnp.arange(half, dtype=F32) / half)
    ang_a = (jnp.arange(seq // ROT_BLOCK) * ROT_BLOCK).astype(F32)[:, None] * inv[None, :]
    ang_b = jnp.arange(ROT_BLOCK).astype(F32)[:, None] * inv[None, :]
    return jnp.cos(ang_a), jnp.sin(ang_a), jnp.cos(ang_b), jnp.sin(ang_b)


def _decay_tables(dk, dv):
    c = RET_CHUNK
    log_gamma = jnp.log1p(-jnp.exp2(-5.0 - jnp.arange(N_HEADS, dtype=F32)))
    idx = jnp.arange(c, dtype=F32)
    row_dec = jnp.exp(log_gamma[:, None] * (idx + 1.0 - c))
    causal = (idx[:, None] >= idx[None, :]).astype(F32)
    dmask = row_dec[:, :, None] * causal[None]
    k_dec = jnp.exp(log_gamma[:, None] * (c - 1.0 - idx)) * (dk ** -0.5)
    q_dec = jnp.exp(log_gamma[:, None] * (idx + 1.0))
    chunk_dec = jnp.exp(log_gamma * c)
    half = dk // 2
    kdec_tile = jnp.tile(jnp.repeat(k_dec.T, half, axis=1), (ROW_TILE // c, 1))
    qdec_tile = jnp.broadcast_to(q_dec[:, :, None], (N_HEADS, c, dv))
    return dmask, kdec_tile, qdec_tile, chunk_dec


def _pool_bands(tm):
    def bands(lag):
        return jnp.stack([((lag >= 0) & (lag < w)) for w in POOL_WINDOWS]).astype(BF16)
    main = bands(jnp.arange(tm)[:, None] - jnp.arange(tm)[None, :])
    head = bands(jnp.arange(POOL_HALO)[:, None] + POOL_HALO - jnp.arange(POOL_HALO)[None, :])
    return main, head


def kernel(x, c, ada_w, ada_b, norm_gain, ret_w_in, ret_gn_gain, ret_w_out, ffn_w_gate, ffn_w_up, ffn_w_down,
           pool_w, pool_b, pool_scale, moe_router, moe_w_gate, moe_w_up, moe_w_down, final_norm_gain):
    bsz, seq, d = x.shape
    assert bsz == 1 and ada_w.shape[0] == 2
    xt = x.reshape(seq, d)
    qk_total = d
    v_total = ret_w_out.shape[1]
    dk = qk_total // N_HEADS
    dv = v_total // N_HEADS

    mod = _modulation(c, ada_w, ada_b)
    def mods(i):
        parts = [mod[i, :, k * d:(k + 1) * d] for k in range(N_MOD)]
        sh1, sc1, g1, sh2, sc2, g2 = parts
        return sh1, 1.0 + sc1, g1, sh2, 1.0 + sc2, g2

    sh1, sc1p, g1, sh2, sc2p, g2 = mods(0)
    dmask, kdec_tile, qdec_tile, chunk_dec = _decay_tables(dk, dv)
    proj = _ret_proj(xt, norm_gain[0, 0][None], sc1p, sh1, ret_w_in[0].astype(BF16), _rotary_tables(seq, dk),
                     kdec_tile, qk_total, v_total)
    y = _retention_core(proj, chunk_dec, dmask, qdec_tile, ret_gn_gain[0][None], qk_total, v_total)
    x1 = _ret_out(y, ret_w_out[0].astype(BF16), xt, g1)
    x2 = _dense_ffn(x1, norm_gain[0, 1][None], sc2p, sh2, g2,
                    ffn_w_gate[0].astype(BF16), ffn_w_up[0].astype(BF16), ffn_w_down[0].astype(BF16))

    sh1, sc1p, g1, sh2, sc2p, g2 = mods(1)
    band, bandh = _pool_bands(POOL_SUB_ROWS)
    x3 = _pool_mixer(x2, norm_gain[1, 0][None], sc1p, sh1, g1, band, bandh,
                     pool_w[0].astype(BF16), pool_b[0].reshape(1, d), pool_scale[0][None])

    wr = jnp.pad(moe_router[0], ((0, 0), (0, LANES - N_EXPERTS)))
    wr_hi = wr.astype(BF16)
    wr_lo = (wr - wr_hi.astype(F32)).astype(BF16)
    h4, gates, sel = _router(x3, norm_gain[1, 1][None], sc2p, sh2, wr_hi, wr_lo)
    tri = (jnp.arange(RANK_ROWS)[:, None] >= jnp.arange(RANK_ROWS)[None, :]).astype(BF16)
    rank, tot = _ranks(sel, tri)
    pos, be, nv, beu, ng, clo, chi, win = _routing_tables(rank, tot, sel)
    xb, gs = _gather(beu, ng, clo, chi, h4, pos.T, gates[:, :N_EXPERTS].T)
    yb = _experts(be, nv, xb, gs, moe_w_gate[0], moe_w_up[0], moe_w_down[0])
    out = _combine(win, x3, pos, g2, final_norm_gain[None], yb)
    return out.reshape(bsz, seq, d)
```

```python
import functools

import jax
import jax.numpy as jnp
from jax import lax
from jax.experimental import pallas as pl
from jax.experimental.pallas import tpu as pltpu

F32 = jnp.float32
BF16 = jnp.bfloat16

EPS = 1e-6
N_HEADS = 4
RET_CHUNK = 256
ROPE_BASE = 10000.0
ROT_BLOCK = 128
POOL_WINDOWS = (2, 4, 8, 16)
POOL_HALO = 16
N_EXPERTS = 8
N_MOD = 6

VMEM_LIMIT_BYTES = 56 * 1024 * 1024

ROW_TILE = 1024
RET_ROWS = 512
POOL_ROWS = 512
POOL_SUB_ROWS = 256
RANK_ROWS = 512
MOE_BLOCK = 2048
MOE_DOT_ROWS = 512
MOE_GATHER_UNIT = 1024
MOE_GATHER_ROWS = 256
MOE_FF_TILE = 512
GATHER_CHUNK = 512
COMBINE_TILE = 256
COMBINE_ROWS = 128
LANES = 128


def _silu(v):
    return v / (1.0 + jnp.exp(-v))


def _norm_mod(x, gain, scale1p, shift):
    ms = jnp.mean(x * x, axis=-1, keepdims=True)
    return (x * lax.rsqrt(ms + EPS) * gain) * scale1p + shift


def _params(*sem):
    return pltpu.CompilerParams(dimension_semantics=sem, vmem_limit_bytes=VMEM_LIMIT_BYTES)


def _resident(shape):
    nd = len(shape)
    return pl.BlockSpec(shape, lambda *_: (0,) * nd, pipeline_mode=pl.Buffered(1))


def _mod_kernel(c_ref, w_ref, b_ref, o_ref):
    ca = _silu(c_ref[...])
    o_ref[0] = jnp.sum(ca * w_ref[0], axis=0, keepdims=True) + b_ref[0]


def _modulation(c, ada_w, ada_b):
    depth, d, n = ada_w.shape
    tn = 1024
    return pl.pallas_call(
        _mod_kernel,
        grid=(depth, n // tn),
        in_specs=[
            pl.BlockSpec((d, 1), lambda i, j: (0, 0)),
            pl.BlockSpec((1, d, tn), lambda i, j: (i, 0, j)),
            pl.BlockSpec((1, 1, tn), lambda i, j: (i, 0, j)),
        ],
        out_specs=pl.BlockSpec((1, 1, tn), lambda i, j: (i, 0, j)),
        out_shape=jax.ShapeDtypeStruct((depth, 1, n), F32),
        compiler_params=_params("arbitrary", "arbitrary"),
        name="mod",
    )(c.reshape(d, 1), ada_w, ada_b.reshape(depth, 1, n))


def _proj_kernel(x_ref, gain_ref, sc_ref, sh_ref, w_ref, cosa_ref, sina_ref, cosb_ref, sinb_ref, kdec_ref, o_ref,
                 h_ref, cos_ref, sin_ref, *, n_qk_tiles, n_v_tiles, dk):
    j = pl.program_id(1)

    @pl.when(j == 0)
    def _():
        h_ref[...] = _norm_mod(x_ref[...], gain_ref[...], sc_ref[...], sh_ref[...]).astype(BF16)
        cb = cosb_ref[...]
        sb = sinb_ref[...]
        for a in range(cosa_ref.shape[0]):
            ca = cosa_ref[a:a + 1, :]
            sa = sina_ref[a:a + 1, :]
            cos_ref[a * ROT_BLOCK:(a + 1) * ROT_BLOCK, :] = ca * cb - sa * sb
            sin_ref[a * ROT_BLOCK:(a + 1) * ROT_BLOCK, :] = sa * cb + ca * sb

    half = dk // 2
    n_chunks = w_ref.shape[1] // dk

    def chunk_dot(c):
        return jnp.dot(h_ref[...], w_ref[:, c * dk:(c + 1) * dk].astype(BF16), preferred_element_type=F32)

    def rotary(scale_ref):
        cos = cos_ref[...]
        sin = sin_ref[...]
        for c in range(n_chunks):
            acc = chunk_dot(c)
            t1 = acc[:, :half]
            t2 = acc[:, half:]
            o1 = t1 * cos - t2 * sin
            o2 = t1 * sin + t2 * cos
            if scale_ref is not None:
                s = scale_ref[:, c * half:(c + 1) * half]
                o1 = o1 * s
                o2 = o2 * s
            o_ref[:, c * dk:c * dk + half] = o1.astype(BF16)
            o_ref[:, c * dk + half:(c + 1) * dk] = o2.astype(BF16)

    @pl.when(j < n_qk_tiles)
    def _():
        rotary(None)

    @pl.when((j >= n_qk_tiles) & (j < 2 * n_qk_tiles))
    def _():
        rotary(kdec_ref)

    @pl.when((j >= 2 * n_qk_tiles) & (j < 2 * n_qk_tiles + n_v_tiles))
    def _():
        for c in range(n_chunks):
            o_ref[:, c * dk:(c + 1) * dk] = chunk_dot(c).astype(BF16)

    @pl.when(j >= 2 * n_qk_tiles + n_v_tiles)
    def _():
        for c in range(n_chunks):
            o_ref[:, c * dk:(c + 1) * dk] = _silu(chunk_dot(c)).astype(BF16)


def _ret_proj(x, gain, sc1p, sh, w_in, rot, kdec, qk_total, v_total):
    t, d = x.shape
    n = w_in.shape[1]
    tm, tn = ROW_TILE, 1024
    dk = qk_total // N_HEADS
    half = dk // 2
    cosa, sina, cosb, sinb = rot
    kern = functools.partial(_proj_kernel, n_qk_tiles=qk_total // tn, n_v_tiles=v_total // tn, dk=dk)
    vec = pl.BlockSpec((1, d), lambda i, j: (0, 0))
    rot_a = pl.BlockSpec((tm // ROT_BLOCK, half), lambda i, j: (i, 0))
    rot_b = pl.BlockSpec((ROT_BLOCK, half), lambda i, j: (0, 0))
    return pl.pallas_call(
        kern,
        grid=(t // tm, n // tn),
        in_specs=[
            pl.BlockSpec((tm, d), lambda i, j: (i, 0)),
            vec, vec, vec,
            pl.BlockSpec((d, tn), lambda i, j: (0, j)),
            rot_a, rot_a, rot_b, rot_b,
            pl.BlockSpec((tm, N_HEADS * half), lambda i, j: (0, 0)),
        ],
        out_specs=pl.BlockSpec((tm, tn), lambda i, j: (i, j)),
        out_shape=jax.ShapeDtypeStruct((t, n), BF16),
        scratch_shapes=[pltpu.VMEM((tm, d), BF16), pltpu.VMEM((tm, half), F32), pltpu.VMEM((tm, half), F32)],
        compiler_params=_params("arbitrary", "arbitrary"),
        name="proj",
    )(x, gain, sc1p, sh, w_in, cosa, sina, cosb, sinb, kdec)


def _ret_kernel(cdec_ref, q_ref, k_ref, v_ref, sg_ref, dmask_ref, qdec_ref, gn_ref, y_ref, state_ref,
                *, dk, dv):
    @pl.when(pl.program_id(0) == 0)
    def _():
        state_ref[...] = jnp.zeros_like(state_ref)

    rows_per_step = q_ref.shape[0]
    for n in range(rows_per_step // RET_CHUNK):
        rows = slice(n * RET_CHUNK, (n + 1) * RET_CHUNK)
        for h in range(N_HEADS):
            qcols = slice(h * dk, (h + 1) * dk)
            vcols = slice(h * dv, (h + 1) * dv)
            q = q_ref[rows, qcols]
            ks = k_ref[rows, qcols]
            v = v_ref[rows, vcols]
            s = lax.dot_general(q, ks, (((1,), (1,)), ((), ())), preferred_element_type=F32)
            a = (s * dmask_ref[h]).astype(BF16)
            intra = jnp.dot(a, v, preferred_element_type=F32)
            st = state_ref[h]
            cross = jnp.dot(q, st.astype(BF16), preferred_element_type=F32)
            o = intra + cross * qdec_ref[h]
            kv = lax.dot_general(ks, v, (((0,), (0,)), ((), ())), preferred_element_type=F32)
            state_ref[h] = st * cdec_ref[h] + kv
            mu = jnp.mean(o, axis=-1, keepdims=True)
            dlt = o - mu
            var = jnp.mean(dlt * dlt, axis=-1, keepdims=True)
            on = dlt * lax.rsqrt(var + EPS)
            y = on * gn_ref[:, vcols] * sg_ref[rows, vcols].astype(F32)
            y_ref[rows, vcols] = y.astype(BF16)


def _retention_core(proj, cdec, dmask, qdec, gn_gain, qk_total, v_total):
    t = proj.shape[0]
    r = RET_ROWS
    dk = qk_total // N_HEADS
    dv = v_total // N_HEADS
    assert v_total == 2 * qk_total
    kern = functools.partial(_ret_kernel, dk=dk, dv=dv)
    return pl.pallas_call(
        kern,
        grid=(t // r,),
        in_specs=[
            pl.BlockSpec(memory_space=pltpu.SMEM),
            pl.BlockSpec((r, qk_total), lambda i: (i, 0)),
            pl.BlockSpec((r, qk_total), lambda i: (i, 1)),
            pl.BlockSpec((r, v_total), lambda i: (i, 1)),
            pl.BlockSpec((r, v_total), lambda i: (i, 2)),
            pl.BlockSpec((N_HEADS, RET_CHUNK, RET_CHUNK), lambda i: (0, 0, 0)),
            pl.BlockSpec((N_HEADS, RET_CHUNK, dv), lambda i: (0, 0, 0)),
            pl.BlockSpec((1, v_total), lambda i: (0, 0)),
        ],
        out_specs=pl.BlockSpec((r, v_total), lambda i: (i, 0)),
        out_shape=jax.ShapeDtypeStruct((t, v_total), BF16),
        scratch_shapes=[pltpu.VMEM((N_HEADS, dk, dv), F32)],
        compiler_params=_params("arbitrary"),
        name="ret",
    )(cdec, proj, proj, proj, proj, dmask, qdec, gn_gain)


def _out_kernel(y_ref, w_ref, x_ref, g_ref, o_ref):
    m = jnp.dot(y_ref[...], w_ref[...], preferred_element_type=F32)
    o_ref[...] = x_ref[...] + g_ref[...] * m


def _ret_out(y, w_out, x, g1):
    t, d = x.shape
    kdim = y.shape[1]
    tm = ROW_TILE
    return pl.pallas_call(
        _out_kernel,
        grid=(t // tm,),
        in_specs=[
            pl.BlockSpec((tm, kdim), lambda i: (i, 0)),
            _resident((kdim, d)),
            pl.BlockSpec((tm, d), lambda i: (i, 0)),
            pl.BlockSpec((1, d), lambda i: (0, 0)),
        ],
        out_specs=pl.BlockSpec((tm, d), lambda i: (i, 0)),
        out_shape=jax.ShapeDtypeStruct((t, d), F32),
        compiler_params=_params("arbitrary"),
        name="out",
    )(y, w_out, x, g1)


def _ffn_kernel(x_ref, gain_ref, sc_ref, sh_ref, g_ref, wg_ref, wu_ref, wd_ref, o_ref, *, ff_tile):
    x = x_ref[...]
    h = _norm_mod(x, gain_ref[...], sc_ref[...], sh_ref[...]).astype(BF16)
    d_ff = wg_ref.shape[1]
    acc = jnp.zeros(x.shape, F32)
    for lo in range(0, d_ff, ff_tile):
        hi = min(lo + ff_tile, d_ff)
        a = jnp.dot(h, wg_ref[:, lo:hi], preferred_element_type=F32)
        b = jnp.dot(h, wu_ref[:, lo:hi], preferred_element_type=F32)
        act = (_silu(a) * b).astype(BF16)
        acc = acc + jnp.dot(act, wd_ref[lo:hi, :], preferred_element_type=F32)
    o_ref[...] = x + g_ref[...] * acc


def _dense_ffn(x, gain, sc1p, sh, g2, wg, wu, wd):
    t, d = x.shape
    d_ff = wg.shape[1]
    tm = ROW_TILE
    vec = pl.BlockSpec((1, d), lambda i: (0, 0))
    return pl.pallas_call(
        functools.partial(_ffn_kernel, ff_tile=512),
        grid=(t // tm,),
        in_specs=[
            pl.BlockSpec((tm, d), lambda i: (i, 0)),
            vec, vec, vec, vec,
            _resident((d, d_ff)), _resident((d, d_ff)), _resident((d_ff, d)),
        ],
        out_specs=pl.BlockSpec((tm, d), lambda i: (i, 0)),
        out_shape=jax.ShapeDtypeStruct((t, d), F32),
        compiler_params=_params("arbitrary"),
        name="ffn",
    )(x, gain, sc1p, sh, g2, wg, wu, wd)


def _pool_kernel(x_ref, gain_ref, sc_ref, sh_ref, g_ref, band_ref, bandh_ref, wp_ref, bp_ref, ps_ref, o_ref,
                 halo_ref):
    i = pl.program_id(0)
    tm = x_ref.shape[0]
    gw = wp_ref.shape[1]

    @pl.when(i == 0)
    def _():
        halo_ref[...] = jnp.zeros_like(halo_ref)

    def split(v):
        hi = v.astype(BF16)
        return hi, (v - hi.astype(F32)).astype(BF16)

    sub = band_ref.shape[1]
    for s in range(tm // sub):
        rows = slice(s * sub, (s + 1) * sub)
        x = x_ref[rows, :]
        h = _norm_mod(x, gain_ref[...], sc_ref[...], sh_ref[...])
        h_hi, h_lo = split(h)
        halo_hi, halo_lo = split(halo_ref[...])
        halo_ref[...] = h[sub - POOL_HALO:, :]
        t1 = (lax.broadcasted_iota(jnp.int32, (sub, 1), 0) + (i * tm + s * sub + 1)).astype(F32)
        for g, w in enumerate(POOL_WINDOWS):
            cols = slice(g * gw, (g + 1) * gw)
            band = band_ref[g]
            bandh = bandh_ref[g]
            win = (jnp.dot(band, h_hi[:, cols], preferred_element_type=F32)
                   + jnp.dot(band, h_lo[:, cols], preferred_element_type=F32))
            head = (jnp.dot(bandh, halo_hi[:, cols], preferred_element_type=F32)
                    + jnp.dot(bandh, halo_lo[:, cols], preferred_element_type=F32))
            win = jnp.concatenate([win[:POOL_HALO] + head, win[POOL_HALO:]], axis=0)
            pooled = win / jnp.minimum(t1, float(w)) - h[:, cols]
            y = jnp.dot(pooled.astype(BF16), wp_ref[g], preferred_element_type=F32) + bp_ref[:, cols]
            o_ref[rows, cols] = x[:, cols] + g_ref[:, cols] * (y * ps_ref[:, cols])


def _pool_mixer(x, gain, sc1p, sh, g1, band, bandh, wp, bp, ps):
    t, d = x.shape
    tm = POOL_ROWS
    vec = pl.BlockSpec((1, d), lambda i: (0, 0))
    return pl.pallas_call(
        _pool_kernel,
        grid=(t // tm,),
        in_specs=[
            pl.BlockSpec((tm, d), lambda i: (i, 0)),
            vec, vec, vec, vec,
            pl.BlockSpec(band.shape, lambda i: (0, 0, 0)),
            pl.BlockSpec(bandh.shape, lambda i: (0, 0, 0)),
            pl.BlockSpec(wp.shape, lambda i: (0, 0, 0)),
            vec, vec,
        ],
        out_specs=pl.BlockSpec((tm, d), lambda i: (i, 0)),
        out_shape=jax.ShapeDtypeStruct((t, d), F32),
        scratch_shapes=[pltpu.VMEM((POOL_HALO, d), F32)],
        compiler_params=_params("arbitrary"),
        name="pool",
    )(x, gain, sc1p, sh, g1, band, bandh, wp, bp, ps)


def _router_kernel(x_ref, gain_ref, sc_ref, sh_ref, w_ref, h_ref, gates_ref, sel_ref):
    h = _norm_mod(x_ref[...], gain_ref[...], sc_ref[...], sh_ref[...])
    h_hi = h.astype(BF16)
    h_ref[...] = h_hi
    h_lo = (h - h_hi.astype(F32)).astype(BF16)
    w = w_ref[...]
    p_hi = jnp.dot(h_hi, w, preferred_element_type=F32)
    p_lo = jnp.dot(h_lo, w, preferred_element_type=F32)
    logits = p_hi[:, :LANES] + p_hi[:, LANES:] + p_lo[:, :LANES] + p_lo[:, LANES:]
    lane = lax.broadcasted_iota(jnp.int32, logits.shape, 1).astype(F32)
    neg = jnp.float32(-jnp.inf)
    lg = jnp.where(lane < float(N_EXPERTS), logits, neg)
    m0 = jnp.max(lg, axis=-1, keepdims=True)
    i0 = jnp.min(jnp.where(lg == m0, lane, float(LANES)), axis=-1, keepdims=True)
    lg1 = jnp.where(lane == i0, neg, lg)
    m1 = jnp.max(lg1, axis=-1, keepdims=True)
    i1 = jnp.min(jnp.where(lg1 == m1, lane, float(LANES)), axis=-1, keepdims=True)
    e1 = jnp.exp(m1 - m0)
    den = 1.0 + e1
    is0 = lane == i0
    is1 = lane == i1
    gates_ref[...] = jnp.where(is0, 1.0 / den, jnp.where(is1, e1 / den, 0.0))
    sel_ref[...] = jnp.where(is0 | is1, 1.0, 0.0).astype(BF16)


def _router(x, gain, sc1p, sh, w_cat):
    t, d = x.shape
    tm = ROW_TILE
    vec = pl.BlockSpec((1, d), lambda i: (0, 0))
    wspec = pl.BlockSpec((d, 2 * LANES), lambda i: (0, 0))
    return pl.pallas_call(
        _router_kernel,
        grid=(t // tm,),
        in_specs=[pl.BlockSpec((tm, d), lambda i: (i, 0)), vec, vec, vec, wspec],
        out_specs=[
            pl.BlockSpec((tm, d), lambda i: (i, 0)),
            pl.BlockSpec((tm, LANES), lambda i: (i, 0)),
            pl.BlockSpec((tm, LANES), lambda i: (i, 0)),
        ],
        out_shape=[
            jax.ShapeDtypeStruct((t, d), BF16),
            jax.ShapeDtypeStruct((t, LANES), F32),
            jax.ShapeDtypeStruct((t, LANES), BF16),
        ],
        compiler_params=_params("arbitrary"),
        name="router",
    )(x, gain, sc1p, sh, w_cat)


def _rank_kernel(sel_ref, tri_ref, rank_ref, tot_ref, carry_ref):
    @pl.when(pl.program_id(0) == 0)
    def _():
        carry_ref[...] = jnp.zeros_like(carry_ref)

    s = sel_ref[...]
    cum = jnp.dot(tri_ref[...], s, preferred_element_type=F32)
    carry = carry_ref[...]
    rank_ref[...] = cum - s.astype(F32) + carry
    tot = carry + cum[cum.shape[0] - 1:, :]
    carry_ref[...] = tot
    tot_ref[0] = tot


def _ranks(sel, tri):
    t = sel.shape[0]
    tm = RANK_ROWS
    return pl.pallas_call(
        _rank_kernel,
        grid=(t // tm,),
        in_specs=[pl.BlockSpec((tm, LANES), lambda i: (i, 0)), pl.BlockSpec((tm, tm), lambda i: (0, 0))],
        out_specs=[pl.BlockSpec((tm, LANES), lambda i: (i, 0)), pl.BlockSpec((1, 1, LANES), lambda i: (i, 0, 0))],
        out_shape=[jax.ShapeDtypeStruct((t, LANES), F32), jax.ShapeDtypeStruct((t // tm, 1, LANES), F32)],
        scratch_shapes=[pltpu.VMEM((1, LANES), F32)],
        compiler_params=_params("arbitrary"),
        name="rank",
    )(sel, tri)


def _gather_kernel(be_ref, ng_ref, clo_ref, chi_ref, h_ref, post_ref, gatest_ref, xb_ref, gs_ref, acc_ref):
    u = pl.program_id(0)
    e = be_ref[u]
    unit_rows = xb_ref.shape[0]
    n_groups = unit_rows // MOE_GATHER_ROWS
    for q in range(n_groups):
        rows = slice(q * MOE_GATHER_ROWS, (q + 1) * MOE_GATHER_ROWS)
        gi = u * n_groups + q

        @pl.when(ng_ref[gi] > 0)
        def _():
            acc_ref[...] = jnp.zeros_like(acc_ref)
            row_pos = (lax.broadcasted_iota(jnp.int32, (MOE_GATHER_ROWS, GATHER_CHUNK), 0)
                       + (u * unit_rows + q * MOE_GATHER_ROWS))

            def body(c, gsum):
                off = pl.multiple_of(c * GATHER_CHUNK, GATHER_CHUNK)
                p = post_ref[pl.ds(e, 1), pl.ds(off, GATHER_CHUNK)]
                match = row_pos == p
                onehot = jnp.where(match, 1.0, 0.0).astype(BF16)
                acc_ref[...] += jnp.dot(onehot, h_ref[pl.ds(off, GATHER_CHUNK), :], preferred_element_type=F32)
                gate = gatest_ref[pl.ds(e, 1), pl.ds(off, GATHER_CHUNK)]
                return gsum + jnp.sum(jnp.where(match, gate, 0.0), axis=1, keepdims=True)

            gsum = lax.fori_loop(clo_ref[gi], chi_ref[gi], body, jnp.zeros((MOE_GATHER_ROWS, 1), F32))
            xb_ref[rows, :] = acc_ref[...].astype(BF16)
            gs_ref[rows, :] = gsum

        @pl.when(ng_ref[gi] == 0)
        def _():
            xb_ref[rows, :] = jnp.zeros((MOE_GATHER_ROWS, xb_ref.shape[1]), BF16)
            gs_ref[rows, :] = jnp.zeros((MOE_GATHER_ROWS, 1), F32)


def _gather(beu, ng, clo, chi, h, post, gatest):
    t, d = h.shape
    n_units = beu.shape[0]
    grid_spec = pltpu.PrefetchScalarGridSpec(
        num_scalar_prefetch=4,
        grid=(n_units,),
        in_specs=[
            pl.BlockSpec((t, d), lambda u, *_: (0, 0), pipeline_mode=pl.Buffered(1)),
            pl.BlockSpec(post.shape, lambda u, *_: (0, 0), pipeline_mode=pl.Buffered(1)),
            pl.BlockSpec(gatest.shape, lambda u, *_: (0, 0), pipeline_mode=pl.Buffered(1)),
        ],
        out_specs=[
            pl.BlockSpec((MOE_GATHER_UNIT, d), lambda u, *_: (u, 0)),
            pl.BlockSpec((MOE_GATHER_UNIT, 1), lambda u, *_: (u, 0)),
        ],
        scratch_shapes=[pltpu.VMEM((MOE_GATHER_ROWS, d), F32)],
    )
    return pl.pallas_call(
        _gather_kernel,
        grid_spec=grid_spec,
        out_shape=[
            jax.ShapeDtypeStruct((n_units * MOE_GATHER_UNIT, d), BF16),
            jax.ShapeDtypeStruct((n_units * MOE_GATHER_UNIT, 1), F32),
        ],
        compiler_params=_params("arbitrary"),
        name="gather",
    )(beu, ng, clo, chi, h, post, gatest)


def _experts_kernel(be_ref, nv_ref, xb_ref, gs_ref, wg_ref, wu_ref, wd_ref, yb_ref,
                    acc_ref, wgb_ref, wub_ref, wdb_ref):
    b = pl.program_id(0)
    f = pl.program_id(1)
    nf = pl.num_programs(1)
    nv = nv_ref[b]
    block_rows = xb_ref.shape[0]

    @pl.when((b == 0) & (f == 0))
    def _():
        acc_ref[...] = jnp.zeros_like(acc_ref)

    def swiglu_rows(lo, n, cast):
        rows = slice(lo, lo + n)
        if cast:
            wgb_ref[...] = wg_ref[0].astype(BF16)
            wub_ref[...] = wu_ref[0].astype(BF16)
            wdb_ref[...] = wd_ref[0].astype(BF16)
        x = xb_ref[rows, :]
        a = jnp.dot(x, wgb_ref[...], preferred_element_type=F32)
        u = jnp.dot(x, wub_ref[...], preferred_element_type=F32)
        act = (_silu(a) * u).astype(BF16)
        prev = jnp.where(f > 0, acc_ref[rows, :], 0.0)
        acc_ref[rows, :] = prev + jnp.dot(act, wdb_ref[...], preferred_element_type=F32)

    pair = 2 * MOE_DOT_ROWS
    for p in range(block_rows // pair):
        lo = p * pair
        full = nv >= lo + pair

        @pl.when(full)
        def _():
            swiglu_rows(lo, pair, p == 0)

        @pl.when(jnp.logical_not(full) & (nv > lo))
        def _():
            swiglu_rows(lo, MOE_DOT_ROWS, p == 0)

        @pl.when(jnp.logical_not(full) & (nv > lo + MOE_DOT_ROWS))
        def _():
            swiglu_rows(lo + MOE_DOT_ROWS, MOE_DOT_ROWS, False)

    @pl.when(f == nf - 1)
    def _():
        for g in range(block_rows // MOE_DOT_ROWS):
            rows = slice(g * MOE_DOT_ROWS, (g + 1) * MOE_DOT_ROWS)

            @pl.when(nv > g * MOE_DOT_ROWS)
            def _():
                yb_ref[rows, :] = (acc_ref[rows, :] * gs_ref[rows, :]).astype(BF16)

            @pl.when(nv <= g * MOE_DOT_ROWS)
            def _():
                yb_ref[rows, :] = jnp.zeros((MOE_DOT_ROWS, yb_ref.shape[1]), BF16)


def _experts(be, nv, xb, gs, wg, wu, wd):
    d = xb.shape[1]
    n_blocks = be.shape[0]
    d_ff = wg.shape[2]
    tf = MOE_FF_TILE
    nf = d_ff // tf
    bsz = MOE_BLOCK

    def ff_idx(f, nvr, b):
        return jnp.where(nvr[b] > 0, f, nf - 1)

    grid_spec = pltpu.PrefetchScalarGridSpec(
        num_scalar_prefetch=2,
        grid=(n_blocks, nf),
        in_specs=[
            pl.BlockSpec((bsz, d), lambda b, f, *_: (b, 0)),
            pl.BlockSpec((bsz, 1), lambda b, f, *_: (b, 0)),
            pl.BlockSpec((1, d, tf), lambda b, f, be_r, nv_r: (be_r[b], 0, ff_idx(f, nv_r, b))),
            pl.BlockSpec((1, d, tf), lambda b, f, be_r, nv_r: (be_r[b], 0, ff_idx(f, nv_r, b))),
            pl.BlockSpec((1, tf, d), lambda b, f, be_r, nv_r: (be_r[b], ff_idx(f, nv_r, b), 0)),
        ],
        out_specs=pl.BlockSpec((bsz, d), lambda b, f, *_: (b, 0)),
        scratch_shapes=[
            pltpu.VMEM((bsz, d), F32),
            pltpu.VMEM((d, tf), BF16), pltpu.VMEM((d, tf), BF16), pltpu.VMEM((tf, d), BF16),
        ],
    )
    return pl.pallas_call(
        _experts_kernel,
        grid_spec=grid_spec,
        out_shape=jax.ShapeDtypeStruct((n_blocks * bsz, d), BF16),
        compiler_params=_params("arbitrary", "arbitrary"),
        name="experts",
    )(be, nv, xb, gs, wg, wu, wd)


def _combine_kernel(win_ref, x_ref, pos_ref, g_ref, fg_ref, *refs):
    yb_refs, o_ref = refs[:-1], refs[-1]
    i = pl.program_id(0)
    n_sub = x_ref.shape[0] // COMBINE_ROWS
    lane = lax.broadcasted_iota(jnp.int32, (COMBINE_ROWS, 2 * COMBINE_ROWS), 1)
    for s in range(n_sub):
        rows = slice(s * COMBINE_ROWS, (s + 1) * COMBINE_ROWS)
        onehots, windows = [], []
        for e in range(N_EXPERTS):
            k = (i * n_sub + s) * N_EXPERTS + e
            base = win_ref[k] * COMBINE_ROWS
            rel = pos_ref[rows, e:e + 1] - base
            onehots.append(jnp.where(lane == rel, 1.0, 0.0).astype(BF16))
            w0 = 2 * (s * N_EXPERTS + e)
            windows += [yb_refs[w0][...], yb_refs[w0 + 1][...]]
        acc = jnp.dot(jnp.concatenate(onehots, axis=1), jnp.concatenate(windows, axis=0),
                      preferred_element_type=F32)
        xo = x_ref[rows, :] + g_ref[...] * acc
        ms = jnp.mean(xo * xo, axis=-1, keepdims=True)
        o_ref[rows, :] = xo * lax.rsqrt(ms + EPS) * fg_ref[...]


def _combine(win, x, pos, g2, fgain, yb):
    t, d = x.shape
    tm = COMBINE_TILE
    n_sub = tm // COMBINE_ROWS
    n_win = yb.shape[0] // COMBINE_ROWS
    vec = pl.BlockSpec((1, d), lambda i, w: (0, 0))

    def first(s, e):
        return lambda i, w: (jnp.minimum(w[(i * n_sub + s) * N_EXPERTS + e], n_win - 1), 0)

    def second(s, e):
        return lambda i, w: (jnp.minimum(w[(i * n_sub + s) * N_EXPERTS + e] + 1, n_win - 1), 0)

    yb_specs = []
    for s in range(n_sub):
        for e in range(N_EXPERTS):
            yb_specs.append(pl.BlockSpec((COMBINE_ROWS, d), first(s, e)))
            yb_specs.append(pl.BlockSpec((COMBINE_ROWS, d), second(s, e)))
    grid_spec = pltpu.PrefetchScalarGridSpec(
        num_scalar_prefetch=1,
        grid=(t // tm,),
        in_specs=[
            pl.BlockSpec((tm, d), lambda i, w: (i, 0)),
            pl.BlockSpec((tm, N_EXPERTS), lambda i, w: (i, 0)),
            vec, vec,
        ] + yb_specs,
        out_specs=pl.BlockSpec((tm, d), lambda i, w: (i, 0)),
    )
    return pl.pallas_call(
        _combine_kernel,
        grid_spec=grid_spec,
        out_shape=jax.ShapeDtypeStruct((t, d), F32),
        compiler_params=_params("arbitrary"),
        name="combine",
    )(win, x, pos, g2, fgain, *([yb] * len(yb_specs)))


def _routing_tables(rank, tot, sel):
    t = rank.shape[0]
    bsz = MOE_BLOCK
    n_blocks = (2 * t) // bsz + N_EXPERTS
    cum = tot[:, 0, :N_EXPERTS].astype(jnp.int32)
    counts = cum[-1]
    padded = ((counts + bsz - 1) // bsz) * bsz
    pad_end = jnp.cumsum(padded)
    start_pad = pad_end - padded
    selected = sel[:, :N_EXPERTS] > 0
    pos = jnp.where(selected, start_pad[None, :] + rank[:, :N_EXPERTS].astype(jnp.int32), -1)

    bstart = jnp.arange(n_blocks, dtype=jnp.int32) * bsz
    be = jnp.minimum(jnp.sum(bstart[:, None] >= pad_end[None, :], axis=1), N_EXPERTS - 1).astype(jnp.int32)
    r0 = bstart - start_pad[be]
    nv = jnp.clip(counts[be] - r0, 0, bsz).astype(jnp.int32)
    n_gather = bsz // MOE_GATHER_ROWS
    goff = jnp.arange(n_gather, dtype=jnp.int32) * MOE_GATHER_ROWS
    r0g = (r0[:, None] + goff[None, :]).reshape(-1)
    ng = jnp.clip(nv[:, None] - goff[None, :], 0, MOE_GATHER_ROWS).reshape(-1)
    beg = jnp.repeat(be, n_gather)
    chunk_end = cum[:, beg].T
    chunk_start = jnp.concatenate([jnp.zeros((1, N_EXPERTS), jnp.int32), cum[:-1]], axis=0)[:, beg].T
    clo = jnp.sum(chunk_end <= r0g[:, None], axis=1).astype(jnp.int32)
    chi = jnp.sum(chunk_start < (r0g + ng)[:, None], axis=1).astype(jnp.int32)
    beu = jnp.repeat(be, bsz // MOE_GATHER_UNIT)

    big = jnp.int32(2 ** 30)
    first = jnp.min(jnp.where(pos >= 0, pos, big).reshape(t // COMBINE_ROWS, COMBINE_ROWS, N_EXPERTS), axis=1)
    win = jnp.where(first == big, 0, first // COMBINE_ROWS).astype(jnp.int32).reshape(-1)
    return pos, be, nv, beu, ng, clo, chi, win


def _rotary_tables(seq, dk):
    half = dk // 2
    inv = ROPE_BASE ** (-jnp.arange(half, dtype=F32) / half)
    ang_a = (jnp.arange(seq // ROT_BLOCK) * ROT_BLOCK).astype(F32)[:, None] * inv[None, :]
    ang_b = jnp.arange(ROT_BLOCK).astype(F32)[:, None] * inv[None, :]
    return jnp.cos(ang_a), jnp.sin(ang_a), jnp.cos(ang_b), jnp.sin(ang_b)


def _decay_tables(dk, dv):
    c = RET_CHUNK
    log_gamma = jnp.log1p(-jnp.exp2(-5.0 - jnp.arange(N_HEADS, dtype=F32)))
    idx = jnp.arange(c, dtype=F32)
    row_dec = jnp.exp(log_gamma[:, None] * (idx + 1.0 - c))
    causal = (idx[:, None] >= idx[None, :]).astype(F32)
    dmask = row_dec[:, :, None] * causal[None]
    k_dec = jnp.exp(log_gamma[:, None] * (c - 1.0 - idx)) * (dk ** -0.5)
    q_dec = jnp.exp(log_gamma[:, None] * (idx + 1.0))
    chunk_dec = jnp.exp(log_gamma * c)
    half = dk // 2
    kdec_tile = jnp.tile(jnp.repeat(k_dec.T, half, axis=1), (ROW_TILE // c, 1))
    qdec_tile = jnp.broadcast_to(q_dec[:, :, None], (N_HEADS, c, dv))
    return dmask, kdec_tile, qdec_tile, chunk_dec


def _pool_bands(tm):
    def bands(lag):
        return jnp.stack([((lag >= 0) & (lag < w)) for w in POOL_WINDOWS]).astype(BF16)
    main = bands(jnp.arange(tm)[:, None] - jnp.arange(tm)[None, :])
    head = bands(jnp.arange(POOL_HALO)[:, None] + POOL_HALO - jnp.arange(POOL_HALO)[None, :])
    return main, head


def kernel(x, c, ada_w, ada_b, norm_gain, ret_w_in, ret_gn_gain, ret_w_out, ffn_w_gate, ffn_w_up, ffn_w_down,
           pool_w, pool_b, pool_scale, moe_router, moe_w_gate, moe_w_up, moe_w_down, final_norm_gain):
    bsz, seq, d = x.shape
    assert bsz == 1 and ada_w.shape[0] == 2
    xt = x.reshape(seq, d)
    qk_total = d
    v_total = ret_w_out.shape[1]
    dk = qk_total // N_HEADS
    dv = v_total // N_HEADS

    mod = _modulation(c, ada_w, ada_b)
    def mods(i):
        parts = [mod[i, :, k * d:(k + 1) * d] for k in range(N_MOD)]
        sh1, sc1, g1, sh2, sc2, g2 = parts
        return sh1, 1.0 + sc1, g1, sh2, 1.0 + sc2, g2

    sh1, sc1p, g1, sh2, sc2p, g2 = mods(0)
    dmask, kdec_tile, qdec_tile, chunk_dec = _decay_tables(dk, dv)
    proj = _ret_proj(xt, norm_gain[0, 0][None], sc1p, sh1, ret_w_in[0], _rotary_tables(seq, dk),
                     kdec_tile, qk_total, v_total)
    y = _retention_core(proj, chunk_dec, dmask, qdec_tile, ret_gn_gain[0][None], qk_total, v_total)
    x1 = _ret_out(y, ret_w_out[0].astype(BF16), xt, g1)
    x2 = _dense_ffn(x1, norm_gain[0, 1][None], sc2p, sh2, g2,
                    ffn_w_gate[0].astype(BF16), ffn_w_up[0].astype(BF16), ffn_w_down[0].astype(BF16))

    sh1, sc1p, g1, sh2, sc2p, g2 = mods(1)
    band, bandh = _pool_bands(POOL_SUB_ROWS)
    x3 = _pool_mixer(x2, norm_gain[1, 0][None], sc1p, sh1, g1, band, bandh,
                     pool_w[0].astype(BF16), pool_b[0].reshape(1, d), pool_scale[0][None])

    wr = jnp.pad(moe_router[0], ((0, 0), (0, LANES - N_EXPERTS)))
    wr_hi = wr.astype(BF16)
    wr_lo = (wr - wr_hi.astype(F32)).astype(BF16)
    h4, gates, sel = _router(x3, norm_gain[1, 1][None], sc2p, sh2, jnp.concatenate([wr_hi, wr_lo], axis=1))
    tri = (jnp.arange(RANK_ROWS)[:, None] >= jnp.arange(RANK_ROWS)[None, :]).astype(BF16)
    rank, tot = _ranks(sel, tri)
    pos, be, nv, beu, ng, clo, chi, win = _routing_tables(rank, tot, sel)
    xb, gs = _gather(beu, ng, clo, chi, h4, pos.T, gates[:, :N_EXPERTS].T)
    yb = _experts(be, nv, xb, gs, moe_w_gate[0], moe_w_up[0], moe_w_down[0])
    out = _combine(win, x3, pos, g2, final_norm_gain[None], yb)
    return out.reshape(bsz, seq, d)
```

```python
import functools

import jax
import jax.numpy as jnp
from jax import lax
from jax.experimental import pallas as pl
from jax.experimental.pallas import tpu as pltpu

F32 = jnp.float32
BF16 = jnp.bfloat16

EPS = 1e-6
N_HEADS = 4
RET_CHUNK = 256
ROPE_BASE = 10000.0
ROT_BLOCK = 128
POOL_WINDOWS = (2, 4, 8, 16)
POOL_HALO = 16
N_EXPERTS = 8
N_MOD = 6

VMEM_LIMIT_BYTES = 56 * 1024 * 1024

ROW_TILE = 1024
PROJ_COL_TILE = 2048
RET_ROWS = 512
POOL_ROWS = 512
POOL_SUB_ROWS = 256
RANK_ROWS = 512
MOE_BLOCK = 2048
MOE_DOT_ROWS = 512
MOE_GATHER_UNIT = 1024
MOE_GATHER_ROWS = 256
MOE_FF_TILE = 512
GATHER_CHUNK = 512
COMBINE_TILE = 256
COMBINE_ROWS = 128
LANES = 128


def _silu(v):
    return v / (1.0 + jnp.exp(-v))


def _norm_mod(x, gain, scale1p, shift):
    ms = jnp.mean(x * x, axis=-1, keepdims=True)
    return (x * lax.rsqrt(ms + EPS) * gain) * scale1p + shift


def _params(*sem):
    return pltpu.CompilerParams(dimension_semantics=sem, vmem_limit_bytes=VMEM_LIMIT_BYTES)


def _resident(shape):
    nd = len(shape)
    return pl.BlockSpec(shape, lambda *_: (0,) * nd, pipeline_mode=pl.Buffered(1))


def _mod_kernel(c_ref, w_ref, b_ref, o_ref):
    ca = _silu(c_ref[...])
    o_ref[0] = jnp.sum(ca * w_ref[0], axis=0, keepdims=True) + b_ref[0]


def _modulation(c, ada_w, ada_b):
    depth, d, n = ada_w.shape
    tn = 1024
    return pl.pallas_call(
        _mod_kernel,
        grid=(depth, n // tn),
        in_specs=[
            pl.BlockSpec((d, 1), lambda i, j: (0, 0)),
            pl.BlockSpec((1, d, tn), lambda i, j: (i, 0, j)),
            pl.BlockSpec((1, 1, tn), lambda i, j: (i, 0, j)),
        ],
        out_specs=pl.BlockSpec((1, 1, tn), lambda i, j: (i, 0, j)),
        out_shape=jax.ShapeDtypeStruct((depth, 1, n), F32),
        compiler_params=_params("arbitrary", "arbitrary"),
        name="mod",
    )(c.reshape(d, 1), ada_w, ada_b.reshape(depth, 1, n))


def _proj_kernel(x_ref, gain_ref, sc_ref, sh_ref, w_ref, cosa_ref, sina_ref, cosb_ref, sinb_ref, kdec_ref, o_ref,
                 h_ref, cos_ref, sin_ref, *, n_qk_chunks, n_v_chunks, n_col_tiles, dk):
    j = pl.program_id(1)

    @pl.when(j == 0)
    def _():
        h_ref[...] = _norm_mod(x_ref[...], gain_ref[...], sc_ref[...], sh_ref[...]).astype(BF16)
        cb = cosb_ref[...]
        sb = sinb_ref[...]
        for a in range(cosa_ref.shape[0]):
            ca = cosa_ref[a:a + 1, :]
            sa = sina_ref[a:a + 1, :]
            cos_ref[a * ROT_BLOCK:(a + 1) * ROT_BLOCK, :] = ca * cb - sa * sb
            sin_ref[a * ROT_BLOCK:(a + 1) * ROT_BLOCK, :] = sa * cb + ca * sb

    half = dk // 2
    n_chunks = w_ref.shape[1] // dk

    def chunk(c, kind, head):
        acc = jnp.dot(h_ref[...], w_ref[:, c * dk:(c + 1) * dk], preferred_element_type=F32)
        if kind == "v":
            o_ref[:, c * dk:(c + 1) * dk] = acc.astype(BF16)
        elif kind == "g":
            o_ref[:, c * dk:(c + 1) * dk] = _silu(acc).astype(BF16)
        else:
            cos = cos_ref[...]
            sin = sin_ref[...]
            t1 = acc[:, :half]
            t2 = acc[:, half:]
            o1 = t1 * cos - t2 * sin
            o2 = t1 * sin + t2 * cos
            if kind == "k":
                s = kdec_ref[:, head * half:(head + 1) * half]
                o1 = o1 * s
                o2 = o2 * s
            o_ref[:, c * dk:c * dk + half] = o1.astype(BF16)
            o_ref[:, c * dk + half:(c + 1) * dk] = o2.astype(BF16)

    def kind_of(gc):
        if gc < n_qk_chunks:
            return "q", gc
        if gc < 2 * n_qk_chunks:
            return "k", gc - n_qk_chunks
        if gc < 2 * n_qk_chunks + n_v_chunks:
            return "v", 0
        return "g", 0

    for jj in range(n_col_tiles):
        @pl.when(j == jj)
        def _():
            for c in range(n_chunks):
                chunk(c, *kind_of(jj * n_chunks + c))


def _ret_proj(x, gain, sc1p, sh, w_in, rot, kdec, qk_total, v_total):
    t, d = x.shape
    n = w_in.shape[1]
    tm, tn = ROW_TILE, PROJ_COL_TILE
    dk = qk_total // N_HEADS
    half = dk // 2
    cosa, sina, cosb, sinb = rot
    kern = functools.partial(_proj_kernel, n_qk_chunks=qk_total // dk, n_v_chunks=v_total // dk,
                             n_col_tiles=n // tn, dk=dk)
    vec = pl.BlockSpec((1, d), lambda i, j: (0, 0))
    rot_a = pl.BlockSpec((tm // ROT_BLOCK, half), lambda i, j: (i, 0))
    rot_b = pl.BlockSpec((ROT_BLOCK, half), lambda i, j: (0, 0))
    return pl.pallas_call(
        kern,
        grid=(t // tm, n // tn),
        in_specs=[
            pl.BlockSpec((tm, d), lambda i, j: (i, 0)),
            vec, vec, vec,
            pl.BlockSpec((d, tn), lambda i, j: (0, j)),
            rot_a, rot_a, rot_b, rot_b,
            pl.BlockSpec((tm, N_HEADS * half), lambda i, j: (0, 0)),
        ],
        out_specs=pl.BlockSpec((tm, tn), lambda i, j: (i, j)),
        out_shape=jax.ShapeDtypeStruct((t, n), BF16),
        scratch_shapes=[pltpu.VMEM((tm, d), BF16), pltpu.VMEM((tm, half), F32), pltpu.VMEM((tm, half), F32)],
        compiler_params=_params("arbitrary", "arbitrary"),
        name="proj",
    )(x, gain, sc1p, sh, w_in, cosa, sina, cosb, sinb, kdec)


def _ret_kernel(cdec_ref, q_ref, k_ref, v_ref, sg_ref, dmask_ref, qdec_ref, gn_ref, y_ref, state_ref,
                *, dk, dv):
    @pl.when(pl.program_id(0) == 0)
    def _():
        state_ref[...] = jnp.zeros_like(state_ref)

    rows_per_step = q_ref.shape[0]
    for n in range(rows_per_step // RET_CHUNK):
        rows = slice(n * RET_CHUNK, (n + 1) * RET_CHUNK)
        for h in range(N_HEADS):
            qcols = slice(h * dk, (h + 1) * dk)
            vcols = slice(h * dv, (h + 1) * dv)
            q = q_ref[rows, qcols]
            ks = k_ref[rows, qcols]
            v = v_ref[rows, vcols]
            s = lax.dot_general(q, ks, (((1,), (1,)), ((), ())), preferred_element_type=F32)
            a = (s * dmask_ref[h]).astype(BF16)
            st = state_ref[h]
            qs = (q.astype(F32) * qdec_ref[h]).astype(BF16)
            o = jnp.dot(jnp.concatenate([a, qs], axis=1), jnp.concatenate([v, st.astype(BF16)], axis=0),
                        preferred_element_type=F32)
            kv = lax.dot_general(ks, v, (((0,), (0,)), ((), ())), preferred_element_type=F32)
            state_ref[h] = st * cdec_ref[h] + kv
            mu = jnp.mean(o, axis=-1, keepdims=True)
            dlt = o - mu
            var = jnp.mean(dlt * dlt, axis=-1, keepdims=True)
            on = dlt * lax.rsqrt(var + EPS)
            y = on * gn_ref[:, vcols] * sg_ref[rows, vcols].astype(F32)
            y_ref[rows, vcols] = y.astype(BF16)


def _retention_core(proj, cdec, dmask, qdec, gn_gain, qk_total, v_total):
    t = proj.shape[0]
    r = RET_ROWS
    dk = qk_total // N_HEADS
    dv = v_total // N_HEADS
    assert v_total == 2 * qk_total
    kern = functools.partial(_ret_kernel, dk=dk, dv=dv)
    return pl.pallas_call(
        kern,
        grid=(t // r,),
        in_specs=[
            pl.BlockSpec(memory_space=pltpu.SMEM),
            pl.BlockSpec((r, qk_total), lambda i: (i, 0)),
            pl.BlockSpec((r, qk_total), lambda i: (i, 1)),
            pl.BlockSpec((r, v_total), lambda i: (i, 1)),
            pl.BlockSpec((r, v_total), lambda i: (i, 2)),
            pl.BlockSpec((N_HEADS, RET_CHUNK, RET_CHUNK), lambda i: (0, 0, 0)),
            pl.BlockSpec((N_HEADS, RET_CHUNK, dk), lambda i: (0, 0, 0)),
            pl.BlockSpec((1, v_total), lambda i: (0, 0)),
        ],
        out_specs=pl.BlockSpec((r, v_total), lambda i: (i, 0)),
        out_shape=jax.ShapeDtypeStruct((t, v_total), BF16),
        scratch_shapes=[pltpu.VMEM((N_HEADS, dk, dv), F32)],
        compiler_params=_params("arbitrary"),
        name="ret",
    )(cdec, proj, proj, proj, proj, dmask, qdec, gn_gain)


def _out_kernel(y_ref, w_ref, x_ref, g_ref, o_ref):
    m = jnp.dot(y_ref[...], w_ref[...], preferred_element_type=F32)
    o_ref[...] = x_ref[...] + g_ref[...] * m


def _ret_out(y, w_out, x, g1):
    t, d = x.shape
    kdim = y.shape[1]
    tm = ROW_TILE
    return pl.pallas_call(
        _out_kernel,
        grid=(t // tm,),
        in_specs=[
            pl.BlockSpec((tm, kdim), lambda i: (i, 0)),
            _resident((kdim, d)),
            pl.BlockSpec((tm, d), lambda i: (i, 0)),
            pl.BlockSpec((1, d), lambda i: (0, 0)),
        ],
        out_specs=pl.BlockSpec((tm, d), lambda i: (i, 0)),
        out_shape=jax.ShapeDtypeStruct((t, d), F32),
        compiler_params=_params("arbitrary"),
        name="out",
    )(y, w_out, x, g1)


def _ffn_kernel(x_ref, gain_ref, sc_ref, sh_ref, g_ref, wg_ref, wu_ref, wd_ref, o_ref, *, ff_tile):
    x = x_ref[...]
    h = _norm_mod(x, gain_ref[...], sc_ref[...], sh_ref[...]).astype(BF16)
    d_ff = wg_ref.shape[1]
    acc = jnp.zeros(x.shape, F32)
    for lo in range(0, d_ff, ff_tile):
        hi = min(lo + ff_tile, d_ff)
        a = jnp.dot(h, wg_ref[:, lo:hi], preferred_element_type=F32)
        b = jnp.dot(h, wu_ref[:, lo:hi], preferred_element_type=F32)
        act = (_silu(a) * b).astype(BF16)
        acc = acc + jnp.dot(act, wd_ref[lo:hi, :], preferred_element_type=F32)
    o_ref[...] = x + g_ref[...] * acc


def _dense_ffn(x, gain, sc1p, sh, g2, wg, wu, wd):
    t, d = x.shape
    d_ff = wg.shape[1]
    tm = ROW_TILE
    vec = pl.BlockSpec((1, d), lambda i: (0, 0))
    return pl.pallas_call(
        functools.partial(_ffn_kernel, ff_tile=512),
        grid=(t // tm,),
        in_specs=[
            pl.BlockSpec((tm, d), lambda i: (i, 0)),
            vec, vec, vec, vec,
            _resident((d, d_ff)), _resident((d, d_ff)), _resident((d_ff, d)),
        ],
        out_specs=pl.BlockSpec((tm, d), lambda i: (i, 0)),
        out_shape=jax.ShapeDtypeStruct((t, d), F32),
        compiler_params=_params("arbitrary"),
        name="ffn",
    )(x, gain, sc1p, sh, g2, wg, wu, wd)


def _pool_kernel(x_ref, gain_ref, sc_ref, sh_ref, g_ref, band_ref, bandh_ref, wp_ref, bp_ref, ps_ref, o_ref,
                 halo_ref):
    i = pl.program_id(0)
    tm = x_ref.shape[0]
    gw = wp_ref.shape[1]

    @pl.when(i == 0)
    def _():
        halo_ref[...] = jnp.zeros_like(halo_ref)

    def split(v):
        hi = v.astype(BF16)
        return hi, (v - hi.astype(F32)).astype(BF16)

    sub = band_ref.shape[1]
    for s in range(tm // sub):
        rows = slice(s * sub, (s + 1) * sub)
        x = x_ref[rows, :]
        h = _norm_mod(x, gain_ref[...], sc_ref[...], sh_ref[...])
        h_hi, h_lo = split(h)
        halo_hi, halo_lo = split(halo_ref[...])
        halo_ref[...] = h[sub - POOL_HALO:, :]
        t1 = (lax.broadcasted_iota(jnp.int32, (sub, 1), 0) + (i * tm + s * sub + 1)).astype(F32)
        for g, w in enumerate(POOL_WINDOWS):
            cols = slice(g * gw, (g + 1) * gw)
            band = band_ref[g]
            bandh = bandh_ref[g]
            win = (jnp.dot(band, h_hi[:, cols], preferred_element_type=F32)
                   + jnp.dot(band, h_lo[:, cols], preferred_element_type=F32))
            head = (jnp.dot(bandh, halo_hi[:, cols], preferred_element_type=F32)
                    + jnp.dot(bandh, halo_lo[:, cols], preferred_element_type=F32))
            win = jnp.concatenate([win[:POOL_HALO] + head, win[POOL_HALO:]], axis=0)
            pooled = win / jnp.minimum(t1, float(w)) - h[:, cols]
            y = jnp.dot(pooled.astype(BF16), wp_ref[g], preferred_element_type=F32) + bp_ref[:, cols]
            o_ref[rows, cols] = x[:, cols] + g_ref[:, cols] * (y * ps_ref[:, cols])


def _pool_mixer(x, gain, sc1p, sh, g1, band, bandh, wp, bp, ps):
    t, d = x.shape
    tm = POOL_ROWS
    vec = pl.BlockSpec((1, d), lambda i: (0, 0))
    return pl.pallas_call(
        _pool_kernel,
        grid=(t // tm,),
        in_specs=[
            pl.BlockSpec((tm, d), lambda i: (i, 0)),
            vec, vec, vec, vec,
            pl.BlockSpec(band.shape, lambda i: (0, 0, 0)),
            pl.BlockSpec(bandh.shape, lambda i: (0, 0, 0)),
            pl.BlockSpec(wp.shape, lambda i: (0, 0, 0)),
            vec, vec,
        ],
        out_specs=pl.BlockSpec((tm, d), lambda i: (i, 0)),
        out_shape=jax.ShapeDtypeStruct((t, d), F32),
        scratch_shapes=[pltpu.VMEM((POOL_HALO, d), F32)],
        compiler_params=_params("arbitrary"),
        name="pool",
    )(x, gain, sc1p, sh, g1, band, bandh, wp, bp, ps)


def _router_kernel(x_ref, gain_ref, sc_ref, sh_ref, w_ref, h_ref, gates_ref, sel_ref):
    h = _norm_mod(x_ref[...], gain_ref[...], sc_ref[...], sh_ref[...])
    h_hi = h.astype(BF16)
    h_ref[...] = h_hi
    h_lo = (h - h_hi.astype(F32)).astype(BF16)
    w = w_ref[...]
    p_hi = jnp.dot(h_hi, w, preferred_element_type=F32)
    p_lo = jnp.dot(h_lo, w, preferred_element_type=F32)
    logits = p_hi[:, :LANES] + p_hi[:, LANES:] + p_lo[:, :LANES] + p_lo[:, LANES:]
    lane = lax.broadcasted_iota(jnp.int32, logits.shape, 1).astype(F32)
    neg = jnp.float32(-jnp.inf)
    lg = jnp.where(lane < float(N_EXPERTS), logits, neg)
    m0 = jnp.max(lg, axis=-1, keepdims=True)
    i0 = jnp.min(jnp.where(lg == m0, lane, float(LANES)), axis=-1, keepdims=True)
    lg1 = jnp.where(lane == i0, neg, lg)
    m1 = jnp.max(lg1, axis=-1, keepdims=True)
    i1 = jnp.min(jnp.where(lg1 == m1, lane, float(LANES)), axis=-1, keepdims=True)
    e1 = jnp.exp(m1 - m0)
    den = 1.0 + e1
    is0 = lane == i0
    is1 = lane == i1
    gates_ref[...] = jnp.where(is0, 1.0 / den, jnp.where(is1, e1 / den, 0.0))
    sel_ref[...] = jnp.where(is0 | is1, 1.0, 0.0).astype(BF16)


def _router(x, gain, sc1p, sh, w_cat):
    t, d = x.shape
    tm = ROW_TILE
    vec = pl.BlockSpec((1, d), lambda i: (0, 0))
    wspec = pl.BlockSpec((d, 2 * LANES), lambda i: (0, 0))
    return pl.pallas_call(
        _router_kernel,
        grid=(t // tm,),
        in_specs=[pl.BlockSpec((tm, d), lambda i: (i, 0)), vec, vec, vec, wspec],
        out_specs=[
            pl.BlockSpec((tm, d), lambda i: (i, 0)),
            pl.BlockSpec((tm, LANES), lambda i: (i, 0)),
            pl.BlockSpec((tm, LANES), lambda i: (i, 0)),
        ],
        out_shape=[
            jax.ShapeDtypeStruct((t, d), BF16),
            jax.ShapeDtypeStruct((t, LANES), F32),
            jax.ShapeDtypeStruct((t, LANES), BF16),
        ],
        compiler_params=_params("arbitrary"),
        name="router",
    )(x, gain, sc1p, sh, w_cat)


def _rank_kernel(sel_ref, tri_ref, rank_ref, tot_ref, carry_ref):
    @pl.when(pl.program_id(0) == 0)
    def _():
        carry_ref[...] = jnp.zeros_like(carry_ref)

    s = sel_ref[...]
    cum = jnp.dot(tri_ref[...], s, preferred_element_type=F32)
    carry = carry_ref[...]
    rank_ref[...] = cum - s.astype(F32) + carry
    tot = carry + cum[cum.shape[0] - 1:, :]
    carry_ref[...] = tot
    tot_ref[0] = tot


def _ranks(sel, tri):
    t = sel.shape[0]
    tm = RANK_ROWS
    return pl.pallas_call(
        _rank_kernel,
        grid=(t // tm,),
        in_specs=[pl.BlockSpec((tm, LANES), lambda i: (i, 0)), pl.BlockSpec((tm, tm), lambda i: (0, 0))],
        out_specs=[pl.BlockSpec((tm, LANES), lambda i: (i, 0)), pl.BlockSpec((1, 1, LANES), lambda i: (i, 0, 0))],
        out_shape=[jax.ShapeDtypeStruct((t, LANES), F32), jax.ShapeDtypeStruct((t // tm, 1, LANES), F32)],
        scratch_shapes=[pltpu.VMEM((1, LANES), F32)],
        compiler_params=_params("arbitrary"),
        name="rank",
    )(sel, tri)


def _gather_kernel(be_ref, ng_ref, clo_ref, chi_ref, h_ref, post_ref, gatest_ref, xb_ref, gs_ref, acc_ref):
    u = pl.program_id(0)
    e = be_ref[u]
    unit_rows = xb_ref.shape[0]
    n_groups = unit_rows // MOE_GATHER_ROWS
    last_chunk = h_ref.shape[0] // GATHER_CHUNK - 1
    g0 = u * n_groups

    @pl.when(ng_ref[g0] > 0)
    def _():
        acc_ref[...] = jnp.zeros_like(acc_ref)
        trips = chi_ref[g0] - clo_ref[g0]
        for q in range(1, n_groups):
            trips = jnp.maximum(trips, chi_ref[g0 + q] - clo_ref[g0 + q])

        def body(i, gsums):
            out = []
            for q in range(n_groups):
                rows = slice(q * MOE_GATHER_ROWS, (q + 1) * MOE_GATHER_ROWS)
                c = clo_ref[g0 + q] + i
                live = c < chi_ref[g0 + q]
                off = pl.multiple_of(jnp.minimum(c, last_chunk) * GATHER_CHUNK, GATHER_CHUNK)
                p = post_ref[pl.ds(e, 1), pl.ds(off, GATHER_CHUNK)]
                p = jnp.where(live, p, -2)
                row_pos = (lax.broadcasted_iota(jnp.int32, (MOE_GATHER_ROWS, GATHER_CHUNK), 0)
                           + (u * unit_rows + q * MOE_GATHER_ROWS))
                match = row_pos == p
                onehot = jnp.where(match, 1.0, 0.0).astype(BF16)
                acc_ref[rows, :] += jnp.dot(onehot, h_ref[pl.ds(off, GATHER_CHUNK), :],
                                            preferred_element_type=F32)
                gate = gatest_ref[pl.ds(e, 1), pl.ds(off, GATHER_CHUNK)]
                out.append(gsums[q] + jnp.sum(jnp.where(match, gate, 0.0), axis=1, keepdims=True))
            return tuple(out)

        zero = jnp.zeros((MOE_GATHER_ROWS, 1), F32)
        gsums = lax.fori_loop(0, trips, body, (zero,) * n_groups)
        xb_ref[...] = acc_ref[...].astype(BF16)
        for q in range(n_groups):
            gs_ref[q * MOE_GATHER_ROWS:(q + 1) * MOE_GATHER_ROWS, :] = gsums[q]

    @pl.when(ng_ref[g0] == 0)
    def _():
        xb_ref[...] = jnp.zeros_like(xb_ref)
        gs_ref[...] = jnp.zeros_like(gs_ref)


def _gather(beu, ng, clo, chi, h, post, gatest):
    t, d = h.shape
    n_units = beu.shape[0]
    grid_spec = pltpu.PrefetchScalarGridSpec(
        num_scalar_prefetch=4,
        grid=(n_units,),
        in_specs=[
            pl.BlockSpec((t, d), lambda u, *_: (0, 0), pipeline_mode=pl.Buffered(1)),
            pl.BlockSpec(post.shape, lambda u, *_: (0, 0), pipeline_mode=pl.Buffered(1)),
            pl.BlockSpec(gatest.shape, lambda u, *_: (0, 0), pipeline_mode=pl.Buffered(1)),
        ],
        out_specs=[
            pl.BlockSpec((MOE_GATHER_UNIT, d), lambda u, *_: (u, 0)),
            pl.BlockSpec((MOE_GATHER_UNIT, 1), lambda u, *_: (u, 0)),
        ],
        scratch_shapes=[pltpu.VMEM((MOE_GATHER_UNIT, d), F32)],
    )
    return pl.pallas_call(
        _gather_kernel,
        grid_spec=grid_spec,
        out_shape=[
            jax.ShapeDtypeStruct((n_units * MOE_GATHER_UNIT, d), BF16),
            jax.ShapeDtypeStruct((n_units * MOE_GATHER_UNIT, 1), F32),
        ],
        compiler_params=_params("arbitrary"),
        name="gather",
    )(beu, ng, clo, chi, h, post, gatest)


def _experts_kernel(be_ref, nv_ref, xb_ref, gs_ref, wg_ref, wu_ref, wd_ref, yb_ref,
                    acc_ref, wgb_ref, wub_ref, wdb_ref):
    b = pl.program_id(0)
    f = pl.program_id(1)
    nf = pl.num_programs(1)
    nv = nv_ref[b]
    block_rows = xb_ref.shape[0]

    @pl.when((b == 0) & (f == 0))
    def _():
        acc_ref[...] = jnp.zeros_like(acc_ref)

    def swiglu_rows(lo, n, cast):
        rows = slice(lo, lo + n)
        if cast:
            wgb_ref[...] = wg_ref[0].astype(BF16)
            wub_ref[...] = wu_ref[0].astype(BF16)
            wdb_ref[...] = wd_ref[0].astype(BF16)
        x = xb_ref[rows, :]
        a = jnp.dot(x, wgb_ref[...], preferred_element_type=F32)
        u = jnp.dot(x, wub_ref[...], preferred_element_type=F32)
        act = (_silu(a) * u).astype(BF16)
        prev = jnp.where(f > 0, acc_ref[rows, :], 0.0)
        acc_ref[rows, :] = prev + jnp.dot(act, wdb_ref[...], preferred_element_type=F32)

    pair = 2 * MOE_DOT_ROWS
    for p in range(block_rows // pair):
        lo = p * pair
        full = nv >= lo + pair

        @pl.when(full)
        def _():
            swiglu_rows(lo, pair, p == 0)

        @pl.when(jnp.logical_not(full) & (nv > lo))
        def _():
            swiglu_rows(lo, MOE_DOT_ROWS, p == 0)

        @pl.when(jnp.logical_not(full) & (nv > lo + MOE_DOT_ROWS))
        def _():
            swiglu_rows(lo + MOE_DOT_ROWS, MOE_DOT_ROWS, False)

    @pl.when(f == nf - 1)
    def _():
        for g in range(block_rows // MOE_DOT_ROWS):
            rows = slice(g * MOE_DOT_ROWS, (g + 1) * MOE_DOT_ROWS)

            @pl.when(nv > g * MOE_DOT_ROWS)
            def _():
                yb_ref[rows, :] = (acc_ref[rows, :] * gs_ref[rows, :]).astype(BF16)

            @pl.when(nv <= g * MOE_DOT_ROWS)
            def _():
                yb_ref[rows, :] = jnp.zeros((MOE_DOT_ROWS, yb_ref.shape[1]), BF16)


def _experts(be, nv, xb, gs, wg, wu, wd):
    d = xb.shape[1]
    n_blocks = be.shape[0]
    d_ff = wg.shape[2]
    tf = MOE_FF_TILE
    nf = d_ff // tf
    bsz = MOE_BLOCK

    def ff_idx(f, nvr, b):
        return jnp.where(nvr[b] > 0, f, nf - 1)

    grid_spec = pltpu.PrefetchScalarGridSpec(
        num_scalar_prefetch=2,
        grid=(n_blocks, nf),
        in_specs=[
            pl.BlockSpec((bsz, d), lambda b, f, *_: (b, 0)),
            pl.BlockSpec((bsz, 1), lambda b, f, *_: (b, 0)),
            pl.BlockSpec((1, d, tf), lambda b, f, be_r, nv_r: (be_r[b], 0, ff_idx(f, nv_r, b))),
            pl.BlockSpec((1, d, tf), lambda b, f, be_r, nv_r: (be_r[b], 0, ff_idx(f, nv_r, b))),
            pl.BlockSpec((1, tf, d), lambda b, f, be_r, nv_r: (be_r[b], ff_idx(f, nv_r, b), 0)),
        ],
        out_specs=pl.BlockSpec((bsz, d), lambda b, f, *_: (b, 0)),
        scratch_shapes=[
            pltpu.VMEM((bsz, d), F32),
            pltpu.VMEM((d, tf), BF16), pltpu.VMEM((d, tf), BF16), pltpu.VMEM((tf, d), BF16),
        ],
    )
    return pl.pallas_call(
        _experts_kernel,
        grid_spec=grid_spec,
        out_shape=jax.ShapeDtypeStruct((n_blocks * bsz, d), BF16),
        compiler_params=_params("arbitrary", "arbitrary"),
        name="experts",
    )(be, nv, xb, gs, wg, wu, wd)


def _combine_kernel(win_ref, x_ref, pos_ref, g_ref, fg_ref, *refs):
    yb_refs, o_ref = refs[:-1], refs[-1]
    i = pl.program_id(0)
    n_sub = x_ref.shape[0] // COMBINE_ROWS
    lane = lax.broadcasted_iota(jnp.int32, (COMBINE_ROWS, 2 * COMBINE_ROWS), 1)
    for s in range(n_sub):
        rows = slice(s * COMBINE_ROWS, (s + 1) * COMBINE_ROWS)
        onehots, windows = [], []
        for e in range(N_EXPERTS):
            k = (i * n_sub + s) * N_EXPERTS + e
            base = win_ref[k] * COMBINE_ROWS
            rel = pos_ref[rows, e:e + 1] - base
            onehots.append(jnp.where(lane == rel, 1.0, 0.0).astype(BF16))
            w0 = 2 * (s * N_EXPERTS + e)
            windows += [yb_refs[w0][...], yb_refs[w0 + 1][...]]
        acc = jnp.dot(jnp.concatenate(onehots, axis=1), jnp.concatenate(windows, axis=0),
                      preferred_element_type=F32)
        xo = x_ref[rows, :] + g_ref[...] * acc
        ms = jnp.mean(xo * xo, axis=-1, keepdims=True)
        o_ref[rows, :] = xo * lax.rsqrt(ms + EPS) * fg_ref[...]


def _combine(win, x, pos, g2, fgain, yb):
    t, d = x.shape
    tm = COMBINE_TILE
    n_sub = tm // COMBINE_ROWS
    n_win = yb.shape[0] // COMBINE_ROWS
    vec = pl.BlockSpec((1, d), lambda i, w: (0, 0))

    def first(s, e):
        return lambda i, w: (jnp.minimum(w[(i * n_sub + s) * N_EXPERTS + e], n_win - 1), 0)

    def second(s, e):
        return lambda i, w: (jnp.minimum(w[(i * n_sub + s) * N_EXPERTS + e] + 1, n_win - 1), 0)

    yb_specs = []
    for s in range(n_sub):
        for e in range(N_EXPERTS):
            yb_specs.append(pl.BlockSpec((COMBINE_ROWS, d), first(s, e)))
            yb_specs.append(pl.BlockSpec((COMBINE_ROWS, d), second(s, e)))
    grid_spec = pltpu.PrefetchScalarGridSpec(
        num_scalar_prefetch=1,
        grid=(t // tm,),
        in_specs=[
            pl.BlockSpec((tm, d), lambda i, w: (i, 0)),
            pl.BlockSpec((tm, N_EXPERTS), lambda i, w: (i, 0)),
            vec, vec,
        ] + yb_specs,
        out_specs=pl.BlockSpec((tm, d), lambda i, w: (i, 0)),
    )
    return pl.pallas_call(
        _combine_kernel,
        grid_spec=grid_spec,
        out_shape=jax.ShapeDtypeStruct((t, d), F32),
        compiler_params=_params("arbitrary"),
        name="combine",
    )(win, x, pos, g2, fgain, *([yb] * len(yb_specs)))


def _routing_tables(rank, tot, sel):
    t = rank.shape[0]
    bsz = MOE_BLOCK
    n_blocks = (2 * t) // bsz + N_EXPERTS
    cum = tot[:, 0, :N_EXPERTS].astype(jnp.int32)
    counts = cum[-1]
    padded = ((counts + bsz - 1) // bsz) * bsz
    pad_end = jnp.cumsum(padded)
    start_pad = pad_end - padded
    selected = sel[:, :N_EXPERTS] > 0
    pos = jnp.where(selected, start_pad[None, :] + rank[:, :N_EXPERTS].astype(jnp.int32), -1)

    bstart = jnp.arange(n_blocks, dtype=jnp.int32) * bsz
    be = jnp.minimum(jnp.sum(bstart[:, None] >= pad_end[None, :], axis=1), N_EXPERTS - 1).astype(jnp.int32)
    r0 = bstart - start_pad[be]
    nv = jnp.clip(counts[be] - r0, 0, bsz).astype(jnp.int32)
    n_gather = bsz // MOE_GATHER_ROWS
    goff = jnp.arange(n_gather, dtype=jnp.int32) * MOE_GATHER_ROWS
    r0g = (r0[:, None] + goff[None, :]).reshape(-1)
    ng = jnp.clip(nv[:, None] - goff[None, :], 0, MOE_GATHER_ROWS).reshape(-1)
    beg = jnp.repeat(be, n_gather)
    chunk_end = cum[:, beg].T
    chunk_start = jnp.concatenate([jnp.zeros((1, N_EXPERTS), jnp.int32), cum[:-1]], axis=0)[:, beg].T
    clo = jnp.sum(chunk_end <= r0g[:, None], axis=1).astype(jnp.int32)
    chi = jnp.sum(chunk_start < (r0g + ng)[:, None], axis=1).astype(jnp.int32)
    beu = jnp.repeat(be, bsz // MOE_GATHER_UNIT)

    big = jnp.int32(2 ** 30)
    first = jnp.min(jnp.where(pos >= 0, pos, big).reshape(t // COMBINE_ROWS, COMBINE_ROWS, N_EXPERTS), axis=1)
    win = jnp.where(first == big, 0, first // COMBINE_ROWS).astype(jnp.int32).reshape(-1)
    return pos, be, nv, beu, ng, clo, chi, win


def _rotary_tables(seq, dk):
    half = dk // 2
    inv = ROPE_BASE ** (-jnp.arange(half, dtype=F32) / half)
    ang_a = (jnp.arange(seq // ROT_BLOCK) * ROT_BLOCK).astype(F32)[:, None] * inv[None, :]
    ang_b = jnp.arange(ROT_BLOCK).astype(F32)[:, None] * inv[None, :]
    return jnp.cos(ang_a), jnp.sin(ang_a), jnp.cos(ang_b), jnp.sin(ang_b)


def _decay_tables(dk, dv):
    c = RET_CHUNK
    log_gamma = jnp.log1p(-jnp.exp2(-5.0 - jnp.arange(N_HEADS, dtype=F32)))
    idx = jnp.arange(c, dtype=F32)
    row_dec = jnp.exp(log_gamma[:, None] * (idx + 1.0 - c))
    causal = (idx[:, None] >= idx[None, :]).astype(F32)
    dmask = row_dec[:, :, None] * causal[None]
    k_dec = jnp.exp(log_gamma[:, None] * (c - 1.0 - idx)) * (dk ** -0.5)
    q_dec = jnp.exp(log_gamma[:, None] * (idx + 1.0))
    chunk_dec = jnp.exp(log_gamma * c)
    half = dk // 2
    kdec_tile = jnp.tile(jnp.repeat(k_dec.T, half, axis=1), (ROW_TILE // c, 1))
    qdec_tile = jnp.broadcast_to(q_dec[:, :, None], (N_HEADS, c, dk))
    return dmask, kdec_tile, qdec_tile, chunk_dec


def _pool_bands(tm):
    def bands(lag):
        return jnp.stack([((lag >= 0) & (lag < w)) for w in POOL_WINDOWS]).astype(BF16)
    main = bands(jnp.arange(tm)[:, None] - jnp.arange(tm)[None, :])
    head = bands(jnp.arange(POOL_HALO)[:, None] + POOL_HALO - jnp.arange(POOL_HALO)[None, :])
    return main, head


def kernel(x, c, ada_w, ada_b, norm_gain, ret_w_in, ret_gn_gain, ret_w_out, ffn_w_gate, ffn_w_up, ffn_w_down,
           pool_w, pool_b, pool_scale, moe_router, moe_w_gate, moe_w_up, moe_w_down, final_norm_gain):
    bsz, seq, d = x.shape
    assert bsz == 1 and ada_w.shape[0] == 2
    xt = x.reshape(seq, d)
    qk_total = d
    v_total = ret_w_out.shape[1]
    dk = qk_total // N_HEADS
    dv = v_total // N_HEADS

    mod = _modulation(c, ada_w, ada_b)
    def mods(i):
        parts = [mod[i, :, k * d:(k + 1) * d] for k in range(N_MOD)]
        sh1, sc1, g1, sh2, sc2, g2 = parts
        return sh1, 1.0 + sc1, g1, sh2, 1.0 + sc2, g2

    sh1, sc1p, g1, sh2, sc2p, g2 = mods(0)
    dmask, kdec_tile, qdec_tile, chunk_dec = _decay_tables(dk, dv)
    proj = _ret_proj(xt, norm_gain[0, 0][None], sc1p, sh1, ret_w_in[0].astype(BF16), _rotary_tables(seq, dk),
                     kdec_tile, qk_total, v_total)
    y = _retention_core(proj, chunk_dec, dmask, qdec_tile, ret_gn_gain[0][None], qk_total, v_total)
    x1 = _ret_out(y, ret_w_out[0].astype(BF16), xt, g1)
    x2 = _dense_ffn(x1, norm_gain[0, 1][None], sc2p, sh2, g2,
                    ffn_w_gate[0].astype(BF16), ffn_w_up[0].astype(BF16), ffn_w_down[0].astype(BF16))

    sh1, sc1p, g1, sh2, sc2p, g2 = mods(1)
    band, bandh = _pool_bands(POOL_SUB_ROWS)
    x3 = _pool_mixer(x2, norm_gain[1, 0][None], sc1p, sh1, g1, band, bandh,
                     pool_w[0].astype(BF16), pool_b[0].reshape(1, d), pool_scale[0][None])

    wr = jnp.pad(moe_router[0], ((0, 0), (0, LANES - N_EXPERTS)))
    wr_hi = wr.astype(BF16)
    wr_lo = (wr - wr_hi.astype(F32)).astype(BF16)
    h4, gates, sel = _router(x3, norm_gain[1, 1][None], sc2p, sh2, jnp.concatenate([wr_hi, wr_lo], axis=1))
    tri = (jnp.arange(RANK_ROWS)[:, None] >= jnp.arange(RANK_ROWS)[None, :]).astype(BF16)
    rank, tot = _ranks(sel, tri)
    pos, be, nv, beu, ng, clo, chi, win = _routing_tables(rank, tot, sel)
    xb, gs = _gather(beu, ng, clo, chi, h4, pos.T, gates[:, :N_EXPERTS].T)
    yb = _experts(be, nv, xb, gs, moe_w_gate[0], moe_w_up[0], moe_w_down[0])
    out = _combine(win, x3, pos, g2, final_norm_gain[None], yb)
    return out.reshape(bsz, seq, d)
```

```python
import functools

import jax
import jax.numpy as jnp
from jax import lax
from jax.experimental import pallas as pl
from jax.experimental.pallas import tpu as pltpu

F32 = jnp.float32
BF16 = jnp.bfloat16

EPS = 1e-6
N_HEADS = 4
RET_CHUNK = 256
ROPE_BASE = 10000.0
ROT_BLOCK = 128
POOL_WINDOWS = (2, 4, 8, 16)
POOL_HALO = 16
N_EXPERTS = 8
N_MOD = 6

VMEM_LIMIT_BYTES = 56 * 1024 * 1024

ROW_TILE = 1024
PROJ_COL_TILE = 3072
RET_ROWS = 1024
POOL_ROWS = 1024
POOL_SUB_ROWS = 256
RANK_ROWS = 512
MOE_BLOCK = 2048
MOE_DOT_ROWS = 512
MOE_GATHER_UNIT = 1024
MOE_GATHER_ROWS = 256
MOE_FF_TILE = 512
GATHER_CHUNK = 512
COMBINE_TILE = 512
COMBINE_ROWS = 128
LANES = 128


def _silu(v):
    return v / (1.0 + jnp.exp(-v))


def _norm_mod(x, gain, scale1p, shift):
    ms = jnp.mean(x * x, axis=-1, keepdims=True)
    return (x * lax.rsqrt(ms + EPS) * gain) * scale1p + shift


def _params(*sem):
    return pltpu.CompilerParams(dimension_semantics=sem, vmem_limit_bytes=VMEM_LIMIT_BYTES)


def _resident(shape):
    nd = len(shape)
    return pl.BlockSpec(shape, lambda *_: (0,) * nd, pipeline_mode=pl.Buffered(1))


def _mod_kernel(c_ref, w_ref, b_ref, o_ref):
    ca = _silu(c_ref[...])
    o_ref[0] = jnp.sum(ca * w_ref[0], axis=0, keepdims=True) + b_ref[0]


def _modulation(c, ada_w, ada_b):
    depth, d, n = ada_w.shape
    tn = 1024
    return pl.pallas_call(
        _mod_kernel,
        grid=(depth, n // tn),
        in_specs=[
            pl.BlockSpec((d, 1), lambda i, j: (0, 0)),
            pl.BlockSpec((1, d, tn), lambda i, j: (i, 0, j)),
            pl.BlockSpec((1, 1, tn), lambda i, j: (i, 0, j)),
        ],
        out_specs=pl.BlockSpec((1, 1, tn), lambda i, j: (i, 0, j)),
        out_shape=jax.ShapeDtypeStruct((depth, 1, n), F32),
        compiler_params=_params("arbitrary", "arbitrary"),
        name="mod",
    )(c.reshape(d, 1), ada_w, ada_b.reshape(depth, 1, n))


def _proj_kernel(x_ref, gain_ref, sc_ref, sh_ref, w_ref, cosa_ref, sina_ref, cosb_ref, sinb_ref, kdec_ref, *refs,
                 n_qk_chunks, n_v_chunks, n_col_tiles, dk, n_cast):
    cast_in = refs[:n_cast]
    o_ref = refs[n_cast]
    cast_out = refs[n_cast + 1:2 * n_cast + 1]
    h_ref, cos_ref, sin_ref = refs[2 * n_cast + 1:]
    j = pl.program_id(1)

    @pl.when(j == 0)
    def _():
        h_ref[...] = _norm_mod(x_ref[...], gain_ref[...], sc_ref[...], sh_ref[...]).astype(BF16)
        for src, dst in zip(cast_in, cast_out):
            dst[...] = src[...].astype(BF16)
        cb = cosb_ref[...]
        sb = sinb_ref[...]
        for a in range(cosa_ref.shape[0]):
            ca = cosa_ref[a:a + 1, :]
            sa = sina_ref[a:a + 1, :]
            cos_ref[a * ROT_BLOCK:(a + 1) * ROT_BLOCK, :] = ca * cb - sa * sb
            sin_ref[a * ROT_BLOCK:(a + 1) * ROT_BLOCK, :] = sa * cb + ca * sb

    half = dk // 2
    n_chunks = w_ref.shape[1] // dk

    def chunk(c, kind, head):
        acc = jnp.dot(h_ref[...], w_ref[:, c * dk:(c + 1) * dk], preferred_element_type=F32)
        if kind == "v":
            o_ref[:, c * dk:(c + 1) * dk] = acc.astype(BF16)
        elif kind == "g":
            o_ref[:, c * dk:(c + 1) * dk] = _silu(acc).astype(BF16)
        else:
            cos = cos_ref[...]
            sin = sin_ref[...]
            t1 = acc[:, :half]
            t2 = acc[:, half:]
            o1 = t1 * cos - t2 * sin
            o2 = t1 * sin + t2 * cos
            if kind == "k":
                s = kdec_ref[:, head * half:(head + 1) * half]
                o1 = o1 * s
                o2 = o2 * s
            o_ref[:, c * dk:c * dk + half] = o1.astype(BF16)
            o_ref[:, c * dk + half:(c + 1) * dk] = o2.astype(BF16)

    def kind_of(gc):
        if gc < n_qk_chunks:
            return "q", gc
        if gc < 2 * n_qk_chunks:
            return "k", gc - n_qk_chunks
        if gc < 2 * n_qk_chunks + n_v_chunks:
            return "v", 0
        return "g", 0

    for jj in range(n_col_tiles):
        @pl.when(j == jj)
        def _():
            for c in range(n_chunks):
                chunk(c, *kind_of(jj * n_chunks + c))


def _ret_proj(x, gain, sc1p, sh, w_in, rot, kdec, qk_total, v_total, to_cast):
    t, d = x.shape
    n = w_in.shape[1]
    tm, tn = ROW_TILE, PROJ_COL_TILE
    n_row_tiles = t // tm
    dk = qk_total // N_HEADS
    half = dk // 2
    cosa, sina, cosb, sinb = rot
    kern = functools.partial(_proj_kernel, n_qk_chunks=qk_total // dk, n_v_chunks=v_total // dk,
                             n_col_tiles=n // tn, dk=dk, n_cast=len(to_cast))
    vec = pl.BlockSpec((1, d), lambda i, j: (0, 0))
    rot_a = pl.BlockSpec((tm // ROT_BLOCK, half), lambda i, j: (i, 0))
    rot_b = pl.BlockSpec((ROT_BLOCK, half), lambda i, j: (0, 0))
    cast_specs = [pl.BlockSpec((w.shape[0] // n_row_tiles, w.shape[1]), lambda i, j: (i, 0)) for w in to_cast]
    outs = pl.pallas_call(
        kern,
        grid=(n_row_tiles, n // tn),
        in_specs=[
            pl.BlockSpec((tm, d), lambda i, j: (i, 0)),
            vec, vec, vec,
            pl.BlockSpec((d, tn), lambda i, j: (0, j)),
            rot_a, rot_a, rot_b, rot_b,
            pl.BlockSpec((tm, N_HEADS * half), lambda i, j: (0, 0)),
        ] + cast_specs,
        out_specs=[pl.BlockSpec((tm, tn), lambda i, j: (i, j))] + cast_specs,
        out_shape=[jax.ShapeDtypeStruct((t, n), BF16)] + [jax.ShapeDtypeStruct(w.shape, BF16) for w in to_cast],
        scratch_shapes=[pltpu.VMEM((tm, d), BF16), pltpu.VMEM((tm, half), F32), pltpu.VMEM((tm, half), F32)],
        compiler_params=_params("arbitrary", "arbitrary"),
        name="proj",
    )(x, gain, sc1p, sh, w_in, cosa, sina, cosb, sinb, kdec, *to_cast)
    return outs[0], outs[1:]


def _ret_kernel(cdec_ref, q_ref, k_ref, v_ref, sg_ref, dmask_ref, qdec_ref, gn_ref, y_ref, state_ref,
                *, dk, dv):
    @pl.when(pl.program_id(0) == 0)
    def _():
        state_ref[...] = jnp.zeros_like(state_ref)

    rows_per_step = q_ref.shape[0]
    for n in range(rows_per_step // RET_CHUNK):
        rows = slice(n * RET_CHUNK, (n + 1) * RET_CHUNK)
        for h in range(N_HEADS):
            qcols = slice(h * dk, (h + 1) * dk)
            vcols = slice(h * dv, (h + 1) * dv)
            q = q_ref[rows, qcols]
            ks = k_ref[rows, qcols]
            v = v_ref[rows, vcols]
            s = lax.dot_general(q, ks, (((1,), (1,)), ((), ())), preferred_element_type=F32)
            a = (s * dmask_ref[h]).astype(BF16)
            intra = jnp.dot(a, v, preferred_element_type=F32)
            st = state_ref[h]
            cross = jnp.dot(q, st.astype(BF16), preferred_element_type=F32)
            o = intra + cross * qdec_ref[h]
            kv = lax.dot_general(ks, v, (((0,), (0,)), ((), ())), preferred_element_type=F32)
            state_ref[h] = st * cdec_ref[h] + kv
            mu = jnp.mean(o, axis=-1, keepdims=True)
            dlt = o - mu
            var = jnp.mean(dlt * dlt, axis=-1, keepdims=True)
            on = dlt * lax.rsqrt(var + EPS)
            y = on * gn_ref[:, vcols] * sg_ref[rows, vcols].astype(F32)
            y_ref[rows, vcols] = y.astype(BF16)


def _retention_core(proj, cdec, dmask, qdec, gn_gain, qk_total, v_total):
    t = proj.shape[0]
    r = RET_ROWS
    dk = qk_total // N_HEADS
    dv = v_total // N_HEADS
    assert v_total == 2 * qk_total
    kern = functools.partial(_ret_kernel, dk=dk, dv=dv)
    return pl.pallas_call(
        kern,
        grid=(t // r,),
        in_specs=[
            pl.BlockSpec(memory_space=pltpu.SMEM),
            pl.BlockSpec((r, qk_total), lambda i: (i, 0)),
            pl.BlockSpec((r, qk_total), lambda i: (i, 1)),
            pl.BlockSpec((r, v_total), lambda i: (i, 1)),
            pl.BlockSpec((r, v_total), lambda i: (i, 2)),
            pl.BlockSpec((N_HEADS, RET_CHUNK, RET_CHUNK), lambda i: (0, 0, 0)),
            pl.BlockSpec((N_HEADS, RET_CHUNK, dv), lambda i: (0, 0, 0)),
            pl.BlockSpec((1, v_total), lambda i: (0, 0)),
        ],
        out_specs=pl.BlockSpec((r, v_total), lambda i: (i, 0)),
        out_shape=jax.ShapeDtypeStruct((t, v_total), BF16),
        scratch_shapes=[pltpu.VMEM((N_HEADS, dk, dv), F32)],
        compiler_params=_params("arbitrary"),
        name="ret",
    )(cdec, proj, proj, proj, proj, dmask, qdec, gn_gain)


def _out_kernel(y_ref, w_ref, x_ref, g_ref, o_ref):
    m = jnp.dot(y_ref[...], w_ref[...], preferred_element_type=F32)
    o_ref[...] = x_ref[...] + g_ref[...] * m


def _ret_out(y, w_out, x, g1):
    t, d = x.shape
    kdim = y.shape[1]
    tm = ROW_TILE
    return pl.pallas_call(
        _out_kernel,
        grid=(t // tm,),
        in_specs=[
            pl.BlockSpec((tm, kdim), lambda i: (i, 0)),
            _resident((kdim, d)),
            pl.BlockSpec((tm, d), lambda i: (i, 0)),
            pl.BlockSpec((1, d), lambda i: (0, 0)),
        ],
        out_specs=pl.BlockSpec((tm, d), lambda i: (i, 0)),
        out_shape=jax.ShapeDtypeStruct((t, d), F32),
        compiler_params=_params("arbitrary"),
        name="out",
    )(y, w_out, x, g1)


def _ffn_kernel(x_ref, gain_ref, sc_ref, sh_ref, g_ref, wg_ref, wu_ref, wd_ref, o_ref, *, ff_tile):
    x = x_ref[...]
    h = _norm_mod(x, gain_ref[...], sc_ref[...], sh_ref[...]).astype(BF16)
    d_ff = wg_ref.shape[1]
    acc = jnp.zeros(x.shape, F32)
    for lo in range(0, d_ff, ff_tile):
        hi = min(lo + ff_tile, d_ff)
        a = jnp.dot(h, wg_ref[:, lo:hi], preferred_element_type=F32)
        b = jnp.dot(h, wu_ref[:, lo:hi], preferred_element_type=F32)
        act = (_silu(a) * b).astype(BF16)
        acc = acc + jnp.dot(act, wd_ref[lo:hi, :], preferred_element_type=F32)
    o_ref[...] = x + g_ref[...] * acc


def _dense_ffn(x, gain, sc1p, sh, g2, wg, wu, wd):
    t, d = x.shape
    d_ff = wg.shape[1]
    tm = ROW_TILE
    vec = pl.BlockSpec((1, d), lambda i: (0, 0))
    return pl.pallas_call(
        functools.partial(_ffn_kernel, ff_tile=512),
        grid=(t // tm,),
        in_specs=[
            pl.BlockSpec((tm, d), lambda i: (i, 0)),
            vec, vec, vec, vec,
            _resident((d, d_ff)), _resident((d, d_ff)), _resident((d_ff, d)),
        ],
        out_specs=pl.BlockSpec((tm, d), lambda i: (i, 0)),
        out_shape=jax.ShapeDtypeStruct((t, d), F32),
        compiler_params=_params("arbitrary"),
        name="ffn",
    )(x, gain, sc1p, sh, g2, wg, wu, wd)


def _pool_kernel(x_ref, gain_ref, sc_ref, sh_ref, g_ref, band_ref, bandh_ref, wp_ref, bp_ref, ps_ref, o_ref,
                 halo_ref):
    i = pl.program_id(0)
    tm = x_ref.shape[0]
    gw = wp_ref.shape[1]

    @pl.when(i == 0)
    def _():
        halo_ref[...] = jnp.zeros_like(halo_ref)

    def split(v):
        hi = v.astype(BF16)
        return hi, (v - hi.astype(F32)).astype(BF16)

    sub = band_ref.shape[1]
    for s in range(tm // sub):
        rows = slice(s * sub, (s + 1) * sub)
        x = x_ref[rows, :]
        h = _norm_mod(x, gain_ref[...], sc_ref[...], sh_ref[...])
        h_hi, h_lo = split(h)
        halo_hi, halo_lo = split(halo_ref[...])
        halo_ref[...] = h[sub - POOL_HALO:, :]
        t1 = (lax.broadcasted_iota(jnp.int32, (sub, 1), 0) + (i * tm + s * sub + 1)).astype(F32)
        for g, w in enumerate(POOL_WINDOWS):
            cols = slice(g * gw, (g + 1) * gw)
            band = band_ref[g]
            bandh = bandh_ref[g]
            win = (jnp.dot(band, h_hi[:, cols], preferred_element_type=F32)
                   + jnp.dot(band, h_lo[:, cols], preferred_element_type=F32))
            head = (jnp.dot(bandh, halo_hi[:, cols], preferred_element_type=F32)
                    + jnp.dot(bandh, halo_lo[:, cols], preferred_element_type=F32))
            win = jnp.concatenate([win[:POOL_HALO] + head, win[POOL_HALO:]], axis=0)
            pooled = win / jnp.minimum(t1, float(w)) - h[:, cols]
            y = jnp.dot(pooled.astype(BF16), wp_ref[g], preferred_element_type=F32) + bp_ref[:, cols]
            o_ref[rows, cols] = x[:, cols] + g_ref[:, cols] * (y * ps_ref[:, cols])


def _pool_mixer(x, gain, sc1p, sh, g1, band, bandh, wp, bp, ps):
    t, d = x.shape
    tm = POOL_ROWS
    vec = pl.BlockSpec((1, d), lambda i: (0, 0))
    return pl.pallas_call(
        _pool_kernel,
        grid=(t // tm,),
        in_specs=[
            pl.BlockSpec((tm, d), lambda i: (i, 0)),
            vec, vec, vec, vec,
            pl.BlockSpec(band.shape, lambda i: (0, 0, 0)),
            pl.BlockSpec(bandh.shape, lambda i: (0, 0, 0)),
            pl.BlockSpec(wp.shape, lambda i: (0, 0, 0)),
            vec, vec,
        ],
        out_specs=pl.BlockSpec((tm, d), lambda i: (i, 0)),
        out_shape=jax.ShapeDtypeStruct((t, d), F32),
        scratch_shapes=[pltpu.VMEM((POOL_HALO, d), F32)],
        compiler_params=_params("arbitrary"),
        name="pool",
    )(x, gain, sc1p, sh, g1, band, bandh, wp, bp, ps)


def _router_kernel(x_ref, gain_ref, sc_ref, sh_ref, w_ref, h_ref, gates_ref, sel_ref):
    h = _norm_mod(x_ref[...], gain_ref[...], sc_ref[...], sh_ref[...])
    h_hi = h.astype(BF16)
    h_ref[...] = h_hi
    h_lo = (h - h_hi.astype(F32)).astype(BF16)
    w = w_ref[...]
    p_hi = jnp.dot(h_hi, w, preferred_element_type=F32)
    p_lo = jnp.dot(h_lo, w, preferred_element_type=F32)
    logits = p_hi[:, :LANES] + p_hi[:, LANES:] + p_lo[:, :LANES] + p_lo[:, LANES:]
    lane = lax.broadcasted_iota(jnp.int32, logits.shape, 1).astype(F32)
    neg = jnp.float32(-jnp.inf)
    lg = jnp.where(lane < float(N_EXPERTS), logits, neg)
    m0 = jnp.max(lg, axis=-1, keepdims=True)
    i0 = jnp.min(jnp.where(lg == m0, lane, float(LANES)), axis=-1, keepdims=True)
    lg1 = jnp.where(lane == i0, neg, lg)
    m1 = jnp.max(lg1, axis=-1, keepdims=True)
    i1 = jnp.min(jnp.where(lg1 == m1, lane, float(LANES)), axis=-1, keepdims=True)
    e1 = jnp.exp(m1 - m0)
    den = 1.0 + e1
    is0 = lane == i0
    is1 = lane == i1
    gates_ref[...] = jnp.where(is0, 1.0 / den, jnp.where(is1, e1 / den, 0.0))
    sel_ref[...] = jnp.where(is0 | is1, 1.0, 0.0).astype(BF16)


def _router(x, gain, sc1p, sh, w_cat):
    t, d = x.shape
    tm = ROW_TILE
    vec = pl.BlockSpec((1, d), lambda i: (0, 0))
    wspec = pl.BlockSpec((d, 2 * LANES), lambda i: (0, 0))
    return pl.pallas_call(
        _router_kernel,
        grid=(t // tm,),
        in_specs=[pl.BlockSpec((tm, d), lambda i: (i, 0)), vec, vec, vec, wspec],
        out_specs=[
            pl.BlockSpec((tm, d), lambda i: (i, 0)),
            pl.BlockSpec((tm, LANES), lambda i: (i, 0)),
            pl.BlockSpec((tm, LANES), lambda i: (i, 0)),
        ],
        out_shape=[
            jax.ShapeDtypeStruct((t, d), BF16),
            jax.ShapeDtypeStruct((t, LANES), F32),
            jax.ShapeDtypeStruct((t, LANES), BF16),
        ],
        compiler_params=_params("arbitrary"),
        name="router",
    )(x, gain, sc1p, sh, w_cat)


def _rank_kernel(sel_ref, tri_ref, rank_ref, tot_ref, carry_ref):
    @pl.when(pl.program_id(0) == 0)
    def _():
        carry_ref[...] = jnp.zeros_like(carry_ref)

    s = sel_ref[...]
    cum = jnp.dot(tri_ref[...], s, preferred_element_type=F32)
    carry = carry_ref[...]
    rank_ref[...] = cum - s.astype(F32) + carry
    tot = carry + cum[cum.shape[0] - 1:, :]
    carry_ref[...] = tot
    tot_ref[0] = tot


def _ranks(sel, tri):
    t = sel.shape[0]
    tm = RANK_ROWS
    return pl.pallas_call(
        _rank_kernel,
        grid=(t // tm,),
        in_specs=[pl.BlockSpec((tm, LANES), lambda i: (i, 0)), pl.BlockSpec((tm, tm), lambda i: (0, 0))],
        out_specs=[pl.BlockSpec((tm, LANES), lambda i: (i, 0)), pl.BlockSpec((1, 1, LANES), lambda i: (i, 0, 0))],
        out_shape=[jax.ShapeDtypeStruct((t, LANES), F32), jax.ShapeDtypeStruct((t // tm, 1, LANES), F32)],
        scratch_shapes=[pltpu.VMEM((1, LANES), F32)],
        compiler_params=_params("arbitrary"),
        name="rank",
    )(sel, tri)


def _gather_kernel(be_ref, ng_ref, clo_ref, chi_ref, h_ref, post_ref, gatest_ref, xb_ref, gs_ref, acc_ref):
    u = pl.program_id(0)
    e = be_ref[u]
    unit_rows = xb_ref.shape[0]
    n_groups = unit_rows // MOE_GATHER_ROWS
    last_chunk = h_ref.shape[0] // GATHER_CHUNK - 1
    g0 = u * n_groups

    @pl.when(ng_ref[g0] > 0)
    def _():
        acc_ref[...] = jnp.zeros_like(acc_ref)
        trips = chi_ref[g0] - clo_ref[g0]
        for q in range(1, n_groups):
            trips = jnp.maximum(trips, chi_ref[g0 + q] - clo_ref[g0 + q])

        def body(i, gsums):
            out = []
            for q in range(n_groups):
                rows = slice(q * MOE_GATHER_ROWS, (q + 1) * MOE_GATHER_ROWS)
                c = clo_ref[g0 + q] + i
                live = c < chi_ref[g0 + q]
                off = pl.multiple_of(jnp.minimum(c, last_chunk) * GATHER_CHUNK, GATHER_CHUNK)
                p = post_ref[pl.ds(e, 1), pl.ds(off, GATHER_CHUNK)]
                p = jnp.where(live, p, -2)
                row_pos = (lax.broadcasted_iota(jnp.int32, (MOE_GATHER_ROWS, GATHER_CHUNK), 0)
                           + (u * unit_rows + q * MOE_GATHER_ROWS))
                match = row_pos == p
                onehot = jnp.where(match, 1.0, 0.0).astype(BF16)
                acc_ref[rows, :] += jnp.dot(onehot, h_ref[pl.ds(off, GATHER_CHUNK), :],
                                            preferred_element_type=F32)
                gate = gatest_ref[pl.ds(e, 1), pl.ds(off, GATHER_CHUNK)]
                out.append(gsums[q] + jnp.sum(jnp.where(match, gate, 0.0), axis=1, keepdims=True))
            return tuple(out)

        zero = jnp.zeros((MOE_GATHER_ROWS, 1), F32)
        gsums = lax.fori_loop(0, trips, body, (zero,) * n_groups)
        xb_ref[...] = acc_ref[...].astype(BF16)
        for q in range(n_groups):
            gs_ref[q * MOE_GATHER_ROWS:(q + 1) * MOE_GATHER_ROWS, :] = gsums[q]

    @pl.when(ng_ref[g0] == 0)
    def _():
        xb_ref[...] = jnp.zeros_like(xb_ref)
        gs_ref[...] = jnp.zeros_like(gs_ref)


def _gather(beu, ng, clo, chi, h, post, gatest):
    t, d = h.shape
    n_units = beu.shape[0]
    grid_spec = pltpu.PrefetchScalarGridSpec(
        num_scalar_prefetch=4,
        grid=(n_units,),
        in_specs=[
            pl.BlockSpec((t, d), lambda u, *_: (0, 0), pipeline_mode=pl.Buffered(1)),
            pl.BlockSpec(post.shape, lambda u, *_: (0, 0), pipeline_mode=pl.Buffered(1)),
            pl.BlockSpec(gatest.shape, lambda u, *_: (0, 0), pipeline_mode=pl.Buffered(1)),
        ],
        out_specs=[
            pl.BlockSpec((MOE_GATHER_UNIT, d), lambda u, *_: (u, 0)),
            pl.BlockSpec((MOE_GATHER_UNIT, 1), lambda u, *_: (u, 0)),
        ],
        scratch_shapes=[pltpu.VMEM((MOE_GATHER_UNIT, d), F32)],
    )
    return pl.pallas_call(
        _gather_kernel,
        grid_spec=grid_spec,
        out_shape=[
            jax.ShapeDtypeStruct((n_units * MOE_GATHER_UNIT, d), BF16),
            jax.ShapeDtypeStruct((n_units * MOE_GATHER_UNIT, 1), F32),
        ],
        compiler_params=_params("arbitrary"),
        name="gather",
    )(beu, ng, clo, chi, h, post, gatest)


def _experts_kernel(be_ref, nv_ref, xb_ref, gs_ref, wg_ref, wu_ref, wd_ref, yb_ref,
                    acc_ref, wgb_ref, wub_ref, wdb_ref):
    b = pl.program_id(0)
    f = pl.program_id(1)
    nf = pl.num_programs(1)
    nv = nv_ref[b]
    block_rows = xb_ref.shape[0]

    @pl.when((b == 0) & (f == 0))
    def _():
        acc_ref[...] = jnp.zeros_like(acc_ref)

    def swiglu_rows(lo, n, cast):
        rows = slice(lo, lo + n)
        if cast:
            wgb_ref[...] = wg_ref[0].astype(BF16)
            wub_ref[...] = wu_ref[0].astype(BF16)
            wdb_ref[...] = wd_ref[0].astype(BF16)
        x = xb_ref[rows, :]
        a = jnp.dot(x, wgb_ref[...], preferred_element_type=F32)
        u = jnp.dot(x, wub_ref[...], preferred_element_type=F32)
        act = (_silu(a) * u).astype(BF16)
        prev = jnp.where(f > 0, acc_ref[rows, :], 0.0)
        acc_ref[rows, :] = prev + jnp.dot(act, wdb_ref[...], preferred_element_type=F32)

    pair = 2 * MOE_DOT_ROWS
    for p in range(block_rows // pair):
        lo = p * pair
        full = nv >= lo + pair

        @pl.when(full)
        def _():
            swiglu_rows(lo, pair, p == 0)

        @pl.when(jnp.logical_not(full) & (nv > lo))
        def _():
            swiglu_rows(lo, MOE_DOT_ROWS, p == 0)

        @pl.when(jnp.logical_not(full) & (nv > lo + MOE_DOT_ROWS))
        def _():
            swiglu_rows(lo + MOE_DOT_ROWS, MOE_DOT_ROWS, False)

    @pl.when(f == nf - 1)
    def _():
        for g in range(block_rows // MOE_DOT_ROWS):
            rows = slice(g * MOE_DOT_ROWS, (g + 1) * MOE_DOT_ROWS)

            @pl.when(nv > g * MOE_DOT_ROWS)
            def _():
                yb_ref[rows, :] = (acc_ref[rows, :] * gs_ref[rows, :]).astype(BF16)

            @pl.when(nv <= g * MOE_DOT_ROWS)
            def _():
                yb_ref[rows, :] = jnp.zeros((MOE_DOT_ROWS, yb_ref.shape[1]), BF16)


def _experts(be, nv, xb, gs, wg, wu, wd):
    d = xb.shape[1]
    n_blocks = be.shape[0]
    d_ff = wg.shape[2]
    tf = MOE_FF_TILE
    nf = d_ff // tf
    bsz = MOE_BLOCK

    def ff_idx(f, nvr, b):
        return jnp.where(nvr[b] > 0, f, nf - 1)

    grid_spec = pltpu.PrefetchScalarGridSpec(
        num_scalar_prefetch=2,
        grid=(n_blocks, nf),
        in_specs=[
            pl.BlockSpec((bsz, d), lambda b, f, *_: (b, 0)),
            pl.BlockSpec((bsz, 1), lambda b, f, *_: (b, 0)),
            pl.BlockSpec((1, d, tf), lambda b, f, be_r, nv_r: (be_r[b], 0, ff_idx(f, nv_r, b))),
            pl.BlockSpec((1, d, tf), lambda b, f, be_r, nv_r: (be_r[b], 0, ff_idx(f, nv_r, b))),
            pl.BlockSpec((1, tf, d), lambda b, f, be_r, nv_r: (be_r[b], ff_idx(f, nv_r, b), 0)),
        ],
        out_specs=pl.BlockSpec((bsz, d), lambda b, f, *_: (b, 0)),
        scratch_shapes=[
            pltpu.VMEM((bsz, d), F32),
            pltpu.VMEM((d, tf), BF16), pltpu.VMEM((d, tf), BF16), pltpu.VMEM((tf, d), BF16),
        ],
    )
    return pl.pallas_call(
        _experts_kernel,
        grid_spec=grid_spec,
        out_shape=jax.ShapeDtypeStruct((n_blocks * bsz, d), BF16),
        compiler_params=_params("arbitrary", "arbitrary"),
        name="experts",
    )(be, nv, xb, gs, wg, wu, wd)


def _combine_kernel(win_ref, x_ref, pos_ref, g_ref, fg_ref, *refs):
    yb_refs, o_ref = refs[:-1], refs[-1]
    i = pl.program_id(0)
    n_sub = x_ref.shape[0] // COMBINE_ROWS
    lane = lax.broadcasted_iota(jnp.int32, (COMBINE_ROWS, 2 * COMBINE_ROWS), 1)
    for s in range(n_sub):
        rows = slice(s * COMBINE_ROWS, (s + 1) * COMBINE_ROWS)
        onehots, windows = [], []
        for e in range(N_EXPERTS):
            k = (i * n_sub + s) * N_EXPERTS + e
            base = win_ref[k] * COMBINE_ROWS
            rel = pos_ref[rows, e:e + 1] - base
            onehots.append(jnp.where(lane == rel, 1.0, 0.0).astype(BF16))
            w0 = 2 * (s * N_EXPERTS + e)
            windows += [yb_refs[w0][...], yb_refs[w0 + 1][...]]
        acc = jnp.dot(jnp.concatenate(onehots, axis=1), jnp.concatenate(windows, axis=0),
                      preferred_element_type=F32)
        xo = x_ref[rows, :] + g_ref[...] * acc
        ms = jnp.mean(xo * xo, axis=-1, keepdims=True)
        o_ref[rows, :] = xo * lax.rsqrt(ms + EPS) * fg_ref[...]


def _combine(win, x, pos, g2, fgain, yb):
    t, d = x.shape
    tm = COMBINE_TILE
    n_sub = tm // COMBINE_ROWS
    n_win = yb.shape[0] // COMBINE_ROWS
    vec = pl.BlockSpec((1, d), lambda i, w: (0, 0))

    def first(s, e):
        return lambda i, w: (jnp.minimum(w[(i * n_sub + s) * N_EXPERTS + e], n_win - 1), 0)

    def second(s, e):
        return lambda i, w: (jnp.minimum(w[(i * n_sub + s) * N_EXPERTS + e] + 1, n_win - 1), 0)

    yb_specs = []
    for s in range(n_sub):
        for e in range(N_EXPERTS):
            yb_specs.append(pl.BlockSpec((COMBINE_ROWS, d), first(s, e)))
            yb_specs.append(pl.BlockSpec((COMBINE_ROWS, d), second(s, e)))
    grid_spec = pltpu.PrefetchScalarGridSpec(
        num_scalar_prefetch=1,
        grid=(t // tm,),
        in_specs=[
            pl.BlockSpec((tm, d), lambda i, w: (i, 0)),
            pl.BlockSpec((tm, N_EXPERTS), lambda i, w: (i, 0)),
            vec, vec,
        ] + yb_specs,
        out_specs=pl.BlockSpec((tm, d), lambda i, w: (i, 0)),
    )
    return pl.pallas_call(
        _combine_kernel,
        grid_spec=grid_spec,
        out_shape=jax.ShapeDtypeStruct((t, d), F32),
        compiler_params=_params("arbitrary"),
        name="combine",
    )(win, x, pos, g2, fgain, *([yb] * len(yb_specs)))


def _routing_tables(rank, tot, sel):
    t = rank.shape[0]
    bsz = MOE_BLOCK
    n_blocks = (2 * t) // bsz + N_EXPERTS
    cum = tot[:, 0, :N_EXPERTS].astype(jnp.int32)
    counts = cum[-1]
    padded = ((counts + bsz - 1) // bsz) * bsz
    pad_end = jnp.cumsum(padded)
    start_pad = pad_end - padded
    selected = sel[:, :N_EXPERTS] > 0
    pos = jnp.where(selected, start_pad[None, :] + rank[:, :N_EXPERTS].astype(jnp.int32), -1)

    bstart = jnp.arange(n_blocks, dtype=jnp.int32) * bsz
    be = jnp.minimum(jnp.sum(bstart[:, None] >= pad_end[None, :], axis=1), N_EXPERTS - 1).astype(jnp.int32)
    r0 = bstart - start_pad[be]
    nv = jnp.clip(counts[be] - r0, 0, bsz).astype(jnp.int32)
    n_gather = bsz // MOE_GATHER_ROWS
    goff = jnp.arange(n_gather, dtype=jnp.int32) * MOE_GATHER_ROWS
    r0g = (r0[:, None] + goff[None, :]).reshape(-1)
    ng = jnp.clip(nv[:, None] - goff[None, :], 0, MOE_GATHER_ROWS).reshape(-1)
    beg = jnp.repeat(be, n_gather)
    chunk_end = cum[:, beg].T
    chunk_start = jnp.concatenate([jnp.zeros((1, N_EXPERTS), jnp.int32), cum[:-1]], axis=0)[:, beg].T
    clo = jnp.sum(chunk_end <= r0g[:, None], axis=1).astype(jnp.int32)
    chi = jnp.sum(chunk_start < (r0g + ng)[:, None], axis=1).astype(jnp.int32)
    beu = jnp.repeat(be, bsz // MOE_GATHER_UNIT)

    big = jnp.int32(2 ** 30)
    first = jnp.min(jnp.where(pos >= 0, pos, big).reshape(t // COMBINE_ROWS, COMBINE_ROWS, N_EXPERTS), axis=1)
    win = jnp.where(first == big, 0, first // COMBINE_ROWS).astype(jnp.int32).reshape(-1)
    return pos, be, nv, beu, ng, clo, chi, win


def _rotary_tables(seq, dk):
    half = dk // 2
    inv = ROPE_BASE ** (-jnp.arange(half, dtype=F32) / half)
    ang_a = (jnp.arange(seq // ROT_BLOCK) * ROT_BLOCK).astype(F32)[:, None] * inv[None, :]
    ang_b = jnp.arange(ROT_BLOCK).astype(F32)[:, None] * inv[None, :]
    return jnp.cos(ang_a), jnp.sin(ang_a), jnp.cos(ang_b), jnp.sin(ang_b)


def _decay_tables(dk, dv):
    c = RET_CHUNK
    log_gamma = jnp.log1p(-jnp.exp2(-5.0 - jnp.arange(N_HEADS, dtype=F32)))
    idx = jnp.arange(c, dtype=F32)
    row_dec = jnp.exp(log_gamma[:, None] * (idx + 1.0 - c))
    causal = (idx[:, None] >= idx[None, :]).astype(F32)
    dmask = row_dec[:, :, None] * causal[None]
    k_dec = jnp.exp(log_gamma[:, None] * (c - 1.0 - idx)) * (dk ** -0.5)
    q_dec = jnp.exp(log_gamma[:, None] * (idx + 1.0))
    chunk_dec = jnp.exp(log_gamma * c)
    half = dk // 2
    kdec_tile = jnp.tile(jnp.repeat(k_dec.T, half, axis=1), (ROW_TILE // c, 1))
    qdec_tile = jnp.broadcast_to(q_dec[:, :, None], (N_HEADS, c, dv))
    return dmask, kdec_tile, qdec_tile, chunk_dec


def _pool_bands(tm):
    def bands(lag):
        return jnp.stack([((lag >= 0) & (lag < w)) for w in POOL_WINDOWS]).astype(BF16)
    main = bands(jnp.arange(tm)[:, None] - jnp.arange(tm)[None, :])
    head = bands(jnp.arange(POOL_HALO)[:, None] + POOL_HALO - jnp.arange(POOL_HALO)[None, :])
    return main, head


def kernel(x, c, ada_w, ada_b, norm_gain, ret_w_in, ret_gn_gain, ret_w_out, ffn_w_gate, ffn_w_up, ffn_w_down,
           pool_w, pool_b, pool_scale, moe_router, moe_w_gate, moe_w_up, moe_w_down, final_norm_gain):
    bsz, seq, d = x.shape
    assert bsz == 1 and ada_w.shape[0] == 2
    xt = x.reshape(seq, d)
    qk_total = d
    v_total = ret_w_out.shape[1]
    dk = qk_total // N_HEADS
    dv = v_total // N_HEADS

    mod = _modulation(c, ada_w, ada_b)
    def mods(i):
        parts = [mod[i, :, k * d:(k + 1) * d] for k in range(N_MOD)]
        sh1, sc1, g1, sh2, sc2, g2 = parts
        return sh1, 1.0 + sc1, g1, sh2, 1.0 + sc2, g2

    sh1, sc1p, g1, sh2, sc2p, g2 = mods(0)
    dmask, kdec_tile, qdec_tile, chunk_dec = _decay_tables(dk, dv)
    proj, (w_out_b, wg_b, wu_b, wd_b) = _ret_proj(
        xt, norm_gain[0, 0][None], sc1p, sh1, ret_w_in[0].astype(BF16), _rotary_tables(seq, dk), kdec_tile,
        qk_total, v_total, [ret_w_out[0], ffn_w_gate[0], ffn_w_up[0], ffn_w_down[0]])
    y = _retention_core(proj, chunk_dec, dmask, qdec_tile, ret_gn_gain[0][None], qk_total, v_total)
    x1 = _ret_out(y, w_out_b, xt, g1)
    x2 = _dense_ffn(x1, norm_gain[0, 1][None], sc2p, sh2, g2, wg_b, wu_b, wd_b)

    sh1, sc1p, g1, sh2, sc2p, g2 = mods(1)
    band, bandh = _pool_bands(POOL_SUB_ROWS)
    x3 = _pool_mixer(x2, norm_gain[1, 0][None], sc1p, sh1, g1, band, bandh,
                     pool_w[0].astype(BF16), pool_b[0].reshape(1, d), pool_scale[0][None])

    wr = jnp.pad(moe_router[0], ((0, 0), (0, LANES - N_EXPERTS)))
    wr_hi = wr.astype(BF16)
    wr_lo = (wr - wr_hi.astype(F32)).astype(BF16)
    h4, gates, sel = _router(x3, norm_gain[1, 1][None], sc2p, sh2, jnp.concatenate([wr_hi, wr_lo], axis=1))
    tri = (jnp.arange(RANK_ROWS)[:, None] >= jnp.arange(RANK_ROWS)[None, :]).astype(BF16)
    rank, tot = _ranks(sel, tri)
    pos, be, nv, beu, ng, clo, chi, win = _routing_tables(rank, tot, sel)
    xb, gs = _gather(beu, ng, clo, chi, h4, pos.T, gates[:, :N_EXPERTS].T)
    yb = _experts(be, nv, xb, gs, moe_w_gate[0], moe_w_up[0], moe_w_down[0])
    out = _combine(win, x3, pos, g2, final_norm_gain[None], yb)
    return out.reshape(bsz, seq, d)
```

```python
import functools

import jax
import jax.numpy as jnp
from jax import lax
from jax.experimental import pallas as pl
from jax.experimental.pallas import tpu as pltpu

F32 = jnp.float32
BF16 = jnp.bfloat16

EPS = 1e-6
N_HEADS = 4
RET_CHUNK = 256
ROPE_BASE = 10000.0
ROT_BLOCK = 128
POOL_WINDOWS = (2, 4, 8, 16)
POOL_HALO = 16
N_EXPERTS = 8
N_MOD = 6

VMEM_LIMIT_BYTES = 56 * 1024 * 1024

ROW_TILE = 1024
PROJ_COL_TILE = 3072
RET_ROWS = 1024
POOL_ROWS = 1024
POOL_SUB_ROWS = 256
RANK_ROWS = 512
MOE_BLOCK = 2048
MOE_DOT_ROWS = 512
MOE_GATHER_UNIT = 1024
MOE_GATHER_ROWS = 256
MOE_FF_TILE = 512
GATHER_CHUNK = 512
COMBINE_TILE = 256
COMBINE_ROWS = 128
LANES = 128


def _silu(v):
    return v / (1.0 + jnp.exp(-v))


def _norm_mod(x, gain, scale1p, shift):
    ms = jnp.mean(x * x, axis=-1, keepdims=True)
    return (x * lax.rsqrt(ms + EPS) * gain) * scale1p + shift


def _params(*sem):
    return pltpu.CompilerParams(dimension_semantics=sem, vmem_limit_bytes=VMEM_LIMIT_BYTES)


def _resident(shape):
    nd = len(shape)
    return pl.BlockSpec(shape, lambda *_: (0,) * nd, pipeline_mode=pl.Buffered(1))


def _mod_kernel(c_ref, w_ref, b_ref, o_ref):
    ca = _silu(c_ref[...])
    o_ref[0] = jnp.sum(ca * w_ref[0], axis=0, keepdims=True) + b_ref[0]


def _modulation(c, ada_w, ada_b):
    depth, d, n = ada_w.shape
    tn = 1024
    return pl.pallas_call(
        _mod_kernel,
        grid=(depth, n // tn),
        in_specs=[
            pl.BlockSpec((d, 1), lambda i, j: (0, 0)),
            pl.BlockSpec((1, d, tn), lambda i, j: (i, 0, j)),
            pl.BlockSpec((1, 1, tn), lambda i, j: (i, 0, j)),
        ],
        out_specs=pl.BlockSpec((1, 1, tn), lambda i, j: (i, 0, j)),
        out_shape=jax.ShapeDtypeStruct((depth, 1, n), F32),
        compiler_params=_params("arbitrary", "arbitrary"),
        name="mod",
    )(c.reshape(d, 1), ada_w, ada_b.reshape(depth, 1, n))


def _proj_kernel(x_ref, gain_ref, sc_ref, sh_ref, w_ref, cosa_ref, sina_ref, cosb_ref, sinb_ref, kdec_ref, *refs,
                 n_qk_chunks, n_v_chunks, n_col_tiles, dk, n_cast):
    cast_in = refs[:n_cast]
    o_ref = refs[n_cast]
    cast_out = refs[n_cast + 1:2 * n_cast + 1]
    h_ref, cos_ref, sin_ref = refs[2 * n_cast + 1:]
    j = pl.program_id(1)

    @pl.when(j == 0)
    def _():
        h_ref[...] = _norm_mod(x_ref[...], gain_ref[...], sc_ref[...], sh_ref[...]).astype(BF16)
        for src, dst in zip(cast_in, cast_out):
            dst[...] = src[...].astype(BF16)
        cb = cosb_ref[...]
        sb = sinb_ref[...]
        for a in range(cosa_ref.shape[0]):
            ca = cosa_ref[a:a + 1, :]
            sa = sina_ref[a:a + 1, :]
            cos_ref[a * ROT_BLOCK:(a + 1) * ROT_BLOCK, :] = ca * cb - sa * sb
            sin_ref[a * ROT_BLOCK:(a + 1) * ROT_BLOCK, :] = sa * cb + ca * sb

    half = dk // 2
    n_chunks = w_ref.shape[1] // dk

    def chunk(c, kind, head):
        acc = jnp.dot(h_ref[...], w_ref[:, c * dk:(c + 1) * dk], preferred_element_type=F32)
        if kind == "v":
            o_ref[:, c * dk:(c + 1) * dk] = acc.astype(BF16)
        elif kind == "g":
            o_ref[:, c * dk:(c + 1) * dk] = _silu(acc).astype(BF16)
        else:
            cos = cos_ref[...]
            sin = sin_ref[...]
            t1 = acc[:, :half]
            t2 = acc[:, half:]
            o1 = t1 * cos - t2 * sin
            o2 = t1 * sin + t2 * cos
            if kind == "k":
                s = kdec_ref[:, head * half:(head + 1) * half]
                o1 = o1 * s
                o2 = o2 * s
            o_ref[:, c * dk:c * dk + half] = o1.astype(BF16)
            o_ref[:, c * dk + half:(c + 1) * dk] = o2.astype(BF16)

    def kind_of(gc):
        if gc < n_qk_chunks:
            return "q", gc
        if gc < 2 * n_qk_chunks:
            return "k", gc - n_qk_chunks
        if gc < 2 * n_qk_chunks + n_v_chunks:
            return "v", 0
        return "g", 0

    for jj in range(n_col_tiles):
        @pl.when(j == jj)
        def _():
            for c in range(n_chunks):
                chunk(c, *kind_of(jj * n_chunks + c))


def _ret_proj(x, gain, sc1p, sh, w_in, rot, kdec, qk_total, v_total, to_cast):
    t, d = x.shape
    n = w_in.shape[1]
    tm, tn = ROW_TILE, PROJ_COL_TILE
    n_row_tiles = t // tm
    dk = qk_total // N_HEADS
    half = dk // 2
    cosa, sina, cosb, sinb = rot
    kern = functools.partial(_proj_kernel, n_qk_chunks=qk_total // dk, n_v_chunks=v_total // dk,
                             n_col_tiles=n // tn, dk=dk, n_cast=len(to_cast))
    vec = pl.BlockSpec((1, d), lambda i, j: (0, 0))
    rot_a = pl.BlockSpec((tm // ROT_BLOCK, half), lambda i, j: (i, 0))
    rot_b = pl.BlockSpec((ROT_BLOCK, half), lambda i, j: (0, 0))
    cast_specs = [pl.BlockSpec((w.shape[0] // n_row_tiles, w.shape[1]), lambda i, j: (i, 0)) for w in to_cast]
    outs = pl.pallas_call(
        kern,
        grid=(n_row_tiles, n // tn),
        in_specs=[
            pl.BlockSpec((tm, d), lambda i, j: (i, 0)),
            vec, vec, vec,
            pl.BlockSpec((d, tn), lambda i, j: (0, j)),
            rot_a, rot_a, rot_b, rot_b,
            pl.BlockSpec((tm, N_HEADS * half), lambda i, j: (0, 0)),
        ] + cast_specs,
        out_specs=[pl.BlockSpec((tm, tn), lambda i, j: (i, j))] + cast_specs,
        out_shape=[jax.ShapeDtypeStruct((t, n), BF16)] + [jax.ShapeDtypeStruct(w.shape, BF16) for w in to_cast],
        scratch_shapes=[pltpu.VMEM((tm, d), BF16), pltpu.VMEM((tm, half), F32), pltpu.VMEM((tm, half), F32)],
        compiler_params=_params("arbitrary", "arbitrary"),
        name="proj",
    )(x, gain, sc1p, sh, w_in, cosa, sina, cosb, sinb, kdec, *to_cast)
    return outs[0], outs[1:]


def _ret_kernel(cdec_ref, q_ref, k_ref, v_ref, sg_ref, dmask_ref, qdec_ref, gn_ref, y_ref, state_ref,
                *, dk, dv):
    @pl.when(pl.program_id(0) == 0)
    def _():
        state_ref[...] = jnp.zeros_like(state_ref)

    rows_per_step = q_ref.shape[0]
    for n in range(rows_per_step // RET_CHUNK):
        rows = slice(n * RET_CHUNK, (n + 1) * RET_CHUNK)
        for h in range(N_HEADS):
            qcols = slice(h * dk, (h + 1) * dk)
            vcols = slice(h * dv, (h + 1) * dv)
            q = q_ref[rows, qcols]
            ks = k_ref[rows, qcols]
            v = v_ref[rows, vcols]
            s = lax.dot_general(q, ks, (((1,), (1,)), ((), ())), preferred_element_type=F32)
            a = (s * dmask_ref[h]).astype(BF16)
            intra = jnp.dot(a, v, preferred_element_type=F32)
            st = state_ref[h]
            cross = jnp.dot(q, st.astype(BF16), preferred_element_type=F32)
            o = intra + cross * qdec_ref[h]
            kv = lax.dot_general(ks, v, (((0,), (0,)), ((), ())), preferred_element_type=F32)
            state_ref[h] = st * cdec_ref[h] + kv
            mu = jnp.mean(o, axis=-1, keepdims=True)
            dlt = o - mu
            var = jnp.mean(dlt * dlt, axis=-1, keepdims=True)
            on = dlt * lax.rsqrt(var + EPS)
            y = on * gn_ref[:, vcols] * sg_ref[rows, vcols].astype(F32)
            y_ref[rows, vcols] = y.astype(BF16)


def _retention_core(proj, cdec, dmask, qdec, gn_gain, qk_total, v_total):
    t = proj.shape[0]
    r = RET_ROWS
    dk = qk_total // N_HEADS
    dv = v_total // N_HEADS
    assert v_total == 2 * qk_total
    kern = functools.partial(_ret_kernel, dk=dk, dv=dv)
    return pl.pallas_call(
        kern,
        grid=(t // r,),
        in_specs=[
            pl.BlockSpec(memory_space=pltpu.SMEM),
            pl.BlockSpec((r, qk_total), lambda i: (i, 0)),
            pl.BlockSpec((r, qk_total), lambda i: (i, 1)),
            pl.BlockSpec((r, v_total), lambda i: (i, 1)),
            pl.BlockSpec((r, v_total), lambda i: (i, 2)),
            pl.BlockSpec((N_HEADS, RET_CHUNK, RET_CHUNK), lambda i: (0, 0, 0)),
            pl.BlockSpec((N_HEADS, RET_CHUNK, dv), lambda i: (0, 0, 0)),
            pl.BlockSpec((1, v_total), lambda i: (0, 0)),
        ],
        out_specs=pl.BlockSpec((r, v_total), lambda i: (i, 0)),
        out_shape=jax.ShapeDtypeStruct((t, v_total), BF16),
        scratch_shapes=[pltpu.VMEM((N_HEADS, dk, dv), F32)],
        compiler_params=_params("arbitrary"),
        name="ret",
    )(cdec, proj, proj, proj, proj, dmask, qdec, gn_gain)


def _out_kernel(y_ref, w_ref, x_ref, g_ref, o_ref):
    m = jnp.dot(y_ref[...], w_ref[...], preferred_element_type=F32)
    o_ref[...] = x_ref[...] + g_ref[...] * m


def _ret_out(y, w_out, x, g1):
    t, d = x.shape
    kdim = y.shape[1]
    tm = ROW_TILE
    return pl.pallas_call(
        _out_kernel,
        grid=(t // tm,),
        in_specs=[
            pl.BlockSpec((tm, kdim), lambda i: (i, 0)),
            _resident((kdim, d)),
            pl.BlockSpec((tm, d), lambda i: (i, 0)),
            pl.BlockSpec((1, d), lambda i: (0, 0)),
        ],
        out_specs=pl.BlockSpec((tm, d), lambda i: (i, 0)),
        out_shape=jax.ShapeDtypeStruct((t, d), F32),
        compiler_params=_params("arbitrary"),
        name="out",
    )(y, w_out, x, g1)


def _ffn_kernel(x_ref, gain_ref, sc_ref, sh_ref, g_ref, wg_ref, wu_ref, wd_ref, o_ref, *, ff_tile):
    x = x_ref[...]
    h = _norm_mod(x, gain_ref[...], sc_ref[...], sh_ref[...]).astype(BF16)
    d_ff = wg_ref.shape[1]
    acc = jnp.zeros(x.shape, F32)
    for lo in range(0, d_ff, ff_tile):
        hi = min(lo + ff_tile, d_ff)
        a = jnp.dot(h, wg_ref[:, lo:hi], preferred_element_type=F32)
        b = jnp.dot(h, wu_ref[:, lo:hi], preferred_element_type=F32)
        act = (_silu(a) * b).astype(BF16)
        acc = acc + jnp.dot(act, wd_ref[lo:hi, :], preferred_element_type=F32)
    o_ref[...] = x + g_ref[...] * acc


def _dense_ffn(x, gain, sc1p, sh, g2, wg, wu, wd):
    t, d = x.shape
    d_ff = wg.shape[1]
    tm = ROW_TILE
    vec = pl.BlockSpec((1, d), lambda i: (0, 0))
    return pl.pallas_call(
        functools.partial(_ffn_kernel, ff_tile=512),
        grid=(t // tm,),
        in_specs=[
            pl.BlockSpec((tm, d), lambda i: (i, 0)),
            vec, vec, vec, vec,
            _resident((d, d_ff)), _resident((d, d_ff)), _resident((d_ff, d)),
        ],
        out_specs=pl.BlockSpec((tm, d), lambda i: (i, 0)),
        out_shape=jax.ShapeDtypeStruct((t, d), F32),
        compiler_params=_params("arbitrary"),
        name="ffn",
    )(x, gain, sc1p, sh, g2, wg, wu, wd)


def _pool_kernel(x_ref, gain_ref, sc_ref, sh_ref, g_ref, band_ref, bandh_ref, wp_ref, bp_ref, ps_ref, o_ref,
                 hbuf_ref):
    i = pl.program_id(0)
    tm = x_ref.shape[0]
    gw = wp_ref.shape[1]

    @pl.when(i == 0)
    def _():
        hbuf_ref[0:POOL_HALO, :] = jnp.zeros((POOL_HALO, hbuf_ref.shape[1]), F32)

    @pl.when(i > 0)
    def _():
        hbuf_ref[0:POOL_HALO, :] = hbuf_ref[tm:tm + POOL_HALO, :]

    hbuf_ref[POOL_HALO:, :] = _norm_mod(x_ref[...], gain_ref[...], sc_ref[...], sh_ref[...])

    def split(v):
        hi = v.astype(BF16)
        return hi, (v - hi.astype(F32)).astype(BF16)

    sub = band_ref.shape[1]
    for s in range(tm // sub):
        rows = slice(s * sub, (s + 1) * sub)
        lo = POOL_HALO + s * sub
        x = x_ref[rows, :]
        h = hbuf_ref[lo:lo + sub, :]
        h_hi, h_lo = split(h)
        halo_hi, halo_lo = split(hbuf_ref[lo - POOL_HALO:lo, :])
        t1 = (lax.broadcasted_iota(jnp.int32, (sub, 1), 0) + (i * tm + s * sub + 1)).astype(F32)
        for g, w in enumerate(POOL_WINDOWS):
            cols = slice(g * gw, (g + 1) * gw)
            band = band_ref[g]
            bandh = bandh_ref[g]
            win = (jnp.dot(band, h_hi[:, cols], preferred_element_type=F32)
                   + jnp.dot(band, h_lo[:, cols], preferred_element_type=F32))
            head = (jnp.dot(bandh, halo_hi[:, cols], preferred_element_type=F32)
                    + jnp.dot(bandh, halo_lo[:, cols], preferred_element_type=F32))
            win = jnp.concatenate([win[:POOL_HALO] + head, win[POOL_HALO:]], axis=0)
            pooled = win * (1.0 / jnp.minimum(t1, float(w))) - h[:, cols]
            y = jnp.dot(pooled.astype(BF16), wp_ref[g], preferred_element_type=F32) + bp_ref[:, cols]
            o_ref[rows, cols] = x[:, cols] + g_ref[:, cols] * (y * ps_ref[:, cols])


def _pool_mixer(x, gain, sc1p, sh, g1, band, bandh, wp, bp, ps):
    t, d = x.shape
    tm = POOL_ROWS
    vec = pl.BlockSpec((1, d), lambda i: (0, 0))
    return pl.pallas_call(
        _pool_kernel,
        grid=(t // tm,),
        in_specs=[
            pl.BlockSpec((tm, d), lambda i: (i, 0)),
            vec, vec, vec, vec,
            pl.BlockSpec(band.shape, lambda i: (0, 0, 0)),
            pl.BlockSpec(bandh.shape, lambda i: (0, 0, 0)),
            pl.BlockSpec(wp.shape, lambda i: (0, 0, 0)),
            vec, vec,
        ],
        out_specs=pl.BlockSpec((tm, d), lambda i: (i, 0)),
        out_shape=jax.ShapeDtypeStruct((t, d), F32),
        scratch_shapes=[pltpu.VMEM((POOL_HALO + tm, d), F32)],
        compiler_params=_params("arbitrary"),
        name="pool",
    )(x, gain, sc1p, sh, g1, band, bandh, wp, bp, ps)


def _router_kernel(x_ref, gain_ref, sc_ref, sh_ref, w_ref, tri_ref, h_ref, gates_ref, sel_ref, rank_ref, tot_ref,
                   carry_ref):
    h = _norm_mod(x_ref[...], gain_ref[...], sc_ref[...], sh_ref[...])
    h_hi = h.astype(BF16)
    h_ref[...] = h_hi
    h_lo = (h - h_hi.astype(F32)).astype(BF16)
    w = w_ref[...]
    p_hi = jnp.dot(h_hi, w, preferred_element_type=F32)
    p_lo = jnp.dot(h_lo, w, preferred_element_type=F32)
    logits = p_hi[:, :LANES] + p_hi[:, LANES:] + p_lo[:, :LANES] + p_lo[:, LANES:]
    lane = lax.broadcasted_iota(jnp.int32, logits.shape, 1).astype(F32)
    neg = jnp.float32(-jnp.inf)
    lg = jnp.where(lane < float(N_EXPERTS), logits, neg)
    m0 = jnp.max(lg, axis=-1, keepdims=True)
    i0 = jnp.min(jnp.where(lg == m0, lane, float(LANES)), axis=-1, keepdims=True)
    lg1 = jnp.where(lane == i0, neg, lg)
    m1 = jnp.max(lg1, axis=-1, keepdims=True)
    i1 = jnp.min(jnp.where(lg1 == m1, lane, float(LANES)), axis=-1, keepdims=True)
    e1 = jnp.exp(m1 - m0)
    den = 1.0 + e1
    is0 = lane == i0
    is1 = lane == i1
    gates_ref[...] = jnp.where(is0, 1.0 / den, jnp.where(is1, e1 / den, 0.0))
    sel = jnp.where(is0 | is1, 1.0, 0.0).astype(BF16)
    sel_ref[...] = sel

    @pl.when(pl.program_id(0) == 0)
    def _():
        carry_ref[...] = jnp.zeros_like(carry_ref)

    carry = carry_ref[...]
    for b in range(sel.shape[0] // RANK_ROWS):
        rows = slice(b * RANK_ROWS, (b + 1) * RANK_ROWS)
        s = sel[rows, :]
        cum = jnp.dot(tri_ref[...], s, preferred_element_type=F32)
        rank_ref[rows, :] = cum - s.astype(F32) + carry
        carry = carry + cum[RANK_ROWS - 1:, :]
        tot_ref[b] = carry
    carry_ref[...] = carry


def _router(x, gain, sc1p, sh, w_cat, tri):
    t, d = x.shape
    tm = ROW_TILE
    n_sub = tm // RANK_ROWS
    vec = pl.BlockSpec((1, d), lambda i: (0, 0))
    wspec = pl.BlockSpec((d, 2 * LANES), lambda i: (0, 0))
    lanes = pl.BlockSpec((tm, LANES), lambda i: (i, 0))
    return pl.pallas_call(
        _router_kernel,
        grid=(t // tm,),
        in_specs=[pl.BlockSpec((tm, d), lambda i: (i, 0)), vec, vec, vec, wspec,
                  pl.BlockSpec((RANK_ROWS, RANK_ROWS), lambda i: (0, 0))],
        out_specs=[
            pl.BlockSpec((tm, d), lambda i: (i, 0)), lanes, lanes, lanes,
            pl.BlockSpec((n_sub, 1, LANES), lambda i: (i, 0, 0)),
        ],
        out_shape=[
            jax.ShapeDtypeStruct((t, d), BF16),
            jax.ShapeDtypeStruct((t, LANES), F32),
            jax.ShapeDtypeStruct((t, LANES), BF16),
            jax.ShapeDtypeStruct((t, LANES), F32),
            jax.ShapeDtypeStruct((t // RANK_ROWS, 1, LANES), F32),
        ],
        scratch_shapes=[pltpu.VMEM((1, LANES), F32)],
        compiler_params=_params("arbitrary"),
        name="router",
    )(x, gain, sc1p, sh, w_cat, tri)


def _gather_kernel(be_ref, ng_ref, clo_ref, chi_ref, h_ref, post_ref, gatest_ref, xb_ref, gs_ref, acc_ref):
    u = pl.program_id(0)
    e = be_ref[u]
    unit_rows = xb_ref.shape[0]
    n_groups = unit_rows // MOE_GATHER_ROWS
    last_chunk = h_ref.shape[0] // GATHER_CHUNK - 1
    g0 = u * n_groups

    @pl.when(ng_ref[g0] > 0)
    def _():
        acc_ref[...] = jnp.zeros_like(acc_ref)
        trips = chi_ref[g0] - clo_ref[g0]
        for q in range(1, n_groups):
            trips = jnp.maximum(trips, chi_ref[g0 + q] - clo_ref[g0 + q])

        def body(i, gsums):
            out = []
            for q in range(n_groups):
                rows = slice(q * MOE_GATHER_ROWS, (q + 1) * MOE_GATHER_ROWS)
                c = clo_ref[g0 + q] + i
                live = c < chi_ref[g0 + q]
                off = pl.multiple_of(jnp.minimum(c, last_chunk) * GATHER_CHUNK, GATHER_CHUNK)
                p = post_ref[pl.ds(e, 1), pl.ds(off, GATHER_CHUNK)]
                p = jnp.where(live, p, -2)
                row_pos = (lax.broadcasted_iota(jnp.int32, (MOE_GATHER_ROWS, GATHER_CHUNK), 0)
                           + (u * unit_rows + q * MOE_GATHER_ROWS))
                match = row_pos == p
                onehot = jnp.where(match, 1.0, 0.0).astype(BF16)
                acc_ref[rows, :] += jnp.dot(onehot, h_ref[pl.ds(off, GATHER_CHUNK), :],
                                            preferred_element_type=F32)
                gate = gatest_ref[pl.ds(e, 1), pl.ds(off, GATHER_CHUNK)]
                out.append(gsums[q] + jnp.sum(jnp.where(match, gate, 0.0), axis=1, keepdims=True))
            return tuple(out)

        zero = jnp.zeros((MOE_GATHER_ROWS, 1), F32)
        gsums = lax.fori_loop(0, trips, body, (zero,) * n_groups)
        xb_ref[...] = acc_ref[...].astype(BF16)
        for q in range(n_groups):
            gs_ref[q * MOE_GATHER_ROWS:(q + 1) * MOE_GATHER_ROWS, :] = gsums[q]

    @pl.when(ng_ref[g0] == 0)
    def _():
        xb_ref[...] = jnp.zeros_like(xb_ref)
        gs_ref[...] = jnp.zeros_like(gs_ref)


def _gather(beu, ng, clo, chi, h, post, gatest):
    t, d = h.shape
    n_units = beu.shape[0]
    grid_spec = pltpu.PrefetchScalarGridSpec(
        num_scalar_prefetch=4,
        grid=(n_units,),
        in_specs=[
            pl.BlockSpec((t, d), lambda u, *_: (0, 0), pipeline_mode=pl.Buffered(1)),
            pl.BlockSpec(post.shape, lambda u, *_: (0, 0), pipeline_mode=pl.Buffered(1)),
            pl.BlockSpec(gatest.shape, lambda u, *_: (0, 0), pipeline_mode=pl.Buffered(1)),
        ],
        out_specs=[
            pl.BlockSpec((MOE_GATHER_UNIT, d), lambda u, *_: (u, 0)),
            pl.BlockSpec((MOE_GATHER_UNIT, 1), lambda u, *_: (u, 0)),
        ],
        scratch_shapes=[pltpu.VMEM((MOE_GATHER_UNIT, d), F32)],
    )
    return pl.pallas_call(
        _gather_kernel,
        grid_spec=grid_spec,
        out_shape=[
            jax.ShapeDtypeStruct((n_units * MOE_GATHER_UNIT, d), BF16),
            jax.ShapeDtypeStruct((n_units * MOE_GATHER_UNIT, 1), F32),
        ],
        compiler_params=_params("arbitrary"),
        name="gather",
    )(beu, ng, clo, chi, h, post, gatest)


def _experts_kernel(be_ref, nv_ref, xb_ref, gs_ref, wg_ref, wu_ref, wd_ref, yb_ref,
                    acc_ref, wgb_ref, wub_ref, wdb_ref):
    b = pl.program_id(0)
    f = pl.program_id(1)
    nf = pl.num_programs(1)
    nv = nv_ref[b]
    block_rows = xb_ref.shape[0]

    @pl.when((b == 0) & (f == 0))
    def _():
        acc_ref[...] = jnp.zeros_like(acc_ref)

    def swiglu_rows(lo, n, cast):
        rows = slice(lo, lo + n)
        if cast:
            wgb_ref[...] = wg_ref[0].astype(BF16)
            wub_ref[...] = wu_ref[0].astype(BF16)
            wdb_ref[...] = wd_ref[0].astype(BF16)
        x = xb_ref[rows, :]
        a = jnp.dot(x, wgb_ref[...], preferred_element_type=F32)
        u = jnp.dot(x, wub_ref[...], preferred_element_type=F32)
        act = (_silu(a) * u).astype(BF16)
        prev = jnp.where(f > 0, acc_ref[rows, :], 0.0)
        acc_ref[rows, :] = prev + jnp.dot(act, wdb_ref[...], preferred_element_type=F32)

    pair = 2 * MOE_DOT_ROWS
    for p in range(block_rows // pair):
        lo = p * pair
        full = nv >= lo + pair

        @pl.when(full)
        def _():
            swiglu_rows(lo, pair, p == 0)

        @pl.when(jnp.logical_not(full) & (nv > lo))
        def _():
            swiglu_rows(lo, MOE_DOT_ROWS, p == 0)

        @pl.when(jnp.logical_not(full) & (nv > lo + MOE_DOT_ROWS))
        def _():
            swiglu_rows(lo + MOE_DOT_ROWS, MOE_DOT_ROWS, False)

    @pl.when(f == nf - 1)
    def _():
        for g in range(block_rows // MOE_DOT_ROWS):
            rows = slice(g * MOE_DOT_ROWS, (g + 1) * MOE_DOT_ROWS)

            @pl.when(nv > g * MOE_DOT_ROWS)
            def _():
                yb_ref[rows, :] = (acc_ref[rows, :] * gs_ref[rows, :]).astype(BF16)

            @pl.when(nv <= g * MOE_DOT_ROWS)
            def _():
                yb_ref[rows, :] = jnp.zeros((MOE_DOT_ROWS, yb_ref.shape[1]), BF16)


def _experts(be, nv, xb, gs, wg, wu, wd):
    d = xb.shape[1]
    n_blocks = be.shape[0]
    d_ff = wg.shape[2]
    tf = MOE_FF_TILE
    nf = d_ff // tf
    bsz = MOE_BLOCK

    def ff_idx(f, nvr, b):
        return jnp.where(nvr[b] > 0, f, nf - 1)

    grid_spec = pltpu.PrefetchScalarGridSpec(
        num_scalar_prefetch=2,
        grid=(n_blocks, nf),
        in_specs=[
            pl.BlockSpec((bsz, d), lambda b, f, *_: (b, 0)),
            pl.BlockSpec((bsz, 1), lambda b, f, *_: (b, 0)),
            pl.BlockSpec((1, d, tf), lambda b, f, be_r, nv_r: (be_r[b], 0, ff_idx(f, nv_r, b))),
            pl.BlockSpec((1, d, tf), lambda b, f, be_r, nv_r: (be_r[b], 0, ff_idx(f, nv_r, b))),
            pl.BlockSpec((1, tf, d), lambda b, f, be_r, nv_r: (be_r[b], ff_idx(f, nv_r, b), 0)),
        ],
        out_specs=pl.BlockSpec((bsz, d), lambda b, f, *_: (b, 0)),
        scratch_shapes=[
            pltpu.VMEM((bsz, d), F32),
            pltpu.VMEM((d, tf), BF16), pltpu.VMEM((d, tf), BF16), pltpu.VMEM((tf, d), BF16),
        ],
    )
    return pl.pallas_call(
        _experts_kernel,
        grid_spec=grid_spec,
        out_shape=jax.ShapeDtypeStruct((n_blocks * bsz, d), BF16),
        compiler_params=_params("arbitrary", "arbitrary"),
        name="experts",
    )(be, nv, xb, gs, wg, wu, wd)


def _combine_kernel(win_ref, x_ref, pos_ref, g_ref, fg_ref, *refs):
    yb_refs, o_ref = refs[:-1], refs[-1]
    i = pl.program_id(0)
    n_sub = x_ref.shape[0] // COMBINE_ROWS
    lane = lax.broadcasted_iota(jnp.int32, (COMBINE_ROWS, 2 * COMBINE_ROWS), 1)
    for s in range(n_sub):
        rows = slice(s * COMBINE_ROWS, (s + 1) * COMBINE_ROWS)
        onehots, windows = [], []
        for e in range(N_EXPERTS):
            k = (i * n_sub + s) * N_EXPERTS + e
            base = win_ref[k] * COMBINE_ROWS
            rel = pos_ref[rows, e:e + 1] - base
            onehots.append(jnp.where(lane == rel, 1.0, 0.0).astype(BF16))
            w0 = 2 * (s * N_EXPERTS + e)
            windows += [yb_refs[w0][...], yb_refs[w0 + 1][...]]
        acc = jnp.dot(jnp.concatenate(onehots, axis=1), jnp.concatenate(windows, axis=0),
                      preferred_element_type=F32)
        xo = x_ref[rows, :] + g_ref[...] * acc
        ms = jnp.mean(xo * xo, axis=-1, keepdims=True)
        o_ref[rows, :] = xo * lax.rsqrt(ms + EPS) * fg_ref[...]


def _combine(win, x, pos, g2, fgain, yb):
    t, d = x.shape
    tm = COMBINE_TILE
    n_sub = tm // COMBINE_ROWS
    n_win = yb.shape[0] // COMBINE_ROWS
    vec = pl.BlockSpec((1, d), lambda i, w: (0, 0))

    def first(s, e):
        return lambda i, w: (jnp.minimum(w[(i * n_sub + s) * N_EXPERTS + e], n_win - 1), 0)

    def second(s, e):
        return lambda i, w: (jnp.minimum(w[(i * n_sub + s) * N_EXPERTS + e] + 1, n_win - 1), 0)

    yb_specs = []
    for s in range(n_sub):
        for e in range(N_EXPERTS):
            yb_specs.append(pl.BlockSpec((COMBINE_ROWS, d), first(s, e)))
            yb_specs.append(pl.BlockSpec((COMBINE_ROWS, d), second(s, e)))
    grid_spec = pltpu.PrefetchScalarGridSpec(
        num_scalar_prefetch=1,
        grid=(t // tm,),
        in_specs=[
            pl.BlockSpec((tm, d), lambda i, w: (i, 0)),
            pl.BlockSpec((tm, N_EXPERTS), lambda i, w: (i, 0)),
            vec, vec,
        ] + yb_specs,
        out_specs=pl.BlockSpec((tm, d), lambda i, w: (i, 0)),
    )
    return pl.pallas_call(
        _combine_kernel,
        grid_spec=grid_spec,
        out_shape=jax.ShapeDtypeStruct((t, d), F32),
        compiler_params=_params("arbitrary"),
        name="combine",
    )(win, x, pos, g2, fgain, *([yb] * len(yb_specs)))


def _routing_tables(rank, tot, sel):
    t = rank.shape[0]
    bsz = MOE_BLOCK
    n_blocks = (2 * t) // bsz + N_EXPERTS
    cum = tot[:, 0, :N_EXPERTS].astype(jnp.int32)
    counts = cum[-1]
    padded = ((counts + bsz - 1) // bsz) * bsz
    pad_end = jnp.cumsum(padded)
    start_pad = pad_end - padded
    selected = sel[:, :N_EXPERTS] > 0
    pos = jnp.where(selected, start_pad[None, :] + rank[:, :N_EXPERTS].astype(jnp.int32), -1)

    bstart = jnp.arange(n_blocks, dtype=jnp.int32) * bsz
    be = jnp.minimum(jnp.sum(bstart[:, None] >= pad_end[None, :], axis=1), N_EXPERTS - 1).astype(jnp.int32)
    r0 = bstart - start_pad[be]
    nv = jnp.clip(counts[be] - r0, 0, bsz).astype(jnp.int32)
    n_gather = bsz // MOE_GATHER_ROWS
    goff = jnp.arange(n_gather, dtype=jnp.int32) * MOE_GATHER_ROWS
    r0g = (r0[:, None] + goff[None, :]).reshape(-1)
    ng = jnp.clip(nv[:, None] - goff[None, :], 0, MOE_GATHER_ROWS).reshape(-1)
    beg = jnp.repeat(be, n_gather)
    chunk_end = cum[:, beg].T
    chunk_start = jnp.concatenate([jnp.zeros((1, N_EXPERTS), jnp.int32), cum[:-1]], axis=0)[:, beg].T
    clo = jnp.sum(chunk_end <= r0g[:, None], axis=1).astype(jnp.int32)
    chi = jnp.sum(chunk_start < (r0g + ng)[:, None], axis=1).astype(jnp.int32)
    beu = jnp.repeat(be, bsz // MOE_GATHER_UNIT)

    big = jnp.int32(2 ** 30)
    first = jnp.min(jnp.where(pos >= 0, pos, big).reshape(t // COMBINE_ROWS, COMBINE_ROWS, N_EXPERTS), axis=1)
    win = jnp.where(first == big, 0, first // COMBINE_ROWS).astype(jnp.int32).reshape(-1)
    return pos, be, nv, beu, ng, clo, chi, win


def _rotary_tables(seq, dk):
    half = dk // 2
    inv = ROPE_BASE ** (-jnp.arange(half, dtype=F32) / half)
    ang_a = (jnp.arange(seq // ROT_BLOCK) * ROT_BLOCK).astype(F32)[:, None] * inv[None, :]
    ang_b = jnp.arange(ROT_BLOCK).astype(F32)[:, None] * inv[None, :]
    return jnp.cos(ang_a), jnp.sin(ang_a), jnp.cos(ang_b), jnp.sin(ang_b)


def _decay_tables(dk, dv):
    c = RET_CHUNK
    log_gamma = jnp.log1p(-jnp.exp2(-5.0 - jnp.arange(N_HEADS, dtype=F32)))
    idx = jnp.arange(c, dtype=F32)
    row_dec = jnp.exp(log_gamma[:, None] * (idx + 1.0 - c))
    causal = (idx[:, None] >= idx[None, :]).astype(F32)
    dmask = row_dec[:, :, None] * causal[None]
    k_dec = jnp.exp(log_gamma[:, None] * (c - 1.0 - idx)) * (dk ** -0.5)
    q_dec = jnp.exp(log_gamma[:, None] * (idx + 1.0))
    chunk_dec = jnp.exp(log_gamma * c)
    half = dk // 2
    kdec_tile = jnp.tile(jnp.repeat(k_dec.T, half, axis=1), (ROW_TILE // c, 1))
    qdec_tile = jnp.broadcast_to(q_dec[:, :, None], (N_HEADS, c, dv))
    return dmask, kdec_tile, qdec_tile, chunk_dec


def _pool_bands(tm):
    def bands(lag):
        return jnp.stack([((lag >= 0) & (lag < w)) for w in POOL_WINDOWS]).astype(BF16)
    main = bands(jnp.arange(tm)[:, None] - jnp.arange(tm)[None, :])
    head = bands(jnp.arange(POOL_HALO)[:, None] + POOL_HALO - jnp.arange(POOL_HALO)[None, :])
    return main, head


def kernel(x, c, ada_w, ada_b, norm_gain, ret_w_in, ret_gn_gain, ret_w_out, ffn_w_gate, ffn_w_up, ffn_w_down,
           pool_w, pool_b, pool_scale, moe_router, moe_w_gate, moe_w_up, moe_w_down, final_norm_gain):
    bsz, seq, d = x.shape
    assert bsz == 1 and ada_w.shape[0] == 2
    xt = x.reshape(seq, d)
    qk_total = d
    v_total = ret_w_out.shape[1]
    dk = qk_total // N_HEADS
    dv = v_total // N_HEADS

    mod = _modulation(c, ada_w, ada_b)
    def mods(i):
        parts = [mod[i, :, k * d:(k + 1) * d] for k in range(N_MOD)]
        sh1, sc1, g1, sh2, sc2, g2 = parts
        return sh1, 1.0 + sc1, g1, sh2, 1.0 + sc2, g2

    sh1, sc1p, g1, sh2, sc2p, g2 = mods(0)
    dmask, kdec_tile, qdec_tile, chunk_dec = _decay_tables(dk, dv)
    proj, (w_out_b, wg_b, wu_b, wd_b) = _ret_proj(
        xt, norm_gain[0, 0][None], sc1p, sh1, ret_w_in[0].astype(BF16), _rotary_tables(seq, dk), kdec_tile,
        qk_total, v_total, [ret_w_out[0], ffn_w_gate[0], ffn_w_up[0], ffn_w_down[0]])
    y = _retention_core(proj, chunk_dec, dmask, qdec_tile, ret_gn_gain[0][None], qk_total, v_total)
    x1 = _ret_out(y, w_out_b, xt, g1)
    x2 = _dense_ffn(x1, norm_gain[0, 1][None], sc2p, sh2, g2, wg_b, wu_b, wd_b)

    sh1, sc1p, g1, sh2, sc2p, g2 = mods(1)
    band, bandh = _pool_bands(POOL_SUB_ROWS)
    x3 = _pool_mixer(x2, norm_gain[1, 0][None], sc1p, sh1, g1, band, bandh,
                     pool_w[0].astype(BF16), pool_b[0].reshape(1, d), pool_scale[0][None])

    wr = jnp.pad(moe_router[0], ((0, 0), (0, LANES - N_EXPERTS)))
    wr_hi = wr.astype(BF16)
    wr_lo = (wr - wr_hi.astype(F32)).astype(BF16)
    tri = (jnp.arange(RANK_ROWS)[:, None] >= jnp.arange(RANK_ROWS)[None, :]).astype(BF16)
    h4, gates, sel, rank, tot = _router(x3, norm_gain[1, 1][None], sc2p, sh2,
                                        jnp.concatenate([wr_hi, wr_lo], axis=1), tri)
    pos, be, nv, beu, ng, clo, chi, win = _routing_tables(rank, tot, sel)
    xb, gs = _gather(beu, ng, clo, chi, h4, pos.T, gates[:, :N_EXPERTS].T)
    yb = _experts(be, nv, xb, gs, moe_w_gate[0], moe_w_up[0], moe_w_down[0])
    out = _combine(win, x3, pos, g2, final_norm_gain[None], yb)
    return out.reshape(bsz, seq, d)
```

```python
import functools

import jax
import jax.numpy as jnp
from jax import lax
from jax.experimental import pallas as pl
from jax.experimental.pallas import tpu as pltpu

F32 = jnp.float32
BF16 = jnp.bfloat16

EPS = 1e-6
N_HEADS = 4
RET_CHUNK = 256
ROPE_BASE = 10000.0
ROT_BLOCK = 128
POOL_WINDOWS = (2, 4, 8, 16)
POOL_HALO = 16
N_EXPERTS = 8
N_MOD = 6

VMEM_LIMIT_BYTES = 56 * 1024 * 1024

ROW_TILE = 1024
PROJ_COL_TILE = 3072
RET_ROWS = 512
POOL_ROWS = 1024
POOL_SUB_ROWS = 256
RANK_ROWS = 512
MOE_BLOCK = 2048
MOE_DOT_ROWS = 512
MOE_GATHER_UNIT = 1024
MOE_GATHER_ROWS = 256
MOE_FF_TILE = 512
GATHER_CHUNK = 512
COMBINE_TILE = 256
COMBINE_ROWS = 128
LANES = 128


def _silu(v):
    return v / (1.0 + jnp.exp(-v))


def _norm_mod(x, gain, scale1p, shift):
    ms = jnp.mean(x * x, axis=-1, keepdims=True)
    return (x * lax.rsqrt(ms + EPS) * gain) * scale1p + shift


def _params(*sem):
    return pltpu.CompilerParams(dimension_semantics=sem, vmem_limit_bytes=VMEM_LIMIT_BYTES)


def _resident(shape):
    nd = len(shape)
    return pl.BlockSpec(shape, lambda *_: (0,) * nd, pipeline_mode=pl.Buffered(1))


def _mod_kernel(c_ref, w_ref, b_ref, o_ref):
    ca = _silu(c_ref[...])
    o_ref[0] = jnp.sum(ca * w_ref[0], axis=0, keepdims=True) + b_ref[0]


def _modulation(c, ada_w, ada_b):
    depth, d, n = ada_w.shape
    tn = 1024
    return pl.pallas_call(
        _mod_kernel,
        grid=(depth, n // tn),
        in_specs=[
            pl.BlockSpec((d, 1), lambda i, j: (0, 0)),
            pl.BlockSpec((1, d, tn), lambda i, j: (i, 0, j)),
            pl.BlockSpec((1, 1, tn), lambda i, j: (i, 0, j)),
        ],
        out_specs=pl.BlockSpec((1, 1, tn), lambda i, j: (i, 0, j)),
        out_shape=jax.ShapeDtypeStruct((depth, 1, n), F32),
        compiler_params=_params("arbitrary", "arbitrary"),
        name="mod",
    )(c.reshape(d, 1), ada_w, ada_b.reshape(depth, 1, n))


def _proj_kernel(x_ref, gain_ref, sc_ref, sh_ref, w_ref, cosa_ref, sina_ref, cosb_ref, sinb_ref, kdec_ref, *refs,
                 n_qk_chunks, n_v_chunks, n_col_tiles, dk, n_cast):
    cast_in = refs[:n_cast]
    o_ref = refs[n_cast]
    cast_out = refs[n_cast + 1:2 * n_cast + 1]
    h_ref, cos_ref, sin_ref = refs[2 * n_cast + 1:]
    j = pl.program_id(1)

    @pl.when(j == 0)
    def _():
        h_ref[...] = _norm_mod(x_ref[...], gain_ref[...], sc_ref[...], sh_ref[...]).astype(BF16)
        for src, dst in zip(cast_in, cast_out):
            dst[...] = src[...].astype(BF16)
        cb = cosb_ref[...]
        sb = sinb_ref[...]
        for a in range(cosa_ref.shape[0]):
            ca = cosa_ref[a:a + 1, :]
            sa = sina_ref[a:a + 1, :]
            cos_ref[a * ROT_BLOCK:(a + 1) * ROT_BLOCK, :] = ca * cb - sa * sb
            sin_ref[a * ROT_BLOCK:(a + 1) * ROT_BLOCK, :] = sa * cb + ca * sb

    half = dk // 2
    n_chunks = w_ref.shape[1] // dk

    def chunk(c, kind, head):
        acc = jnp.dot(h_ref[...], w_ref[:, c * dk:(c + 1) * dk], preferred_element_type=F32)
        if kind == "v":
            o_ref[:, c * dk:(c + 1) * dk] = acc.astype(BF16)
        elif kind == "g":
            o_ref[:, c * dk:(c + 1) * dk] = _silu(acc).astype(BF16)
        else:
            cos = cos_ref[...]
            sin = sin_ref[...]
            t1 = acc[:, :half]
            t2 = acc[:, half:]
            o1 = t1 * cos - t2 * sin
            o2 = t1 * sin + t2 * cos
            if kind == "k":
                s = kdec_ref[:, head * half:(head + 1) * half]
                o1 = o1 * s
                o2 = o2 * s
            o_ref[:, c * dk:c * dk + half] = o1.astype(BF16)
            o_ref[:, c * dk + half:(c + 1) * dk] = o2.astype(BF16)

    def kind_of(gc):
        if gc < n_qk_chunks:
            return "q", gc
        if gc < 2 * n_qk_chunks:
            return "k", gc - n_qk_chunks
        if gc < 2 * n_qk_chunks + n_v_chunks:
            return "v", 0
        return "g", 0

    for jj in range(n_col_tiles):
        @pl.when(j == jj)
        def _():
            for c in range(n_chunks):
                chunk(c, *kind_of(jj * n_chunks + c))


def _ret_proj(x, gain, sc1p, sh, w_in, rot, kdec, qk_total, v_total, to_cast):
    t, d = x.shape
    n = w_in.shape[1]
    tm, tn = ROW_TILE, PROJ_COL_TILE
    n_row_tiles = t // tm
    dk = qk_total // N_HEADS
    half = dk // 2
    cosa, sina, cosb, sinb = rot
    kern = functools.partial(_proj_kernel, n_qk_chunks=qk_total // dk, n_v_chunks=v_total // dk,
                             n_col_tiles=n // tn, dk=dk, n_cast=len(to_cast))
    vec = pl.BlockSpec((1, d), lambda i, j: (0, 0))
    rot_a = pl.BlockSpec((tm // ROT_BLOCK, half), lambda i, j: (i, 0))
    rot_b = pl.BlockSpec((ROT_BLOCK, half), lambda i, j: (0, 0))
    cast_specs = [pl.BlockSpec((w.shape[0] // n_row_tiles, w.shape[1]), lambda i, j: (i, 0)) for w in to_cast]
    outs = pl.pallas_call(
        kern,
        grid=(n_row_tiles, n // tn),
        in_specs=[
            pl.BlockSpec((tm, d), lambda i, j: (i, 0)),
            vec, vec, vec,
            pl.BlockSpec((d, tn), lambda i, j: (0, j)),
            rot_a, rot_a, rot_b, rot_b,
            pl.BlockSpec((tm, N_HEADS * half), lambda i, j: (0, 0)),
        ] + cast_specs,
        out_specs=[pl.BlockSpec((tm, tn), lambda i, j: (i, j))] + cast_specs,
        out_shape=[jax.ShapeDtypeStruct((t, n), BF16)] + [jax.ShapeDtypeStruct(w.shape, BF16) for w in to_cast],
        scratch_shapes=[pltpu.VMEM((tm, d), BF16), pltpu.VMEM((tm, half), F32), pltpu.VMEM((tm, half), F32)],
        compiler_params=_params("arbitrary", "arbitrary"),
        name="proj",
    )(x, gain, sc1p, sh, w_in, cosa, sina, cosb, sinb, kdec, *to_cast)
    return outs[0], outs[1:]


def _ret_kernel(cdec_ref, q_ref, k_ref, v_ref, sg_ref, dmask_ref, qdec_ref, gn_ref, x_ref, g_ref, wo_ref, o_ref,
                state_ref, y_ref, *, dk, dv):
    @pl.when(pl.program_id(0) == 0)
    def _():
        state_ref[...] = jnp.zeros_like(state_ref)

    rows_per_step = q_ref.shape[0]
    for n in range(rows_per_step // RET_CHUNK):
        rows = slice(n * RET_CHUNK, (n + 1) * RET_CHUNK)
        if n > 0:
            prev = slice((n - 1) * RET_CHUNK, n * RET_CHUNK)
            o_ref[prev, :] = x_ref[prev, :] + g_ref[...] * jnp.dot(y_ref[prev, :], wo_ref[...],
                                                                   preferred_element_type=F32)
        for h in range(N_HEADS):
            qcols = slice(h * dk, (h + 1) * dk)
            vcols = slice(h * dv, (h + 1) * dv)
            q = q_ref[rows, qcols]
            ks = k_ref[rows, qcols]
            v = v_ref[rows, vcols]
            s = lax.dot_general(q, ks, (((1,), (1,)), ((), ())), preferred_element_type=F32)
            a = (s * dmask_ref[h]).astype(BF16)
            intra = jnp.dot(a, v, preferred_element_type=F32)
            st = state_ref[h]
            cross = jnp.dot(q, st.astype(BF16), preferred_element_type=F32)
            o = intra + cross * qdec_ref[h]
            kv = lax.dot_general(ks, v, (((0,), (0,)), ((), ())), preferred_element_type=F32)
            state_ref[h] = st * cdec_ref[h] + kv
            mu = jnp.mean(o, axis=-1, keepdims=True)
            dlt = o - mu
            var = jnp.mean(dlt * dlt, axis=-1, keepdims=True)
            on = dlt * lax.rsqrt(var + EPS)
            y = on * gn_ref[:, vcols] * sg_ref[rows, vcols].astype(F32)
            y_ref[rows, vcols] = y.astype(BF16)
    last = slice(rows_per_step - RET_CHUNK, rows_per_step)
    o_ref[last, :] = x_ref[last, :] + g_ref[...] * jnp.dot(y_ref[last, :], wo_ref[...],
                                                           preferred_element_type=F32)


def _retention(proj, cdec, dmask, qdec, gn_gain, x, g1, w_out, qk_total, v_total):
    t, d = x.shape
    r = RET_ROWS
    dk = qk_total // N_HEADS
    dv = v_total // N_HEADS
    assert v_total == 2 * qk_total
    kern = functools.partial(_ret_kernel, dk=dk, dv=dv)
    return pl.pallas_call(
        kern,
        grid=(t // r,),
        in_specs=[
            pl.BlockSpec(memory_space=pltpu.SMEM),
            pl.BlockSpec((r, qk_total), lambda i: (i, 0)),
            pl.BlockSpec((r, qk_total), lambda i: (i, 1)),
            pl.BlockSpec((r, v_total), lambda i: (i, 1)),
            pl.BlockSpec((r, v_total), lambda i: (i, 2)),
            pl.BlockSpec((N_HEADS, RET_CHUNK, RET_CHUNK), lambda i: (0, 0, 0)),
            pl.BlockSpec((N_HEADS, RET_CHUNK, dv), lambda i: (0, 0, 0)),
            pl.BlockSpec((1, v_total), lambda i: (0, 0)),
            pl.BlockSpec((r, d), lambda i: (i, 0)),
            pl.BlockSpec((1, d), lambda i: (0, 0)),
            _resident((v_total, d)),
        ],
        out_specs=pl.BlockSpec((r, d), lambda i: (i, 0)),
        out_shape=jax.ShapeDtypeStruct((t, d), F32),
        scratch_shapes=[pltpu.VMEM((N_HEADS, dk, dv), F32), pltpu.VMEM((r, v_total), BF16)],
        compiler_params=_params("arbitrary"),
        name="ret",
    )(cdec, proj, proj, proj, proj, dmask, qdec, gn_gain, x, g1, w_out)


def _ffn_kernel(x_ref, gain_ref, sc_ref, sh_ref, g_ref, wg_ref, wu_ref, wd_ref, o_ref, *, ff_tile):
    x = x_ref[...]
    h = _norm_mod(x, gain_ref[...], sc_ref[...], sh_ref[...]).astype(BF16)
    d_ff = wg_ref.shape[1]
    acc = jnp.zeros(x.shape, F32)
    for lo in range(0, d_ff, ff_tile):
        hi = min(lo + ff_tile, d_ff)
        a = jnp.dot(h, wg_ref[:, lo:hi], preferred_element_type=F32)
        b = jnp.dot(h, wu_ref[:, lo:hi], preferred_element_type=F32)
        act = (_silu(a) * b).astype(BF16)
        acc = acc + jnp.dot(act, wd_ref[lo:hi, :], preferred_element_type=F32)
    o_ref[...] = x + g_ref[...] * acc


def _dense_ffn(x, gain, sc1p, sh, g2, wg, wu, wd):
    t, d = x.shape
    d_ff = wg.shape[1]
    tm = ROW_TILE
    vec = pl.BlockSpec((1, d), lambda i: (0, 0))
    return pl.pallas_call(
        functools.partial(_ffn_kernel, ff_tile=512),
        grid=(t // tm,),
        in_specs=[
            pl.BlockSpec((tm, d), lambda i: (i, 0)),
            vec, vec, vec, vec,
            _resident((d, d_ff)), _resident((d, d_ff)), _resident((d_ff, d)),
        ],
        out_specs=pl.BlockSpec((tm, d), lambda i: (i, 0)),
        out_shape=jax.ShapeDtypeStruct((t, d), F32),
        compiler_params=_params("arbitrary"),
        name="ffn",
    )(x, gain, sc1p, sh, g2, wg, wu, wd)


def _pool_kernel(x_ref, gain_ref, sc_ref, sh_ref, g_ref, band_ref, bandh_ref, wp_ref, bp_ref, ps_ref, o_ref,
                 hbuf_ref):
    i = pl.program_id(0)
    tm = x_ref.shape[0]
    gw = wp_ref.shape[1]

    @pl.when(i == 0)
    def _():
        hbuf_ref[0:POOL_HALO, :] = jnp.zeros((POOL_HALO, hbuf_ref.shape[1]), F32)

    @pl.when(i > 0)
    def _():
        hbuf_ref[0:POOL_HALO, :] = hbuf_ref[tm:tm + POOL_HALO, :]

    hbuf_ref[POOL_HALO:, :] = _norm_mod(x_ref[...], gain_ref[...], sc_ref[...], sh_ref[...])

    def split(v):
        hi = v.astype(BF16)
        return hi, (v - hi.astype(F32)).astype(BF16)

    sub = band_ref.shape[1]
    for s in range(tm // sub):
        rows = slice(s * sub, (s + 1) * sub)
        lo = POOL_HALO + s * sub
        x = x_ref[rows, :]
        h = hbuf_ref[lo:lo + sub, :]
        h_hi, h_lo = split(h)
        halo_hi, halo_lo = split(hbuf_ref[lo - POOL_HALO:lo, :])
        t1 = (lax.broadcasted_iota(jnp.int32, (sub, 1), 0) + (i * tm + s * sub + 1)).astype(F32)
        for g, w in enumerate(POOL_WINDOWS):
            cols = slice(g * gw, (g + 1) * gw)
            band = band_ref[g]
            bandh = bandh_ref[g]
            win = (jnp.dot(band, h_hi[:, cols], preferred_element_type=F32)
                   + jnp.dot(band, h_lo[:, cols], preferred_element_type=F32))
            head = (jnp.dot(bandh, halo_hi[:, cols], preferred_element_type=F32)
                    + jnp.dot(bandh, halo_lo[:, cols], preferred_element_type=F32))
            win = jnp.concatenate([win[:POOL_HALO] + head, win[POOL_HALO:]], axis=0)
            pooled = win * (1.0 / jnp.minimum(t1, float(w))) - h[:, cols]
            y = jnp.dot(pooled.astype(BF16), wp_ref[g], preferred_element_type=F32) + bp_ref[:, cols]
            o_ref[rows, cols] = x[:, cols] + g_ref[:, cols] * (y * ps_ref[:, cols])


def _route(x, gain_ref, sc_ref, sh_ref, w_ref, tri_ref, h_ref, gates_ref, sel_ref, rank_ref, tot_ref, carry_ref):
    h = _norm_mod(x, gain_ref[...], sc_ref[...], sh_ref[...])
    h_hi = h.astype(BF16)
    h_ref[...] = h_hi
    h_lo = (h - h_hi.astype(F32)).astype(BF16)
    w = w_ref[...]
    p_hi = jnp.dot(h_hi, w, preferred_element_type=F32)
    p_lo = jnp.dot(h_lo, w, preferred_element_type=F32)
    logits = p_hi[:, :LANES] + p_hi[:, LANES:] + p_lo[:, :LANES] + p_lo[:, LANES:]
    lane = lax.broadcasted_iota(jnp.int32, logits.shape, 1).astype(F32)
    neg = jnp.float32(-jnp.inf)
    lg = jnp.where(lane < float(N_EXPERTS), logits, neg)
    m0 = jnp.max(lg, axis=-1, keepdims=True)
    i0 = jnp.min(jnp.where(lg == m0, lane, float(LANES)), axis=-1, keepdims=True)
    lg1 = jnp.where(lane == i0, neg, lg)
    m1 = jnp.max(lg1, axis=-1, keepdims=True)
    i1 = jnp.min(jnp.where(lg1 == m1, lane, float(LANES)), axis=-1, keepdims=True)
    e1 = jnp.exp(m1 - m0)
    den = 1.0 + e1
    is0 = lane == i0
    is1 = lane == i1
    gates_ref[...] = jnp.where(is0, 1.0 / den, jnp.where(is1, e1 / den, 0.0))
    sel = jnp.where(is0 | is1, 1.0, 0.0).astype(BF16)
    sel_ref[...] = sel

    @pl.when(pl.program_id(0) == 0)
    def _():
        carry_ref[...] = jnp.zeros_like(carry_ref)

    carry = carry_ref[...]
    for b in range(sel.shape[0] // RANK_ROWS):
        rows = slice(b * RANK_ROWS, (b + 1) * RANK_ROWS)
        s = sel[rows, :]
        cum = jnp.dot(tri_ref[...], s, preferred_element_type=F32)
        rank_ref[rows, :] = cum - s.astype(F32) + carry
        carry = carry + cum[RANK_ROWS - 1:, :]
        tot_ref[b] = carry
    carry_ref[...] = carry


def _pool_router_kernel(x_ref, gain_ref, sc_ref, sh_ref, g_ref, band_ref, bandh_ref, wp_ref, bp_ref, ps_ref,
                        gain2_ref, sc2_ref, sh2_ref, wr_ref, tri_ref,
                        o_ref, h_ref, gates_ref, sel_ref, rank_ref, tot_ref, hbuf_ref, carry_ref):
    _pool_kernel(x_ref, gain_ref, sc_ref, sh_ref, g_ref, band_ref, bandh_ref, wp_ref, bp_ref, ps_ref, o_ref,
                 hbuf_ref)
    _route(o_ref[...], gain2_ref, sc2_ref, sh2_ref, wr_ref, tri_ref, h_ref, gates_ref, sel_ref, rank_ref, tot_ref,
           carry_ref)


def _pool_router(x, gain, sc1p, sh, g1, band, bandh, wp, bp, ps, gain2, sc2p, sh2, w_cat, tri):
    t, d = x.shape
    tm = POOL_ROWS
    n_sub = tm // RANK_ROWS
    vec = pl.BlockSpec((1, d), lambda i: (0, 0))
    rows = pl.BlockSpec((tm, d), lambda i: (i, 0))
    lanes = pl.BlockSpec((tm, LANES), lambda i: (i, 0))
    return pl.pallas_call(
        _pool_router_kernel,
        grid=(t // tm,),
        in_specs=[
            rows, vec, vec, vec, vec,
            pl.BlockSpec(band.shape, lambda i: (0, 0, 0)),
            pl.BlockSpec(bandh.shape, lambda i: (0, 0, 0)),
            pl.BlockSpec(wp.shape, lambda i: (0, 0, 0)),
            vec, vec,
            vec, vec, vec,
            pl.BlockSpec((d, 2 * LANES), lambda i: (0, 0)),
            pl.BlockSpec((RANK_ROWS, RANK_ROWS), lambda i: (0, 0)),
        ],
        out_specs=[rows, rows, lanes, lanes, lanes, pl.BlockSpec((n_sub, 1, LANES), lambda i: (i, 0, 0))],
        out_shape=[
            jax.ShapeDtypeStruct((t, d), F32),
            jax.ShapeDtypeStruct((t, d), BF16),
            jax.ShapeDtypeStruct((t, LANES), F32),
            jax.ShapeDtypeStruct((t, LANES), BF16),
            jax.ShapeDtypeStruct((t, LANES), F32),
            jax.ShapeDtypeStruct((t // RANK_ROWS, 1, LANES), F32),
        ],
        scratch_shapes=[pltpu.VMEM((POOL_HALO + tm, d), F32), pltpu.VMEM((1, LANES), F32)],
        compiler_params=_params("arbitrary"),
        name="pool_router",
    )(x, gain, sc1p, sh, g1, band, bandh, wp, bp, ps, gain2, sc2p, sh2, w_cat, tri)


def _gather_kernel(be_ref, ng_ref, clo_ref, chi_ref, h_ref, post_ref, gatest_ref, xb_ref, gs_ref, acc_ref):
    u = pl.program_id(0)
    e = be_ref[u]
    unit_rows = xb_ref.shape[0]
    n_groups = unit_rows // MOE_GATHER_ROWS
    last_chunk = h_ref.shape[0] // GATHER_CHUNK - 1
    g0 = u * n_groups

    @pl.when(ng_ref[g0] > 0)
    def _():
        acc_ref[...] = jnp.zeros_like(acc_ref)
        trips = chi_ref[g0] - clo_ref[g0]
        for q in range(1, n_groups):
            trips = jnp.maximum(trips, chi_ref[g0 + q] - clo_ref[g0 + q])

        def body(i, gsums):
            out = []
            for q in range(n_groups):
                rows = slice(q * MOE_GATHER_ROWS, (q + 1) * MOE_GATHER_ROWS)
                c = clo_ref[g0 + q] + i
                live = c < chi_ref[g0 + q]
                off = pl.multiple_of(jnp.minimum(c, last_chunk) * GATHER_CHUNK, GATHER_CHUNK)
                p = post_ref[pl.ds(e, 1), pl.ds(off, GATHER_CHUNK)]
                p = jnp.where(live, p, -2)
                row_pos = (lax.broadcasted_iota(jnp.int32, (MOE_GATHER_ROWS, GATHER_CHUNK), 0)
                           + (u * unit_rows + q * MOE_GATHER_ROWS))
                match = row_pos == p
                onehot = jnp.where(match, 1.0, 0.0).astype(BF16)
                acc_ref[rows, :] += jnp.dot(onehot, h_ref[pl.ds(off, GATHER_CHUNK), :],
                                            preferred_element_type=F32)
                gate = gatest_ref[pl.ds(e, 1), pl.ds(off, GATHER_CHUNK)]
                out.append(gsums[q] + jnp.sum(jnp.where(match, gate, 0.0), axis=1, keepdims=True))
            return tuple(out)

        zero = jnp.zeros((MOE_GATHER_ROWS, 1), F32)
        gsums = lax.fori_loop(0, trips, body, (zero,) * n_groups)
        xb_ref[...] = acc_ref[...].astype(BF16)
        for q in range(n_groups):
            gs_ref[q * MOE_GATHER_ROWS:(q + 1) * MOE_GATHER_ROWS, :] = gsums[q]

    @pl.when(ng_ref[g0] == 0)
    def _():
        xb_ref[...] = jnp.zeros_like(xb_ref)
        gs_ref[...] = jnp.zeros_like(gs_ref)


def _gather(beu, ng, clo, chi, h, post, gatest):
    t, d = h.shape
    n_units = beu.shape[0]
    grid_spec = pltpu.PrefetchScalarGridSpec(
        num_scalar_prefetch=4,
        grid=(n_units,),
        in_specs=[
            pl.BlockSpec((t, d), lambda u, *_: (0, 0), pipeline_mode=pl.Buffered(1)),
            pl.BlockSpec(post.shape, lambda u, *_: (0, 0), pipeline_mode=pl.Buffered(1)),
            pl.BlockSpec(gatest.shape, lambda u, *_: (0, 0), pipeline_mode=pl.Buffered(1)),
        ],
        out_specs=[
            pl.BlockSpec((MOE_GATHER_UNIT, d), lambda u, *_: (u, 0)),
            pl.BlockSpec((MOE_GATHER_UNIT, 1), lambda u, *_: (u, 0)),
        ],
        scratch_shapes=[pltpu.VMEM((MOE_GATHER_UNIT, d), F32)],
    )
    return pl.pallas_call(
        _gather_kernel,
        grid_spec=grid_spec,
        out_shape=[
            jax.ShapeDtypeStruct((n_units * MOE_GATHER_UNIT, d), BF16),
            jax.ShapeDtypeStruct((n_units * MOE_GATHER_UNIT, 1), F32),
        ],
        compiler_params=_params("arbitrary"),
        name="gather",
    )(beu, ng, clo, chi, h, post, gatest)


def _experts_kernel(be_ref, nv_ref, xb_ref, gs_ref, wg_ref, wu_ref, wd_ref, yb_ref,
                    acc_ref, wgb_ref, wub_ref, wdb_ref):
    b = pl.program_id(0)
    f = pl.program_id(1)
    nf = pl.num_programs(1)
    nv = nv_ref[b]
    block_rows = xb_ref.shape[0]

    @pl.when((b == 0) & (f == 0))
    def _():
        acc_ref[...] = jnp.zeros_like(acc_ref)

    def swiglu_rows(lo, n, cast):
        rows = slice(lo, lo + n)
        if cast:
            wgb_ref[...] = wg_ref[0].astype(BF16)
            wub_ref[...] = wu_ref[0].astype(BF16)
            wdb_ref[...] = wd_ref[0].astype(BF16)
        x = xb_ref[rows, :]
        a = jnp.dot(x, wgb_ref[...], preferred_element_type=F32)
        u = jnp.dot(x, wub_ref[...], preferred_element_type=F32)
        act = (_silu(a) * u).astype(BF16)
        prev = jnp.where(f > 0, acc_ref[rows, :], 0.0)
        acc_ref[rows, :] = prev + jnp.dot(act, wdb_ref[...], preferred_element_type=F32)

    pair = 2 * MOE_DOT_ROWS
    for p in range(block_rows // pair):
        lo = p * pair
        full = nv >= lo + pair

        @pl.when(full)
        def _():
            swiglu_rows(lo, pair, p == 0)

        @pl.when(jnp.logical_not(full) & (nv > lo))
        def _():
            swiglu_rows(lo, MOE_DOT_ROWS, p == 0)

        @pl.when(jnp.logical_not(full) & (nv > lo + MOE_DOT_ROWS))
        def _():
            swiglu_rows(lo + MOE_DOT_ROWS, MOE_DOT_ROWS, False)

    @pl.when(f == nf - 1)
    def _():
        for g in range(block_rows // MOE_DOT_ROWS):
            rows = slice(g * MOE_DOT_ROWS, (g + 1) * MOE_DOT_ROWS)

            @pl.when(nv > g * MOE_DOT_ROWS)
            def _():
                yb_ref[rows, :] = (acc_ref[rows, :] * gs_ref[rows, :]).astype(BF16)

            @pl.when(nv <= g * MOE_DOT_ROWS)
            def _():
                yb_ref[rows, :] = jnp.zeros((MOE_DOT_ROWS, yb_ref.shape[1]), BF16)


def _experts(be, nv, xb, gs, wg, wu, wd):
    d = xb.shape[1]
    n_blocks = be.shape[0]
    d_ff = wg.shape[2]
    tf = MOE_FF_TILE
    nf = d_ff // tf
    bsz = MOE_BLOCK

    def ff_idx(f, nvr, b):
        return jnp.where(nvr[b] > 0, f, nf - 1)

    grid_spec = pltpu.PrefetchScalarGridSpec(
        num_scalar_prefetch=2,
        grid=(n_blocks, nf),
        in_specs=[
            pl.BlockSpec((bsz, d), lambda b, f, *_: (b, 0)),
            pl.BlockSpec((bsz, 1), lambda b, f, *_: (b, 0)),
            pl.BlockSpec((1, d, tf), lambda b, f, be_r, nv_r: (be_r[b], 0, ff_idx(f, nv_r, b))),
            pl.BlockSpec((1, d, tf), lambda b, f, be_r, nv_r: (be_r[b], 0, ff_idx(f, nv_r, b))),
            pl.BlockSpec((1, tf, d), lambda b, f, be_r, nv_r: (be_r[b], ff_idx(f, nv_r, b), 0)),
        ],
        out_specs=pl.BlockSpec((bsz, d), lambda b, f, *_: (b, 0)),
        scratch_shapes=[
            pltpu.VMEM((bsz, d), F32),
            pltpu.VMEM((d, tf), BF16), pltpu.VMEM((d, tf), BF16), pltpu.VMEM((tf, d), BF16),
        ],
    )
    return pl.pallas_call(
        _experts_kernel,
        grid_spec=grid_spec,
        out_shape=jax.ShapeDtypeStruct((n_blocks * bsz, d), BF16),
        compiler_params=_params("arbitrary", "arbitrary"),
        name="experts",
    )(be, nv, xb, gs, wg, wu, wd)


def _combine_kernel(win_ref, x_ref, pos_ref, g_ref, fg_ref, *refs):
    yb_refs, o_ref = refs[:-1], refs[-1]
    i = pl.program_id(0)
    n_sub = x_ref.shape[0] // COMBINE_ROWS
    lane = lax.broadcasted_iota(jnp.int32, (COMBINE_ROWS, 2 * COMBINE_ROWS), 1)
    for s in range(n_sub):
        rows = slice(s * COMBINE_ROWS, (s + 1) * COMBINE_ROWS)
        onehots, windows = [], []
        for e in range(N_EXPERTS):
            k = (i * n_sub + s) * N_EXPERTS + e
            base = win_ref[k] * COMBINE_ROWS
            rel = pos_ref[rows, e:e + 1] - base
            onehots.append(jnp.where(lane == rel, 1.0, 0.0).astype(BF16))
            w0 = 2 * (s * N_EXPERTS + e)
            windows += [yb_refs[w0][...], yb_refs[w0 + 1][...]]
        acc = jnp.dot(jnp.concatenate(onehots, axis=1), jnp.concatenate(windows, axis=0),
                      preferred_element_type=F32)
        xo = x_ref[rows, :] + g_ref[...] * acc
        ms = jnp.mean(xo * xo, axis=-1, keepdims=True)
        o_ref[rows, :] = xo * lax.rsqrt(ms + EPS) * fg_ref[...]


def _combine(win, x, pos, g2, fgain, yb):
    t, d = x.shape
    tm = COMBINE_TILE
    n_sub = tm // COMBINE_ROWS
    n_win = yb.shape[0] // COMBINE_ROWS
    vec = pl.BlockSpec((1, d), lambda i, w: (0, 0))

    def first(s, e):
        return lambda i, w: (jnp.minimum(w[(i * n_sub + s) * N_EXPERTS + e], n_win - 1), 0)

    def second(s, e):
        return lambda i, w: (jnp.minimum(w[(i * n_sub + s) * N_EXPERTS + e] + 1, n_win - 1), 0)

    yb_specs = []
    for s in range(n_sub):
        for e in range(N_EXPERTS):
            yb_specs.append(pl.BlockSpec((COMBINE_ROWS, d), first(s, e)))
            yb_specs.append(pl.BlockSpec((COMBINE_ROWS, d), second(s, e)))
    grid_spec = pltpu.PrefetchScalarGridSpec(
        num_scalar_prefetch=1,
        grid=(t // tm,),
        in_specs=[
            pl.BlockSpec((tm, d), lambda i, w: (i, 0)),
            pl.BlockSpec((tm, N_EXPERTS), lambda i, w: (i, 0)),
            vec, vec,
        ] + yb_specs,
        out_specs=pl.BlockSpec((tm, d), lambda i, w: (i, 0)),
    )
    return pl.pallas_call(
        _combine_kernel,
        grid_spec=grid_spec,
        out_shape=jax.ShapeDtypeStruct((t, d), F32),
        compiler_params=_params("arbitrary"),
        name="combine",
    )(win, x, pos, g2, fgain, *([yb] * len(yb_specs)))


def _routing_tables(rank, tot, sel):
    t = rank.shape[0]
    bsz = MOE_BLOCK
    n_blocks = (2 * t) // bsz + N_EXPERTS
    cum = tot[:, 0, :N_EXPERTS].astype(jnp.int32)
    counts = cum[-1]
    padded = ((counts + bsz - 1) // bsz) * bsz
    pad_end = jnp.cumsum(padded)
    start_pad = pad_end - padded
    selected = sel[:, :N_EXPERTS] > 0
    pos = jnp.where(selected, start_pad[None, :] + rank[:, :N_EXPERTS].astype(jnp.int32), -1)

    bstart = jnp.arange(n_blocks, dtype=jnp.int32) * bsz
    be = jnp.minimum(jnp.sum(bstart[:, None] >= pad_end[None, :], axis=1), N_EXPERTS - 1).astype(jnp.int32)
    r0 = bstart - start_pad[be]
    nv = jnp.clip(counts[be] - r0, 0, bsz).astype(jnp.int32)
    n_gather = bsz // MOE_GATHER_ROWS
    goff = jnp.arange(n_gather, dtype=jnp.int32) * MOE_GATHER_ROWS
    r0g = (r0[:, None] + goff[None, :]).reshape(-1)
    ng = jnp.clip(nv[:, None] - goff[None, :], 0, MOE_GATHER_ROWS).reshape(-1)
    beg = jnp.repeat(be, n_gather)
    chunk_end = cum[:, beg].T
    chunk_start = jnp.concatenate([jnp.zeros((1, N_EXPERTS), jnp.int32), cum[:-1]], axis=0)[:, beg].T
    clo = jnp.sum(chunk_end <= r0g[:, None], axis=1).astype(jnp.int32)
    chi = jnp.sum(chunk_start < (r0g + ng)[:, None], axis=1).astype(jnp.int32)
    beu = jnp.repeat(be, bsz // MOE_GATHER_UNIT)

    big = jnp.int32(2 ** 30)
    first = jnp.min(jnp.where(pos >= 0, pos, big).reshape(t // COMBINE_ROWS, COMBINE_ROWS, N_EXPERTS), axis=1)
    win = jnp.where(first == big, 0, first // COMBINE_ROWS).astype(jnp.int32).reshape(-1)
    return pos, be, nv, beu, ng, clo, chi, win


def _rotary_tables(seq, dk):
    half = dk // 2
    inv = ROPE_BASE ** (-jnp.arange(half, dtype=F32) / half)
    ang_a = (jnp.arange(seq // ROT_BLOCK) * ROT_BLOCK).astype(F32)[:, None] * inv[None, :]
    ang_b = jnp.arange(ROT_BLOCK).astype(F32)[:, None] * inv[None, :]
    return jnp.cos(ang_a), jnp.sin(ang_a), jnp.cos(ang_b), jnp.sin(ang_b)


def _decay_tables(dk, dv):
    c = RET_CHUNK
    log_gamma = jnp.log1p(-jnp.exp2(-5.0 - jnp.arange(N_HEADS, dtype=F32)))
    idx = jnp.arange(c, dtype=F32)
    row_dec = jnp.exp(log_gamma[:, None] * (idx + 1.0 - c))
    causal = (idx[:, None] >= idx[None, :]).astype(F32)
    dmask = row_dec[:, :, None] * causal[None]
    k_dec = jnp.exp(log_gamma[:, None] * (c - 1.0 - idx)) * (dk ** -0.5)
    q_dec = jnp.exp(log_gamma[:, None] * (idx + 1.0))
    chunk_dec = jnp.exp(log_gamma * c)
    half = dk // 2
    kdec_tile = jnp.tile(jnp.repeat(k_dec.T, half, axis=1), (ROW_TILE // c, 1))
    qdec_tile = jnp.broadcast_to(q_dec[:, :, None], (N_HEADS, c, dv))
    return dmask, kdec_tile, qdec_tile, chunk_dec


def _pool_bands(tm):
    def bands(lag):
        return jnp.stack([((lag >= 0) & (lag < w)) for w in POOL_WINDOWS]).astype(BF16)
    main = bands(jnp.arange(tm)[:, None] - jnp.arange(tm)[None, :])
    head = bands(jnp.arange(POOL_HALO)[:, None] + POOL_HALO - jnp.arange(POOL_HALO)[None, :])
    return main, head


def kernel(x, c, ada_w, ada_b, norm_gain, ret_w_in, ret_gn_gain, ret_w_out, ffn_w_gate, ffn_w_up, ffn_w_down,
           pool_w, pool_b, pool_scale, moe_router, moe_w_gate, moe_w_up, moe_w_down, final_norm_gain):
    bsz, seq, d = x.shape
    assert bsz == 1 and ada_w.shape[0] == 2
    xt = x.reshape(seq, d)
    qk_total = d
    v_total = ret_w_out.shape[1]
    dk = qk_total // N_HEADS
    dv = v_total // N_HEADS

    mod = _modulation(c, ada_w, ada_b)
    def mods(i):
        parts = [mod[i, :, k * d:(k + 1) * d] for k in range(N_MOD)]
        sh1, sc1, g1, sh2, sc2, g2 = parts
        return sh1, 1.0 + sc1, g1, sh2, 1.0 + sc2, g2

    sh1, sc1p, g1, sh2, sc2p, g2 = mods(0)
    dmask, kdec_tile, qdec_tile, chunk_dec = _decay_tables(dk, dv)
    proj, (w_out_b, wg_b, wu_b, wd_b) = _ret_proj(
        xt, norm_gain[0, 0][None], sc1p, sh1, ret_w_in[0].astype(BF16), _rotary_tables(seq, dk), kdec_tile,
        qk_total, v_total, [ret_w_out[0], ffn_w_gate[0], ffn_w_up[0], ffn_w_down[0]])
    x1 = _retention(proj, chunk_dec, dmask, qdec_tile, ret_gn_gain[0][None], xt, g1, w_out_b, qk_total, v_total)
    x2 = _dense_ffn(x1, norm_gain[0, 1][None], sc2p, sh2, g2, wg_b, wu_b, wd_b)

    sh1, sc1p, g1, sh2, sc2p, g2 = mods(1)
    band, bandh = _pool_bands(POOL_SUB_ROWS)
    wr = jnp.pad(moe_router[0], ((0, 0), (0, LANES - N_EXPERTS)))
    wr_hi = wr.astype(BF16)
    wr_lo = (wr - wr_hi.astype(F32)).astype(BF16)
    tri = (jnp.arange(RANK_ROWS)[:, None] >= jnp.arange(RANK_ROWS)[None, :]).astype(BF16)
    x3, h4, gates, sel, rank, tot = _pool_router(
        x2, norm_gain[1, 0][None], sc1p, sh1, g1, band, bandh,
        pool_w[0].astype(BF16), pool_b[0].reshape(1, d), pool_scale[0][None],
        norm_gain[1, 1][None], sc2p, sh2, jnp.concatenate([wr_hi, wr_lo], axis=1), tri)
    pos, be, nv, beu, ng, clo, chi, win = _routing_tables(rank, tot, sel)
    xb, gs = _gather(beu, ng, clo, chi, h4, pos.T, gates[:, :N_EXPERTS].T)
    yb = _experts(be, nv, xb, gs, moe_w_gate[0], moe_w_up[0], moe_w_down[0])
    out = _combine(win, x3, pos, g2, final_norm_gain[None], yb)
    return out.reshape(bsz, seq, d)
```

```python
import functools

import jax
import jax.numpy as jnp
from jax import lax
from jax.experimental import pallas as pl
from jax.experimental.pallas import tpu as pltpu

F32 = jnp.float32
BF16 = jnp.bfloat16

EPS = 1e-6
N_HEADS = 4
RET_CHUNK = 256
ROPE_BASE = 10000.0
ROT_BLOCK = 128
POOL_WINDOWS = (2, 4, 8, 16)
POOL_HALO = 16
N_EXPERTS = 8
N_MOD = 6

VMEM_LIMIT_BYTES = 56 * 1024 * 1024

ROW_TILE = 1024
PROJ_COL_TILE = 3072
RET_ROWS = 512
POOL_ROWS = 1024
POOL_SUB_ROWS = 256
RANK_ROWS = 512
MOE_BLOCK = 2048
MOE_DOT_ROWS = 512
MOE_GATHER_UNIT = 1024
MOE_GATHER_ROWS = 256
MOE_FF_TILE = 512
GATHER_CHUNK = 512
COMBINE_TILE = 256
COMBINE_ROWS = 128
LANES = 128


def _silu(v):
    return v / (1.0 + jnp.exp(-v))


def _norm_mod(x, gain, scale1p, shift):
    ms = jnp.mean(x * x, axis=-1, keepdims=True)
    return (x * lax.rsqrt(ms + EPS)) * (gain * scale1p) + shift


def _params(*sem):
    return pltpu.CompilerParams(dimension_semantics=sem, vmem_limit_bytes=VMEM_LIMIT_BYTES)


def _resident(shape):
    nd = len(shape)
    return pl.BlockSpec(shape, lambda *_: (0,) * nd, pipeline_mode=pl.Buffered(1))


def _mod_kernel(c_ref, w_ref, b_ref, o_ref):
    ca = _silu(c_ref[...])
    o_ref[0] = jnp.sum(ca * w_ref[0], axis=0, keepdims=True) + b_ref[0]


def _modulation(c, ada_w, ada_b):
    depth, d, n = ada_w.shape
    tn = 1024
    return pl.pallas_call(
        _mod_kernel,
        grid=(depth, n // tn),
        in_specs=[
            pl.BlockSpec((d, 1), lambda i, j: (0, 0)),
            pl.BlockSpec((1, d, tn), lambda i, j: (i, 0, j)),
            pl.BlockSpec((1, 1, tn), lambda i, j: (i, 0, j)),
        ],
        out_specs=pl.BlockSpec((1, 1, tn), lambda i, j: (i, 0, j)),
        out_shape=jax.ShapeDtypeStruct((depth, 1, n), F32),
        compiler_params=_params("arbitrary", "arbitrary"),
        name="mod",
    )(c.reshape(d, 1), ada_w, ada_b.reshape(depth, 1, n))


def _proj_kernel(x_ref, gain_ref, sc_ref, sh_ref, w_ref, cosa_ref, sina_ref, cosb_ref, sinb_ref, kdec_ref, *refs,
                 n_qk_chunks, n_v_chunks, n_col_tiles, dk, n_cast):
    cast_in = refs[:n_cast]
    o_ref = refs[n_cast]
    cast_out = refs[n_cast + 1:2 * n_cast + 1]
    h_ref, cos_ref, sin_ref = refs[2 * n_cast + 1:]
    j = pl.program_id(1)

    @pl.when(j == 0)
    def _():
        h_ref[...] = _norm_mod(x_ref[...], gain_ref[...], sc_ref[...], sh_ref[...]).astype(BF16)
        for src, dst in zip(cast_in, cast_out):
            dst[...] = src[...].astype(BF16)
        cb = cosb_ref[...]
        sb = sinb_ref[...]
        for a in range(cosa_ref.shape[0]):
            ca = cosa_ref[a:a + 1, :]
            sa = sina_ref[a:a + 1, :]
            cos_ref[a * ROT_BLOCK:(a + 1) * ROT_BLOCK, :] = ca * cb - sa * sb
            sin_ref[a * ROT_BLOCK:(a + 1) * ROT_BLOCK, :] = sa * cb + ca * sb

    half = dk // 2
    n_chunks = w_ref.shape[1] // dk

    def chunk(c, kind, head):
        acc = jnp.dot(h_ref[...], w_ref[:, c * dk:(c + 1) * dk], preferred_element_type=F32)
        if kind == "v":
            o_ref[:, c * dk:(c + 1) * dk] = acc.astype(BF16)
        elif kind == "g":
            o_ref[:, c * dk:(c + 1) * dk] = _silu(acc).astype(BF16)
        else:
            cos = cos_ref[...]
            sin = sin_ref[...]
            t1 = acc[:, :half]
            t2 = acc[:, half:]
            o1 = t1 * cos - t2 * sin
            o2 = t1 * sin + t2 * cos
            if kind == "k":
                s = kdec_ref[:, head * half:(head + 1) * half]
                o1 = o1 * s
                o2 = o2 * s
            o_ref[:, c * dk:c * dk + half] = o1.astype(BF16)
            o_ref[:, c * dk + half:(c + 1) * dk] = o2.astype(BF16)

    def kind_of(gc):
        if gc < n_qk_chunks:
            return "q", gc
        if gc < 2 * n_qk_chunks:
            return "k", gc - n_qk_chunks
        if gc < 2 * n_qk_chunks + n_v_chunks:
            return "v", 0
        return "g", 0

    for jj in range(n_col_tiles):
        @pl.when(j == jj)
        def _():
            for c in range(n_chunks):
                chunk(c, *kind_of(jj * n_chunks + c))


def _ret_proj(x, gain, sc1p, sh, w_in, rot, kdec, qk_total, v_total, to_cast):
    t, d = x.shape
    n = w_in.shape[1]
    tm, tn = ROW_TILE, PROJ_COL_TILE
    n_row_tiles = t // tm
    dk = qk_total // N_HEADS
    half = dk // 2
    cosa, sina, cosb, sinb = rot
    kern = functools.partial(_proj_kernel, n_qk_chunks=qk_total // dk, n_v_chunks=v_total // dk,
                             n_col_tiles=n // tn, dk=dk, n_cast=len(to_cast))
    vec = pl.BlockSpec((1, d), lambda i, j: (0, 0))
    rot_a = pl.BlockSpec((tm // ROT_BLOCK, half), lambda i, j: (i, 0))
    rot_b = pl.BlockSpec((ROT_BLOCK, half), lambda i, j: (0, 0))
    cast_specs = [pl.BlockSpec((w.shape[0] // n_row_tiles, w.shape[1]), lambda i, j: (i, 0)) for w in to_cast]
    outs = pl.pallas_call(
        kern,
        grid=(n_row_tiles, n // tn),
        in_specs=[
            pl.BlockSpec((tm, d), lambda i, j: (i, 0)),
            vec, vec, vec,
            pl.BlockSpec((d, tn), lambda i, j: (0, j)),
            rot_a, rot_a, rot_b, rot_b,
            pl.BlockSpec((tm, N_HEADS * half), lambda i, j: (0, 0)),
        ] + cast_specs,
        out_specs=[pl.BlockSpec((tm, tn), lambda i, j: (i, j))] + cast_specs,
        out_shape=[jax.ShapeDtypeStruct((t, n), BF16)] + [jax.ShapeDtypeStruct(w.shape, BF16) for w in to_cast],
        scratch_shapes=[pltpu.VMEM((tm, d), BF16), pltpu.VMEM((tm, half), F32), pltpu.VMEM((tm, half), F32)],
        compiler_params=_params("arbitrary", "arbitrary"),
        name="proj",
    )(x, gain, sc1p, sh, w_in, cosa, sina, cosb, sinb, kdec, *to_cast)
    return outs[0], outs[1:]


def _ret_kernel(cdec_ref, q_ref, k_ref, v_ref, sg_ref, dmask_ref, qdec_ref, gn_ref, x_ref, g_ref, wo_ref, o_ref,
                state_ref, y_ref, *, dk, dv):
    @pl.when(pl.program_id(0) == 0)
    def _():
        state_ref[...] = jnp.zeros_like(state_ref)

    rows_per_step = q_ref.shape[0]
    for n in range(rows_per_step // RET_CHUNK):
        rows = slice(n * RET_CHUNK, (n + 1) * RET_CHUNK)
        if n > 0:
            prev = slice((n - 1) * RET_CHUNK, n * RET_CHUNK)
            o_ref[prev, :] = x_ref[prev, :] + g_ref[...] * jnp.dot(y_ref[prev, :], wo_ref[...],
                                                                   preferred_element_type=F32)
        for h in range(N_HEADS):
            qcols = slice(h * dk, (h + 1) * dk)
            vcols = slice(h * dv, (h + 1) * dv)
            q = q_ref[rows, qcols]
            ks = k_ref[rows, qcols]
            v = v_ref[rows, vcols]
            s = lax.dot_general(q, ks, (((1,), (1,)), ((), ())), preferred_element_type=F32)
            a = (s * dmask_ref[h]).astype(BF16)
            intra = jnp.dot(a, v, preferred_element_type=F32)
            st = state_ref[h]
            cross = jnp.dot(q, st.astype(BF16), preferred_element_type=F32)
            o = intra + cross * qdec_ref[h]
            kv = lax.dot_general(ks, v, (((0,), (0,)), ((), ())), preferred_element_type=F32)
            state_ref[h] = st * cdec_ref[h] + kv
            mu = jnp.mean(o, axis=-1, keepdims=True)
            dlt = o - mu
            var = jnp.mean(dlt * dlt, axis=-1, keepdims=True)
            on = dlt * lax.rsqrt(var + EPS)
            y = on * gn_ref[:, vcols] * sg_ref[rows, vcols].astype(F32)
            y_ref[rows, vcols] = y.astype(BF16)
    last = slice(rows_per_step - RET_CHUNK, rows_per_step)
    o_ref[last, :] = x_ref[last, :] + g_ref[...] * jnp.dot(y_ref[last, :], wo_ref[...],
                                                           preferred_element_type=F32)


def _retention(proj, cdec, dmask, qdec, gn_gain, x, g1, w_out, qk_total, v_total):
    t, d = x.shape
    r = RET_ROWS
    dk = qk_total // N_HEADS
    dv = v_total // N_HEADS
    assert v_total == 2 * qk_total
    kern = functools.partial(_ret_kernel, dk=dk, dv=dv)
    return pl.pallas_call(
        kern,
        grid=(t // r,),
        in_specs=[
            pl.BlockSpec(memory_space=pltpu.SMEM),
            pl.BlockSpec((r, qk_total), lambda i: (i, 0)),
            pl.BlockSpec((r, qk_total), lambda i: (i, 1)),
            pl.BlockSpec((r, v_total), lambda i: (i, 1)),
            pl.BlockSpec((r, v_total), lambda i: (i, 2)),
            pl.BlockSpec((N_HEADS, RET_CHUNK, RET_CHUNK), lambda i: (0, 0, 0)),
            pl.BlockSpec((N_HEADS, RET_CHUNK, dv), lambda i: (0, 0, 0)),
            pl.BlockSpec((1, v_total), lambda i: (0, 0)),
            pl.BlockSpec((r, d), lambda i: (i, 0)),
            pl.BlockSpec((1, d), lambda i: (0, 0)),
            _resident((v_total, d)),
        ],
        out_specs=pl.BlockSpec((r, d), lambda i: (i, 0)),
        out_shape=jax.ShapeDtypeStruct((t, d), F32),
        scratch_shapes=[pltpu.VMEM((N_HEADS, dk, dv), F32), pltpu.VMEM((r, v_total), BF16)],
        compiler_params=_params("arbitrary"),
        name="ret",
    )(cdec, proj, proj, proj, proj, dmask, qdec, gn_gain, x, g1, w_out)


def _ffn_kernel(x_ref, gain_ref, sc_ref, sh_ref, g_ref, wg_ref, wu_ref, wd_ref, o_ref, *, ff_tile):
    x = x_ref[...]
    h = _norm_mod(x, gain_ref[...], sc_ref[...], sh_ref[...]).astype(BF16)
    d_ff = wg_ref.shape[1]
    acc = jnp.zeros(x.shape, F32)
    for lo in range(0, d_ff, ff_tile):
        hi = min(lo + ff_tile, d_ff)
        a = jnp.dot(h, wg_ref[:, lo:hi], preferred_element_type=F32)
        b = jnp.dot(h, wu_ref[:, lo:hi], preferred_element_type=F32)
        act = (_silu(a) * b).astype(BF16)
        acc = acc + jnp.dot(act, wd_ref[lo:hi, :], preferred_element_type=F32)
    o_ref[...] = x + g_ref[...] * acc


def _dense_ffn(x, gain, sc1p, sh, g2, wg, wu, wd):
    t, d = x.shape
    d_ff = wg.shape[1]
    tm = ROW_TILE
    vec = pl.BlockSpec((1, d), lambda i: (0, 0))
    return pl.pallas_call(
        functools.partial(_ffn_kernel, ff_tile=512),
        grid=(t // tm,),
        in_specs=[
            pl.BlockSpec((tm, d), lambda i: (i, 0)),
            vec, vec, vec, vec,
            _resident((d, d_ff)), _resident((d, d_ff)), _resident((d_ff, d)),
        ],
        out_specs=pl.BlockSpec((tm, d), lambda i: (i, 0)),
        out_shape=jax.ShapeDtypeStruct((t, d), F32),
        compiler_params=_params("arbitrary"),
        name="ffn",
    )(x, gain, sc1p, sh, g2, wg, wu, wd)


def _pool_kernel(x_ref, gain_ref, sc_ref, sh_ref, g_ref, band_ref, bandh_ref, wp_ref, bp_ref, ps_ref, o_ref,
                 hbuf_ref):
    i = pl.program_id(0)
    tm = x_ref.shape[0]
    gw = wp_ref.shape[1]

    @pl.when(i == 0)
    def _():
        hbuf_ref[0:POOL_HALO, :] = jnp.zeros((POOL_HALO, hbuf_ref.shape[1]), F32)

    @pl.when(i > 0)
    def _():
        hbuf_ref[0:POOL_HALO, :] = hbuf_ref[tm:tm + POOL_HALO, :]

    hbuf_ref[POOL_HALO:, :] = _norm_mod(x_ref[...], gain_ref[...], sc_ref[...], sh_ref[...])

    def split(v):
        hi = v.astype(BF16)
        return hi, (v - hi.astype(F32)).astype(BF16)

    sub = band_ref.shape[1]
    for s in range(tm // sub):
        rows = slice(s * sub, (s + 1) * sub)
        lo = POOL_HALO + s * sub
        x = x_ref[rows, :]
        h = hbuf_ref[lo:lo + sub, :]
        h_hi, h_lo = split(h)
        halo_hi, halo_lo = split(hbuf_ref[lo - POOL_HALO:lo, :])
        t1 = (lax.broadcasted_iota(jnp.int32, (sub, 1), 0) + (i * tm + s * sub + 1)).astype(F32)
        for g, w in enumerate(POOL_WINDOWS):
            cols = slice(g * gw, (g + 1) * gw)
            band = band_ref[g]
            bandh = bandh_ref[g]
            win = (jnp.dot(band, h_hi[:, cols], preferred_element_type=F32)
                   + jnp.dot(band, h_lo[:, cols], preferred_element_type=F32))
            head = (jnp.dot(bandh, halo_hi[:, cols], preferred_element_type=F32)
                    + jnp.dot(bandh, halo_lo[:, cols], preferred_element_type=F32))
            win = jnp.concatenate([win[:POOL_HALO] + head, win[POOL_HALO:]], axis=0)
            pooled = win * (1.0 / jnp.minimum(t1, float(w))) - h[:, cols]
            y = jnp.dot(pooled.astype(BF16), wp_ref[g], preferred_element_type=F32) + bp_ref[:, cols]
            o_ref[rows, cols] = x[:, cols] + (g_ref[:, cols] * ps_ref[:, cols]) * y


def _route(x, gain_ref, sc_ref, sh_ref, w_ref, tri_ref, h_ref, gates_ref, sel_ref, rank_ref, tot_ref, carry_ref):
    h = _norm_mod(x, gain_ref[...], sc_ref[...], sh_ref[...])
    h_hi = h.astype(BF16)
    h_ref[...] = h_hi
    h_lo = (h - h_hi.astype(F32)).astype(BF16)
    w = w_ref[...]
    p_hi = jnp.dot(h_hi, w, preferred_element_type=F32)
    p_lo = jnp.dot(h_lo, w, preferred_element_type=F32)
    logits = p_hi[:, :LANES] + p_hi[:, LANES:] + p_lo[:, :LANES] + p_lo[:, LANES:]
    lane = lax.broadcasted_iota(jnp.int32, logits.shape, 1).astype(F32)
    neg = jnp.float32(-jnp.inf)
    lg = jnp.where(lane < float(N_EXPERTS), logits, neg)
    m0 = jnp.max(lg, axis=-1, keepdims=True)
    i0 = jnp.min(jnp.where(lg == m0, lane, float(LANES)), axis=-1, keepdims=True)
    lg1 = jnp.where(lane == i0, neg, lg)
    m1 = jnp.max(lg1, axis=-1, keepdims=True)
    i1 = jnp.min(jnp.where(lg1 == m1, lane, float(LANES)), axis=-1, keepdims=True)
    e1 = jnp.exp(m1 - m0)
    den = 1.0 + e1
    is0 = lane == i0
    is1 = lane == i1
    gates_ref[...] = jnp.where(is0, 1.0 / den, jnp.where(is1, e1 / den, 0.0))
    sel = jnp.where(is0 | is1, 1.0, 0.0).astype(BF16)
    sel_ref[...] = sel

    @pl.when(pl.program_id(0) == 0)
    def _():
        carry_ref[...] = jnp.zeros_like(carry_ref)

    carry = carry_ref[...]
    for b in range(sel.shape[0] // RANK_ROWS):
        rows = slice(b * RANK_ROWS, (b + 1) * RANK_ROWS)
        s = sel[rows, :]
        cum = jnp.dot(tri_ref[...], s, preferred_element_type=F32)
        rank_ref[rows, :] = cum - s.astype(F32) + carry
        carry = carry + cum[RANK_ROWS - 1:, :]
        tot_ref[b] = carry
    carry_ref[...] = carry


def _pool_router_kernel(x_ref, gain_ref, sc_ref, sh_ref, g_ref, band_ref, bandh_ref, wp_ref, bp_ref, ps_ref,
                        gain2_ref, sc2_ref, sh2_ref, wr_ref, tri_ref,
                        o_ref, h_ref, gates_ref, sel_ref, rank_ref, tot_ref, hbuf_ref, carry_ref):
    _pool_kernel(x_ref, gain_ref, sc_ref, sh_ref, g_ref, band_ref, bandh_ref, wp_ref, bp_ref, ps_ref, o_ref,
                 hbuf_ref)
    _route(o_ref[...], gain2_ref, sc2_ref, sh2_ref, wr_ref, tri_ref, h_ref, gates_ref, sel_ref, rank_ref, tot_ref,
           carry_ref)


def _pool_router(x, gain, sc1p, sh, g1, band, bandh, wp, bp, ps, gain2, sc2p, sh2, w_cat, tri):
    t, d = x.shape
    tm = POOL_ROWS
    n_sub = tm // RANK_ROWS
    vec = pl.BlockSpec((1, d), lambda i: (0, 0))
    rows = pl.BlockSpec((tm, d), lambda i: (i, 0))
    lanes = pl.BlockSpec((tm, LANES), lambda i: (i, 0))
    return pl.pallas_call(
        _pool_router_kernel,
        grid=(t // tm,),
        in_specs=[
            rows, vec, vec, vec, vec,
            pl.BlockSpec(band.shape, lambda i: (0, 0, 0)),
            pl.BlockSpec(bandh.shape, lambda i: (0, 0, 0)),
            pl.BlockSpec(wp.shape, lambda i: (0, 0, 0)),
            vec, vec,
            vec, vec, vec,
            pl.BlockSpec((d, 2 * LANES), lambda i: (0, 0)),
            pl.BlockSpec((RANK_ROWS, RANK_ROWS), lambda i: (0, 0)),
        ],
        out_specs=[rows, rows, lanes, lanes, lanes, pl.BlockSpec((n_sub, 1, LANES), lambda i: (i, 0, 0))],
        out_shape=[
            jax.ShapeDtypeStruct((t, d), F32),
            jax.ShapeDtypeStruct((t, d), BF16),
            jax.ShapeDtypeStruct((t, LANES), F32),
            jax.ShapeDtypeStruct((t, LANES), BF16),
            jax.ShapeDtypeStruct((t, LANES), F32),
            jax.ShapeDtypeStruct((t // RANK_ROWS, 1, LANES), F32),
        ],
        scratch_shapes=[pltpu.VMEM((POOL_HALO + tm, d), F32), pltpu.VMEM((1, LANES), F32)],
        compiler_params=_params("arbitrary"),
        name="pool_router",
    )(x, gain, sc1p, sh, g1, band, bandh, wp, bp, ps, gain2, sc2p, sh2, w_cat, tri)


def _gather_kernel(be_ref, ng_ref, clo_ref, chi_ref, h_ref, post_ref, gatest_ref, xb_ref, gs_ref, acc_ref):
    u = pl.program_id(0)
    e = be_ref[u]
    unit_rows = xb_ref.shape[0]
    n_groups = unit_rows // MOE_GATHER_ROWS
    last_chunk = h_ref.shape[0] // GATHER_CHUNK - 1
    g0 = u * n_groups

    @pl.when(ng_ref[g0] > 0)
    def _():
        acc_ref[...] = jnp.zeros_like(acc_ref)
        trips = chi_ref[g0] - clo_ref[g0]
        for q in range(1, n_groups):
            trips = jnp.maximum(trips, chi_ref[g0 + q] - clo_ref[g0 + q])

        def body(i, gsums):
            out = []
            for q in range(n_groups):
                rows = slice(q * MOE_GATHER_ROWS, (q + 1) * MOE_GATHER_ROWS)
                c = clo_ref[g0 + q] + i
                live = c < chi_ref[g0 + q]
                off = pl.multiple_of(jnp.minimum(c, last_chunk) * GATHER_CHUNK, GATHER_CHUNK)
                p = post_ref[pl.ds(e, 1), pl.ds(off, GATHER_CHUNK)]
                p = jnp.where(live, p, -2)
                row_pos = (lax.broadcasted_iota(jnp.int32, (MOE_GATHER_ROWS, GATHER_CHUNK), 0)
                           + (u * unit_rows + q * MOE_GATHER_ROWS))
                match = row_pos == p
                onehot = jnp.where(match, 1.0, 0.0).astype(BF16)
                acc_ref[rows, :] += jnp.dot(onehot, h_ref[pl.ds(off, GATHER_CHUNK), :],
                                            preferred_element_type=F32)
                gate = gatest_ref[pl.ds(e, 1), pl.ds(off, GATHER_CHUNK)]
                out.append(gsums[q] + jnp.sum(jnp.where(match, gate, 0.0), axis=1, keepdims=True))
            return tuple(out)

        zero = jnp.zeros((MOE_GATHER_ROWS, 1), F32)
        gsums = lax.fori_loop(0, trips, body, (zero,) * n_groups)
        xb_ref[...] = acc_ref[...].astype(BF16)
        for q in range(n_groups):
            gs_ref[q * MOE_GATHER_ROWS:(q + 1) * MOE_GATHER_ROWS, :] = gsums[q]

    @pl.when(ng_ref[g0] == 0)
    def _():
        xb_ref[...] = jnp.zeros_like(xb_ref)
        gs_ref[...] = jnp.zeros_like(gs_ref)


def _gather(beu, ng, clo, chi, h, post, gatest):
    t, d = h.shape
    n_units = beu.shape[0]
    grid_spec = pltpu.PrefetchScalarGridSpec(
        num_scalar_prefetch=4,
        grid=(n_units,),
        in_specs=[
            pl.BlockSpec((t, d), lambda u, *_: (0, 0), pipeline_mode=pl.Buffered(1)),
            pl.BlockSpec(post.shape, lambda u, *_: (0, 0), pipeline_mode=pl.Buffered(1)),
            pl.BlockSpec(gatest.shape, lambda u, *_: (0, 0), pipeline_mode=pl.Buffered(1)),
        ],
        out_specs=[
            pl.BlockSpec((MOE_GATHER_UNIT, d), lambda u, *_: (u, 0)),
            pl.BlockSpec((MOE_GATHER_UNIT, 1), lambda u, *_: (u, 0)),
        ],
        scratch_shapes=[pltpu.VMEM((MOE_GATHER_UNIT, d), F32)],
    )
    return pl.pallas_call(
        _gather_kernel,
        grid_spec=grid_spec,
        out_shape=[
            jax.ShapeDtypeStruct((n_units * MOE_GATHER_UNIT, d), BF16),
            jax.ShapeDtypeStruct((n_units * MOE_GATHER_UNIT, 1), F32),
        ],
        compiler_params=_params("arbitrary"),
        name="gather",
    )(beu, ng, clo, chi, h, post, gatest)


def _experts_kernel(be_ref, nv_ref, xb_ref, gs_ref, wg_ref, wu_ref, wd_ref, yb_ref,
                    acc_ref, wgb_ref, wub_ref, wdb_ref):
    b = pl.program_id(0)
    f = pl.program_id(1)
    nf = pl.num_programs(1)
    nv = nv_ref[b]
    block_rows = xb_ref.shape[0]

    @pl.when((b == 0) & (f == 0))
    def _():
        acc_ref[...] = jnp.zeros_like(acc_ref)

    def swiglu_rows(lo, n, cast):
        rows = slice(lo, lo + n)
        if cast:
            wgb_ref[...] = wg_ref[0].astype(BF16)
            wub_ref[...] = wu_ref[0].astype(BF16)
            wdb_ref[...] = wd_ref[0].astype(BF16)
        x = xb_ref[rows, :]
        a = jnp.dot(x, wgb_ref[...], preferred_element_type=F32)
        u = jnp.dot(x, wub_ref[...], preferred_element_type=F32)
        act = (_silu(a) * u).astype(BF16)
        prev = jnp.where(f > 0, acc_ref[rows, :], 0.0)
        acc_ref[rows, :] = prev + jnp.dot(act, wdb_ref[...], preferred_element_type=F32)

    pair = 2 * MOE_DOT_ROWS
    half_group = MOE_DOT_ROWS // 2
    for p in range(block_rows // pair):
        lo = p * pair
        full = nv >= lo + pair

        @pl.when(full)
        def _():
            swiglu_rows(lo, pair, p == 0)

        for hh in range(2):
            glo = lo + hh * MOE_DOT_ROWS
            used = jnp.logical_not(full) & (nv > glo)
            tall = nv > glo + half_group
            first = p == 0 and hh == 0

            @pl.when(used & tall)
            def _():
                swiglu_rows(glo, MOE_DOT_ROWS, first)

            @pl.when(used & jnp.logical_not(tall))
            def _():
                swiglu_rows(glo, half_group, first)

    @pl.when(f == nf - 1)
    def _():
        for g in range(block_rows // MOE_DOT_ROWS):
            rows = slice(g * MOE_DOT_ROWS, (g + 1) * MOE_DOT_ROWS)

            @pl.when(nv > g * MOE_DOT_ROWS)
            def _():
                yb_ref[rows, :] = (acc_ref[rows, :] * gs_ref[rows, :]).astype(BF16)

            @pl.when(nv <= g * MOE_DOT_ROWS)
            def _():
                yb_ref[rows, :] = jnp.zeros((MOE_DOT_ROWS, yb_ref.shape[1]), BF16)


def _experts(be, nv, xb, gs, wg, wu, wd):
    d = xb.shape[1]
    n_blocks = be.shape[0]
    d_ff = wg.shape[2]
    tf = MOE_FF_TILE
    nf = d_ff // tf
    bsz = MOE_BLOCK

    def ff_idx(f, nvr, b):
        return jnp.where(nvr[b] > 0, f, nf - 1)

    grid_spec = pltpu.PrefetchScalarGridSpec(
        num_scalar_prefetch=2,
        grid=(n_blocks, nf),
        in_specs=[
            pl.BlockSpec((bsz, d), lambda b, f, *_: (b, 0)),
            pl.BlockSpec((bsz, 1), lambda b, f, *_: (b, 0)),
            pl.BlockSpec((1, d, tf), lambda b, f, be_r, nv_r: (be_r[b], 0, ff_idx(f, nv_r, b))),
            pl.BlockSpec((1, d, tf), lambda b, f, be_r, nv_r: (be_r[b], 0, ff_idx(f, nv_r, b))),
            pl.BlockSpec((1, tf, d), lambda b, f, be_r, nv_r: (be_r[b], ff_idx(f, nv_r, b), 0)),
        ],
        out_specs=pl.BlockSpec((bsz, d), lambda b, f, *_: (b, 0)),
        scratch_shapes=[
            pltpu.VMEM((bsz, d), F32),
            pltpu.VMEM((d, tf), BF16), pltpu.VMEM((d, tf), BF16), pltpu.VMEM((tf, d), BF16),
        ],
    )
    return pl.pallas_call(
        _experts_kernel,
        grid_spec=grid_spec,
        out_shape=jax.ShapeDtypeStruct((n_blocks * bsz, d), BF16),
        compiler_params=_params("arbitrary", "arbitrary"),
        name="experts",
    )(be, nv, xb, gs, wg, wu, wd)


def _combine_kernel(win_ref, x_ref, pos_ref, g_ref, fg_ref, *refs):
    yb_refs, o_ref = refs[:-1], refs[-1]
    i = pl.program_id(0)
    n_sub = x_ref.shape[0] // COMBINE_ROWS
    lane = lax.broadcasted_iota(jnp.int32, (COMBINE_ROWS, 2 * COMBINE_ROWS), 1)
    for s in range(n_sub):
        rows = slice(s * COMBINE_ROWS, (s + 1) * COMBINE_ROWS)
        onehots, windows = [], []
        for e in range(N_EXPERTS):
            k = (i * n_sub + s) * N_EXPERTS + e
            base = win_ref[k] * COMBINE_ROWS
            rel = pos_ref[rows, e:e + 1] - base
            onehots.append(jnp.where(lane == rel, 1.0, 0.0).astype(BF16))
            w0 = 2 * (s * N_EXPERTS + e)
            windows += [yb_refs[w0][...], yb_refs[w0 + 1][...]]
        acc = jnp.dot(jnp.concatenate(onehots, axis=1), jnp.concatenate(windows, axis=0),
                      preferred_element_type=F32)
        xo = x_ref[rows, :] + g_ref[...] * acc
        ms = jnp.mean(xo * xo, axis=-1, keepdims=True)
        o_ref[rows, :] = xo * lax.rsqrt(ms + EPS) * fg_ref[...]


def _combine(win, x, pos, g2, fgain, yb):
    t, d = x.shape
    tm = COMBINE_TILE
    n_sub = tm // COMBINE_ROWS
    n_win = yb.shape[0] // COMBINE_ROWS
    vec = pl.BlockSpec((1, d), lambda i, w: (0, 0))

    def first(s, e):
        return lambda i, w: (jnp.minimum(w[(i * n_sub + s) * N_EXPERTS + e], n_win - 1), 0)

    def second(s, e):
        return lambda i, w: (jnp.minimum(w[(i * n_sub + s) * N_EXPERTS + e] + 1, n_win - 1), 0)

    yb_specs = []
    for s in range(n_sub):
        for e in range(N_EXPERTS):
            yb_specs.append(pl.BlockSpec((COMBINE_ROWS, d), first(s, e)))
            yb_specs.append(pl.BlockSpec((COMBINE_ROWS, d), second(s, e)))
    grid_spec = pltpu.PrefetchScalarGridSpec(
        num_scalar_prefetch=1,
        grid=(t // tm,),
        in_specs=[
            pl.BlockSpec((tm, d), lambda i, w: (i, 0)),
            pl.BlockSpec((tm, N_EXPERTS), lambda i, w: (i, 0)),
            vec, vec,
        ] + yb_specs,
        out_specs=pl.BlockSpec((tm, d), lambda i, w: (i, 0)),
    )
    return pl.pallas_call(
        _combine_kernel,
        grid_spec=grid_spec,
        out_shape=jax.ShapeDtypeStruct((t, d), F32),
        compiler_params=_params("arbitrary"),
        name="combine",
    )(win, x, pos, g2, fgain, *([yb] * len(yb_specs)))


def _routing_tables(rank, tot, sel):
    t = rank.shape[0]
    bsz = MOE_BLOCK
    n_blocks = (2 * t) // bsz + N_EXPERTS
    cum = tot[:, 0, :N_EXPERTS].astype(jnp.int32)
    counts = cum[-1]
    padded = ((counts + bsz - 1) // bsz) * bsz
    pad_end = jnp.cumsum(padded)
    start_pad = pad_end - padded
    selected = sel[:, :N_EXPERTS] > 0
    pos = jnp.where(selected, start_pad[None, :] + rank[:, :N_EXPERTS].astype(jnp.int32), -1)

    bstart = jnp.arange(n_blocks, dtype=jnp.int32) * bsz
    be = jnp.minimum(jnp.sum(bstart[:, None] >= pad_end[None, :], axis=1), N_EXPERTS - 1).astype(jnp.int32)
    r0 = bstart - start_pad[be]
    nv = jnp.clip(counts[be] - r0, 0, bsz).astype(jnp.int32)
    n_gather = bsz // MOE_GATHER_ROWS
    goff = jnp.arange(n_gather, dtype=jnp.int32) * MOE_GATHER_ROWS
    r0g = (r0[:, None] + goff[None, :]).reshape(-1)
    ng = jnp.clip(nv[:, None] - goff[None, :], 0, MOE_GATHER_ROWS).reshape(-1)
    beg = jnp.repeat(be, n_gather)
    chunk_end = cum[:, beg].T
    chunk_start = jnp.concatenate([jnp.zeros((1, N_EXPERTS), jnp.int32), cum[:-1]], axis=0)[:, beg].T
    clo = jnp.sum(chunk_end <= r0g[:, None], axis=1).astype(jnp.int32)
    chi = jnp.sum(chunk_start < (r0g + ng)[:, None], axis=1).astype(jnp.int32)
    beu = jnp.repeat(be, bsz // MOE_GATHER_UNIT)

    big = jnp.int32(2 ** 30)
    first = jnp.min(jnp.where(pos >= 0, pos, big).reshape(t // COMBINE_ROWS, COMBINE_ROWS, N_EXPERTS), axis=1)
    win = jnp.where(first == big, 0, first // COMBINE_ROWS).astype(jnp.int32).reshape(-1)
    return pos, be, nv, beu, ng, clo, chi, win


def _rotary_tables(seq, dk):
    half = dk // 2
    inv = ROPE_BASE ** (-jnp.arange(half, dtype=F32) / half)
    ang_a = (jnp.arange(seq // ROT_BLOCK) * ROT_BLOCK).astype(F32)[:, None] * inv[None, :]
    ang_b = jnp.arange(ROT_BLOCK).astype(F32)[:, None] * inv[None, :]
    return jnp.cos(ang_a), jnp.sin(ang_a), jnp.cos(ang_b), jnp.sin(ang_b)


def _decay_tables(dk, dv):
    c = RET_CHUNK
    log_gamma = jnp.log1p(-jnp.exp2(-5.0 - jnp.arange(N_HEADS, dtype=F32)))
    idx = jnp.arange(c, dtype=F32)
    row_dec = jnp.exp(log_gamma[:, None] * (idx + 1.0 - c))
    causal = (idx[:, None] >= idx[None, :]).astype(F32)
    dmask = row_dec[:, :, None] * causal[None]
    k_dec = jnp.exp(log_gamma[:, None] * (c - 1.0 - idx)) * (dk ** -0.5)
    q_dec = jnp.exp(log_gamma[:, None] * (idx + 1.0))
    chunk_dec = jnp.exp(log_gamma * c)
    half = dk // 2
    kdec_tile = jnp.tile(jnp.repeat(k_dec.T, half, axis=1), (ROW_TILE // c, 1))
    qdec_tile = jnp.broadcast_to(q_dec[:, :, None], (N_HEADS, c, dv))
    return dmask, kdec_tile, qdec_tile, chunk_dec


def _pool_bands(tm):
    def bands(lag):
        return jnp.stack([((lag >= 0) & (lag < w)) for w in POOL_WINDOWS]).astype(BF16)
    main = bands(jnp.arange(tm)[:, None] - jnp.arange(tm)[None, :])
    head = bands(jnp.arange(POOL_HALO)[:, None] + POOL_HALO - jnp.arange(POOL_HALO)[None, :])
    return main, head


def kernel(x, c, ada_w, ada_b, norm_gain, ret_w_in, ret_gn_gain, ret_w_out, ffn_w_gate, ffn_w_up, ffn_w_down,
           pool_w, pool_b, pool_scale, moe_router, moe_w_gate, moe_w_up, moe_w_down, final_norm_gain):
    bsz, seq, d = x.shape
    assert bsz == 1 and ada_w.shape[0] == 2
    xt = x.reshape(seq, d)
    qk_total = d
    v_total = ret_w_out.shape[1]
    dk = qk_total // N_HEADS
    dv = v_total // N_HEADS

    mod = _modulation(c, ada_w, ada_b)
    def mods(i):
        parts = [mod[i, :, k * d:(k + 1) * d] for k in range(N_MOD)]
        sh1, sc1, g1, sh2, sc2, g2 = parts
        return sh1, 1.0 + sc1, g1, sh2, 1.0 + sc2, g2

    sh1, sc1p, g1, sh2, sc2p, g2 = mods(0)
    dmask, kdec_tile, qdec_tile, chunk_dec = _decay_tables(dk, dv)
    proj, (w_out_b, wg_b, wu_b, wd_b) = _ret_proj(
        xt, norm_gain[0, 0][None], sc1p, sh1, ret_w_in[0].astype(BF16), _rotary_tables(seq, dk), kdec_tile,
        qk_total, v_total, [ret_w_out[0], ffn_w_gate[0], ffn_w_up[0], ffn_w_down[0]])
    x1 = _retention(proj, chunk_dec, dmask, qdec_tile, ret_gn_gain[0][None], xt, g1, w_out_b, qk_total, v_total)
    x2 = _dense_ffn(x1, norm_gain[0, 1][None], sc2p, sh2, g2, wg_b, wu_b, wd_b)

    sh1, sc1p, g1, sh2, sc2p, g2 = mods(1)
    band, bandh = _pool_bands(POOL_SUB_ROWS)
    wr = jnp.pad(moe_router[0], ((0, 0), (0, LANES - N_EXPERTS)))
    wr_hi = wr.astype(BF16)
    wr_lo = (wr - wr_hi.astype(F32)).astype(BF16)
    tri = (jnp.arange(RANK_ROWS)[:, None] >= jnp.arange(RANK_ROWS)[None, :]).astype(BF16)
    x3, h4, gates, sel, rank, tot = _pool_router(
        x2, norm_gain[1, 0][None], sc1p, sh1, g1, band, bandh,
        pool_w[0].astype(BF16), pool_b[0].reshape(1, d), pool_scale[0][None],
        norm_gain[1, 1][None], sc2p, sh2, jnp.concatenate([wr_hi, wr_lo], axis=1), tri)
    pos, be, nv, beu, ng, clo, chi, win = _routing_tables(rank, tot, sel)
    xb, gs = _gather(beu, ng, clo, chi, h4, pos.T, gates[:, :N_EXPERTS].T)
    yb = _experts(be, nv, xb, gs, moe_w_gate[0], moe_w_up[0], moe_w_down[0])
    out = _combine(win, x3, pos, g2, final_norm_gain[None], yb)
    return out.reshape(bsz, seq, d)
```

```python
import functools

import jax
import jax.numpy as jnp
from jax import lax
from jax.experimental import pallas as pl
from jax.experimental.pallas import tpu as pltpu

F32 = jnp.float32
BF16 = jnp.bfloat16

EPS = 1e-6
N_HEADS = 4
RET_CHUNK = 256
ROPE_BASE = 10000.0
ROT_BLOCK = 128
POOL_WINDOWS = (2, 4, 8, 16)
POOL_HALO = 16
N_EXPERTS = 8
N_MOD = 6

VMEM_LIMIT_BYTES = 56 * 1024 * 1024

ROW_TILE = 1024
PROJ_COL_TILE = 3072
RET_ROWS = 512
POOL_ROWS = 1024
POOL_SUB_ROWS = 256
RANK_ROWS = 512
MOE_BLOCK = 2048
MOE_DOT_ROWS = 512
MOE_GATHER_UNIT = 1024
MOE_GATHER_ROWS = 256
MOE_FF_TILE = 512
GATHER_CHUNK = 512
COMBINE_TILE = 256
COMBINE_ROWS = 256
COMBINE_WIN = 128
COMBINE_WINDOWS = COMBINE_ROWS // COMBINE_WIN + 1
LANES = 128


def _silu(v):
    return v / (1.0 + jnp.exp(-v))


def _norm_mod(x, gain, scale1p, shift):
    ms = jnp.mean(x * x, axis=-1, keepdims=True)
    return (x * lax.rsqrt(ms + EPS)) * (gain * scale1p) + shift


def _params(*sem):
    return pltpu.CompilerParams(dimension_semantics=sem, vmem_limit_bytes=VMEM_LIMIT_BYTES)


def _resident(shape):
    nd = len(shape)
    return pl.BlockSpec(shape, lambda *_: (0,) * nd, pipeline_mode=pl.Buffered(1))


def _mod_kernel(c_ref, w_ref, b_ref, o_ref):
    ca = _silu(c_ref[...])
    o_ref[0] = jnp.sum(ca * w_ref[0], axis=0, keepdims=True) + b_ref[0]


def _modulation(c, ada_w, ada_b):
    depth, d, n = ada_w.shape
    tn = 1024
    return pl.pallas_call(
        _mod_kernel,
        grid=(depth, n // tn),
        in_specs=[
            pl.BlockSpec((d, 1), lambda i, j: (0, 0)),
            pl.BlockSpec((1, d, tn), lambda i, j: (i, 0, j)),
            pl.BlockSpec((1, 1, tn), lambda i, j: (i, 0, j)),
        ],
        out_specs=pl.BlockSpec((1, 1, tn), lambda i, j: (i, 0, j)),
        out_shape=jax.ShapeDtypeStruct((depth, 1, n), F32),
        compiler_params=_params("arbitrary", "arbitrary"),
        name="mod",
    )(c.reshape(d, 1), ada_w, ada_b.reshape(depth, 1, n))


def _proj_kernel(x_ref, gain_ref, sc_ref, sh_ref, w_ref, cosa_ref, sina_ref, cosb_ref, sinb_ref, kdec_ref, *refs,
                 n_qk_chunks, n_v_chunks, n_col_tiles, dk, n_cast):
    cast_in = refs[:n_cast]
    o_ref = refs[n_cast]
    cast_out = refs[n_cast + 1:2 * n_cast + 1]
    h_ref, cos_ref, sin_ref = refs[2 * n_cast + 1:]
    j = pl.program_id(1)

    @pl.when(j == 0)
    def _():
        h_ref[...] = _norm_mod(x_ref[...], gain_ref[...], sc_ref[...], sh_ref[...]).astype(BF16)
        for src, dst in zip(cast_in, cast_out):
            dst[...] = src[...].astype(BF16)
        cb = cosb_ref[...]
        sb = sinb_ref[...]
        for a in range(cosa_ref.shape[0]):
            ca = cosa_ref[a:a + 1, :]
            sa = sina_ref[a:a + 1, :]
            cos_ref[a * ROT_BLOCK:(a + 1) * ROT_BLOCK, :] = ca * cb - sa * sb
            sin_ref[a * ROT_BLOCK:(a + 1) * ROT_BLOCK, :] = sa * cb + ca * sb

    half = dk // 2
    n_chunks = w_ref.shape[1] // dk

    def chunk(c, kind, head):
        acc = jnp.dot(h_ref[...], w_ref[:, c * dk:(c + 1) * dk], preferred_element_type=F32)
        if kind == "v":
            o_ref[:, c * dk:(c + 1) * dk] = acc.astype(BF16)
        elif kind == "g":
            o_ref[:, c * dk:(c + 1) * dk] = _silu(acc).astype(BF16)
        else:
            cos = cos_ref[...]
            sin = sin_ref[...]
            t1 = acc[:, :half]
            t2 = acc[:, half:]
            o1 = t1 * cos - t2 * sin
            o2 = t1 * sin + t2 * cos
            if kind == "k":
                s = kdec_ref[:, head * half:(head + 1) * half]
                o1 = o1 * s
                o2 = o2 * s
            o_ref[:, c * dk:c * dk + half] = o1.astype(BF16)
            o_ref[:, c * dk + half:(c + 1) * dk] = o2.astype(BF16)

    def kind_of(gc):
        if gc < n_qk_chunks:
            return "q", gc
        if gc < 2 * n_qk_chunks:
            return "k", gc - n_qk_chunks
        if gc < 2 * n_qk_chunks + n_v_chunks:
            return "v", 0
        return "g", 0

    for jj in range(n_col_tiles):
        @pl.when(j == jj)
        def _():
            for c in range(n_chunks):
                chunk(c, *kind_of(jj * n_chunks + c))


def _ret_proj(x, gain, sc1p, sh, w_in, rot, kdec, qk_total, v_total, to_cast):
    t, d = x.shape
    n = w_in.shape[1]
    tm, tn = ROW_TILE, PROJ_COL_TILE
    n_row_tiles = t // tm
    dk = qk_total // N_HEADS
    half = dk // 2
    cosa, sina, cosb, sinb = rot
    kern = functools.partial(_proj_kernel, n_qk_chunks=qk_total // dk, n_v_chunks=v_total // dk,
                             n_col_tiles=n // tn, dk=dk, n_cast=len(to_cast))
    vec = pl.BlockSpec((1, d), lambda i, j: (0, 0))
    rot_a = pl.BlockSpec((tm // ROT_BLOCK, half), lambda i, j: (i, 0))
    rot_b = pl.BlockSpec((ROT_BLOCK, half), lambda i, j: (0, 0))
    cast_specs = [pl.BlockSpec((w.shape[0] // n_row_tiles, w.shape[1]), lambda i, j: (i, 0)) for w in to_cast]
    outs = pl.pallas_call(
        kern,
        grid=(n_row_tiles, n // tn),
        in_specs=[
            pl.BlockSpec((tm, d), lambda i, j: (i, 0)),
            vec, vec, vec,
            pl.BlockSpec((d, tn), lambda i, j: (0, j)),
            rot_a, rot_a, rot_b, rot_b,
            pl.BlockSpec((tm, N_HEADS * half), lambda i, j: (0, 0)),
        ] + cast_specs,
        out_specs=[pl.BlockSpec((tm, tn), lambda i, j: (i, j))] + cast_specs,
        out_shape=[jax.ShapeDtypeStruct((t, n), BF16)] + [jax.ShapeDtypeStruct(w.shape, BF16) for w in to_cast],
        scratch_shapes=[pltpu.VMEM((tm, d), BF16), pltpu.VMEM((tm, half), F32), pltpu.VMEM((tm, half), F32)],
        compiler_params=_params("arbitrary", "arbitrary"),
        name="proj",
    )(x, gain, sc1p, sh, w_in, cosa, sina, cosb, sinb, kdec, *to_cast)
    return outs[0], outs[1:]


def _ret_kernel(cdec_ref, q_ref, k_ref, v_ref, sg_ref, dmask_ref, qdec_ref, gn_ref, x_ref, g_ref, wo_ref, o_ref,
                state_ref, y_ref, *, dk, dv):
    @pl.when(pl.program_id(0) == 0)
    def _():
        state_ref[...] = jnp.zeros_like(state_ref)

    rows_per_step = q_ref.shape[0]
    for n in range(rows_per_step // RET_CHUNK):
        rows = slice(n * RET_CHUNK, (n + 1) * RET_CHUNK)
        if n > 0:
            prev = slice((n - 1) * RET_CHUNK, n * RET_CHUNK)
            o_ref[prev, :] = x_ref[prev, :] + g_ref[...] * jnp.dot(y_ref[prev, :], wo_ref[...],
                                                                   preferred_element_type=F32)
        for h in range(N_HEADS):
            qcols = slice(h * dk, (h + 1) * dk)
            vcols = slice(h * dv, (h + 1) * dv)
            q = q_ref[rows, qcols]
            ks = k_ref[rows, qcols]
            v = v_ref[rows, vcols]
            s = lax.dot_general(q, ks, (((1,), (1,)), ((), ())), preferred_element_type=F32)
            a = (s * dmask_ref[h]).astype(BF16)
            intra = jnp.dot(a, v, preferred_element_type=F32)
            st = state_ref[h]
            cross = jnp.dot(q, st.astype(BF16), preferred_element_type=F32)
            o = intra + cross * qdec_ref[h]
            kv = lax.dot_general(ks, v, (((0,), (0,)), ((), ())), preferred_element_type=F32)
            state_ref[h] = st * cdec_ref[h] + kv
            mu = jnp.mean(o, axis=-1, keepdims=True)
            dlt = o - mu
            var = jnp.mean(dlt * dlt, axis=-1, keepdims=True)
            on = dlt * lax.rsqrt(var + EPS)
            y = on * gn_ref[:, vcols] * sg_ref[rows, vcols].astype(F32)
            y_ref[rows, vcols] = y.astype(BF16)
    last = slice(rows_per_step - RET_CHUNK, rows_per_step)
    o_ref[last, :] = x_ref[last, :] + g_ref[...] * jnp.dot(y_ref[last, :], wo_ref[...],
                                                           preferred_element_type=F32)


def _retention(proj, cdec, dmask, qdec, gn_gain, x, g1, w_out, qk_total, v_total):
    t, d = x.shape
    r = RET_ROWS
    dk = qk_total // N_HEADS
    dv = v_total // N_HEADS
    assert v_total == 2 * qk_total
    kern = functools.partial(_ret_kernel, dk=dk, dv=dv)
    return pl.pallas_call(
        kern,
        grid=(t // r,),
        in_specs=[
            pl.BlockSpec(memory_space=pltpu.SMEM),
            pl.BlockSpec((r, qk_total), lambda i: (i, 0)),
            pl.BlockSpec((r, qk_total), lambda i: (i, 1)),
            pl.BlockSpec((r, v_total), lambda i: (i, 1)),
            pl.BlockSpec((r, v_total), lambda i: (i, 2)),
            pl.BlockSpec((N_HEADS, RET_CHUNK, RET_CHUNK), lambda i: (0, 0, 0)),
            pl.BlockSpec((N_HEADS, RET_CHUNK, dv), lambda i: (0, 0, 0)),
            pl.BlockSpec((1, v_total), lambda i: (0, 0)),
            pl.BlockSpec((r, d), lambda i: (i, 0)),
            pl.BlockSpec((1, d), lambda i: (0, 0)),
            _resident((v_total, d)),
        ],
        out_specs=pl.BlockSpec((r, d), lambda i: (i, 0)),
        out_shape=jax.ShapeDtypeStruct((t, d), F32),
        scratch_shapes=[pltpu.VMEM((N_HEADS, dk, dv), F32), pltpu.VMEM((r, v_total), BF16)],
        compiler_params=_params("arbitrary"),
        name="ret",
    )(cdec, proj, proj, proj, proj, dmask, qdec, gn_gain, x, g1, w_out)


def _ffn_kernel(x_ref, gain_ref, sc_ref, sh_ref, g_ref, wg_ref, wu_ref, wd_ref, o_ref, *, ff_tile):
    x = x_ref[...]
    h = _norm_mod(x, gain_ref[...], sc_ref[...], sh_ref[...]).astype(BF16)
    d_ff = wg_ref.shape[1]
    acc = jnp.zeros(x.shape, F32)
    for lo in range(0, d_ff, ff_tile):
        hi = min(lo + ff_tile, d_ff)
        a = jnp.dot(h, wg_ref[:, lo:hi], preferred_element_type=F32)
        b = jnp.dot(h, wu_ref[:, lo:hi], preferred_element_type=F32)
        act = (_silu(a) * b).astype(BF16)
        acc = acc + jnp.dot(act, wd_ref[lo:hi, :], preferred_element_type=F32)
    o_ref[...] = x + g_ref[...] * acc


def _dense_ffn(x, gain, sc1p, sh, g2, wg, wu, wd):
    t, d = x.shape
    d_ff = wg.shape[1]
    tm = ROW_TILE
    vec = pl.BlockSpec((1, d), lambda i: (0, 0))
    return pl.pallas_call(
        functools.partial(_ffn_kernel, ff_tile=512),
        grid=(t // tm,),
        in_specs=[
            pl.BlockSpec((tm, d), lambda i: (i, 0)),
            vec, vec, vec, vec,
            _resident((d, d_ff)), _resident((d, d_ff)), _resident((d_ff, d)),
        ],
        out_specs=pl.BlockSpec((tm, d), lambda i: (i, 0)),
        out_shape=jax.ShapeDtypeStruct((t, d), F32),
        compiler_params=_params("arbitrary"),
        name="ffn",
    )(x, gain, sc1p, sh, g2, wg, wu, wd)


def _pool_kernel(x_ref, gain_ref, sc_ref, sh_ref, g_ref, band_ref, bandh_ref, wp_ref, bp_ref, ps_ref, o_ref,
                 hbuf_ref):
    i = pl.program_id(0)
    tm = x_ref.shape[0]
    gw = wp_ref.shape[1]

    @pl.when(i == 0)
    def _():
        hbuf_ref[0:POOL_HALO, :] = jnp.zeros((POOL_HALO, hbuf_ref.shape[1]), F32)

    @pl.when(i > 0)
    def _():
        hbuf_ref[0:POOL_HALO, :] = hbuf_ref[tm:tm + POOL_HALO, :]

    hbuf_ref[POOL_HALO:, :] = _norm_mod(x_ref[...], gain_ref[...], sc_ref[...], sh_ref[...])

    def split(v):
        hi = v.astype(BF16)
        return hi, (v - hi.astype(F32)).astype(BF16)

    sub = band_ref.shape[1]
    for s in range(tm // sub):
        rows = slice(s * sub, (s + 1) * sub)
        lo = POOL_HALO + s * sub
        x = x_ref[rows, :]
        h = hbuf_ref[lo:lo + sub, :]
        h_hi, h_lo = split(h)
        halo_hi, halo_lo = split(hbuf_ref[lo - POOL_HALO:lo, :])
        t1 = (lax.broadcasted_iota(jnp.int32, (sub, 1), 0) + (i * tm + s * sub + 1)).astype(F32)
        for g, w in enumerate(POOL_WINDOWS):
            cols = slice(g * gw, (g + 1) * gw)
            band = band_ref[g]
            bandh = bandh_ref[g]
            win = (jnp.dot(band, h_hi[:, cols], preferred_element_type=F32)
                   + jnp.dot(band, h_lo[:, cols], preferred_element_type=F32))
            head = (jnp.dot(bandh, halo_hi[:, cols], preferred_element_type=F32)
                    + jnp.dot(bandh, halo_lo[:, cols], preferred_element_type=F32))
            win = jnp.concatenate([win[:POOL_HALO] + head, win[POOL_HALO:]], axis=0)
            pooled = win * (1.0 / jnp.minimum(t1, float(w))) - h[:, cols]
            y = jnp.dot(pooled.astype(BF16), wp_ref[g], preferred_element_type=F32) + bp_ref[:, cols]
            o_ref[rows, cols] = x[:, cols] + (g_ref[:, cols] * ps_ref[:, cols]) * y


def _route(x, gain_ref, sc_ref, sh_ref, w_ref, tri_ref, h_ref, gates_ref, sel_ref, rank_ref, tot_ref, carry_ref):
    h = _norm_mod(x, gain_ref[...], sc_ref[...], sh_ref[...])
    h_hi = h.astype(BF16)
    h_ref[...] = h_hi
    h_lo = (h - h_hi.astype(F32)).astype(BF16)
    w = w_ref[...]
    p_hi = jnp.dot(h_hi, w, preferred_element_type=F32)
    p_lo = jnp.dot(h_lo, w, preferred_element_type=F32)
    logits = p_hi[:, :LANES] + p_hi[:, LANES:] + p_lo[:, :LANES] + p_lo[:, LANES:]
    lane = lax.broadcasted_iota(jnp.int32, logits.shape, 1).astype(F32)
    neg = jnp.float32(-jnp.inf)
    lg = jnp.where(lane < float(N_EXPERTS), logits, neg)
    m0 = jnp.max(lg, axis=-1, keepdims=True)
    i0 = jnp.min(jnp.where(lg == m0, lane, float(LANES)), axis=-1, keepdims=True)
    lg1 = jnp.where(lane == i0, neg, lg)
    m1 = jnp.max(lg1, axis=-1, keepdims=True)
    i1 = jnp.min(jnp.where(lg1 == m1, lane, float(LANES)), axis=-1, keepdims=True)
    e1 = jnp.exp(m1 - m0)
    den = 1.0 + e1
    is0 = lane == i0
    is1 = lane == i1
    gates_ref[...] = jnp.where(is0, 1.0 / den, jnp.where(is1, e1 / den, 0.0))
    sel = jnp.where(is0 | is1, 1.0, 0.0).astype(BF16)
    sel_ref[...] = sel

    @pl.when(pl.program_id(0) == 0)
    def _():
        carry_ref[...] = jnp.zeros_like(carry_ref)

    carry = carry_ref[...]
    for b in range(sel.shape[0] // RANK_ROWS):
        rows = slice(b * RANK_ROWS, (b + 1) * RANK_ROWS)
        s = sel[rows, :]
        cum = jnp.dot(tri_ref[...], s, preferred_element_type=F32)
        rank_ref[rows, :] = cum - s.astype(F32) + carry
        carry = carry + cum[RANK_ROWS - 1:, :]
        tot_ref[b] = carry
    carry_ref[...] = carry


def _pool_router_kernel(x_ref, gain_ref, sc_ref, sh_ref, g_ref, band_ref, bandh_ref, wp_ref, bp_ref, ps_ref,
                        gain2_ref, sc2_ref, sh2_ref, wr_ref, tri_ref,
                        o_ref, h_ref, gates_ref, sel_ref, rank_ref, tot_ref, hbuf_ref, carry_ref):
    _pool_kernel(x_ref, gain_ref, sc_ref, sh_ref, g_ref, band_ref, bandh_ref, wp_ref, bp_ref, ps_ref, o_ref,
                 hbuf_ref)
    _route(o_ref[...], gain2_ref, sc2_ref, sh2_ref, wr_ref, tri_ref, h_ref, gates_ref, sel_ref, rank_ref, tot_ref,
           carry_ref)


def _pool_router(x, gain, sc1p, sh, g1, band, bandh, wp, bp, ps, gain2, sc2p, sh2, w_cat, tri):
    t, d = x.shape
    tm = POOL_ROWS
    n_sub = tm // RANK_ROWS
    vec = pl.BlockSpec((1, d), lambda i: (0, 0))
    rows = pl.BlockSpec((tm, d), lambda i: (i, 0))
    lanes = pl.BlockSpec((tm, LANES), lambda i: (i, 0))
    return pl.pallas_call(
        _pool_router_kernel,
        grid=(t // tm,),
        in_specs=[
            rows, vec, vec, vec, vec,
            pl.BlockSpec(band.shape, lambda i: (0, 0, 0)),
            pl.BlockSpec(bandh.shape, lambda i: (0, 0, 0)),
            pl.BlockSpec(wp.shape, lambda i: (0, 0, 0)),
            vec, vec,
            vec, vec, vec,
            pl.BlockSpec((d, 2 * LANES), lambda i: (0, 0)),
            pl.BlockSpec((RANK_ROWS, RANK_ROWS), lambda i: (0, 0)),
        ],
        out_specs=[rows, rows, lanes, lanes, lanes, pl.BlockSpec((n_sub, 1, LANES), lambda i: (i, 0, 0))],
        out_shape=[
            jax.ShapeDtypeStruct((t, d), F32),
            jax.ShapeDtypeStruct((t, d), BF16),
            jax.ShapeDtypeStruct((t, LANES), F32),
            jax.ShapeDtypeStruct((t, LANES), BF16),
            jax.ShapeDtypeStruct((t, LANES), F32),
            jax.ShapeDtypeStruct((t // RANK_ROWS, 1, LANES), F32),
        ],
        scratch_shapes=[pltpu.VMEM((POOL_HALO + tm, d), F32), pltpu.VMEM((1, LANES), F32)],
        compiler_params=_params("arbitrary"),
        name="pool_router",
    )(x, gain, sc1p, sh, g1, band, bandh, wp, bp, ps, gain2, sc2p, sh2, w_cat, tri)


def _gather_kernel(be_ref, ng_ref, clo_ref, chi_ref, h_ref, post_ref, gatest_ref, xb_ref, gs_ref, acc_ref):
    u = pl.program_id(0)
    e = be_ref[u]
    unit_rows = xb_ref.shape[0]
    n_groups = unit_rows // MOE_GATHER_ROWS
    last_chunk = h_ref.shape[0] // GATHER_CHUNK - 1
    g0 = u * n_groups

    @pl.when(ng_ref[g0] > 0)
    def _():
        acc_ref[...] = jnp.zeros_like(acc_ref)
        trips = chi_ref[g0] - clo_ref[g0]
        for q in range(1, n_groups):
            trips = jnp.maximum(trips, chi_ref[g0 + q] - clo_ref[g0 + q])

        def body(i, gsums):
            out = []
            for q in range(n_groups):
                rows = slice(q * MOE_GATHER_ROWS, (q + 1) * MOE_GATHER_ROWS)
                c = clo_ref[g0 + q] + i
                live = c < chi_ref[g0 + q]
                off = pl.multiple_of(jnp.minimum(c, last_chunk) * GATHER_CHUNK, GATHER_CHUNK)
                p = post_ref[pl.ds(e, 1), pl.ds(off, GATHER_CHUNK)]
                p = jnp.where(live, p, -2)
                row_pos = (lax.broadcasted_iota(jnp.int32, (MOE_GATHER_ROWS, GATHER_CHUNK), 0)
                           + (u * unit_rows + q * MOE_GATHER_ROWS))
                match = row_pos == p
                onehot = jnp.where(match, 1.0, 0.0).astype(BF16)
                acc_ref[rows, :] += jnp.dot(onehot, h_ref[pl.ds(off, GATHER_CHUNK), :],
                                            preferred_element_type=F32)
                gate = gatest_ref[pl.ds(e, 1), pl.ds(off, GATHER_CHUNK)]
                out.append(gsums[q] + jnp.sum(jnp.where(match, gate, 0.0), axis=1, keepdims=True))
            return tuple(out)

        zero = jnp.zeros((MOE_GATHER_ROWS, 1), F32)
        gsums = lax.fori_loop(0, trips, body, (zero,) * n_groups)
        xb_ref[...] = acc_ref[...].astype(BF16)
        for q in range(n_groups):
            gs_ref[q * MOE_GATHER_ROWS:(q + 1) * MOE_GATHER_ROWS, :] = gsums[q]

    @pl.when(ng_ref[g0] == 0)
    def _():
        xb_ref[...] = jnp.zeros_like(xb_ref)
        gs_ref[...] = jnp.zeros_like(gs_ref)


def _gather(beu, ng, clo, chi, h, post, gatest):
    t, d = h.shape
    n_units = beu.shape[0]
    grid_spec = pltpu.PrefetchScalarGridSpec(
        num_scalar_prefetch=4,
        grid=(n_units,),
        in_specs=[
            pl.BlockSpec((t, d), lambda u, *_: (0, 0), pipeline_mode=pl.Buffered(1)),
            pl.BlockSpec(post.shape, lambda u, *_: (0, 0), pipeline_mode=pl.Buffered(1)),
            pl.BlockSpec(gatest.shape, lambda u, *_: (0, 0), pipeline_mode=pl.Buffered(1)),
        ],
        out_specs=[
            pl.BlockSpec((MOE_GATHER_UNIT, d), lambda u, *_: (u, 0)),
            pl.BlockSpec((MOE_GATHER_UNIT, 1), lambda u, *_: (u, 0)),
        ],
        scratch_shapes=[pltpu.VMEM((MOE_GATHER_UNIT, d), F32)],
    )
    return pl.pallas_call(
        _gather_kernel,
        grid_spec=grid_spec,
        out_shape=[
            jax.ShapeDtypeStruct((n_units * MOE_GATHER_UNIT, d), BF16),
            jax.ShapeDtypeStruct((n_units * MOE_GATHER_UNIT, 1), F32),
        ],
        compiler_params=_params("arbitrary"),
        name="gather",
    )(beu, ng, clo, chi, h, post, gatest)


def _experts_kernel(be_ref, nv_ref, xb_ref, gs_ref, wg_ref, wu_ref, wd_ref, yb_ref,
                    acc_ref, wgb_ref, wub_ref, wdb_ref):
    b = pl.program_id(0)
    f = pl.program_id(1)
    nf = pl.num_programs(1)
    nv = nv_ref[b]
    block_rows = xb_ref.shape[0]

    @pl.when((b == 0) & (f == 0))
    def _():
        acc_ref[...] = jnp.zeros_like(acc_ref)

    def swiglu_rows(lo, n, cast):
        rows = slice(lo, lo + n)
        if cast:
            wgb_ref[...] = wg_ref[0].astype(BF16)
            wub_ref[...] = wu_ref[0].astype(BF16)
            wdb_ref[...] = wd_ref[0].astype(BF16)
        x = xb_ref[rows, :]
        a = jnp.dot(x, wgb_ref[...], preferred_element_type=F32)
        u = jnp.dot(x, wub_ref[...], preferred_element_type=F32)
        act = (_silu(a) * u).astype(BF16)
        prev = jnp.where(f > 0, acc_ref[rows, :], 0.0)
        acc_ref[rows, :] = prev + jnp.dot(act, wdb_ref[...], preferred_element_type=F32)

    pair = 2 * MOE_DOT_ROWS
    half_group = MOE_DOT_ROWS // 2
    for p in range(block_rows // pair):
        lo = p * pair
        full = nv >= lo + pair

        @pl.when(full)
        def _():
            swiglu_rows(lo, pair, p == 0)

        for hh in range(2):
            glo = lo + hh * MOE_DOT_ROWS
            used = jnp.logical_not(full) & (nv > glo)
            tall = nv > glo + half_group
            first = p == 0 and hh == 0

            @pl.when(used & tall)
            def _():
                swiglu_rows(glo, MOE_DOT_ROWS, first)

            @pl.when(used & jnp.logical_not(tall))
            def _():
                swiglu_rows(glo, half_group, first)

    @pl.when(f == nf - 1)
    def _():
        for g in range(block_rows // MOE_DOT_ROWS):
            rows = slice(g * MOE_DOT_ROWS, (g + 1) * MOE_DOT_ROWS)

            @pl.when(nv > g * MOE_DOT_ROWS)
            def _():
                yb_ref[rows, :] = (acc_ref[rows, :] * gs_ref[rows, :]).astype(BF16)

            @pl.when(nv <= g * MOE_DOT_ROWS)
            def _():
                yb_ref[rows, :] = jnp.zeros((MOE_DOT_ROWS, yb_ref.shape[1]), BF16)


def _experts(be, nv, xb, gs, wg, wu, wd):
    d = xb.shape[1]
    n_blocks = be.shape[0]
    d_ff = wg.shape[2]
    tf = MOE_FF_TILE
    nf = d_ff // tf
    bsz = MOE_BLOCK

    def ff_idx(f, nvr, b):
        return jnp.where(nvr[b] > 0, f, nf - 1)

    grid_spec = pltpu.PrefetchScalarGridSpec(
        num_scalar_prefetch=2,
        grid=(n_blocks, nf),
        in_specs=[
            pl.BlockSpec((bsz, d), lambda b, f, *_: (b, 0)),
            pl.BlockSpec((bsz, 1), lambda b, f, *_: (b, 0)),
            pl.BlockSpec((1, d, tf), lambda b, f, be_r, nv_r: (be_r[b], 0, ff_idx(f, nv_r, b))),
            pl.BlockSpec((1, d, tf), lambda b, f, be_r, nv_r: (be_r[b], 0, ff_idx(f, nv_r, b))),
            pl.BlockSpec((1, tf, d), lambda b, f, be_r, nv_r: (be_r[b], ff_idx(f, nv_r, b), 0)),
        ],
        out_specs=pl.BlockSpec((bsz, d), lambda b, f, *_: (b, 0)),
        scratch_shapes=[
            pltpu.VMEM((bsz, d), F32),
            pltpu.VMEM((d, tf), BF16), pltpu.VMEM((d, tf), BF16), pltpu.VMEM((tf, d), BF16),
        ],
    )
    return pl.pallas_call(
        _experts_kernel,
        grid_spec=grid_spec,
        out_shape=jax.ShapeDtypeStruct((n_blocks * bsz, d), BF16),
        compiler_params=_params("arbitrary", "arbitrary"),
        name="experts",
    )(be, nv, xb, gs, wg, wu, wd)


def _combine_kernel(win_ref, x_ref, pos_ref, g_ref, fg_ref, *refs):
    yb_refs, o_ref = refs[:-1], refs[-1]
    i = pl.program_id(0)
    n_sub = x_ref.shape[0] // COMBINE_ROWS
    lane = lax.broadcasted_iota(jnp.int32, (COMBINE_ROWS, COMBINE_WINDOWS * COMBINE_WIN), 1)
    for s in range(n_sub):
        rows = slice(s * COMBINE_ROWS, (s + 1) * COMBINE_ROWS)
        onehots, windows = [], []
        for e in range(N_EXPERTS):
            k = (i * n_sub + s) * N_EXPERTS + e
            base = win_ref[k] * COMBINE_WIN
            rel = pos_ref[rows, e:e + 1] - base
            onehots.append(jnp.where(lane == rel, 1.0, 0.0).astype(BF16))
            w0 = COMBINE_WINDOWS * (s * N_EXPERTS + e)
            windows += [yb_refs[w0 + j][...] for j in range(COMBINE_WINDOWS)]
        acc = jnp.dot(jnp.concatenate(onehots, axis=1), jnp.concatenate(windows, axis=0),
                      preferred_element_type=F32)
        xo = x_ref[rows, :] + g_ref[...] * acc
        ms = jnp.mean(xo * xo, axis=-1, keepdims=True)
        o_ref[rows, :] = xo * lax.rsqrt(ms + EPS) * fg_ref[...]


def _combine(win, x, pos, g2, fgain, yb):
    t, d = x.shape
    tm = COMBINE_TILE
    n_sub = tm // COMBINE_ROWS
    n_win = yb.shape[0] // COMBINE_WIN
    vec = pl.BlockSpec((1, d), lambda i, w: (0, 0))

    def window(s, e, j):
        return lambda i, w: (jnp.minimum(w[(i * n_sub + s) * N_EXPERTS + e] + j, n_win - 1), 0)

    yb_specs = []
    for s in range(n_sub):
        for e in range(N_EXPERTS):
            for j in range(COMBINE_WINDOWS):
                yb_specs.append(pl.BlockSpec((COMBINE_WIN, d), window(s, e, j)))
    grid_spec = pltpu.PrefetchScalarGridSpec(
        num_scalar_prefetch=1,
        grid=(t // tm,),
        in_specs=[
            pl.BlockSpec((tm, d), lambda i, w: (i, 0)),
            pl.BlockSpec((tm, N_EXPERTS), lambda i, w: (i, 0)),
            vec, vec,
        ] + yb_specs,
        out_specs=pl.BlockSpec((tm, d), lambda i, w: (i, 0)),
    )
    return pl.pallas_call(
        _combine_kernel,
        grid_spec=grid_spec,
        out_shape=jax.ShapeDtypeStruct((t, d), F32),
        compiler_params=_params("arbitrary"),
        name="combine",
    )(win, x, pos, g2, fgain, *([yb] * len(yb_specs)))


def _routing_tables(rank, tot, sel):
    t = rank.shape[0]
    bsz = MOE_BLOCK
    n_blocks = (2 * t) // bsz + N_EXPERTS
    cum = tot[:, 0, :N_EXPERTS].astype(jnp.int32)
    counts = cum[-1]
    padded = ((counts + bsz - 1) // bsz) * bsz
    pad_end = jnp.cumsum(padded)
    start_pad = pad_end - padded
    selected = sel[:, :N_EXPERTS] > 0
    pos = jnp.where(selected, start_pad[None, :] + rank[:, :N_EXPERTS].astype(jnp.int32), -1)

    bstart = jnp.arange(n_blocks, dtype=jnp.int32) * bsz
    be = jnp.minimum(jnp.sum(bstart[:, None] >= pad_end[None, :], axis=1), N_EXPERTS - 1).astype(jnp.int32)
    r0 = bstart - start_pad[be]
    nv = jnp.clip(counts[be] - r0, 0, bsz).astype(jnp.int32)
    n_gather = bsz // MOE_GATHER_ROWS
    goff = jnp.arange(n_gather, dtype=jnp.int32) * MOE_GATHER_ROWS
    r0g = (r0[:, None] + goff[None, :]).reshape(-1)
    ng = jnp.clip(nv[:, None] - goff[None, :], 0, MOE_GATHER_ROWS).reshape(-1)
    beg = jnp.repeat(be, n_gather)
    chunk_end = cum[:, beg].T
    chunk_start = jnp.concatenate([jnp.zeros((1, N_EXPERTS), jnp.int32), cum[:-1]], axis=0)[:, beg].T
    clo = jnp.sum(chunk_end <= r0g[:, None], axis=1).astype(jnp.int32)
    chi = jnp.sum(chunk_start < (r0g + ng)[:, None], axis=1).astype(jnp.int32)
    beu = jnp.repeat(be, bsz // MOE_GATHER_UNIT)

    big = jnp.int32(2 ** 30)
    first = jnp.min(jnp.where(pos >= 0, pos, big).reshape(t // COMBINE_ROWS, COMBINE_ROWS, N_EXPERTS), axis=1)
    win = jnp.where(first == big, 0, first // COMBINE_WIN).astype(jnp.int32).reshape(-1)
    return pos, be, nv, beu, ng, clo, chi, win


def _rotary_tables(seq, dk):
    half = dk // 2
    inv = ROPE_BASE ** (-jnp.arange(half, dtype=F32) / half)
    ang_a = (jnp.arange(seq // ROT_BLOCK) * ROT_BLOCK).astype(F32)[:, None] * inv[None, :]
    ang_b = jnp.arange(ROT_BLOCK).astype(F32)[:, None] * inv[None, :]
    return jnp.cos(ang_a), jnp.sin(ang_a), jnp.cos(ang_b), jnp.sin(ang_b)


def _decay_tables(dk, dv):
    c = RET_CHUNK
    log_gamma = jnp.log1p(-jnp.exp2(-5.0 - jnp.arange(N_HEADS, dtype=F32)))
    idx = jnp.arange(c, dtype=F32)
    row_dec = jnp.exp(log_gamma[:, None] * (idx + 1.0 - c))
    causal = (idx[:, None] >= idx[None, :]).astype(F32)
    dmask = row_dec[:, :, None] * causal[None]
    k_dec = jnp.exp(log_gamma[:, None] * (c - 1.0 - idx)) * (dk ** -0.5)
    q_dec = jnp.exp(log_gamma[:, None] * (idx + 1.0))
    chunk_dec = jnp.exp(log_gamma * c)
    half = dk // 2
    kdec_tile = jnp.tile(jnp.repeat(k_dec.T, half, axis=1), (ROW_TILE // c, 1))
    qdec_tile = jnp.broadcast_to(q_dec[:, :, None], (N_HEADS, c, dv))
    return dmask, kdec_tile, qdec_tile, chunk_dec


def _pool_bands(tm):
    def bands(lag):
        return jnp.stack([((lag >= 0) & (lag < w)) for w in POOL_WINDOWS]).astype(BF16)
    main = bands(jnp.arange(tm)[:, None] - jnp.arange(tm)[None, :])
    head = bands(jnp.arange(POOL_HALO)[:, None] + POOL_HALO - jnp.arange(POOL_HALO)[None, :])
    return main, head


def kernel(x, c, ada_w, ada_b, norm_gain, ret_w_in, ret_gn_gain, ret_w_out, ffn_w_gate, ffn_w_up, ffn_w_down,
           pool_w, pool_b, pool_scale, moe_router, moe_w_gate, moe_w_up, moe_w_down, final_norm_gain):
    bsz, seq, d = x.shape
    assert bsz == 1 and ada_w.shape[0] == 2
    xt = x.reshape(seq, d)
    qk_total = d
    v_total = ret_w_out.shape[1]
    dk = qk_total // N_HEADS
    dv = v_total // N_HEADS

    mod = _modulation(c, ada_w, ada_b)
    def mods(i):
        parts = [mod[i, :, k * d:(k + 1) * d] for k in range(N_MOD)]
        sh1, sc1, g1, sh2, sc2, g2 = parts
        return sh1, 1.0 + sc1, g1, sh2, 1.0 + sc2, g2

    sh1, sc1p, g1, sh2, sc2p, g2 = mods(0)
    dmask, kdec_tile, qdec_tile, chunk_dec = _decay_tables(dk, dv)
    proj, (w_out_b, wg_b, wu_b, wd_b) = _ret_proj(
        xt, norm_gain[0, 0][None], sc1p, sh1, ret_w_in[0].astype(BF16), _rotary_tables(seq, dk), kdec_tile,
        qk_total, v_total, [ret_w_out[0], ffn_w_gate[0], ffn_w_up[0], ffn_w_down[0]])
    x1 = _retention(proj, chunk_dec, dmask, qdec_tile, ret_gn_gain[0][None], xt, g1, w_out_b, qk_total, v_total)
    x2 = _dense_ffn(x1, norm_gain[0, 1][None], sc2p, sh2, g2, wg_b, wu_b, wd_b)

    sh1, sc1p, g1, sh2, sc2p, g2 = mods(1)
    band, bandh = _pool_bands(POOL_SUB_ROWS)
    wr = jnp.pad(moe_router[0], ((0, 0), (0, LANES - N_EXPERTS)))
    wr_hi = wr.astype(BF16)
    wr_lo = (wr - wr_hi.astype(F32)).astype(BF16)
    tri = (jnp.arange(RANK_ROWS)[:, None] >= jnp.arange(RANK_ROWS)[None, :]).astype(BF16)
    x3, h4, gates, sel, rank, tot = _pool_router(
        x2, norm_gain[1, 0][None], sc1p, sh1, g1, band, bandh,
        pool_w[0].astype(BF16), pool_b[0].reshape(1, d), pool_scale[0][None],
        norm_gain[1, 1][None], sc2p, sh2, jnp.concatenate([wr_hi, wr_lo], axis=1), tri)
    pos, be, nv, beu, ng, clo, chi, win = _routing_tables(rank, tot, sel)
    xb, gs = _gather(beu, ng, clo, chi, h4, pos.T, gates[:, :N_EXPERTS].T)
    yb = _experts(be, nv, xb, gs, moe_w_gate[0], moe_w_up[0], moe_w_down[0])
    out = _combine(win, x3, pos, g2, final_norm_gain[None], yb)
    return out.reshape(bsz, seq, d)
```

```python
import functools

import jax
import jax.numpy as jnp
from jax import lax
from jax.experimental import pallas as pl
from jax.experimental.pallas import tpu as pltpu

F32 = jnp.float32
BF16 = jnp.bfloat16

EPS = 1e-6
N_HEADS = 4
RET_CHUNK = 256
ROPE_BASE = 10000.0
ROT_BLOCK = 128
POOL_WINDOWS = (2, 4, 8, 16)
POOL_HALO = 16
N_EXPERTS = 8
N_MOD = 6

VMEM_LIMIT_BYTES = 56 * 1024 * 1024

ROW_TILE = 1024
PROJ_COL_TILE = 3072
RET_ROWS = 512
POOL_ROWS = 1024
POOL_SUB_ROWS = 256
RANK_ROWS = 512
MOE_BLOCK = 2048
MOE_DOT_ROWS = 512
MOE_GATHER_UNIT = 1024
MOE_GATHER_ROWS = 256
MOE_FF_TILE = 512
GATHER_CHUNK = 512
COMBINE_TILE = 256
COMBINE_ROWS = 256
COMBINE_WIN = 128
COMBINE_WINDOWS = COMBINE_ROWS // COMBINE_WIN + 1
LANES = 128


def _silu(v):
    return v / (1.0 + jnp.exp(-v))


def _norm_mod(x, gain, scale1p, shift):
    ms = jnp.mean(x * x, axis=-1, keepdims=True)
    return (x * lax.rsqrt(ms + EPS)) * (gain * scale1p) + shift


def _params(*sem):
    return pltpu.CompilerParams(dimension_semantics=sem, vmem_limit_bytes=VMEM_LIMIT_BYTES)


def _resident(shape):
    nd = len(shape)
    return pl.BlockSpec(shape, lambda *_: (0,) * nd, pipeline_mode=pl.Buffered(1))


def _mod_kernel(c_ref, w_ref, b_ref, o_ref):
    ca = _silu(c_ref[...])
    o_ref[0] = jnp.sum(ca * w_ref[0], axis=0, keepdims=True) + b_ref[0]


def _modulation(c, ada_w, ada_b):
    depth, d, n = ada_w.shape
    tn = 1024
    return pl.pallas_call(
        _mod_kernel,
        grid=(depth, n // tn),
        in_specs=[
            pl.BlockSpec((d, 1), lambda i, j: (0, 0)),
            pl.BlockSpec((1, d, tn), lambda i, j: (i, 0, j)),
            pl.BlockSpec((1, 1, tn), lambda i, j: (i, 0, j)),
        ],
        out_specs=pl.BlockSpec((1, 1, tn), lambda i, j: (i, 0, j)),
        out_shape=jax.ShapeDtypeStruct((depth, 1, n), F32),
        compiler_params=_params("arbitrary", "arbitrary"),
        name="mod",
    )(c.reshape(d, 1), ada_w, ada_b.reshape(depth, 1, n))


def _proj_kernel(x_ref, gain_ref, sc_ref, sh_ref, w_ref, cosa_ref, sina_ref, cosb_ref, sinb_ref, kdec_ref, *refs,
                 n_qk_chunks, n_v_chunks, n_col_tiles, dk, n_cast):
    cast_in = refs[:n_cast]
    o_ref = refs[n_cast]
    cast_out = refs[n_cast + 1:2 * n_cast + 1]
    h_ref, cos_ref, sin_ref = refs[2 * n_cast + 1:]
    j = pl.program_id(1)

    @pl.when(j == 0)
    def _():
        h_ref[...] = _norm_mod(x_ref[...], gain_ref[...], sc_ref[...], sh_ref[...]).astype(BF16)
        for src, dst in zip(cast_in, cast_out):
            dst[...] = src[...].astype(BF16)
        cb = cosb_ref[...]
        sb = sinb_ref[...]
        for a in range(cosa_ref.shape[0]):
            ca = cosa_ref[a:a + 1, :]
            sa = sina_ref[a:a + 1, :]
            cos_ref[a * ROT_BLOCK:(a + 1) * ROT_BLOCK, :] = ca * cb - sa * sb
            sin_ref[a * ROT_BLOCK:(a + 1) * ROT_BLOCK, :] = sa * cb + ca * sb

    half = dk // 2
    n_chunks = w_ref.shape[1] // dk

    def chunk(c, kind, head):
        acc = jnp.dot(h_ref[...], w_ref[:, c * dk:(c + 1) * dk], preferred_element_type=F32)
        if kind == "v":
            o_ref[:, c * dk:(c + 1) * dk] = acc.astype(BF16)
        elif kind == "g":
            o_ref[:, c * dk:(c + 1) * dk] = _silu(acc).astype(BF16)
        else:
            cos = cos_ref[...]
            sin = sin_ref[...]
            t1 = acc[:, :half]
            t2 = acc[:, half:]
            o1 = t1 * cos - t2 * sin
            o2 = t1 * sin + t2 * cos
            if kind == "k":
                s = kdec_ref[:, head * half:(head + 1) * half]
                o1 = o1 * s
                o2 = o2 * s
            o_ref[:, c * dk:c * dk + half] = o1.astype(BF16)
            o_ref[:, c * dk + half:(c + 1) * dk] = o2.astype(BF16)

    def kind_of(gc):
        if gc < n_qk_chunks:
            return "q", gc
        if gc < 2 * n_qk_chunks:
            return "k", gc - n_qk_chunks
        if gc < 2 * n_qk_chunks + n_v_chunks:
            return "v", 0
        return "g", 0

    for jj in range(n_col_tiles):
        @pl.when(j == jj)
        def _():
            for c in range(n_chunks):
                chunk(c, *kind_of(jj * n_chunks + c))


def _ret_proj(x, gain, sc1p, sh, w_in, rot, kdec, qk_total, v_total, to_cast):
    t, d = x.shape
    n = w_in.shape[1]
    tm, tn = ROW_TILE, PROJ_COL_TILE
    n_row_tiles = t // tm
    dk = qk_total // N_HEADS
    half = dk // 2
    cosa, sina, cosb, sinb = rot
    kern = functools.partial(_proj_kernel, n_qk_chunks=qk_total // dk, n_v_chunks=v_total // dk,
                             n_col_tiles=n // tn, dk=dk, n_cast=len(to_cast))
    vec = pl.BlockSpec((1, d), lambda i, j: (0, 0))
    rot_a = pl.BlockSpec((tm // ROT_BLOCK, half), lambda i, j: (i, 0))
    rot_b = pl.BlockSpec((ROT_BLOCK, half), lambda i, j: (0, 0))
    cast_specs = [pl.BlockSpec((w.shape[0] // n_row_tiles, w.shape[1]), lambda i, j: (i, 0)) for w in to_cast]
    outs = pl.pallas_call(
        kern,
        grid=(n_row_tiles, n // tn),
        in_specs=[
            pl.BlockSpec((tm, d), lambda i, j: (i, 0)),
            vec, vec, vec,
            pl.BlockSpec((d, tn), lambda i, j: (0, j)),
            rot_a, rot_a, rot_b, rot_b,
            pl.BlockSpec((tm, N_HEADS * half), lambda i, j: (0, 0)),
        ] + cast_specs,
        out_specs=[pl.BlockSpec((tm, tn), lambda i, j: (i, j))] + cast_specs,
        out_shape=[jax.ShapeDtypeStruct((t, n), BF16)] + [jax.ShapeDtypeStruct(w.shape, BF16) for w in to_cast],
        scratch_shapes=[pltpu.VMEM((tm, d), BF16), pltpu.VMEM((tm, half), F32), pltpu.VMEM((tm, half), F32)],
        compiler_params=_params("arbitrary", "arbitrary"),
        name="proj",
    )(x, gain, sc1p, sh, w_in, cosa, sina, cosb, sinb, kdec, *to_cast)
    return outs[0], outs[1:]


def _ret_kernel(cdec_ref, q_ref, k_ref, v_ref, sg_ref, dmask_ref, qdec_ref, gn_ref, x_ref, g_ref, wo_ref, o_ref,
                state_ref, y_ref, *, dk, dv):
    @pl.when(pl.program_id(0) == 0)
    def _():
        state_ref[...] = jnp.zeros_like(state_ref)

    rows_per_step = q_ref.shape[0]
    for n in range(rows_per_step // RET_CHUNK):
        rows = slice(n * RET_CHUNK, (n + 1) * RET_CHUNK)
        if n > 0:
            prev = slice((n - 1) * RET_CHUNK, n * RET_CHUNK)
            o_ref[prev, :] = x_ref[prev, :] + g_ref[...] * jnp.dot(y_ref[prev, :], wo_ref[...],
                                                                   preferred_element_type=F32)
        for h in range(N_HEADS):
            qcols = slice(h * dk, (h + 1) * dk)
            vcols = slice(h * dv, (h + 1) * dv)
            q = q_ref[rows, qcols]
            ks = k_ref[rows, qcols]
            v = v_ref[rows, vcols]
            s = lax.dot_general(q, ks, (((1,), (1,)), ((), ())), preferred_element_type=F32)
            a = (s * dmask_ref[h]).astype(BF16)
            intra = jnp.dot(a, v, preferred_element_type=F32)
            st = state_ref[h]
            cross = jnp.dot(q, st.astype(BF16), preferred_element_type=F32)
            o = intra + cross * qdec_ref[h]
            kv = lax.dot_general(ks, v, (((0,), (0,)), ((), ())), preferred_element_type=F32)
            state_ref[h] = st * cdec_ref[h] + kv
            mu = jnp.mean(o, axis=-1, keepdims=True)
            dlt = o - mu
            var = jnp.mean(dlt * dlt, axis=-1, keepdims=True)
            on = dlt * lax.rsqrt(var + EPS)
            y = on * gn_ref[:, vcols] * sg_ref[rows, vcols].astype(F32)
            y_ref[rows, vcols] = y.astype(BF16)
    last = slice(rows_per_step - RET_CHUNK, rows_per_step)
    o_ref[last, :] = x_ref[last, :] + g_ref[...] * jnp.dot(y_ref[last, :], wo_ref[...],
                                                           preferred_element_type=F32)


def _retention(proj, cdec, dmask, qdec, gn_gain, x, g1, w_out, qk_total, v_total):
    t, d = x.shape
    r = RET_ROWS
    dk = qk_total // N_HEADS
    dv = v_total // N_HEADS
    assert v_total == 2 * qk_total
    kern = functools.partial(_ret_kernel, dk=dk, dv=dv)
    return pl.pallas_call(
        kern,
        grid=(t // r,),
        in_specs=[
            pl.BlockSpec(memory_space=pltpu.SMEM),
            pl.BlockSpec((r, qk_total), lambda i: (i, 0)),
            pl.BlockSpec((r, qk_total), lambda i: (i, 1)),
            pl.BlockSpec((r, v_total), lambda i: (i, 1)),
            pl.BlockSpec((r, v_total), lambda i: (i, 2)),
            pl.BlockSpec((N_HEADS, RET_CHUNK, RET_CHUNK), lambda i: (0, 0, 0)),
            pl.BlockSpec((N_HEADS, RET_CHUNK, dv), lambda i: (0, 0, 0)),
            pl.BlockSpec((1, v_total), lambda i: (0, 0)),
            pl.BlockSpec((r, d), lambda i: (i, 0)),
            pl.BlockSpec((1, d), lambda i: (0, 0)),
            _resident((v_total, d)),
        ],
        out_specs=pl.BlockSpec((r, d), lambda i: (i, 0)),
        out_shape=jax.ShapeDtypeStruct((t, d), F32),
        scratch_shapes=[pltpu.VMEM((N_HEADS, dk, dv), F32), pltpu.VMEM((r, v_total), BF16)],
        compiler_params=_params("arbitrary"),
        name="ret",
    )(cdec, proj, proj, proj, proj, dmask, qdec, gn_gain, x, g1, w_out)


def _ffn_kernel(x_ref, gain_ref, sc_ref, sh_ref, g_ref, wg_ref, wu_ref, wd_ref, o_ref, *, ff_tile):
    x = x_ref[...]
    h = _norm_mod(x, gain_ref[...], sc_ref[...], sh_ref[...]).astype(BF16)
    d_ff = wg_ref.shape[1]
    acc = jnp.zeros(x.shape, F32)
    for lo in range(0, d_ff, ff_tile):
        hi = min(lo + ff_tile, d_ff)
        a = jnp.dot(h, wg_ref[:, lo:hi], preferred_element_type=F32)
        b = jnp.dot(h, wu_ref[:, lo:hi], preferred_element_type=F32)
        act = (_silu(a) * b).astype(BF16)
        acc = acc + jnp.dot(act, wd_ref[lo:hi, :], preferred_element_type=F32)
    o_ref[...] = x + g_ref[...] * acc


def _dense_ffn(x, gain, sc1p, sh, g2, wg, wu, wd):
    t, d = x.shape
    d_ff = wg.shape[1]
    tm = ROW_TILE
    vec = pl.BlockSpec((1, d), lambda i: (0, 0))
    return pl.pallas_call(
        functools.partial(_ffn_kernel, ff_tile=512),
        grid=(t // tm,),
        in_specs=[
            pl.BlockSpec((tm, d), lambda i: (i, 0)),
            vec, vec, vec, vec,
            _resident((d, d_ff)), _resident((d, d_ff)), _resident((d_ff, d)),
        ],
        out_specs=pl.BlockSpec((tm, d), lambda i: (i, 0)),
        out_shape=jax.ShapeDtypeStruct((t, d), F32),
        compiler_params=_params("arbitrary"),
        name="ffn",
    )(x, gain, sc1p, sh, g2, wg, wu, wd)


def _pool_kernel(x_ref, gain_ref, sc_ref, sh_ref, g_ref, band_ref, wp_ref, bp_ref, ps_ref, o_ref,
                 hbuf_ref):
    i = pl.program_id(0)
    tm = x_ref.shape[0]
    gw = wp_ref.shape[1]

    @pl.when(i == 0)
    def _():
        hbuf_ref[0:POOL_HALO, :] = jnp.zeros((POOL_HALO, hbuf_ref.shape[1]), F32)

    @pl.when(i > 0)
    def _():
        hbuf_ref[0:POOL_HALO, :] = hbuf_ref[tm:tm + POOL_HALO, :]

    hbuf_ref[POOL_HALO:, :] = _norm_mod(x_ref[...], gain_ref[...], sc_ref[...], sh_ref[...])

    def split(v):
        hi = v.astype(BF16)
        return hi, (v - hi.astype(F32)).astype(BF16)

    sub = band_ref.shape[1]
    for s in range(tm // sub):
        rows = slice(s * sub, (s + 1) * sub)
        lo = POOL_HALO + s * sub
        x = x_ref[rows, :]
        h = hbuf_ref[lo:lo + sub, :]
        h_hi, h_lo = split(h)
        t1 = (lax.broadcasted_iota(jnp.int32, (sub, 1), 0) + (i * tm + s * sub + 1)).astype(F32)
        first_rows = lax.broadcasted_iota(jnp.int32, (POOL_HALO, 1), 0)
        for g, w in enumerate(POOL_WINDOWS):
            cols = slice(g * gw, (g + 1) * gw)
            band = band_ref[g]
            win = (jnp.dot(band, h_hi[:, cols], preferred_element_type=F32)
                   + jnp.dot(band, h_lo[:, cols], preferred_element_type=F32))
            head = jnp.zeros((POOL_HALO, gw), F32)
            for k in range(1, w):
                prev_row = hbuf_ref[lo - k:lo - k + 1, cols]
                head = head + jnp.where(first_rows < w - k, prev_row, 0.0)
            win = jnp.concatenate([win[:POOL_HALO] + head, win[POOL_HALO:]], axis=0)
            pooled = win * (1.0 / jnp.minimum(t1, float(w))) - h[:, cols]
            y = jnp.dot(pooled.astype(BF16), wp_ref[g], preferred_element_type=F32) + bp_ref[:, cols]
            o_ref[rows, cols] = x[:, cols] + (g_ref[:, cols] * ps_ref[:, cols]) * y


def _route(x, gain_ref, sc_ref, sh_ref, w_ref, tri_ref, h_ref, gates_ref, sel_ref, rank_ref, tot_ref, carry_ref):
    h = _norm_mod(x, gain_ref[...], sc_ref[...], sh_ref[...])
    h_hi = h.astype(BF16)
    h_ref[...] = h_hi
    h_lo = (h - h_hi.astype(F32)).astype(BF16)
    w = w_ref[...]
    p_hi = jnp.dot(h_hi, w, preferred_element_type=F32)
    p_lo = jnp.dot(h_lo, w, preferred_element_type=F32)
    logits = p_hi[:, :LANES] + p_hi[:, LANES:] + p_lo[:, :LANES] + p_lo[:, LANES:]
    lane = lax.broadcasted_iota(jnp.int32, logits.shape, 1).astype(F32)
    neg = jnp.float32(-jnp.inf)
    lg = jnp.where(lane < float(N_EXPERTS), logits, neg)
    m0 = jnp.max(lg, axis=-1, keepdims=True)
    i0 = jnp.min(jnp.where(lg == m0, lane, float(LANES)), axis=-1, keepdims=True)
    lg1 = jnp.where(lane == i0, neg, lg)
    m1 = jnp.max(lg1, axis=-1, keepdims=True)
    i1 = jnp.min(jnp.where(lg1 == m1, lane, float(LANES)), axis=-1, keepdims=True)
    e1 = jnp.exp(m1 - m0)
    den = 1.0 + e1
    is0 = lane == i0
    is1 = lane == i1
    gates_ref[...] = jnp.where(is0, 1.0 / den, jnp.where(is1, e1 / den, 0.0))
    sel = jnp.where(is0 | is1, 1.0, 0.0).astype(BF16)
    sel_ref[...] = sel

    @pl.when(pl.program_id(0) == 0)
    def _():
        carry_ref[...] = jnp.zeros_like(carry_ref)

    carry = carry_ref[...]
    for b in range(sel.shape[0] // RANK_ROWS):
        rows = slice(b * RANK_ROWS, (b + 1) * RANK_ROWS)
        s = sel[rows, :]
        cum = jnp.dot(tri_ref[...], s, preferred_element_type=F32)
        rank_ref[rows, :] = cum - s.astype(F32) + carry
        carry = carry + cum[RANK_ROWS - 1:, :]
        tot_ref[b] = carry
    carry_ref[...] = carry


def _pool_router_kernel(x_ref, gain_ref, sc_ref, sh_ref, g_ref, band_ref, wp_ref, bp_ref, ps_ref,
                        gain2_ref, sc2_ref, sh2_ref, wr_ref, tri_ref,
                        o_ref, h_ref, gates_ref, sel_ref, rank_ref, tot_ref, hbuf_ref, carry_ref):
    _pool_kernel(x_ref, gain_ref, sc_ref, sh_ref, g_ref, band_ref, wp_ref, bp_ref, ps_ref, o_ref,
                 hbuf_ref)
    _route(o_ref[...], gain2_ref, sc2_ref, sh2_ref, wr_ref, tri_ref, h_ref, gates_ref, sel_ref, rank_ref, tot_ref,
           carry_ref)


def _pool_router(x, gain, sc1p, sh, g1, band, wp, bp, ps, gain2, sc2p, sh2, w_cat, tri):
    t, d = x.shape
    tm = POOL_ROWS
    n_sub = tm // RANK_ROWS
    vec = pl.BlockSpec((1, d), lambda i: (0, 0))
    rows = pl.BlockSpec((tm, d), lambda i: (i, 0))
    lanes = pl.BlockSpec((tm, LANES), lambda i: (i, 0))
    return pl.pallas_call(
        _pool_router_kernel,
        grid=(t // tm,),
        in_specs=[
            rows, vec, vec, vec, vec,
            pl.BlockSpec(band.shape, lambda i: (0, 0, 0)),
            pl.BlockSpec(wp.shape, lambda i: (0, 0, 0)),
            vec, vec,
            vec, vec, vec,
            pl.BlockSpec((d, 2 * LANES), lambda i: (0, 0)),
            pl.BlockSpec((RANK_ROWS, RANK_ROWS), lambda i: (0, 0)),
        ],
        out_specs=[rows, rows, lanes, lanes, lanes, pl.BlockSpec((n_sub, 1, LANES), lambda i: (i, 0, 0))],
        out_shape=[
            jax.ShapeDtypeStruct((t, d), F32),
            jax.ShapeDtypeStruct((t, d), BF16),
            jax.ShapeDtypeStruct((t, LANES), F32),
            jax.ShapeDtypeStruct((t, LANES), BF16),
            jax.ShapeDtypeStruct((t, LANES), F32),
            jax.ShapeDtypeStruct((t // RANK_ROWS, 1, LANES), F32),
        ],
        scratch_shapes=[pltpu.VMEM((POOL_HALO + tm, d), F32), pltpu.VMEM((1, LANES), F32)],
        compiler_params=_params("arbitrary"),
        name="pool_router",
    )(x, gain, sc1p, sh, g1, band, wp, bp, ps, gain2, sc2p, sh2, w_cat, tri)


def _gather_kernel(be_ref, ng_ref, clo_ref, chi_ref, h_ref, post_ref, gatest_ref, xb_ref, gs_ref, acc_ref):
    u = pl.program_id(0)
    e = be_ref[u]
    unit_rows = xb_ref.shape[0]
    n_groups = unit_rows // MOE_GATHER_ROWS
    last_chunk = h_ref.shape[0] // GATHER_CHUNK - 1
    g0 = u * n_groups

    @pl.when(ng_ref[g0] > 0)
    def _():
        acc_ref[...] = jnp.zeros_like(acc_ref)
        trips = chi_ref[g0] - clo_ref[g0]
        for q in range(1, n_groups):
            trips = jnp.maximum(trips, chi_ref[g0 + q] - clo_ref[g0 + q])

        def body(i, gsums):
            out = []
            for q in range(n_groups):
                rows = slice(q * MOE_GATHER_ROWS, (q + 1) * MOE_GATHER_ROWS)
                c = clo_ref[g0 + q] + i
                live = c < chi_ref[g0 + q]
                off = pl.multiple_of(jnp.minimum(c, last_chunk) * GATHER_CHUNK, GATHER_CHUNK)
                p = post_ref[pl.ds(e, 1), pl.ds(off, GATHER_CHUNK)]
                p = jnp.where(live, p, -2)
                row_pos = (lax.broadcasted_iota(jnp.int32, (MOE_GATHER_ROWS, GATHER_CHUNK), 0)
                           + (u * unit_rows + q * MOE_GATHER_ROWS))
                match = row_pos == p
                onehot = jnp.where(match, 1.0, 0.0).astype(BF16)
                acc_ref[rows, :] += jnp.dot(onehot, h_ref[pl.ds(off, GATHER_CHUNK), :],
                                            preferred_element_type=F32)
                gate = gatest_ref[pl.ds(e, 1), pl.ds(off, GATHER_CHUNK)]
                out.append(gsums[q] + jnp.sum(jnp.where(match, gate, 0.0), axis=1, keepdims=True))
            return tuple(out)

        zero = jnp.zeros((MOE_GATHER_ROWS, 1), F32)
        gsums = lax.fori_loop(0, trips, body, (zero,) * n_groups)
        xb_ref[...] = acc_ref[...].astype(BF16)
        for q in range(n_groups):
            gs_ref[q * MOE_GATHER_ROWS:(q + 1) * MOE_GATHER_ROWS, :] = gsums[q]

    @pl.when(ng_ref[g0] == 0)
    def _():
        xb_ref[...] = jnp.zeros_like(xb_ref)
        gs_ref[...] = jnp.zeros_like(gs_ref)


def _gather(beu, ng, clo, chi, h, post, gatest):
    t, d = h.shape
    n_units = beu.shape[0]
    grid_spec = pltpu.PrefetchScalarGridSpec(
        num_scalar_prefetch=4,
        grid=(n_units,),
        in_specs=[
            pl.BlockSpec((t, d), lambda u, *_: (0, 0), pipeline_mode=pl.Buffered(1)),
            pl.BlockSpec(post.shape, lambda u, *_: (0, 0), pipeline_mode=pl.Buffered(1)),
            pl.BlockSpec(gatest.shape, lambda u, *_: (0, 0), pipeline_mode=pl.Buffered(1)),
        ],
        out_specs=[
            pl.BlockSpec((MOE_GATHER_UNIT, d), lambda u, *_: (u, 0)),
            pl.BlockSpec((MOE_GATHER_UNIT, 1), lambda u, *_: (u, 0)),
        ],
        scratch_shapes=[pltpu.VMEM((MOE_GATHER_UNIT, d), F32)],
    )
    return pl.pallas_call(
        _gather_kernel,
        grid_spec=grid_spec,
        out_shape=[
            jax.ShapeDtypeStruct((n_units * MOE_GATHER_UNIT, d), BF16),
            jax.ShapeDtypeStruct((n_units * MOE_GATHER_UNIT, 1), F32),
        ],
        compiler_params=_params("arbitrary"),
        name="gather",
    )(beu, ng, clo, chi, h, post, gatest)


def _experts_kernel(be_ref, nv_ref, xb_ref, gs_ref, wg_ref, wu_ref, wd_ref, yb_ref,
                    acc_ref, wgb_ref, wub_ref, wdb_ref):
    b = pl.program_id(0)
    f = pl.program_id(1)
    nf = pl.num_programs(1)
    nv = nv_ref[b]
    block_rows = xb_ref.shape[0]

    @pl.when((b == 0) & (f == 0))
    def _():
        acc_ref[...] = jnp.zeros_like(acc_ref)

    def swiglu_rows(lo, n, cast):
        rows = slice(lo, lo + n)
        if cast:
            wgb_ref[...] = wg_ref[0].astype(BF16)
            wub_ref[...] = wu_ref[0].astype(BF16)
            wdb_ref[...] = wd_ref[0].astype(BF16)
        x = xb_ref[rows, :]
        a = jnp.dot(x, wgb_ref[...], preferred_element_type=F32)
        u = jnp.dot(x, wub_ref[...], preferred_element_type=F32)
        act = (_silu(a) * u).astype(BF16)
        prev = jnp.where(f > 0, acc_ref[rows, :], 0.0)
        new = prev + jnp.dot(act, wdb_ref[...], preferred_element_type=F32)
        acc_ref[rows, :] = new
        yb_ref[rows, :] = (new * gs_ref[rows, :]).astype(BF16)

    pair = 2 * MOE_DOT_ROWS
    half_group = MOE_DOT_ROWS // 2
    for p in range(block_rows // pair):
        lo = p * pair
        full = nv >= lo + pair

        @pl.when(full)
        def _():
            swiglu_rows(lo, pair, p == 0)

        for hh in range(2):
            glo = lo + hh * MOE_DOT_ROWS
            used = jnp.logical_not(full) & (nv > glo)
            tall = nv > glo + half_group
            first = p == 0 and hh == 0

            @pl.when(used & tall)
            def _():
                swiglu_rows(glo, MOE_DOT_ROWS, first)

            @pl.when(used & jnp.logical_not(tall))
            def _():
                swiglu_rows(glo, half_group, first)

    @pl.when(f == nf - 1)
    def _():
        for q in range(block_rows // half_group):
            @pl.when(nv <= q * half_group)
            def _():
                yb_ref[q * half_group:(q + 1) * half_group, :] = jnp.zeros((half_group, yb_ref.shape[1]), BF16)


def _experts(be, nv, xb, gs, wg, wu, wd):
    d = xb.shape[1]
    n_blocks = be.shape[0]
    d_ff = wg.shape[2]
    tf = MOE_FF_TILE
    nf = d_ff // tf
    bsz = MOE_BLOCK

    def ff_idx(f, nvr, b):
        return jnp.where(nvr[b] > 0, f, nf - 1)

    grid_spec = pltpu.PrefetchScalarGridSpec(
        num_scalar_prefetch=2,
        grid=(n_blocks, nf),
        in_specs=[
            pl.BlockSpec((bsz, d), lambda b, f, *_: (b, 0)),
            pl.BlockSpec((bsz, 1), lambda b, f, *_: (b, 0)),
            pl.BlockSpec((1, d, tf), lambda b, f, be_r, nv_r: (be_r[b], 0, ff_idx(f, nv_r, b))),
            pl.BlockSpec((1, d, tf), lambda b, f, be_r, nv_r: (be_r[b], 0, ff_idx(f, nv_r, b))),
            pl.BlockSpec((1, tf, d), lambda b, f, be_r, nv_r: (be_r[b], ff_idx(f, nv_r, b), 0)),
        ],
        out_specs=pl.BlockSpec((bsz, d), lambda b, f, *_: (b, 0)),
        scratch_shapes=[
            pltpu.VMEM((bsz, d), F32),
            pltpu.VMEM((d, tf), BF16), pltpu.VMEM((d, tf), BF16), pltpu.VMEM((tf, d), BF16),
        ],
    )
    return pl.pallas_call(
        _experts_kernel,
        grid_spec=grid_spec,
        out_shape=jax.ShapeDtypeStruct((n_blocks * bsz, d), BF16),
        compiler_params=_params("arbitrary", "arbitrary"),
        name="experts",
    )(be, nv, xb, gs, wg, wu, wd)


def _combine_kernel(win_ref, x_ref, pos_ref, g_ref, fg_ref, *refs):
    yb_refs, o_ref = refs[:-1], refs[-1]
    i = pl.program_id(0)
    n_sub = x_ref.shape[0] // COMBINE_ROWS
    lane = lax.broadcasted_iota(jnp.int32, (COMBINE_ROWS, COMBINE_WINDOWS * COMBINE_WIN), 1)
    for s in range(n_sub):
        rows = slice(s * COMBINE_ROWS, (s + 1) * COMBINE_ROWS)
        onehots, windows = [], []
        for e in range(N_EXPERTS):
            k = (i * n_sub + s) * N_EXPERTS + e
            base = win_ref[k] * COMBINE_WIN
            rel = pos_ref[rows, e:e + 1] - base
            onehots.append(jnp.where(lane == rel, 1.0, 0.0).astype(BF16))
            w0 = COMBINE_WINDOWS * (s * N_EXPERTS + e)
            windows += [yb_refs[w0 + j][...] for j in range(COMBINE_WINDOWS)]
        acc = jnp.dot(jnp.concatenate(onehots, axis=1), jnp.concatenate(windows, axis=0),
                      preferred_element_type=F32)
        xo = x_ref[rows, :] + g_ref[...] * acc
        ms = jnp.mean(xo * xo, axis=-1, keepdims=True)
        o_ref[rows, :] = xo * lax.rsqrt(ms + EPS) * fg_ref[...]


def _combine(win, x, pos, g2, fgain, yb):
    t, d = x.shape
    tm = COMBINE_TILE
    n_sub = tm // COMBINE_ROWS
    n_win = yb.shape[0] // COMBINE_WIN
    vec = pl.BlockSpec((1, d), lambda i, w: (0, 0))

    def window(s, e, j):
        return lambda i, w: (jnp.minimum(w[(i * n_sub + s) * N_EXPERTS + e] + j, n_win - 1), 0)

    yb_specs = []
    for s in range(n_sub):
        for e in range(N_EXPERTS):
            for j in range(COMBINE_WINDOWS):
                yb_specs.append(pl.BlockSpec((COMBINE_WIN, d), window(s, e, j)))
    grid_spec = pltpu.PrefetchScalarGridSpec(
        num_scalar_prefetch=1,
        grid=(t // tm,),
        in_specs=[
            pl.BlockSpec((tm, d), lambda i, w: (i, 0)),
            pl.BlockSpec((tm, N_EXPERTS), lambda i, w: (i, 0)),
            vec, vec,
        ] + yb_specs,
        out_specs=pl.BlockSpec((tm, d), lambda i, w: (i, 0)),
    )
    return pl.pallas_call(
        _combine_kernel,
        grid_spec=grid_spec,
        out_shape=jax.ShapeDtypeStruct((t, d), F32),
        compiler_params=_params("arbitrary"),
        name="combine",
    )(win, x, pos, g2, fgain, *([yb] * len(yb_specs)))


def _routing_tables(rank, tot, sel):
    t = rank.shape[0]
    bsz = MOE_BLOCK
    n_blocks = (2 * t) // bsz + N_EXPERTS
    cum = tot[:, 0, :N_EXPERTS].astype(jnp.int32)
    counts = cum[-1]
    padded = ((counts + bsz - 1) // bsz) * bsz
    pad_end = jnp.cumsum(padded)
    start_pad = pad_end - padded
    selected = sel[:, :N_EXPERTS] > 0
    pos = jnp.where(selected, start_pad[None, :] + rank[:, :N_EXPERTS].astype(jnp.int32), -1)

    bstart = jnp.arange(n_blocks, dtype=jnp.int32) * bsz
    be = jnp.minimum(jnp.sum(bstart[:, None] >= pad_end[None, :], axis=1), N_EXPERTS - 1).astype(jnp.int32)
    r0 = bstart - start_pad[be]
    nv = jnp.clip(counts[be] - r0, 0, bsz).astype(jnp.int32)
    n_gather = bsz // MOE_GATHER_ROWS
    goff = jnp.arange(n_gather, dtype=jnp.int32) * MOE_GATHER_ROWS
    r0g = (r0[:, None] + goff[None, :]).reshape(-1)
    ng = jnp.clip(nv[:, None] - goff[None, :], 0, MOE_GATHER_ROWS).reshape(-1)
    beg = jnp.repeat(be, n_gather)
    chunk_end = cum[:, beg].T
    chunk_start = jnp.concatenate([jnp.zeros((1, N_EXPERTS), jnp.int32), cum[:-1]], axis=0)[:, beg].T
    clo = jnp.sum(chunk_end <= r0g[:, None], axis=1).astype(jnp.int32)
    chi = jnp.sum(chunk_start < (r0g + ng)[:, None], axis=1).astype(jnp.int32)
    beu = jnp.repeat(be, bsz // MOE_GATHER_UNIT)

    big = jnp.int32(2 ** 30)
    first = jnp.min(jnp.where(pos >= 0, pos, big).reshape(t // COMBINE_ROWS, COMBINE_ROWS, N_EXPERTS), axis=1)
    win = jnp.where(first == big, 0, first // COMBINE_WIN).astype(jnp.int32).reshape(-1)
    return pos, be, nv, beu, ng, clo, chi, win


def _rotary_tables(seq, dk):
    half = dk // 2
    inv = ROPE_BASE ** (-jnp.arange(half, dtype=F32) / half)
    ang_a = (jnp.arange(seq // ROT_BLOCK) * ROT_BLOCK).astype(F32)[:, None] * inv[None, :]
    ang_b = jnp.arange(ROT_BLOCK).astype(F32)[:, None] * inv[None, :]
    return jnp.cos(ang_a), jnp.sin(ang_a), jnp.cos(ang_b), jnp.sin(ang_b)


def _decay_tables(dk, dv):
    c = RET_CHUNK
    log_gamma = jnp.log1p(-jnp.exp2(-5.0 - jnp.arange(N_HEADS, dtype=F32)))
    idx = jnp.arange(c, dtype=F32)
    row_dec = jnp.exp(log_gamma[:, None] * (idx + 1.0 - c))
    causal = (idx[:, None] >= idx[None, :]).astype(F32)
    dmask = row_dec[:, :, None] * causal[None]
    k_dec = jnp.exp(log_gamma[:, None] * (c - 1.0 - idx)) * (dk ** -0.5)
    q_dec = jnp.exp(log_gamma[:, None] * (idx + 1.0))
    chunk_dec = jnp.exp(log_gamma * c)
    half = dk // 2
    kdec_tile = jnp.tile(jnp.repeat(k_dec.T, half, axis=1), (ROW_TILE // c, 1))
    qdec_tile = jnp.broadcast_to(q_dec[:, :, None], (N_HEADS, c, dv))
    return dmask, kdec_tile, qdec_tile, chunk_dec


def _pool_bands(tm):
    lag = jnp.arange(tm)[:, None] - jnp.arange(tm)[None, :]
    return jnp.stack([((lag >= 0) & (lag < w)) for w in POOL_WINDOWS]).astype(BF16)


def kernel(x, c, ada_w, ada_b, norm_gain, ret_w_in, ret_gn_gain, ret_w_out, ffn_w_gate, ffn_w_up, ffn_w_down,
           pool_w, pool_b, pool_scale, moe_router, moe_w_gate, moe_w_up, moe_w_down, final_norm_gain):
    bsz, seq, d = x.shape
    assert bsz == 1 and ada_w.shape[0] == 2
    xt = x.reshape(seq, d)
    qk_total = d
    v_total = ret_w_out.shape[1]
    dk = qk_total // N_HEADS
    dv = v_total // N_HEADS

    mod = _modulation(c, ada_w, ada_b)
    def mods(i):
        parts = [mod[i, :, k * d:(k + 1) * d] for k in range(N_MOD)]
        sh1, sc1, g1, sh2, sc2, g2 = parts
        return sh1, 1.0 + sc1, g1, sh2, 1.0 + sc2, g2

    sh1, sc1p, g1, sh2, sc2p, g2 = mods(0)
    dmask, kdec_tile, qdec_tile, chunk_dec = _decay_tables(dk, dv)
    proj, (w_out_b, wg_b, wu_b, wd_b) = _ret_proj(
        xt, norm_gain[0, 0][None], sc1p, sh1, ret_w_in[0].astype(BF16), _rotary_tables(seq, dk), kdec_tile,
        qk_total, v_total, [ret_w_out[0], ffn_w_gate[0], ffn_w_up[0], ffn_w_down[0]])
    x1 = _retention(proj, chunk_dec, dmask, qdec_tile, ret_gn_gain[0][None], xt, g1, w_out_b, qk_total, v_total)
    x2 = _dense_ffn(x1, norm_gain[0, 1][None], sc2p, sh2, g2, wg_b, wu_b, wd_b)

    sh1, sc1p, g1, sh2, sc2p, g2 = mods(1)
    band = _pool_bands(POOL_SUB_ROWS)
    wr = jnp.pad(moe_router[0], ((0, 0), (0, LANES - N_EXPERTS)))
    wr_hi = wr.astype(BF16)
    wr_lo = (wr - wr_hi.astype(F32)).astype(BF16)
    tri = (jnp.arange(RANK_ROWS)[:, None] >= jnp.arange(RANK_ROWS)[None, :]).astype(BF16)
    x3, h4, gates, sel, rank, tot = _pool_router(
        x2, norm_gain[1, 0][None], sc1p, sh1, g1, band,
        pool_w[0].astype(BF16), pool_b[0].reshape(1, d), pool_scale[0][None],
        norm_gain[1, 1][None], sc2p, sh2, jnp.concatenate([wr_hi, wr_lo], axis=1), tri)
    pos, be, nv, beu, ng, clo, chi, win = _routing_tables(rank, tot, sel)
    xb, gs = _gather(beu, ng, clo, chi, h4, pos.T, gates[:, :N_EXPERTS].T)
    yb = _experts(be, nv, xb, gs, moe_w_gate[0], moe_w_up[0], moe_w_down[0])
    out = _combine(win, x3, pos, g2, final_norm_gain[None], yb)
    return out.reshape(bsz, seq, d)
```

```python
import functools

import jax
import jax.numpy as jnp
from jax import lax
from jax.experimental import pallas as pl
from jax.experimental.pallas import tpu as pltpu

F32 = jnp.float32
BF16 = jnp.bfloat16

EPS = 1e-6
N_HEADS = 4
RET_CHUNK = 256
ROPE_BASE = 10000.0
ROT_BLOCK = 128
POOL_WINDOWS = (2, 4, 8, 16)
POOL_HALO = 16
N_EXPERTS = 8
N_MOD = 6

VMEM_LIMIT_BYTES = 56 * 1024 * 1024

ROW_TILE = 1024
PROJ_COL_TILE = 3072
RET_ROWS = 512
POOL_ROWS = 1024
POOL_SUB_ROWS = 256
RANK_ROWS = 512
MOE_BLOCK = 2048
MOE_DOT_ROWS = 512
MOE_GATHER_UNIT = 1024
MOE_GATHER_ROWS = 256
MOE_FF_TILE = 512
GATHER_CHUNK = 512
COMBINE_TILE = 256
COMBINE_ROWS = 256
COMBINE_WIN = 128
COMBINE_WINDOWS = COMBINE_ROWS // COMBINE_WIN + 1
LANES = 128


def _silu(v):
    return v / (1.0 + jnp.exp(-v))


def _norm_mod(x, gain, scale1p, shift):
    ms = jnp.mean(x * x, axis=-1, keepdims=True)
    return (x * lax.rsqrt(ms + EPS)) * (gain * scale1p) + shift


def _params(*sem):
    return pltpu.CompilerParams(dimension_semantics=sem, vmem_limit_bytes=VMEM_LIMIT_BYTES)


def _resident(shape):
    nd = len(shape)
    return pl.BlockSpec(shape, lambda *_: (0,) * nd, pipeline_mode=pl.Buffered(1))


def _mod_kernel(c_ref, w_ref, b_ref, o_ref):
    ca = _silu(c_ref[...])
    o_ref[0] = jnp.sum(ca * w_ref[0], axis=0, keepdims=True) + b_ref[0]


def _modulation(c, ada_w, ada_b):
    depth, d, n = ada_w.shape
    tn = 1024
    return pl.pallas_call(
        _mod_kernel,
        grid=(depth, n // tn),
        in_specs=[
            pl.BlockSpec((d, 1), lambda i, j: (0, 0)),
            pl.BlockSpec((1, d, tn), lambda i, j: (i, 0, j)),
            pl.BlockSpec((1, 1, tn), lambda i, j: (i, 0, j)),
        ],
        out_specs=pl.BlockSpec((1, 1, tn), lambda i, j: (i, 0, j)),
        out_shape=jax.ShapeDtypeStruct((depth, 1, n), F32),
        compiler_params=_params("arbitrary", "arbitrary"),
        name="mod",
    )(c.reshape(d, 1), ada_w, ada_b.reshape(depth, 1, n))


def _proj_kernel(x_ref, gain_ref, sc_ref, sh_ref, w_ref, cosa_ref, sina_ref, cosb_ref, sinb_ref, kdec_ref, *refs,
                 n_qk_chunks, n_v_chunks, n_col_tiles, dk, n_cast):
    cast_in = refs[:n_cast]
    o_ref = refs[n_cast]
    cast_out = refs[n_cast + 1:2 * n_cast + 1]
    h_ref, cos_ref, sin_ref = refs[2 * n_cast + 1:]
    j = pl.program_id(1)

    @pl.when(j == 0)
    def _():
        h_ref[...] = _norm_mod(x_ref[...], gain_ref[...], sc_ref[...], sh_ref[...]).astype(BF16)
        for src, dst in zip(cast_in, cast_out):
            dst[...] = src[...].astype(BF16)
        cb = cosb_ref[...]
        sb = sinb_ref[...]
        for a in range(cosa_ref.shape[0]):
            ca = cosa_ref[a:a + 1, :]
            sa = sina_ref[a:a + 1, :]
            cos_ref[a * ROT_BLOCK:(a + 1) * ROT_BLOCK, :] = ca * cb - sa * sb
            sin_ref[a * ROT_BLOCK:(a + 1) * ROT_BLOCK, :] = sa * cb + ca * sb

    half = dk // 2
    n_chunks = w_ref.shape[1] // dk

    def chunk(c, kind, head):
        acc = jnp.dot(h_ref[...], w_ref[:, c * dk:(c + 1) * dk], preferred_element_type=F32)
        if kind == "v":
            o_ref[:, c * dk:(c + 1) * dk] = acc.astype(BF16)
        elif kind == "g":
            o_ref[:, c * dk:(c + 1) * dk] = _silu(acc).astype(BF16)
        else:
            cos = cos_ref[...]
            sin = sin_ref[...]
            t1 = acc[:, :half]
            t2 = acc[:, half:]
            o1 = t1 * cos - t2 * sin
            o2 = t1 * sin + t2 * cos
            if kind == "k":
                s = kdec_ref[:, head * half:(head + 1) * half]
                o1 = o1 * s
                o2 = o2 * s
            o_ref[:, c * dk:c * dk + half] = o1.astype(BF16)
            o_ref[:, c * dk + half:(c + 1) * dk] = o2.astype(BF16)

    def kind_of(gc):
        if gc < n_qk_chunks:
            return "q", gc
        if gc < 2 * n_qk_chunks:
            return "k", gc - n_qk_chunks
        if gc < 2 * n_qk_chunks + n_v_chunks:
            return "v", 0
        return "g", 0

    for jj in range(n_col_tiles):
        @pl.when(j == jj)
        def _():
            for c in range(n_chunks):
                chunk(c, *kind_of(jj * n_chunks + c))


def _ret_proj(x, gain, sc1p, sh, w_in, rot, kdec, qk_total, v_total, to_cast):
    t, d = x.shape
    n = w_in.shape[1]
    tm, tn = ROW_TILE, PROJ_COL_TILE
    n_row_tiles = t // tm
    dk = qk_total // N_HEADS
    half = dk // 2
    cosa, sina, cosb, sinb = rot
    kern = functools.partial(_proj_kernel, n_qk_chunks=qk_total // dk, n_v_chunks=v_total // dk,
                             n_col_tiles=n // tn, dk=dk, n_cast=len(to_cast))
    vec = pl.BlockSpec((1, d), lambda i, j: (0, 0))
    rot_a = pl.BlockSpec((tm // ROT_BLOCK, half), lambda i, j: (i, 0))
    rot_b = pl.BlockSpec((ROT_BLOCK, half), lambda i, j: (0, 0))
    cast_specs = [pl.BlockSpec((w.shape[0] // n_row_tiles, w.shape[1]), lambda i, j: (i, 0)) for w in to_cast]
    outs = pl.pallas_call(
        kern,
        grid=(n_row_tiles, n // tn),
        in_specs=[
            pl.BlockSpec((tm, d), lambda i, j: (i, 0)),
            vec, vec, vec,
            pl.BlockSpec((d, tn), lambda i, j: (0, j)),
            rot_a, rot_a, rot_b, rot_b,
            pl.BlockSpec((tm, N_HEADS * half), lambda i, j: (0, 0)),
        ] + cast_specs,
        out_specs=[pl.BlockSpec((tm, tn), lambda i, j: (i, j))] + cast_specs,
        out_shape=[jax.ShapeDtypeStruct((t, n), BF16)] + [jax.ShapeDtypeStruct(w.shape, BF16) for w in to_cast],
        scratch_shapes=[pltpu.VMEM((tm, d), BF16), pltpu.VMEM((tm, half), F32), pltpu.VMEM((tm, half), F32)],
        compiler_params=_params("arbitrary", "arbitrary"),
        name="proj",
    )(x, gain, sc1p, sh, w_in, cosa, sina, cosb, sinb, kdec, *to_cast)
    return outs[0], outs[1:]


def _ret_kernel(cdec_ref, q_ref, k_ref, v_ref, sg_ref, dmask_ref, qdec_ref, gn_ref, x_ref, g_ref, wo_ref, o_ref,
                state_ref, *, dk, dv):
    @pl.when(pl.program_id(0) == 0)
    def _():
        state_ref[...] = jnp.zeros_like(state_ref)

    rows_per_step = q_ref.shape[0]
    for n in range(rows_per_step // RET_CHUNK):
        rows = slice(n * RET_CHUNK, (n + 1) * RET_CHUNK)
        mixed = None
        for h in range(N_HEADS):
            qcols = slice(h * dk, (h + 1) * dk)
            vcols = slice(h * dv, (h + 1) * dv)
            q = q_ref[rows, qcols]
            ks = k_ref[rows, qcols]
            v = v_ref[rows, vcols]
            s = lax.dot_general(q, ks, (((1,), (1,)), ((), ())), preferred_element_type=F32)
            a = (s * dmask_ref[h]).astype(BF16)
            intra = jnp.dot(a, v, preferred_element_type=F32)
            st = state_ref[h]
            cross = jnp.dot(q, st.astype(BF16), preferred_element_type=F32)
            o = intra + cross * qdec_ref[h]
            kv = lax.dot_general(ks, v, (((0,), (0,)), ((), ())), preferred_element_type=F32)
            state_ref[h] = st * cdec_ref[h] + kv
            mu = jnp.mean(o, axis=-1, keepdims=True)
            dlt = o - mu
            var = jnp.mean(dlt * dlt, axis=-1, keepdims=True)
            on = dlt * lax.rsqrt(var + EPS)
            y = on * gn_ref[:, vcols] * sg_ref[rows, vcols].astype(F32)
            part = jnp.dot(y.astype(BF16), wo_ref[vcols, :], preferred_element_type=F32)
            mixed = part if mixed is None else mixed + part
        o_ref[rows, :] = x_ref[rows, :] + g_ref[...] * mixed


def _retention(proj, cdec, dmask, qdec, gn_gain, x, g1, w_out, qk_total, v_total):
    t, d = x.shape
    r = RET_ROWS
    dk = qk_total // N_HEADS
    dv = v_total // N_HEADS
    assert v_total == 2 * qk_total
    kern = functools.partial(_ret_kernel, dk=dk, dv=dv)
    return pl.pallas_call(
        kern,
        grid=(t // r,),
        in_specs=[
            pl.BlockSpec(memory_space=pltpu.SMEM),
            pl.BlockSpec((r, qk_total), lambda i: (i, 0)),
            pl.BlockSpec((r, qk_total), lambda i: (i, 1)),
            pl.BlockSpec((r, v_total), lambda i: (i, 1)),
            pl.BlockSpec((r, v_total), lambda i: (i, 2)),
            pl.BlockSpec((N_HEADS, RET_CHUNK, RET_CHUNK), lambda i: (0, 0, 0)),
            pl.BlockSpec((N_HEADS, RET_CHUNK, dv), lambda i: (0, 0, 0)),
            pl.BlockSpec((1, v_total), lambda i: (0, 0)),
            pl.BlockSpec((r, d), lambda i: (i, 0)),
            pl.BlockSpec((1, d), lambda i: (0, 0)),
            _resident((v_total, d)),
        ],
        out_specs=pl.BlockSpec((r, d), lambda i: (i, 0)),
        out_shape=jax.ShapeDtypeStruct((t, d), F32),
        scratch_shapes=[pltpu.VMEM((N_HEADS, dk, dv), F32)],
        compiler_params=_params("arbitrary"),
        name="ret",
    )(cdec, proj, proj, proj, proj, dmask, qdec, gn_gain, x, g1, w_out)


def _ffn_kernel(x_ref, gain_ref, sc_ref, sh_ref, g_ref, wg_ref, wu_ref, wd_ref, o_ref, *, ff_tile):
    x = x_ref[...]
    h = _norm_mod(x, gain_ref[...], sc_ref[...], sh_ref[...]).astype(BF16)
    d_ff = wg_ref.shape[1]
    acc = jnp.zeros(x.shape, F32)
    for lo in range(0, d_ff, ff_tile):
        hi = min(lo + ff_tile, d_ff)
        a = jnp.dot(h, wg_ref[:, lo:hi], preferred_element_type=F32)
        b = jnp.dot(h, wu_ref[:, lo:hi], preferred_element_type=F32)
        act = (_silu(a) * b).astype(BF16)
        acc = acc + jnp.dot(act, wd_ref[lo:hi, :], preferred_element_type=F32)
    o_ref[...] = x + g_ref[...] * acc


def _dense_ffn(x, gain, sc1p, sh, g2, wg, wu, wd):
    t, d = x.shape
    d_ff = wg.shape[1]
    tm = ROW_TILE
    vec = pl.BlockSpec((1, d), lambda i: (0, 0))
    return pl.pallas_call(
        functools.partial(_ffn_kernel, ff_tile=512),
        grid=(t // tm,),
        in_specs=[
            pl.BlockSpec((tm, d), lambda i: (i, 0)),
            vec, vec, vec, vec,
            _resident((d, d_ff)), _resident((d, d_ff)), _resident((d_ff, d)),
        ],
        out_specs=pl.BlockSpec((tm, d), lambda i: (i, 0)),
        out_shape=jax.ShapeDtypeStruct((t, d), F32),
        compiler_params=_params("arbitrary"),
        name="ffn",
    )(x, gain, sc1p, sh, g2, wg, wu, wd)


def _pool_kernel(x_ref, gain_ref, sc_ref, sh_ref, g_ref, band_ref, wp_ref, bp_ref, ps_ref, o_ref,
                 hbuf_ref):
    i = pl.program_id(0)
    tm = x_ref.shape[0]
    gw = wp_ref.shape[1]

    @pl.when(i == 0)
    def _():
        hbuf_ref[0:POOL_HALO, :] = jnp.zeros((POOL_HALO, hbuf_ref.shape[1]), F32)

    @pl.when(i > 0)
    def _():
        hbuf_ref[0:POOL_HALO, :] = hbuf_ref[tm:tm + POOL_HALO, :]

    hbuf_ref[POOL_HALO:, :] = _norm_mod(x_ref[...], gain_ref[...], sc_ref[...], sh_ref[...])

    def split(v):
        hi = v.astype(BF16)
        return hi, (v - hi.astype(F32)).astype(BF16)

    sub = band_ref.shape[1]
    for s in range(tm // sub):
        rows = slice(s * sub, (s + 1) * sub)
        lo = POOL_HALO + s * sub
        x = x_ref[rows, :]
        h = hbuf_ref[lo:lo + sub, :]
        h_hi, h_lo = split(h)
        t1 = (lax.broadcasted_iota(jnp.int32, (sub, 1), 0) + (i * tm + s * sub + 1)).astype(F32)
        first_rows = lax.broadcasted_iota(jnp.int32, (POOL_HALO, 1), 0)
        for g, w in enumerate(POOL_WINDOWS):
            cols = slice(g * gw, (g + 1) * gw)
            band = band_ref[g]
            win = (jnp.dot(band, h_hi[:, cols], preferred_element_type=F32)
                   + jnp.dot(band, h_lo[:, cols], preferred_element_type=F32))
            head = jnp.zeros((POOL_HALO, gw), F32)
            for k in range(1, w):
                prev_row = hbuf_ref[lo - k:lo - k + 1, cols]
                head = head + jnp.where(first_rows < w - k, prev_row, 0.0)
            win = jnp.concatenate([win[:POOL_HALO] + head, win[POOL_HALO:]], axis=0)
            pooled = win * (1.0 / jnp.minimum(t1, float(w))) - h[:, cols]
            y = jnp.dot(pooled.astype(BF16), wp_ref[g], preferred_element_type=F32) + bp_ref[:, cols]
            o_ref[rows, cols] = x[:, cols] + (g_ref[:, cols] * ps_ref[:, cols]) * y


def _route(x, gain_ref, sc_ref, sh_ref, w_ref, tri_ref, h_ref, gates_ref, sel_ref, rank_ref, tot_ref, carry_ref):
    h = _norm_mod(x, gain_ref[...], sc_ref[...], sh_ref[...])
    h_hi = h.astype(BF16)
    h_ref[...] = h_hi
    h_lo = (h - h_hi.astype(F32)).astype(BF16)
    w = w_ref[...]
    p_hi = jnp.dot(h_hi, w, preferred_element_type=F32)
    p_lo = jnp.dot(h_lo, w, preferred_element_type=F32)
    logits = p_hi[:, :LANES] + p_hi[:, LANES:] + p_lo[:, :LANES] + p_lo[:, LANES:]
    lane = lax.broadcasted_iota(jnp.int32, logits.shape, 1).astype(F32)
    neg = jnp.float32(-jnp.inf)
    lg = jnp.where(lane < float(N_EXPERTS), logits, neg)
    m0 = jnp.max(lg, axis=-1, keepdims=True)
    i0 = jnp.min(jnp.where(lg == m0, lane, float(LANES)), axis=-1, keepdims=True)
    lg1 = jnp.where(lane == i0, neg, lg)
    m1 = jnp.max(lg1, axis=-1, keepdims=True)
    i1 = jnp.min(jnp.where(lg1 == m1, lane, float(LANES)), axis=-1, keepdims=True)
    e1 = jnp.exp(m1 - m0)
    den = 1.0 + e1
    is0 = lane == i0
    is1 = lane == i1
    gates_ref[...] = jnp.where(is0, 1.0 / den, jnp.where(is1, e1 / den, 0.0))
    sel = jnp.where(is0 | is1, 1.0, 0.0).astype(BF16)
    sel_ref[...] = sel

    @pl.when(pl.program_id(0) == 0)
    def _():
        carry_ref[...] = jnp.zeros_like(carry_ref)

    carry = carry_ref[...]
    for b in range(sel.shape[0] // RANK_ROWS):
        rows = slice(b * RANK_ROWS, (b + 1) * RANK_ROWS)
        s = sel[rows, :]
        cum = jnp.dot(tri_ref[...], s, preferred_element_type=F32)
        rank_ref[rows, :] = cum - s.astype(F32) + carry
        carry = carry + cum[RANK_ROWS - 1:, :]
        tot_ref[b] = carry
    carry_ref[...] = carry


def _pool_router_kernel(x_ref, gain_ref, sc_ref, sh_ref, g_ref, band_ref, wp_ref, bp_ref, ps_ref,
                        gain2_ref, sc2_ref, sh2_ref, wr_ref, tri_ref,
                        o_ref, h_ref, gates_ref, sel_ref, rank_ref, tot_ref, hbuf_ref, carry_ref):
    _pool_kernel(x_ref, gain_ref, sc_ref, sh_ref, g_ref, band_ref, wp_ref, bp_ref, ps_ref, o_ref,
                 hbuf_ref)
    _route(o_ref[...], gain2_ref, sc2_ref, sh2_ref, wr_ref, tri_ref, h_ref, gates_ref, sel_ref, rank_ref, tot_ref,
           carry_ref)


def _pool_router(x, gain, sc1p, sh, g1, band, wp, bp, ps, gain2, sc2p, sh2, w_cat, tri):
    t, d = x.shape
    tm = POOL_ROWS
    n_sub = tm // RANK_ROWS
    vec = pl.BlockSpec((1, d), lambda i: (0, 0))
    rows = pl.BlockSpec((tm, d), lambda i: (i, 0))
    lanes = pl.BlockSpec((tm, LANES), lambda i: (i, 0))
    return pl.pallas_call(
        _pool_router_kernel,
        grid=(t // tm,),
        in_specs=[
            rows, vec, vec, vec, vec,
            pl.BlockSpec(band.shape, lambda i: (0, 0, 0)),
            pl.BlockSpec(wp.shape, lambda i: (0, 0, 0)),
            vec, vec,
            vec, vec, vec,
            pl.BlockSpec((d, 2 * LANES), lambda i: (0, 0)),
            pl.BlockSpec((RANK_ROWS, RANK_ROWS), lambda i: (0, 0)),
        ],
        out_specs=[rows, rows, lanes, lanes, lanes, pl.BlockSpec((n_sub, 1, LANES), lambda i: (i, 0, 0))],
        out_shape=[
            jax.ShapeDtypeStruct((t, d), F32),
            jax.ShapeDtypeStruct((t, d), BF16),
            jax.ShapeDtypeStruct((t, LANES), F32),
            jax.ShapeDtypeStruct((t, LANES), BF16),
            jax.ShapeDtypeStruct((t, LANES), F32),
            jax.ShapeDtypeStruct((t // RANK_ROWS, 1, LANES), F32),
        ],
        scratch_shapes=[pltpu.VMEM((POOL_HALO + tm, d), F32), pltpu.VMEM((1, LANES), F32)],
        compiler_params=_params("arbitrary"),
        name="pool_router",
    )(x, gain, sc1p, sh, g1, band, wp, bp, ps, gain2, sc2p, sh2, w_cat, tri)


def _gather_kernel(be_ref, ng_ref, clo_ref, chi_ref, h_ref, post_ref, gatest_ref, xb_ref, gs_ref, acc_ref):
    u = pl.program_id(0)
    e = be_ref[u]
    unit_rows = xb_ref.shape[0]
    n_groups = unit_rows // MOE_GATHER_ROWS
    last_chunk = h_ref.shape[0] // GATHER_CHUNK - 1
    g0 = u * n_groups

    @pl.when(ng_ref[g0] > 0)
    def _():
        acc_ref[...] = jnp.zeros_like(acc_ref)
        trips = chi_ref[g0] - clo_ref[g0]
        for q in range(1, n_groups):
            trips = jnp.maximum(trips, chi_ref[g0 + q] - clo_ref[g0 + q])

        def body(i, gsums):
            out = []
            for q in range(n_groups):
                rows = slice(q * MOE_GATHER_ROWS, (q + 1) * MOE_GATHER_ROWS)
                c = clo_ref[g0 + q] + i
                live = c < chi_ref[g0 + q]
                off = pl.multiple_of(jnp.minimum(c, last_chunk) * GATHER_CHUNK, GATHER_CHUNK)
                p = post_ref[pl.ds(e, 1), pl.ds(off, GATHER_CHUNK)]
                p = jnp.where(live, p, -2)
                row_pos = (lax.broadcasted_iota(jnp.int32, (MOE_GATHER_ROWS, GATHER_CHUNK), 0)
                           + (u * unit_rows + q * MOE_GATHER_ROWS))
                match = row_pos == p
                onehot = jnp.where(match, 1.0, 0.0).astype(BF16)
                acc_ref[rows, :] += jnp.dot(onehot, h_ref[pl.ds(off, GATHER_CHUNK), :],
                                            preferred_element_type=F32)
                gate = gatest_ref[pl.ds(e, 1), pl.ds(off, GATHER_CHUNK)]
                out.append(gsums[q] + jnp.sum(jnp.where(match, gate, 0.0), axis=1, keepdims=True))
            return tuple(out)

        zero = jnp.zeros((MOE_GATHER_ROWS, 1), F32)
        gsums = lax.fori_loop(0, trips, body, (zero,) * n_groups)
        xb_ref[...] = acc_ref[...].astype(BF16)
        for q in range(n_groups):
            gs_ref[q * MOE_GATHER_ROWS:(q + 1) * MOE_GATHER_ROWS, :] = gsums[q]

    @pl.when(ng_ref[g0] == 0)
    def _():
        xb_ref[...] = jnp.zeros_like(xb_ref)
        gs_ref[...] = jnp.zeros_like(gs_ref)


def _gather(beu, ng, clo, chi, h, post, gatest):
    t, d = h.shape
    n_units = beu.shape[0]
    grid_spec = pltpu.PrefetchScalarGridSpec(
        num_scalar_prefetch=4,
        grid=(n_units,),
        in_specs=[
            pl.BlockSpec((t, d), lambda u, *_: (0, 0), pipeline_mode=pl.Buffered(1)),
            pl.BlockSpec(post.shape, lambda u, *_: (0, 0), pipeline_mode=pl.Buffered(1)),
            pl.BlockSpec(gatest.shape, lambda u, *_: (0, 0), pipeline_mode=pl.Buffered(1)),
        ],
        out_specs=[
            pl.BlockSpec((MOE_GATHER_UNIT, d), lambda u, *_: (u, 0)),
            pl.BlockSpec((MOE_GATHER_UNIT, 1), lambda u, *_: (u, 0)),
        ],
        scratch_shapes=[pltpu.VMEM((MOE_GATHER_UNIT, d), F32)],
    )
    return pl.pallas_call(
        _gather_kernel,
        grid_spec=grid_spec,
        out_shape=[
            jax.ShapeDtypeStruct((n_units * MOE_GATHER_UNIT, d), BF16),
            jax.ShapeDtypeStruct((n_units * MOE_GATHER_UNIT, 1), F32),
        ],
        compiler_params=_params("arbitrary"),
        name="gather",
    )(beu, ng, clo, chi, h, post, gatest)


def _experts_kernel(be_ref, nv_ref, xb_ref, gs_ref, wg_ref, wu_ref, wd_ref, yb_ref,
                    acc_ref, wgb_ref, wub_ref, wdb_ref):
    b = pl.program_id(0)
    f = pl.program_id(1)
    nf = pl.num_programs(1)
    nv = nv_ref[b]
    block_rows = xb_ref.shape[0]

    @pl.when((b == 0) & (f == 0))
    def _():
        acc_ref[...] = jnp.zeros_like(acc_ref)

    def swiglu_rows(lo, n, cast):
        rows = slice(lo, lo + n)
        if cast:
            wgb_ref[...] = wg_ref[0].astype(BF16)
            wub_ref[...] = wu_ref[0].astype(BF16)
            wdb_ref[...] = wd_ref[0].astype(BF16)
        x = xb_ref[rows, :]
        a = jnp.dot(x, wgb_ref[...], preferred_element_type=F32)
        u = jnp.dot(x, wub_ref[...], preferred_element_type=F32)
        act = (_silu(a) * u).astype(BF16)
        prev = jnp.where(f > 0, acc_ref[rows, :], 0.0)
        new = prev + jnp.dot(act, wdb_ref[...], preferred_element_type=F32)
        acc_ref[rows, :] = new
        yb_ref[rows, :] = (new * gs_ref[rows, :]).astype(BF16)

    pair = 2 * MOE_DOT_ROWS
    half_group = MOE_DOT_ROWS // 2
    for p in range(block_rows // pair):
        lo = p * pair
        full = nv >= lo + pair

        @pl.when(full)
        def _():
            swiglu_rows(lo, pair, p == 0)

        for hh in range(2):
            glo = lo + hh * MOE_DOT_ROWS
            used = jnp.logical_not(full) & (nv > glo)
            tall = nv > glo + half_group
            first = p == 0 and hh == 0

            @pl.when(used & tall)
            def _():
                swiglu_rows(glo, MOE_DOT_ROWS, first)

            @pl.when(used & jnp.logical_not(tall))
            def _():
                swiglu_rows(glo, half_group, first)

    @pl.when(f == nf - 1)
    def _():
        for q in range(block_rows // half_group):
            @pl.when(nv <= q * half_group)
            def _():
                yb_ref[q * half_group:(q + 1) * half_group, :] = jnp.zeros((half_group, yb_ref.shape[1]), BF16)


def _experts(be, nv, xb, gs, wg, wu, wd):
    d = xb.shape[1]
    n_blocks = be.shape[0]
    d_ff = wg.shape[2]
    tf = MOE_FF_TILE
    nf = d_ff // tf
    bsz = MOE_BLOCK

    def ff_idx(f, nvr, b):
        return jnp.where(nvr[b] > 0, f, nf - 1)

    grid_spec = pltpu.PrefetchScalarGridSpec(
        num_scalar_prefetch=2,
        grid=(n_blocks, nf),
        in_specs=[
            pl.BlockSpec((bsz, d), lambda b, f, *_: (b, 0)),
            pl.BlockSpec((bsz, 1), lambda b, f, *_: (b, 0)),
            pl.BlockSpec((1, d, tf), lambda b, f, be_r, nv_r: (be_r[b], 0, ff_idx(f, nv_r, b))),
            pl.BlockSpec((1, d, tf), lambda b, f, be_r, nv_r: (be_r[b], 0, ff_idx(f, nv_r, b))),
            pl.BlockSpec((1, tf, d), lambda b, f, be_r, nv_r: (be_r[b], ff_idx(f, nv_r, b), 0)),
        ],
        out_specs=pl.BlockSpec((bsz, d), lambda b, f, *_: (b, 0)),
        scratch_shapes=[
            pltpu.VMEM((bsz, d), F32),
            pltpu.VMEM((d, tf), BF16), pltpu.VMEM((d, tf), BF16), pltpu.VMEM((tf, d), BF16),
        ],
    )
    return pl.pallas_call(
        _experts_kernel,
        grid_spec=grid_spec,
        out_shape=jax.ShapeDtypeStruct((n_blocks * bsz, d), BF16),
        compiler_params=_params("arbitrary", "arbitrary"),
        name="experts",
    )(be, nv, xb, gs, wg, wu, wd)


def _combine_kernel(win_ref, x_ref, pos_ref, g_ref, fg_ref, *refs):
    yb_refs, o_ref = refs[:-1], refs[-1]
    i = pl.program_id(0)
    n_sub = x_ref.shape[0] // COMBINE_ROWS
    lane = lax.broadcasted_iota(jnp.int32, (COMBINE_ROWS, COMBINE_WINDOWS * COMBINE_WIN), 1)
    for s in range(n_sub):
        rows = slice(s * COMBINE_ROWS, (s + 1) * COMBINE_ROWS)
        onehots, windows = [], []
        for e in range(N_EXPERTS):
            k = (i * n_sub + s) * N_EXPERTS + e
            base = win_ref[k] * COMBINE_WIN
            rel = pos_ref[rows, e:e + 1] - base
            onehots.append(jnp.where(lane == rel, 1.0, 0.0).astype(BF16))
            w0 = COMBINE_WINDOWS * (s * N_EXPERTS + e)
            windows += [yb_refs[w0 + j][...] for j in range(COMBINE_WINDOWS)]
        acc = jnp.dot(jnp.concatenate(onehots, axis=1), jnp.concatenate(windows, axis=0),
                      preferred_element_type=F32)
        xo = x_ref[rows, :] + g_ref[...] * acc
        ms = jnp.mean(xo * xo, axis=-1, keepdims=True)
        o_ref[rows, :] = xo * lax.rsqrt(ms + EPS) * fg_ref[...]


def _combine(win, x, pos, g2, fgain, yb):
    t, d = x.shape
    tm = COMBINE_TILE
    n_sub = tm // COMBINE_ROWS
    n_win = yb.shape[0] // COMBINE_WIN
    vec = pl.BlockSpec((1, d), lambda i, w: (0, 0))

    def window(s, e, j):
        return lambda i, w: (jnp.minimum(w[(i * n_sub + s) * N_EXPERTS + e] + j, n_win - 1), 0)

    yb_specs = []
    for s in range(n_sub):
        for e in range(N_EXPERTS):
            for j in range(COMBINE_WINDOWS):
                yb_specs.append(pl.BlockSpec((COMBINE_WIN, d), window(s, e, j)))
    grid_spec = pltpu.PrefetchScalarGridSpec(
        num_scalar_prefetch=1,
        grid=(t // tm,),
        in_specs=[
            pl.BlockSpec((tm, d), lambda i, w: (i, 0)),
            pl.BlockSpec((tm, N_EXPERTS), lambda i, w: (i, 0)),
            vec, vec,
        ] + yb_specs,
        out_specs=pl.BlockSpec((tm, d), lambda i, w: (i, 0)),
    )
    return pl.pallas_call(
        _combine_kernel,
        grid_spec=grid_spec,
        out_shape=jax.ShapeDtypeStruct((t, d), F32),
        compiler_params=_params("arbitrary"),
        name="combine",
    )(win, x, pos, g2, fgain, *([yb] * len(yb_specs)))


def _routing_tables(rank, tot, sel):
    t = rank.shape[0]
    bsz = MOE_BLOCK
    n_blocks = (2 * t) // bsz + N_EXPERTS
    cum = tot[:, 0, :N_EXPERTS].astype(jnp.int32)
    counts = cum[-1]
    padded = ((counts + bsz - 1) // bsz) * bsz
    pad_end = jnp.cumsum(padded)
    start_pad = pad_end - padded
    selected = sel[:, :N_EXPERTS] > 0
    pos = jnp.where(selected, start_pad[None, :] + rank[:, :N_EXPERTS].astype(jnp.int32), -1)

    bstart = jnp.arange(n_blocks, dtype=jnp.int32) * bsz
    be = jnp.minimum(jnp.sum(bstart[:, None] >= pad_end[None, :], axis=1), N_EXPERTS - 1).astype(jnp.int32)
    r0 = bstart - start_pad[be]
    nv = jnp.clip(counts[be] - r0, 0, bsz).astype(jnp.int32)
    n_gather = bsz // MOE_GATHER_ROWS
    goff = jnp.arange(n_gather, dtype=jnp.int32) * MOE_GATHER_ROWS
    r0g = (r0[:, None] + goff[None, :]).reshape(-1)
    ng = jnp.clip(nv[:, None] - goff[None, :], 0, MOE_GATHER_ROWS).reshape(-1)
    beg = jnp.repeat(be, n_gather)
    chunk_end = cum[:, beg].T
    chunk_start = jnp.concatenate([jnp.zeros((1, N_EXPERTS), jnp.int32), cum[:-1]], axis=0)[:, beg].T
    clo = jnp.sum(chunk_end <= r0g[:, None], axis=1).astype(jnp.int32)
    chi = jnp.sum(chunk_start < (r0g + ng)[:, None], axis=1).astype(jnp.int32)
    beu = jnp.repeat(be, bsz // MOE_GATHER_UNIT)

    big = jnp.int32(2 ** 30)
    first = jnp.min(jnp.where(pos >= 0, pos, big).reshape(t // COMBINE_ROWS, COMBINE_ROWS, N_EXPERTS), axis=1)
    win = jnp.where(first == big, 0, first // COMBINE_WIN).astype(jnp.int32).reshape(-1)
    return pos, be, nv, beu, ng, clo, chi, win


def _rotary_tables(seq, dk):
    half = dk // 2
    inv = ROPE_BASE ** (-jnp.arange(half, dtype=F32) / half)
    ang_a = (jnp.arange(seq // ROT_BLOCK) * ROT_BLOCK).astype(F32)[:, None] * inv[None, :]
    ang_b = jnp.arange(ROT_BLOCK).astype(F32)[:, None] * inv[None, :]
    return jnp.cos(ang_a), jnp.sin(ang_a), jnp.cos(ang_b), jnp.sin(ang_b)


def _decay_tables(dk, dv):
    c = RET_CHUNK
    log_gamma = jnp.log1p(-jnp.exp2(-5.0 - jnp.arange(N_HEADS, dtype=F32)))
    idx = jnp.arange(c, dtype=F32)
    row_dec = jnp.exp(log_gamma[:, None] * (idx + 1.0 - c))
    causal = (idx[:, None] >= idx[None, :]).astype(F32)
    dmask = row_dec[:, :, None] * causal[None]
    k_dec = jnp.exp(log_gamma[:, None] * (c - 1.0 - idx)) * (dk ** -0.5)
    q_dec = jnp.exp(log_gamma[:, None] * (idx + 1.0))
    chunk_dec = jnp.exp(log_gamma * c)
    half = dk // 2
    kdec_tile = jnp.tile(jnp.repeat(k_dec.T, half, axis=1), (ROW_TILE // c, 1))
    qdec_tile = jnp.broadcast_to(q_dec[:, :, None], (N_HEADS, c, dv))
    return dmask, kdec_tile, qdec_tile, chunk_dec


def _pool_bands(tm):
    lag = jnp.arange(tm)[:, None] - jnp.arange(tm)[None, :]
    return jnp.stack([((lag >= 0) & (lag < w)) for w in POOL_WINDOWS]).astype(BF16)


def kernel(x, c, ada_w, ada_b, norm_gain, ret_w_in, ret_gn_gain, ret_w_out, ffn_w_gate, ffn_w_up, ffn_w_down,
           pool_w, pool_b, pool_scale, moe_router, moe_w_gate, moe_w_up, moe_w_down, final_norm_gain):
    bsz, seq, d = x.shape
    assert bsz == 1 and ada_w.shape[0] == 2
    xt = x.reshape(seq, d)
    qk_total = d
    v_total = ret_w_out.shape[1]
    dk = qk_total // N_HEADS
    dv = v_total // N_HEADS

    mod = _modulation(c, ada_w, ada_b)
    def mods(i):
        parts = [mod[i, :, k * d:(k + 1) * d] for k in range(N_MOD)]
        sh1, sc1, g1, sh2, sc2, g2 = parts
        return sh1, 1.0 + sc1, g1, sh2, 1.0 + sc2, g2

    sh1, sc1p, g1, sh2, sc2p, g2 = mods(0)
    dmask, kdec_tile, qdec_tile, chunk_dec = _decay_tables(dk, dv)
    proj, (w_out_b, wg_b, wu_b, wd_b) = _ret_proj(
        xt, norm_gain[0, 0][None], sc1p, sh1, ret_w_in[0].astype(BF16), _rotary_tables(seq, dk), kdec_tile,
        qk_total, v_total, [ret_w_out[0], ffn_w_gate[0], ffn_w_up[0], ffn_w_down[0]])
    x1 = _retention(proj, chunk_dec, dmask, qdec_tile, ret_gn_gain[0][None], xt, g1, w_out_b, qk_total, v_total)
    x2 = _dense_ffn(x1, norm_gain[0, 1][None], sc2p, sh2, g2, wg_b, wu_b, wd_b)

    sh1, sc1p, g1, sh2, sc2p, g2 = mods(1)
    band = _pool_bands(POOL_SUB_ROWS)
    wr = jnp.pad(moe_router[0], ((0, 0), (0, LANES - N_EXPERTS)))
    wr_hi = wr.astype(BF16)
    wr_lo = (wr - wr_hi.astype(F32)).astype(BF16)
    tri = (jnp.arange(RANK_ROWS)[:, None] >= jnp.arange(RANK_ROWS)[None, :]).astype(BF16)
    x3, h4, gates, sel, rank, tot = _pool_router(
        x2, norm_gain[1, 0][None], sc1p, sh1, g1, band,
        pool_w[0].astype(BF16), pool_b[0].reshape(1, d), pool_scale[0][None],
        norm_gain[1, 1][None], sc2p, sh2, jnp.concatenate([wr_hi, wr_lo], axis=1), tri)
    pos, be, nv, beu, ng, clo, chi, win = _routing_tables(rank, tot, sel)
    xb, gs = _gather(beu, ng, clo, chi, h4, pos.T, gates[:, :N_EXPERTS].T)
    yb = _experts(be, nv, xb, gs, moe_w_gate[0], moe_w_up[0], moe_w_down[0])
    out = _combine(win, x3, pos, g2, final_norm_gain[None], yb)
    return out.reshape(bsz, seq, d)
```

```python
import functools

import jax
import jax.numpy as jnp
from jax import lax
from jax.experimental import pallas as pl
from jax.experimental.pallas import tpu as pltpu

F32 = jnp.float32
BF16 = jnp.bfloat16

EPS = 1e-6
N_HEADS = 4
RET_CHUNK = 256
ROPE_BASE = 10000.0
ROT_BLOCK = 128
POOL_WINDOWS = (2, 4, 8, 16)
POOL_HALO = 16
N_EXPERTS = 8
N_MOD = 6

VMEM_LIMIT_BYTES = 56 * 1024 * 1024

ROW_TILE = 1024
PROJ_COL_TILE = 3072
RET_ROWS = 512
POOL_ROWS = 1024
POOL_SUB_ROWS = 256
RANK_ROWS = 512
MOE_BLOCK = 2048
MOE_DOT_ROWS = 512
MOE_GATHER_UNIT = 1024
MOE_GATHER_ROWS = 256
MOE_FF_TILE = 512
GATHER_CHUNK = 512
COMBINE_TILE = 256
COMBINE_ROWS = 256
COMBINE_WIN = 128
COMBINE_WINDOWS = COMBINE_ROWS // COMBINE_WIN + 1
LANES = 128


def _silu(v):
    return v / (1.0 + jnp.exp(-v))


def _norm_mod(x, gain, scale1p, shift):
    ms = jnp.mean(x * x, axis=-1, keepdims=True)
    return (x * lax.rsqrt(ms + EPS)) * (gain * scale1p) + shift


def _params(*sem):
    return pltpu.CompilerParams(dimension_semantics=sem, vmem_limit_bytes=VMEM_LIMIT_BYTES)


def _resident(shape):
    nd = len(shape)
    return pl.BlockSpec(shape, lambda *_: (0,) * nd, pipeline_mode=pl.Buffered(1))


def _mod_kernel(c_ref, w_ref, b_ref, o_ref):
    ca = _silu(c_ref[...])
    o_ref[0] = jnp.sum(ca * w_ref[0], axis=0, keepdims=True) + b_ref[0]


def _modulation(c, ada_w, ada_b):
    depth, d, n = ada_w.shape
    tn = 1024
    return pl.pallas_call(
        _mod_kernel,
        grid=(depth, n // tn),
        in_specs=[
            pl.BlockSpec((d, 1), lambda i, j: (0, 0)),
            pl.BlockSpec((1, d, tn), lambda i, j: (i, 0, j)),
            pl.BlockSpec((1, 1, tn), lambda i, j: (i, 0, j)),
        ],
        out_specs=pl.BlockSpec((1, 1, tn), lambda i, j: (i, 0, j)),
        out_shape=jax.ShapeDtypeStruct((depth, 1, n), F32),
        compiler_params=_params("arbitrary", "arbitrary"),
        name="mod",
    )(c.reshape(d, 1), ada_w, ada_b.reshape(depth, 1, n))


def _proj_kernel(x_ref, gain_ref, sc_ref, sh_ref, w_ref, cosa_ref, sina_ref, cosb_ref, sinb_ref, kdec_ref, *refs,
                 n_qk_chunks, n_v_chunks, n_col_tiles, dk, n_cast):
    cast_in = refs[:n_cast]
    o_ref = refs[n_cast]
    cast_out = refs[n_cast + 1:2 * n_cast + 1]
    h_ref, cos_ref, sin_ref = refs[2 * n_cast + 1:]
    j = pl.program_id(1)

    @pl.when(j == 0)
    def _():
        h_ref[...] = _norm_mod(x_ref[...], gain_ref[...], sc_ref[...], sh_ref[...]).astype(BF16)
        for src, dst in zip(cast_in, cast_out):
            dst[...] = src[...].astype(BF16)
        cb = cosb_ref[...]
        sb = sinb_ref[...]
        for a in range(cosa_ref.shape[0]):
            ca = cosa_ref[a:a + 1, :]
            sa = sina_ref[a:a + 1, :]
            cos_ref[a * ROT_BLOCK:(a + 1) * ROT_BLOCK, :] = ca * cb - sa * sb
            sin_ref[a * ROT_BLOCK:(a + 1) * ROT_BLOCK, :] = sa * cb + ca * sb

    half = dk // 2
    n_chunks = w_ref.shape[1] // dk

    def chunk(c, kind, head):
        acc = jnp.dot(h_ref[...], w_ref[:, c * dk:(c + 1) * dk], preferred_element_type=F32)
        if kind == "v":
            o_ref[:, c * dk:(c + 1) * dk] = acc.astype(BF16)
        elif kind == "g":
            o_ref[:, c * dk:(c + 1) * dk] = _silu(acc).astype(BF16)
        else:
            cos = cos_ref[...]
            sin = sin_ref[...]
            t1 = acc[:, :half]
            t2 = acc[:, half:]
            o1 = t1 * cos - t2 * sin
            o2 = t1 * sin + t2 * cos
            if kind == "k":
                s = kdec_ref[:, head * half:(head + 1) * half]
                o1 = o1 * s
                o2 = o2 * s
            o_ref[:, c * dk:c * dk + half] = o1.astype(BF16)
            o_ref[:, c * dk + half:(c + 1) * dk] = o2.astype(BF16)

    def kind_of(gc):
        if gc < n_qk_chunks:
            return "q", gc
        if gc < 2 * n_qk_chunks:
            return "k", gc - n_qk_chunks
        if gc < 2 * n_qk_chunks + n_v_chunks:
            return "v", 0
        return "g", 0

    for jj in range(n_col_tiles):
        @pl.when(j == jj)
        def _():
            for c in range(n_chunks):
                chunk(c, *kind_of(jj * n_chunks + c))


def _ret_proj(x, gain, sc1p, sh, w_in, rot, kdec, qk_total, v_total, to_cast):
    t, d = x.shape
    n = w_in.shape[1]
    tm, tn = ROW_TILE, PROJ_COL_TILE
    n_row_tiles = t // tm
    dk = qk_total // N_HEADS
    half = dk // 2
    cosa, sina, cosb, sinb = rot
    kern = functools.partial(_proj_kernel, n_qk_chunks=qk_total // dk, n_v_chunks=v_total // dk,
                             n_col_tiles=n // tn, dk=dk, n_cast=len(to_cast))
    vec = pl.BlockSpec((1, d), lambda i, j: (0, 0))
    rot_a = pl.BlockSpec((tm // ROT_BLOCK, half), lambda i, j: (i, 0))
    rot_b = pl.BlockSpec((ROT_BLOCK, half), lambda i, j: (0, 0))
    cast_specs = [pl.BlockSpec((w.shape[0] // n_row_tiles, w.shape[1]), lambda i, j: (i, 0)) for w in to_cast]
    outs = pl.pallas_call(
        kern,
        grid=(n_row_tiles, n // tn),
        in_specs=[
            pl.BlockSpec((tm, d), lambda i, j: (i, 0)),
            vec, vec, vec,
            pl.BlockSpec((d, tn), lambda i, j: (0, j)),
            rot_a, rot_a, rot_b, rot_b,
            pl.BlockSpec((tm, N_HEADS * half), lambda i, j: (0, 0)),
        ] + cast_specs,
        out_specs=[pl.BlockSpec((tm, tn), lambda i, j: (i, j))] + cast_specs,
        out_shape=[jax.ShapeDtypeStruct((t, n), BF16)] + [jax.ShapeDtypeStruct(w.shape, BF16) for w in to_cast],
        scratch_shapes=[pltpu.VMEM((tm, d), BF16), pltpu.VMEM((tm, half), F32), pltpu.VMEM((tm, half), F32)],
        compiler_params=_params("arbitrary", "arbitrary"),
        name="proj",
    )(x, gain, sc1p, sh, w_in, cosa, sina, cosb, sinb, kdec, *to_cast)
    return outs[0], outs[1:]


def _ret_kernel(cdec_ref, q_ref, k_ref, v_ref, sg_ref, dmask_ref, qdec_ref, gn_ref, x_ref, g_ref, wo_ref, o_ref,
                state_ref, y_ref, *, dk, dv):
    @pl.when(pl.program_id(0) == 0)
    def _():
        state_ref[...] = jnp.zeros_like(state_ref)

    rows_per_step = q_ref.shape[0]
    for n in range(rows_per_step // RET_CHUNK):
        rows = slice(n * RET_CHUNK, (n + 1) * RET_CHUNK)
        if n > 0:
            prev = slice((n - 1) * RET_CHUNK, n * RET_CHUNK)
            o_ref[prev, :] = x_ref[prev, :] + g_ref[...] * jnp.dot(y_ref[prev, :], wo_ref[...],
                                                                   preferred_element_type=F32)
        for h in range(N_HEADS):
            qcols = slice(h * dk, (h + 1) * dk)
            vcols = slice(h * dv, (h + 1) * dv)
            q = q_ref[rows, qcols]
            ks = k_ref[rows, qcols]
            v = v_ref[rows, vcols]
            s = lax.dot_general(q, ks, (((1,), (1,)), ((), ())), preferred_element_type=F32)
            a = (s * dmask_ref[h]).astype(BF16)
            intra = jnp.dot(a, v, preferred_element_type=F32)
            st = state_ref[h]
            cross = jnp.dot(q, st.astype(BF16), preferred_element_type=F32)
            o = intra + cross * qdec_ref[h]
            kv = lax.dot_general(ks, v, (((0,), (0,)), ((), ())), preferred_element_type=F32)
            state_ref[h] = st * cdec_ref[h] + kv
            mu = jnp.mean(o, axis=-1, keepdims=True)
            dlt = o - mu
            var = jnp.mean(dlt * dlt, axis=-1, keepdims=True)
            on = dlt * lax.rsqrt(var + EPS)
            y = on * gn_ref[:, vcols] * sg_ref[rows, vcols].astype(F32)
            y_ref[rows, vcols] = y.astype(BF16)
    last = slice(rows_per_step - RET_CHUNK, rows_per_step)
    o_ref[last, :] = x_ref[last, :] + g_ref[...] * jnp.dot(y_ref[last, :], wo_ref[...],
                                                           preferred_element_type=F32)


def _retention(proj, cdec, dmask, qdec, gn_gain, x, g1, w_out, qk_total, v_total):
    t, d = x.shape
    r = RET_ROWS
    dk = qk_total // N_HEADS
    dv = v_total // N_HEADS
    assert v_total == 2 * qk_total
    kern = functools.partial(_ret_kernel, dk=dk, dv=dv)
    return pl.pallas_call(
        kern,
        grid=(t // r,),
        in_specs=[
            pl.BlockSpec(memory_space=pltpu.SMEM),
            pl.BlockSpec((r, qk_total), lambda i: (i, 0)),
            pl.BlockSpec((r, qk_total), lambda i: (i, 1)),
            pl.BlockSpec((r, v_total), lambda i: (i, 1)),
            pl.BlockSpec((r, v_total), lambda i: (i, 2)),
            pl.BlockSpec((N_HEADS, RET_CHUNK, RET_CHUNK), lambda i: (0, 0, 0)),
            pl.BlockSpec((N_HEADS, RET_CHUNK, dv), lambda i: (0, 0, 0)),
            pl.BlockSpec((1, v_total), lambda i: (0, 0)),
            pl.BlockSpec((r, d), lambda i: (i, 0)),
            pl.BlockSpec((1, d), lambda i: (0, 0)),
            _resident((v_total, d)),
        ],
        out_specs=pl.BlockSpec((r, d), lambda i: (i, 0)),
        out_shape=jax.ShapeDtypeStruct((t, d), F32),
        scratch_shapes=[pltpu.VMEM((N_HEADS, dk, dv), F32), pltpu.VMEM((r, v_total), BF16)],
        compiler_params=_params("arbitrary"),
        name="ret",
    )(cdec, proj, proj, proj, proj, dmask, qdec, gn_gain, x, g1, w_out)


def _ffn_kernel(x_ref, gain_ref, sc_ref, sh_ref, g_ref, wg_ref, wu_ref, wd_ref, o_ref, *, ff_tile):
    x = x_ref[...]
    h = _norm_mod(x, gain_ref[...], sc_ref[...], sh_ref[...]).astype(BF16)
    d_ff = wg_ref.shape[1]
    acc = jnp.zeros(x.shape, F32)
    for lo in range(0, d_ff, ff_tile):
        hi = min(lo + ff_tile, d_ff)
        a = jnp.dot(h, wg_ref[:, lo:hi], preferred_element_type=F32)
        b = jnp.dot(h, wu_ref[:, lo:hi], preferred_element_type=F32)
        act = (_silu(a) * b).astype(BF16)
        acc = acc + jnp.dot(act, wd_ref[lo:hi, :], preferred_element_type=F32)
    o_ref[...] = x + g_ref[...] * acc


def _dense_ffn(x, gain, sc1p, sh, g2, wg, wu, wd):
    t, d = x.shape
    d_ff = wg.shape[1]
    tm = ROW_TILE
    vec = pl.BlockSpec((1, d), lambda i: (0, 0))
    return pl.pallas_call(
        functools.partial(_ffn_kernel, ff_tile=512),
        grid=(t // tm,),
        in_specs=[
            pl.BlockSpec((tm, d), lambda i: (i, 0)),
            vec, vec, vec, vec,
            _resident((d, d_ff)), _resident((d, d_ff)), _resident((d_ff, d)),
        ],
        out_specs=pl.BlockSpec((tm, d), lambda i: (i, 0)),
        out_shape=jax.ShapeDtypeStruct((t, d), F32),
        compiler_params=_params("arbitrary"),
        name="ffn",
    )(x, gain, sc1p, sh, g2, wg, wu, wd)


def _pool_kernel(x_ref, gain_ref, sc_ref, sh_ref, g_ref, band_ref, wp_ref, bp_ref, ps_ref, o_ref,
                 hbuf_ref):
    i = pl.program_id(0)
    tm = x_ref.shape[0]
    gw = wp_ref.shape[1]

    @pl.when(i == 0)
    def _():
        hbuf_ref[0:POOL_HALO, :] = jnp.zeros((POOL_HALO, hbuf_ref.shape[1]), F32)

    @pl.when(i > 0)
    def _():
        hbuf_ref[0:POOL_HALO, :] = hbuf_ref[tm:tm + POOL_HALO, :]

    hbuf_ref[POOL_HALO:, :] = _norm_mod(x_ref[...], gain_ref[...], sc_ref[...], sh_ref[...])

    def split(v):
        hi = v.astype(BF16)
        return hi, (v - hi.astype(F32)).astype(BF16)

    sub = band_ref.shape[1]
    for s in range(tm // sub):
        rows = slice(s * sub, (s + 1) * sub)
        lo = POOL_HALO + s * sub
        x = x_ref[rows, :]
        h = hbuf_ref[lo:lo + sub, :]
        h_hi, h_lo = split(h)
        t1 = (lax.broadcasted_iota(jnp.int32, (sub, 1), 0) + (i * tm + s * sub + 1)).astype(F32)
        first_rows = lax.broadcasted_iota(jnp.int32, (POOL_HALO, 1), 0)
        for g, w in enumerate(POOL_WINDOWS):
            cols = slice(g * gw, (g + 1) * gw)
            band = band_ref[g]
            win = (jnp.dot(band, h_hi[:, cols], preferred_element_type=F32)
                   + jnp.dot(band, h_lo[:, cols], preferred_element_type=F32))
            head = jnp.zeros((POOL_HALO, gw), F32)
            for k in range(1, w):
                prev_row = hbuf_ref[lo - k:lo - k + 1, cols]
                head = head + jnp.where(first_rows < w - k, prev_row, 0.0)
            win = jnp.concatenate([win[:POOL_HALO] + head, win[POOL_HALO:]], axis=0)
            pooled = win * (1.0 / jnp.minimum(t1, float(w))) - h[:, cols]
            y = jnp.dot(pooled.astype(BF16), wp_ref[g], preferred_element_type=F32) + bp_ref[:, cols]
            o_ref[rows, cols] = x[:, cols] + (g_ref[:, cols] * ps_ref[:, cols]) * y


def _route(x, gain_ref, sc_ref, sh_ref, w_ref, tri_ref, h_ref, gates_ref, sel_ref, rank_ref, tot_ref, carry_ref):
    h = _norm_mod(x, gain_ref[...], sc_ref[...], sh_ref[...])
    h_hi = h.astype(BF16)
    h_ref[...] = h_hi
    h_lo = (h - h_hi.astype(F32)).astype(BF16)
    w = w_ref[...]
    p_hi = jnp.dot(h_hi, w, preferred_element_type=F32)
    p_lo = jnp.dot(h_lo, w, preferred_element_type=F32)
    logits = p_hi[:, :LANES] + p_hi[:, LANES:] + p_lo[:, :LANES] + p_lo[:, LANES:]
    lane = lax.broadcasted_iota(jnp.int32, logits.shape, 1).astype(F32)
    neg = jnp.float32(-jnp.inf)
    lg = jnp.where(lane < float(N_EXPERTS), logits, neg)
    m0 = jnp.max(lg, axis=-1, keepdims=True)
    i0 = jnp.min(jnp.where(lg == m0, lane, float(LANES)), axis=-1, keepdims=True)
    lg1 = jnp.where(lane == i0, neg, lg)
    m1 = jnp.max(lg1, axis=-1, keepdims=True)
    i1 = jnp.min(jnp.where(lg1 == m1, lane, float(LANES)), axis=-1, keepdims=True)
    e1 = jnp.exp(m1 - m0)
    den = 1.0 + e1
    is0 = lane == i0
    is1 = lane == i1
    gates_ref[...] = jnp.where(is0, 1.0 / den, jnp.where(is1, e1 / den, 0.0))
    sel = jnp.where(is0 | is1, 1.0, 0.0).astype(BF16)
    sel_ref[...] = sel

    @pl.when(pl.program_id(0) == 0)
    def _():
        carry_ref[...] = jnp.zeros_like(carry_ref)

    carry = carry_ref[...]
    for b in range(sel.shape[0] // RANK_ROWS):
        rows = slice(b * RANK_ROWS, (b + 1) * RANK_ROWS)
        s = sel[rows, :]
        cum = jnp.dot(tri_ref[...], s, preferred_element_type=F32)
        rank_ref[rows, :] = cum - s.astype(F32) + carry
        carry = carry + cum[RANK_ROWS - 1:, :]
        tot_ref[b] = carry
    carry_ref[...] = carry


def _pool_router_kernel(x_ref, gain_ref, sc_ref, sh_ref, g_ref, band_ref, wp_ref, bp_ref, ps_ref,
                        gain2_ref, sc2_ref, sh2_ref, wr_ref, tri_ref,
                        o_ref, h_ref, gates_ref, sel_ref, rank_ref, tot_ref, hbuf_ref, carry_ref):
    _pool_kernel(x_ref, gain_ref, sc_ref, sh_ref, g_ref, band_ref, wp_ref, bp_ref, ps_ref, o_ref,
                 hbuf_ref)
    _route(o_ref[...], gain2_ref, sc2_ref, sh2_ref, wr_ref, tri_ref, h_ref, gates_ref, sel_ref, rank_ref, tot_ref,
           carry_ref)


def _pool_router(x, gain, sc1p, sh, g1, band, wp, bp, ps, gain2, sc2p, sh2, w_cat, tri):
    t, d = x.shape
    tm = POOL_ROWS
    n_sub = tm // RANK_ROWS
    vec = pl.BlockSpec((1, d), lambda i: (0, 0))
    rows = pl.BlockSpec((tm, d), lambda i: (i, 0))
    lanes = pl.BlockSpec((tm, LANES), lambda i: (i, 0))
    return pl.pallas_call(
        _pool_router_kernel,
        grid=(t // tm,),
        in_specs=[
            rows, vec, vec, vec, vec,
            pl.BlockSpec(band.shape, lambda i: (0, 0, 0)),
            pl.BlockSpec(wp.shape, lambda i: (0, 0, 0)),
            vec, vec,
            vec, vec, vec,
            pl.BlockSpec((d, 2 * LANES), lambda i: (0, 0)),
            pl.BlockSpec((RANK_ROWS, RANK_ROWS), lambda i: (0, 0)),
        ],
        out_specs=[rows, rows, lanes, lanes, lanes, pl.BlockSpec((n_sub, 1, LANES), lambda i: (i, 0, 0))],
        out_shape=[
            jax.ShapeDtypeStruct((t, d), F32),
            jax.ShapeDtypeStruct((t, d), BF16),
            jax.ShapeDtypeStruct((t, LANES), F32),
            jax.ShapeDtypeStruct((t, LANES), BF16),
            jax.ShapeDtypeStruct((t, LANES), F32),
            jax.ShapeDtypeStruct((t // RANK_ROWS, 1, LANES), F32),
        ],
        scratch_shapes=[pltpu.VMEM((POOL_HALO + tm, d), F32), pltpu.VMEM((1, LANES), F32)],
        compiler_params=_params("arbitrary"),
        name="pool_router",
    )(x, gain, sc1p, sh, g1, band, wp, bp, ps, gain2, sc2p, sh2, w_cat, tri)


def _gather_kernel(be_ref, ng_ref, clo_ref, chi_ref, h_ref, post_ref, gatest_ref, xb_ref, gs_ref, acc_ref):
    u = pl.program_id(0)
    e = be_ref[u]
    unit_rows = xb_ref.shape[0]
    n_groups = unit_rows // MOE_GATHER_ROWS
    last_chunk = h_ref.shape[0] // GATHER_CHUNK - 1
    g0 = u * n_groups

    @pl.when(ng_ref[g0] > 0)
    def _():
        acc_ref[...] = jnp.zeros_like(acc_ref)
        trips = chi_ref[g0] - clo_ref[g0]
        for q in range(1, n_groups):
            trips = jnp.maximum(trips, chi_ref[g0 + q] - clo_ref[g0 + q])

        def body(i, gsums):
            out = []
            for q in range(n_groups):
                rows = slice(q * MOE_GATHER_ROWS, (q + 1) * MOE_GATHER_ROWS)
                c = clo_ref[g0 + q] + i
                live = c < chi_ref[g0 + q]
                off = pl.multiple_of(jnp.minimum(c, last_chunk) * GATHER_CHUNK, GATHER_CHUNK)
                p = post_ref[pl.ds(e, 1), pl.ds(off, GATHER_CHUNK)]
                p = jnp.where(live, p, -2)
                row_pos = (lax.broadcasted_iota(jnp.int32, (MOE_GATHER_ROWS, GATHER_CHUNK), 0)
                           + (u * unit_rows + q * MOE_GATHER_ROWS))
                match = row_pos == p
                onehot = jnp.where(match, 1.0, 0.0).astype(BF16)
                acc_ref[rows, :] += jnp.dot(onehot, h_ref[pl.ds(off, GATHER_CHUNK), :],
                                            preferred_element_type=F32)
                gate = gatest_ref[pl.ds(e, 1), pl.ds(off, GATHER_CHUNK)]
                out.append(gsums[q] + jnp.sum(jnp.where(match, gate, 0.0), axis=1, keepdims=True))
            return tuple(out)

        zero = jnp.zeros((MOE_GATHER_ROWS, 1), F32)
        gsums = lax.fori_loop(0, trips, body, (zero,) * n_groups)
        xb_ref[...] = acc_ref[...].astype(BF16)
        for q in range(n_groups):
            gs_ref[q * MOE_GATHER_ROWS:(q + 1) * MOE_GATHER_ROWS, :] = gsums[q]

    @pl.when(ng_ref[g0] == 0)
    def _():
        xb_ref[...] = jnp.zeros_like(xb_ref)
        gs_ref[...] = jnp.zeros_like(gs_ref)


def _gather(beu, ng, clo, chi, h, post, gatest):
    t, d = h.shape
    n_units = beu.shape[0]
    grid_spec = pltpu.PrefetchScalarGridSpec(
        num_scalar_prefetch=4,
        grid=(n_units,),
        in_specs=[
            pl.BlockSpec((t, d), lambda u, *_: (0, 0), pipeline_mode=pl.Buffered(1)),
            pl.BlockSpec(post.shape, lambda u, *_: (0, 0), pipeline_mode=pl.Buffered(1)),
            pl.BlockSpec(gatest.shape, lambda u, *_: (0, 0), pipeline_mode=pl.Buffered(1)),
        ],
        out_specs=[
            pl.BlockSpec((MOE_GATHER_UNIT, d), lambda u, *_: (u, 0)),
            pl.BlockSpec((MOE_GATHER_UNIT, 1), lambda u, *_: (u, 0)),
        ],
        scratch_shapes=[pltpu.VMEM((MOE_GATHER_UNIT, d), F32)],
    )
    return pl.pallas_call(
        _gather_kernel,
        grid_spec=grid_spec,
        out_shape=[
            jax.ShapeDtypeStruct((n_units * MOE_GATHER_UNIT, d), BF16),
            jax.ShapeDtypeStruct((n_units * MOE_GATHER_UNIT, 1), F32),
        ],
        compiler_params=_params("arbitrary"),
        name="gather",
    )(beu, ng, clo, chi, h, post, gatest)


def _experts_kernel(be_ref, nv_ref, xb_ref, gs_ref, wg_ref, wu_ref, wd_ref, yb_ref,
                    acc_ref, wgb_ref, wub_ref, wdb_ref):
    b = pl.program_id(0)
    f = pl.program_id(1)
    nf = pl.num_programs(1)
    nv = nv_ref[b]
    block_rows = xb_ref.shape[0]

    @pl.when((b == 0) & (f == 0))
    def _():
        acc_ref[...] = jnp.zeros_like(acc_ref)

    def swiglu_rows(lo, n, cast):
        rows = slice(lo, lo + n)
        if cast:
            wgb_ref[...] = wg_ref[0].astype(BF16)
            wub_ref[...] = wu_ref[0].astype(BF16)
            wdb_ref[...] = wd_ref[0].astype(BF16)
        x = xb_ref[rows, :]
        a = jnp.dot(x, wgb_ref[...], preferred_element_type=F32)
        u = jnp.dot(x, wub_ref[...], preferred_element_type=F32)
        act = (_silu(a) * u).astype(BF16)
        prev = jnp.where(f > 0, acc_ref[rows, :], 0.0)
        new = prev + jnp.dot(act, wdb_ref[...], preferred_element_type=F32)
        acc_ref[rows, :] = new
        yb_ref[rows, :] = (new * gs_ref[rows, :]).astype(BF16)

    pair = 2 * MOE_DOT_ROWS
    half_group = MOE_DOT_ROWS // 2
    for p in range(block_rows // pair):
        lo = p * pair
        full = nv >= lo + pair

        @pl.when(full)
        def _():
            swiglu_rows(lo, pair, p == 0)

        for hh in range(2):
            glo = lo + hh * MOE_DOT_ROWS
            used = jnp.logical_not(full) & (nv > glo)
            tall = nv > glo + half_group
            first = p == 0 and hh == 0

            @pl.when(used & tall)
            def _():
                swiglu_rows(glo, MOE_DOT_ROWS, first)

            @pl.when(used & jnp.logical_not(tall))
            def _():
                swiglu_rows(glo, half_group, first)

    @pl.when(f == nf - 1)
    def _():
        for q in range(block_rows // half_group):
            @pl.when(nv <= q * half_group)
            def _():
                yb_ref[q * half_group:(q + 1) * half_group, :] = jnp.zeros((half_group, yb_ref.shape[1]), BF16)


def _experts(be, nv, xb, gs, wg, wu, wd):
    d = xb.shape[1]
    n_blocks = be.shape[0]
    d_ff = wg.shape[2]
    tf = MOE_FF_TILE
    nf = d_ff // tf
    bsz = MOE_BLOCK

    def ff_idx(f, nvr, b):
        return jnp.where(nvr[b] > 0, f, nf - 1)

    grid_spec = pltpu.PrefetchScalarGridSpec(
        num_scalar_prefetch=2,
        grid=(n_blocks, nf),
        in_specs=[
            pl.BlockSpec((bsz, d), lambda b, f, *_: (b, 0)),
            pl.BlockSpec((bsz, 1), lambda b, f, *_: (b, 0)),
            pl.BlockSpec((1, d, tf), lambda b, f, be_r, nv_r: (be_r[b], 0, ff_idx(f, nv_r, b))),
            pl.BlockSpec((1, d, tf), lambda b, f, be_r, nv_r: (be_r[b], 0, ff_idx(f, nv_r, b))),
            pl.BlockSpec((1, tf, d), lambda b, f, be_r, nv_r: (be_r[b], ff_idx(f, nv_r, b), 0)),
        ],
        out_specs=pl.BlockSpec((bsz, d), lambda b, f, *_: (b, 0)),
        scratch_shapes=[
            pltpu.VMEM((bsz, d), F32),
            pltpu.VMEM((d, tf), BF16), pltpu.VMEM((d, tf), BF16), pltpu.VMEM((tf, d), BF16),
        ],
    )
    return pl.pallas_call(
        _experts_kernel,
        grid_spec=grid_spec,
        out_shape=jax.ShapeDtypeStruct((n_blocks * bsz, d), BF16),
        compiler_params=_params("arbitrary", "arbitrary"),
        name="experts",
    )(be, nv, xb, gs, wg, wu, wd)


def _combine_kernel(win_ref, far_ref, x_ref, pos_ref, g_ref, fg_ref, *refs):
    yb_refs, o_ref, acc_ref = refs[:-2], refs[-2], refs[-1]
    i = pl.program_id(0)
    n_sub = x_ref.shape[0] // COMBINE_ROWS
    near = COMBINE_WINDOWS - 1
    for s in range(n_sub):
        rows = slice(s * COMBINE_ROWS, (s + 1) * COMBINE_ROWS)

        def onehot_dot(first_chunk, n_chunks):
            lane = lax.broadcasted_iota(jnp.int32, (COMBINE_ROWS, n_chunks * COMBINE_WIN), 1)
            onehots, windows = [], []
            for e in range(N_EXPERTS):
                k = (i * n_sub + s) * N_EXPERTS + e
                base = (win_ref[k] + first_chunk) * COMBINE_WIN
                rel = pos_ref[rows, e:e + 1] - base
                onehots.append(jnp.where(lane == rel, 1.0, 0.0).astype(BF16))
                w0 = COMBINE_WINDOWS * (s * N_EXPERTS + e) + first_chunk
                windows += [yb_refs[w0 + j][...] for j in range(n_chunks)]
            return jnp.dot(jnp.concatenate(onehots, axis=1), jnp.concatenate(windows, axis=0),
                           preferred_element_type=F32)

        acc_ref[rows, :] = onehot_dot(0, near)

        @pl.when(far_ref[i * n_sub + s] > 0)
        def _():
            acc_ref[rows, :] += onehot_dot(near, 1)

        xo = x_ref[rows, :] + g_ref[...] * acc_ref[rows, :]
        ms = jnp.mean(xo * xo, axis=-1, keepdims=True)
        o_ref[rows, :] = xo * lax.rsqrt(ms + EPS) * fg_ref[...]


def _combine(win, far, x, pos, g2, fgain, yb):
    t, d = x.shape
    tm = COMBINE_TILE
    n_sub = tm // COMBINE_ROWS
    n_win = yb.shape[0] // COMBINE_WIN
    vec = pl.BlockSpec((1, d), lambda i, w, f: (0, 0))

    def window(s, e, j):
        return lambda i, w, f: (jnp.minimum(w[(i * n_sub + s) * N_EXPERTS + e] + j, n_win - 1), 0)

    yb_specs = []
    for s in range(n_sub):
        for e in range(N_EXPERTS):
            for j in range(COMBINE_WINDOWS):
                yb_specs.append(pl.BlockSpec((COMBINE_WIN, d), window(s, e, j)))
    grid_spec = pltpu.PrefetchScalarGridSpec(
        num_scalar_prefetch=2,
        grid=(t // tm,),
        in_specs=[
            pl.BlockSpec((tm, d), lambda i, w, f: (i, 0)),
            pl.BlockSpec((tm, N_EXPERTS), lambda i, w, f: (i, 0)),
            vec, vec,
        ] + yb_specs,
        out_specs=pl.BlockSpec((tm, d), lambda i, w, f: (i, 0)),
        scratch_shapes=[pltpu.VMEM((tm, d), F32)],
    )
    return pl.pallas_call(
        _combine_kernel,
        grid_spec=grid_spec,
        out_shape=jax.ShapeDtypeStruct((t, d), F32),
        compiler_params=_params("arbitrary"),
        name="combine",
    )(win, far, x, pos, g2, fgain, *([yb] * len(yb_specs)))


def _routing_tables(rank, tot, sel):
    t = rank.shape[0]
    bsz = MOE_BLOCK
    n_blocks = (2 * t) // bsz + N_EXPERTS
    cum = tot[:, 0, :N_EXPERTS].astype(jnp.int32)
    counts = cum[-1]
    padded = ((counts + bsz - 1) // bsz) * bsz
    pad_end = jnp.cumsum(padded)
    start_pad = pad_end - padded
    selected = sel[:, :N_EXPERTS] > 0
    pos = jnp.where(selected, start_pad[None, :] + rank[:, :N_EXPERTS].astype(jnp.int32), -1)

    bstart = jnp.arange(n_blocks, dtype=jnp.int32) * bsz
    be = jnp.minimum(jnp.sum(bstart[:, None] >= pad_end[None, :], axis=1), N_EXPERTS - 1).astype(jnp.int32)
    r0 = bstart - start_pad[be]
    nv = jnp.clip(counts[be] - r0, 0, bsz).astype(jnp.int32)
    n_gather = bsz // MOE_GATHER_ROWS
    goff = jnp.arange(n_gather, dtype=jnp.int32) * MOE_GATHER_ROWS
    r0g = (r0[:, None] + goff[None, :]).reshape(-1)
    ng = jnp.clip(nv[:, None] - goff[None, :], 0, MOE_GATHER_ROWS).reshape(-1)
    beg = jnp.repeat(be, n_gather)
    chunk_end = cum[:, beg].T
    chunk_start = jnp.concatenate([jnp.zeros((1, N_EXPERTS), jnp.int32), cum[:-1]], axis=0)[:, beg].T
    clo = jnp.sum(chunk_end <= r0g[:, None], axis=1).astype(jnp.int32)
    chi = jnp.sum(chunk_start < (r0g + ng)[:, None], axis=1).astype(jnp.int32)
    beu = jnp.repeat(be, bsz // MOE_GATHER_UNIT)

    big = jnp.int32(2 ** 30)
    first = jnp.min(jnp.where(pos >= 0, pos, big).reshape(t // COMBINE_ROWS, COMBINE_ROWS, N_EXPERTS), axis=1)
    win2 = jnp.where(first == big, 0, first // COMBINE_WIN).astype(jnp.int32)
    last = jnp.max(pos.reshape(t // COMBINE_ROWS, COMBINE_ROWS, N_EXPERTS), axis=1)
    far = jnp.any((last >= 0) & (last // COMBINE_WIN >= win2 + (COMBINE_WINDOWS - 1)), axis=1).astype(jnp.int32)
    return pos, be, nv, beu, ng, clo, chi, win2.reshape(-1), far


def _rotary_tables(seq, dk):
    half = dk // 2
    inv = ROPE_BASE ** (-jnp.arange(half, dtype=F32) / half)
    ang_a = (jnp.arange(seq // ROT_BLOCK) * ROT_BLOCK).astype(F32)[:, None] * inv[None, :]
    ang_b = jnp.arange(ROT_BLOCK).astype(F32)[:, None] * inv[None, :]
    return jnp.cos(ang_a), jnp.sin(ang_a), jnp.cos(ang_b), jnp.sin(ang_b)


def _decay_tables(dk, dv):
    c = RET_CHUNK
    log_gamma = jnp.log1p(-jnp.exp2(-5.0 - jnp.arange(N_HEADS, dtype=F32)))
    idx = jnp.arange(c, dtype=F32)
    row_dec = jnp.exp(log_gamma[:, None] * (idx + 1.0 - c))
    causal = (idx[:, None] >= idx[None, :]).astype(F32)
    dmask = row_dec[:, :, None] * causal[None]
    k_dec = jnp.exp(log_gamma[:, None] * (c - 1.0 - idx)) * (dk ** -0.5)
    q_dec = jnp.exp(log_gamma[:, None] * (idx + 1.0))
    chunk_dec = jnp.exp(log_gamma * c)
    half = dk // 2
    kdec_tile = jnp.tile(jnp.repeat(k_dec.T, half, axis=1), (ROW_TILE // c, 1))
    qdec_tile = jnp.broadcast_to(q_dec[:, :, None], (N_HEADS, c, dv))
    return dmask, kdec_tile, qdec_tile, chunk_dec


def _pool_bands(tm):
    lag = jnp.arange(tm)[:, None] - jnp.arange(tm)[None, :]
    return jnp.stack([((lag >= 0) & (lag < w)) for w in POOL_WINDOWS]).astype(BF16)


def kernel(x, c, ada_w, ada_b, norm_gain, ret_w_in, ret_gn_gain, ret_w_out, ffn_w_gate, ffn_w_up, ffn_w_down,
           pool_w, pool_b, pool_scale, moe_router, moe_w_gate, moe_w_up, moe_w_down, final_norm_gain):
    bsz, seq, d = x.shape
    assert bsz == 1 and ada_w.shape[0] == 2
    xt = x.reshape(seq, d)
    qk_total = d
    v_total = ret_w_out.shape[1]
    dk = qk_total // N_HEADS
    dv = v_total // N_HEADS

    mod = _modulation(c, ada_w, ada_b)
    def mods(i):
        parts = [mod[i, :, k * d:(k + 1) * d] for k in range(N_MOD)]
        sh1, sc1, g1, sh2, sc2, g2 = parts
        return sh1, 1.0 + sc1, g1, sh2, 1.0 + sc2, g2

    sh1, sc1p, g1, sh2, sc2p, g2 = mods(0)
    dmask, kdec_tile, qdec_tile, chunk_dec = _decay_tables(dk, dv)
    proj, (w_out_b, wg_b, wu_b, wd_b) = _ret_proj(
        xt, norm_gain[0, 0][None], sc1p, sh1, ret_w_in[0].astype(BF16), _rotary_tables(seq, dk), kdec_tile,
        qk_total, v_total, [ret_w_out[0], ffn_w_gate[0], ffn_w_up[0], ffn_w_down[0]])
    x1 = _retention(proj, chunk_dec, dmask, qdec_tile, ret_gn_gain[0][None], xt, g1, w_out_b, qk_total, v_total)
    x2 = _dense_ffn(x1, norm_gain[0, 1][None], sc2p, sh2, g2, wg_b, wu_b, wd_b)

    sh1, sc1p, g1, sh2, sc2p, g2 = mods(1)
    band = _pool_bands(POOL_SUB_ROWS)
    wr = jnp.pad(moe_router[0], ((0, 0), (0, LANES - N_EXPERTS)))
    wr_hi = wr.astype(BF16)
    wr_lo = (wr - wr_hi.astype(F32)).astype(BF16)
    tri = (jnp.arange(RANK_ROWS)[:, None] >= jnp.arange(RANK_ROWS)[None, :]).astype(BF16)
    x3, h4, gates, sel, rank, tot = _pool_router(
        x2, norm_gain[1, 0][None], sc1p, sh1, g1, band,
        pool_w[0].astype(BF16), pool_b[0].reshape(1, d), pool_scale[0][None],
        norm_gain[1, 1][None], sc2p, sh2, jnp.concatenate([wr_hi, wr_lo], axis=1), tri)
    pos, be, nv, beu, ng, clo, chi, win, far = _routing_tables(rank, tot, sel)
    xb, gs = _gather(beu, ng, clo, chi, h4, pos.T, gates[:, :N_EXPERTS].T)
    yb = _experts(be, nv, xb, gs, moe_w_gate[0], moe_w_up[0], moe_w_down[0])
    out = _combine(win, far, x3, pos, g2, final_norm_gain[None], yb)
    return out.reshape(bsz, seq, d)
```

```python
import functools

import jax
import jax.numpy as jnp
from jax import lax
from jax.experimental import pallas as pl
from jax.experimental.pallas import tpu as pltpu

F32 = jnp.float32
BF16 = jnp.bfloat16

EPS = 1e-6
N_HEADS = 4
RET_CHUNK = 256
ROPE_BASE = 10000.0
ROT_BLOCK = 128
POOL_WINDOWS = (2, 4, 8, 16)
POOL_HALO = 16
N_EXPERTS = 8
N_MOD = 6

VMEM_LIMIT_BYTES = 56 * 1024 * 1024

ROW_TILE = 1024
PROJ_COL_TILE = 3072
RET_ROWS = 1024
POOL_ROWS = 1024
POOL_SUB_ROWS = 256
RANK_ROWS = 512
MOE_BLOCK = 2048
MOE_DOT_ROWS = 512
MOE_GATHER_UNIT = 1024
MOE_GATHER_ROWS = 256
MOE_FF_TILE = 512
GATHER_CHUNK = 512
COMBINE_TILE = 256
COMBINE_ROWS = 256
COMBINE_WIN = 128
COMBINE_WINDOWS = COMBINE_ROWS // COMBINE_WIN + 1
LANES = 128


def _silu(v):
    return v / (1.0 + jnp.exp(-v))


def _norm_mod(x, gain, scale1p, shift):
    ms = jnp.mean(x * x, axis=-1, keepdims=True)
    return (x * lax.rsqrt(ms + EPS)) * (gain * scale1p) + shift


def _params(*sem):
    return pltpu.CompilerParams(dimension_semantics=sem, vmem_limit_bytes=VMEM_LIMIT_BYTES)


def _resident(shape):
    nd = len(shape)
    return pl.BlockSpec(shape, lambda *_: (0,) * nd, pipeline_mode=pl.Buffered(1))


def _mod_kernel(c_ref, w_ref, b_ref, o_ref):
    ca = _silu(c_ref[...])
    o_ref[0] = jnp.sum(ca * w_ref[0], axis=0, keepdims=True) + b_ref[0]


def _modulation(c, ada_w, ada_b):
    depth, d, n = ada_w.shape
    tn = 1024
    return pl.pallas_call(
        _mod_kernel,
        grid=(depth, n // tn),
        in_specs=[
            pl.BlockSpec((d, 1), lambda i, j: (0, 0)),
            pl.BlockSpec((1, d, tn), lambda i, j: (i, 0, j)),
            pl.BlockSpec((1, 1, tn), lambda i, j: (i, 0, j)),
        ],
        out_specs=pl.BlockSpec((1, 1, tn), lambda i, j: (i, 0, j)),
        out_shape=jax.ShapeDtypeStruct((depth, 1, n), F32),
        compiler_params=_params("arbitrary", "arbitrary"),
        name="mod",
    )(c.reshape(d, 1), ada_w, ada_b.reshape(depth, 1, n))


def _proj_kernel(x_ref, gain_ref, sc_ref, sh_ref, w_ref, cosa_ref, sina_ref, cosb_ref, sinb_ref, kdec_ref, *refs,
                 n_qk_chunks, n_v_chunks, n_col_tiles, dk, n_cast):
    cast_in = refs[:n_cast]
    o_ref = refs[n_cast]
    cast_out = refs[n_cast + 1:2 * n_cast + 1]
    h_ref, cos_ref, sin_ref = refs[2 * n_cast + 1:]
    j = pl.program_id(1)

    @pl.when(j == 0)
    def _():
        h_ref[...] = _norm_mod(x_ref[...], gain_ref[...], sc_ref[...], sh_ref[...]).astype(BF16)
        for src, dst in zip(cast_in, cast_out):
            dst[...] = src[...].astype(BF16)
        cb = cosb_ref[...]
        sb = sinb_ref[...]
        for a in range(cosa_ref.shape[0]):
            ca = cosa_ref[a:a + 1, :]
            sa = sina_ref[a:a + 1, :]
            cos_ref[a * ROT_BLOCK:(a + 1) * ROT_BLOCK, :] = ca * cb - sa * sb
            sin_ref[a * ROT_BLOCK:(a + 1) * ROT_BLOCK, :] = sa * cb + ca * sb

    half = dk // 2
    n_chunks = w_ref.shape[1] // dk

    def chunk(c, kind, head):
        acc = jnp.dot(h_ref[...], w_ref[:, c * dk:(c + 1) * dk], preferred_element_type=F32)
        if kind == "v":
            o_ref[:, c * dk:(c + 1) * dk] = acc.astype(BF16)
        elif kind == "g":
            o_ref[:, c * dk:(c + 1) * dk] = _silu(acc).astype(BF16)
        else:
            cos = cos_ref[...]
            sin = sin_ref[...]
            t1 = acc[:, :half]
            t2 = acc[:, half:]
            o1 = t1 * cos - t2 * sin
            o2 = t1 * sin + t2 * cos
            if kind == "k":
                s = kdec_ref[:, head * half:(head + 1) * half]
                o1 = o1 * s
                o2 = o2 * s
            o_ref[:, c * dk:c * dk + half] = o1.astype(BF16)
            o_ref[:, c * dk + half:(c + 1) * dk] = o2.astype(BF16)

    def kind_of(gc):
        if gc < n_qk_chunks:
            return "q", gc
        if gc < 2 * n_qk_chunks:
            return "k", gc - n_qk_chunks
        if gc < 2 * n_qk_chunks + n_v_chunks:
            return "v", 0
        return "g", 0

    for jj in range(n_col_tiles):
        @pl.when(j == jj)
        def _():
            for c in range(n_chunks):
                chunk(c, *kind_of(jj * n_chunks + c))


def _ret_proj(x, gain, sc1p, sh, w_in, rot, kdec, qk_total, v_total, to_cast):
    t, d = x.shape
    n = w_in.shape[1]
    tm, tn = ROW_TILE, PROJ_COL_TILE
    n_row_tiles = t // tm
    dk = qk_total // N_HEADS
    half = dk // 2
    cosa, sina, cosb, sinb = rot
    kern = functools.partial(_proj_kernel, n_qk_chunks=qk_total // dk, n_v_chunks=v_total // dk,
                             n_col_tiles=n // tn, dk=dk, n_cast=len(to_cast))
    vec = pl.BlockSpec((1, d), lambda i, j: (0, 0))
    rot_a = pl.BlockSpec((tm // ROT_BLOCK, half), lambda i, j: (i, 0))
    rot_b = pl.BlockSpec((ROT_BLOCK, half), lambda i, j: (0, 0))
    cast_specs = [pl.BlockSpec((w.shape[0] // n_row_tiles, w.shape[1]), lambda i, j: (i, 0)) for w in to_cast]
    outs = pl.pallas_call(
        kern,
        grid=(n_row_tiles, n // tn),
        in_specs=[
            pl.BlockSpec((tm, d), lambda i, j: (i, 0)),
            vec, vec, vec,
            pl.BlockSpec((d, tn), lambda i, j: (0, j)),
            rot_a, rot_a, rot_b, rot_b,
            pl.BlockSpec((tm, N_HEADS * half), lambda i, j: (0, 0)),
        ] + cast_specs,
        out_specs=[pl.BlockSpec((tm, tn), lambda i, j: (i, j))] + cast_specs,
        out_shape=[jax.ShapeDtypeStruct((t, n), BF16)] + [jax.ShapeDtypeStruct(w.shape, BF16) for w in to_cast],
        scratch_shapes=[pltpu.VMEM((tm, d), BF16), pltpu.VMEM((tm, half), F32), pltpu.VMEM((tm, half), F32)],
        compiler_params=_params("arbitrary", "arbitrary"),
        name="proj",
    )(x, gain, sc1p, sh, w_in, cosa, sina, cosb, sinb, kdec, *to_cast)
    return outs[0], outs[1:]


def _ret_kernel(cdec_ref, q_ref, k_ref, v_ref, sg_ref, dmask_ref, qdec_ref, gn_ref, x_ref, g_ref, wo_ref, o_ref,
                state_ref, y_ref, *, dk, dv):
    @pl.when(pl.program_id(0) == 0)
    def _():
        state_ref[...] = jnp.zeros_like(state_ref)

    rows_per_step = q_ref.shape[0]
    for n in range(rows_per_step // RET_CHUNK):
        rows = slice(n * RET_CHUNK, (n + 1) * RET_CHUNK)
        if n > 0:
            prev = slice((n - 1) * RET_CHUNK, n * RET_CHUNK)
            o_ref[prev, :] = x_ref[prev, :] + g_ref[...] * jnp.dot(y_ref[prev, :], wo_ref[...],
                                                                   preferred_element_type=F32)
        for h in range(N_HEADS):
            qcols = slice(h * dk, (h + 1) * dk)
            vcols = slice(h * dv, (h + 1) * dv)
            q = q_ref[rows, qcols]
            ks = k_ref[rows, qcols]
            v = v_ref[rows, vcols]
            s = lax.dot_general(q, ks, (((1,), (1,)), ((), ())), preferred_element_type=F32)
            a = (s * dmask_ref[h]).astype(BF16)
            intra = jnp.dot(a, v, preferred_element_type=F32)
            st = state_ref[h]
            cross = jnp.dot(q, st.astype(BF16), preferred_element_type=F32)
            o = intra + cross * qdec_ref[h]
            kv = lax.dot_general(ks, v, (((0,), (0,)), ((), ())), preferred_element_type=F32)
            state_ref[h] = st * cdec_ref[h] + kv
            mu = jnp.mean(o, axis=-1, keepdims=True)
            dlt = o - mu
            var = jnp.mean(dlt * dlt, axis=-1, keepdims=True)
            on = dlt * lax.rsqrt(var + EPS)
            y = on * gn_ref[:, vcols] * sg_ref[rows, vcols].astype(F32)
            y_ref[rows, vcols] = y.astype(BF16)
    last = slice(rows_per_step - RET_CHUNK, rows_per_step)
    o_ref[last, :] = x_ref[last, :] + g_ref[...] * jnp.dot(y_ref[last, :], wo_ref[...],
                                                           preferred_element_type=F32)


def _retention(proj, cdec, dmask, qdec, gn_gain, x, g1, w_out, qk_total, v_total):
    t, d = x.shape
    r = RET_ROWS
    dk = qk_total // N_HEADS
    dv = v_total // N_HEADS
    assert v_total == 2 * qk_total
    kern = functools.partial(_ret_kernel, dk=dk, dv=dv)
    return pl.pallas_call(
        kern,
        grid=(t // r,),
        in_specs=[
            pl.BlockSpec(memory_space=pltpu.SMEM),
            pl.BlockSpec((r, qk_total), lambda i: (i, 0)),
            pl.BlockSpec((r, qk_total), lambda i: (i, 1)),
            pl.BlockSpec((r, v_total), lambda i: (i, 1)),
            pl.BlockSpec((r, v_total), lambda i: (i, 2)),
            _resident((N_HEADS, RET_CHUNK, RET_CHUNK)),
            _resident((N_HEADS, RET_CHUNK, dv)),
            pl.BlockSpec((1, v_total), lambda i: (0, 0)),
            pl.BlockSpec((r, d), lambda i: (i, 0)),
            pl.BlockSpec((1, d), lambda i: (0, 0)),
            _resident((v_total, d)),
        ],
        out_specs=pl.BlockSpec((r, d), lambda i: (i, 0)),
        out_shape=jax.ShapeDtypeStruct((t, d), F32),
        scratch_shapes=[pltpu.VMEM((N_HEADS, dk, dv), F32), pltpu.VMEM((r, v_total), BF16)],
        compiler_params=_params("arbitrary"),
        name="ret",
    )(cdec, proj, proj, proj, proj, dmask, qdec, gn_gain, x, g1, w_out)


def _ffn_kernel(x_ref, gain_ref, sc_ref, sh_ref, g_ref, wg_ref, wu_ref, wd_ref, o_ref, *, ff_tile):
    x = x_ref[...]
    h = _norm_mod(x, gain_ref[...], sc_ref[...], sh_ref[...]).astype(BF16)
    d_ff = wg_ref.shape[1]
    acc = jnp.zeros(x.shape, F32)
    for lo in range(0, d_ff, ff_tile):
        hi = min(lo + ff_tile, d_ff)
        a = jnp.dot(h, wg_ref[:, lo:hi], preferred_element_type=F32)
        b = jnp.dot(h, wu_ref[:, lo:hi], preferred_element_type=F32)
        act = (_silu(a) * b).astype(BF16)
        acc = acc + jnp.dot(act, wd_ref[lo:hi, :], preferred_element_type=F32)
    o_ref[...] = x + g_ref[...] * acc


def _dense_ffn(x, gain, sc1p, sh, g2, wg, wu, wd):
    t, d = x.shape
    d_ff = wg.shape[1]
    tm = ROW_TILE
    vec = pl.BlockSpec((1, d), lambda i: (0, 0))
    return pl.pallas_call(
        functools.partial(_ffn_kernel, ff_tile=512),
        grid=(t // tm,),
        in_specs=[
            pl.BlockSpec((tm, d), lambda i: (i, 0)),
            vec, vec, vec, vec,
            _resident((d, d_ff)), _resident((d, d_ff)), _resident((d_ff, d)),
        ],
        out_specs=pl.BlockSpec((tm, d), lambda i: (i, 0)),
        out_shape=jax.ShapeDtypeStruct((t, d), F32),
        compiler_params=_params("arbitrary"),
        name="ffn",
    )(x, gain, sc1p, sh, g2, wg, wu, wd)


def _pool_kernel(x_ref, gain_ref, sc_ref, sh_ref, g_ref, band_ref, wp_ref, bp_ref, ps_ref, o_ref,
                 hbuf_ref):
    i = pl.program_id(0)
    tm = x_ref.shape[0]
    gw = wp_ref.shape[1]

    @pl.when(i == 0)
    def _():
        hbuf_ref[0:POOL_HALO, :] = jnp.zeros((POOL_HALO, hbuf_ref.shape[1]), F32)

    @pl.when(i > 0)
    def _():
        hbuf_ref[0:POOL_HALO, :] = hbuf_ref[tm:tm + POOL_HALO, :]

    hbuf_ref[POOL_HALO:, :] = _norm_mod(x_ref[...], gain_ref[...], sc_ref[...], sh_ref[...])

    def split(v):
        hi = v.astype(BF16)
        return hi, (v - hi.astype(F32)).astype(BF16)

    sub = band_ref.shape[1]
    for s in range(tm // sub):
        rows = slice(s * sub, (s + 1) * sub)
        lo = POOL_HALO + s * sub
        x = x_ref[rows, :]
        h = hbuf_ref[lo:lo + sub, :]
        h_hi, h_lo = split(h)
        t1 = (lax.broadcasted_iota(jnp.int32, (sub, 1), 0) + (i * tm + s * sub + 1)).astype(F32)
        first_rows = lax.broadcasted_iota(jnp.int32, (POOL_HALO, 1), 0)
        for g, w in enumerate(POOL_WINDOWS):
            cols = slice(g * gw, (g + 1) * gw)
            band = band_ref[g]
            win = (jnp.dot(band, h_hi[:, cols], preferred_element_type=F32)
                   + jnp.dot(band, h_lo[:, cols], preferred_element_type=F32))
            head = jnp.zeros((POOL_HALO, gw), F32)
            for k in range(1, w):
                prev_row = hbuf_ref[lo - k:lo - k + 1, cols]
                head = head + jnp.where(first_rows < w - k, prev_row, 0.0)
            win = jnp.concatenate([win[:POOL_HALO] + head, win[POOL_HALO:]], axis=0)
            pooled = win * (1.0 / jnp.minimum(t1, float(w))) - h[:, cols]
            y = jnp.dot(pooled.astype(BF16), wp_ref[g], preferred_element_type=F32) + bp_ref[:, cols]
            o_ref[rows, cols] = x[:, cols] + (g_ref[:, cols] * ps_ref[:, cols]) * y


def _route(x, gain_ref, sc_ref, sh_ref, w_ref, tri_ref, h_ref, gates_ref, sel_ref, rank_ref, tot_ref, carry_ref):
    h = _norm_mod(x, gain_ref[...], sc_ref[...], sh_ref[...])
    h_hi = h.astype(BF16)
    h_ref[...] = h_hi
    h_lo = (h - h_hi.astype(F32)).astype(BF16)
    w = w_ref[...]
    p_hi = jnp.dot(h_hi, w, preferred_element_type=F32)
    p_lo = jnp.dot(h_lo, w, preferred_element_type=F32)
    logits = p_hi[:, :LANES] + p_hi[:, LANES:] + p_lo[:, :LANES] + p_lo[:, LANES:]
    lane = lax.broadcasted_iota(jnp.int32, logits.shape, 1).astype(F32)
    neg = jnp.float32(-jnp.inf)
    lg = jnp.where(lane < float(N_EXPERTS), logits, neg)
    m0 = jnp.max(lg, axis=-1, keepdims=True)
    i0 = jnp.min(jnp.where(lg == m0, lane, float(LANES)), axis=-1, keepdims=True)
    lg1 = jnp.where(lane == i0, neg, lg)
    m1 = jnp.max(lg1, axis=-1, keepdims=True)
    i1 = jnp.min(jnp.where(lg1 == m1, lane, float(LANES)), axis=-1, keepdims=True)
    e1 = jnp.exp(m1 - m0)
    den = 1.0 + e1
    is0 = lane == i0
    is1 = lane == i1
    gates_ref[...] = jnp.where(is0, 1.0 / den, jnp.where(is1, e1 / den, 0.0))
    sel = jnp.where(is0 | is1, 1.0, 0.0).astype(BF16)
    sel_ref[...] = sel

    @pl.when(pl.program_id(0) == 0)
    def _():
        carry_ref[...] = jnp.zeros_like(carry_ref)

    carry = carry_ref[...]
    for b in range(sel.shape[0] // RANK_ROWS):
        rows = slice(b * RANK_ROWS, (b + 1) * RANK_ROWS)
        s = sel[rows, :]
        cum = jnp.dot(tri_ref[...], s, preferred_element_type=F32)
        rank_ref[rows, :] = cum - s.astype(F32) + carry
        carry = carry + cum[RANK_ROWS - 1:, :]
        tot_ref[b] = carry
    carry_ref[...] = carry


def _pool_router_kernel(x_ref, gain_ref, sc_ref, sh_ref, g_ref, band_ref, wp_ref, bp_ref, ps_ref,
                        gain2_ref, sc2_ref, sh2_ref, wr_ref, tri_ref,
                        o_ref, h_ref, gates_ref, sel_ref, rank_ref, tot_ref, hbuf_ref, carry_ref):
    _pool_kernel(x_ref, gain_ref, sc_ref, sh_ref, g_ref, band_ref, wp_ref, bp_ref, ps_ref, o_ref,
                 hbuf_ref)
    _route(o_ref[...], gain2_ref, sc2_ref, sh2_ref, wr_ref, tri_ref, h_ref, gates_ref, sel_ref, rank_ref, tot_ref,
           carry_ref)


def _pool_router(x, gain, sc1p, sh, g1, band, wp, bp, ps, gain2, sc2p, sh2, w_cat, tri):
    t, d = x.shape
    tm = POOL_ROWS
    n_sub = tm // RANK_ROWS
    vec = pl.BlockSpec((1, d), lambda i: (0, 0))
    rows = pl.BlockSpec((tm, d), lambda i: (i, 0))
    lanes = pl.BlockSpec((tm, LANES), lambda i: (i, 0))
    return pl.pallas_call(
        _pool_router_kernel,
        grid=(t // tm,),
        in_specs=[
            rows, vec, vec, vec, vec,
            pl.BlockSpec(band.shape, lambda i: (0, 0, 0)),
            pl.BlockSpec(wp.shape, lambda i: (0, 0, 0)),
            vec, vec,
            vec, vec, vec,
            pl.BlockSpec((d, 2 * LANES), lambda i: (0, 0)),
            pl.BlockSpec((RANK_ROWS, RANK_ROWS), lambda i: (0, 0)),
        ],
        out_specs=[rows, rows, lanes, lanes, lanes, pl.BlockSpec((n_sub, 1, LANES), lambda i: (i, 0, 0))],
        out_shape=[
            jax.ShapeDtypeStruct((t, d), F32),
            jax.ShapeDtypeStruct((t, d), BF16),
            jax.ShapeDtypeStruct((t, LANES), F32),
            jax.ShapeDtypeStruct((t, LANES), BF16),
            jax.ShapeDtypeStruct((t, LANES), F32),
            jax.ShapeDtypeStruct((t // RANK_ROWS, 1, LANES), F32),
        ],
        scratch_shapes=[pltpu.VMEM((POOL_HALO + tm, d), F32), pltpu.VMEM((1, LANES), F32)],
        compiler_params=_params("arbitrary"),
        name="pool_router",
    )(x, gain, sc1p, sh, g1, band, wp, bp, ps, gain2, sc2p, sh2, w_cat, tri)


def _gather_kernel(be_ref, ng_ref, clo_ref, chi_ref, h_ref, post_ref, gatest_ref, xb_ref, gs_ref, acc_ref):
    u = pl.program_id(0)
    e = be_ref[u]
    unit_rows = xb_ref.shape[0]
    n_groups = unit_rows // MOE_GATHER_ROWS
    last_chunk = h_ref.shape[0] // GATHER_CHUNK - 1
    g0 = u * n_groups

    @pl.when(ng_ref[g0] > 0)
    def _():
        acc_ref[...] = jnp.zeros_like(acc_ref)
        trips = chi_ref[g0] - clo_ref[g0]
        for q in range(1, n_groups):
            trips = jnp.maximum(trips, chi_ref[g0 + q] - clo_ref[g0 + q])

        def body(i, gsums):
            out = []
            for q in range(n_groups):
                rows = slice(q * MOE_GATHER_ROWS, (q + 1) * MOE_GATHER_ROWS)
                c = clo_ref[g0 + q] + i
                live = c < chi_ref[g0 + q]
                off = pl.multiple_of(jnp.minimum(c, last_chunk) * GATHER_CHUNK, GATHER_CHUNK)
                p = post_ref[pl.ds(e, 1), pl.ds(off, GATHER_CHUNK)]
                p = jnp.where(live, p, -2)
                row_pos = (lax.broadcasted_iota(jnp.int32, (MOE_GATHER_ROWS, GATHER_CHUNK), 0)
                           + (u * unit_rows + q * MOE_GATHER_ROWS))
                match = row_pos == p
                onehot = jnp.where(match, 1.0, 0.0).astype(BF16)
                acc_ref[rows, :] += jnp.dot(onehot, h_ref[pl.ds(off, GATHER_CHUNK), :],
                                            preferred_element_type=F32)
                gate = gatest_ref[pl.ds(e, 1), pl.ds(off, GATHER_CHUNK)]
                out.append(gsums[q] + jnp.sum(jnp.where(match, gate, 0.0), axis=1, keepdims=True))
            return tuple(out)

        zero = jnp.zeros((MOE_GATHER_ROWS, 1), F32)
        gsums = lax.fori_loop(0, trips, body, (zero,) * n_groups)
        xb_ref[...] = acc_ref[...].astype(BF16)
        for q in range(n_groups):
            gs_ref[q * MOE_GATHER_ROWS:(q + 1) * MOE_GATHER_ROWS, :] = gsums[q]

    @pl.when(ng_ref[g0] == 0)
    def _():
        xb_ref[...] = jnp.zeros_like(xb_ref)
        gs_ref[...] = jnp.zeros_like(gs_ref)


def _gather(beu, ng, clo, chi, h, post, gatest):
    t, d = h.shape
    n_units = beu.shape[0]
    grid_spec = pltpu.PrefetchScalarGridSpec(
        num_scalar_prefetch=4,
        grid=(n_units,),
        in_specs=[
            pl.BlockSpec((t, d), lambda u, *_: (0, 0), pipeline_mode=pl.Buffered(1)),
            pl.BlockSpec(post.shape, lambda u, *_: (0, 0), pipeline_mode=pl.Buffered(1)),
            pl.BlockSpec(gatest.shape, lambda u, *_: (0, 0), pipeline_mode=pl.Buffered(1)),
        ],
        out_specs=[
            pl.BlockSpec((MOE_GATHER_UNIT, d), lambda u, *_: (u, 0)),
            pl.BlockSpec((MOE_GATHER_UNIT, 1), lambda u, *_: (u, 0)),
        ],
        scratch_shapes=[pltpu.VMEM((MOE_GATHER_UNIT, d), F32)],
    )
    return pl.pallas_call(
        _gather_kernel,
        grid_spec=grid_spec,
        out_shape=[
            jax.ShapeDtypeStruct((n_units * MOE_GATHER_UNIT, d), BF16),
            jax.ShapeDtypeStruct((n_units * MOE_GATHER_UNIT, 1), F32),
        ],
        compiler_params=_params("arbitrary"),
        name="gather",
    )(beu, ng, clo, chi, h, post, gatest)


def _experts_kernel(be_ref, nv_ref, xb_ref, gs_ref, wg_ref, wu_ref, wd_ref, yb_ref,
                    acc_ref, wgb_ref, wub_ref, wdb_ref):
    b = pl.program_id(0)
    f = pl.program_id(1)
    nf = pl.num_programs(1)
    nv = nv_ref[b]
    block_rows = xb_ref.shape[0]

    @pl.when((b == 0) & (f == 0))
    def _():
        acc_ref[...] = jnp.zeros_like(acc_ref)

    def swiglu_rows(lo, n, cast):
        rows = slice(lo, lo + n)
        if cast:
            wgb_ref[...] = wg_ref[0].astype(BF16)
            wub_ref[...] = wu_ref[0].astype(BF16)
            wdb_ref[...] = wd_ref[0].astype(BF16)
        x = xb_ref[rows, :]
        a = jnp.dot(x, wgb_ref[...], preferred_element_type=F32)
        u = jnp.dot(x, wub_ref[...], preferred_element_type=F32)
        act = (_silu(a) * u).astype(BF16)
        prev = jnp.where(f > 0, acc_ref[rows, :], 0.0)
        new = prev + jnp.dot(act, wdb_ref[...], preferred_element_type=F32)
        acc_ref[rows, :] = new
        yb_ref[rows, :] = (new * gs_ref[rows, :]).astype(BF16)

    pair = 2 * MOE_DOT_ROWS
    half_group = MOE_DOT_ROWS // 2
    for p in range(block_rows // pair):
        lo = p * pair
        full = nv >= lo + pair

        @pl.when(full)
        def _():
            swiglu_rows(lo, pair, p == 0)

        for hh in range(2):
            glo = lo + hh * MOE_DOT_ROWS
            used = jnp.logical_not(full) & (nv > glo)
            tall = nv > glo + half_group
            first = p == 0 and hh == 0

            @pl.when(used & tall)
            def _():
                swiglu_rows(glo, MOE_DOT_ROWS, first)

            @pl.when(used & jnp.logical_not(tall))
            def _():
                swiglu_rows(glo, half_group, first)

    @pl.when(f == nf - 1)
    def _():
        for q in range(block_rows // half_group):
            @pl.when(nv <= q * half_group)
            def _():
                yb_ref[q * half_group:(q + 1) * half_group, :] = jnp.zeros((half_group, yb_ref.shape[1]), BF16)


def _experts(be, nv, xb, gs, wg, wu, wd):
    d = xb.shape[1]
    n_blocks = be.shape[0]
    d_ff = wg.shape[2]
    tf = MOE_FF_TILE
    nf = d_ff // tf
    bsz = MOE_BLOCK

    def ff_idx(f, nvr, b):
        return jnp.where(nvr[b] > 0, f, nf - 1)

    grid_spec = pltpu.PrefetchScalarGridSpec(
        num_scalar_prefetch=2,
        grid=(n_blocks, nf),
        in_specs=[
            pl.BlockSpec((bsz, d), lambda b, f, *_: (b, 0)),
            pl.BlockSpec((bsz, 1), lambda b, f, *_: (b, 0)),
            pl.BlockSpec((1, d, tf), lambda b, f, be_r, nv_r: (be_r[b], 0, ff_idx(f, nv_r, b))),
            pl.BlockSpec((1, d, tf), lambda b, f, be_r, nv_r: (be_r[b], 0, ff_idx(f, nv_r, b))),
            pl.BlockSpec((1, tf, d), lambda b, f, be_r, nv_r: (be_r[b], ff_idx(f, nv_r, b), 0)),
        ],
        out_specs=pl.BlockSpec((bsz, d), lambda b, f, *_: (b, 0)),
        scratch_shapes=[
            pltpu.VMEM((bsz, d), F32),
            pltpu.VMEM((d, tf), BF16), pltpu.VMEM((d, tf), BF16), pltpu.VMEM((tf, d), BF16),
        ],
    )
    return pl.pallas_call(
        _experts_kernel,
        grid_spec=grid_spec,
        out_shape=jax.ShapeDtypeStruct((n_blocks * bsz, d), BF16),
        compiler_params=_params("arbitrary", "arbitrary"),
        name="experts",
    )(be, nv, xb, gs, wg, wu, wd)


def _combine_kernel(win_ref, far_ref, x_ref, pos_ref, g_ref, fg_ref, *refs):
    yb_refs, o_ref, acc_ref = refs[:-2], refs[-2], refs[-1]
    i = pl.program_id(0)
    n_sub = x_ref.shape[0] // COMBINE_ROWS
    near = COMBINE_WINDOWS - 1
    for s in range(n_sub):
        rows = slice(s * COMBINE_ROWS, (s + 1) * COMBINE_ROWS)

        def onehot_dot(first_chunk, n_chunks):
            lane = lax.broadcasted_iota(jnp.int32, (COMBINE_ROWS, n_chunks * COMBINE_WIN), 1)
            onehots, windows = [], []
            for e in range(N_EXPERTS):
                k = (i * n_sub + s) * N_EXPERTS + e
                base = (win_ref[k] + first_chunk) * COMBINE_WIN
                rel = pos_ref[rows, e:e + 1] - base
                onehots.append(jnp.where(lane == rel, 1.0, 0.0).astype(BF16))
                w0 = COMBINE_WINDOWS * (s * N_EXPERTS + e) + first_chunk
                windows += [yb_refs[w0 + j][...] for j in range(n_chunks)]
            return jnp.dot(jnp.concatenate(onehots, axis=1), jnp.concatenate(windows, axis=0),
                           preferred_element_type=F32)

        acc_ref[rows, :] = onehot_dot(0, near)

        @pl.when(far_ref[i * n_sub + s] > 0)
        def _():
            acc_ref[rows, :] += onehot_dot(near, 1)

        xo = x_ref[rows, :] + g_ref[...] * acc_ref[rows, :]
        ms = jnp.mean(xo * xo, axis=-1, keepdims=True)
        o_ref[rows, :] = xo * lax.rsqrt(ms + EPS) * fg_ref[...]


def _combine(win, far, x, pos, g2, fgain, yb):
    t, d = x.shape
    tm = COMBINE_TILE
    n_sub = tm // COMBINE_ROWS
    n_win = yb.shape[0] // COMBINE_WIN
    vec = pl.BlockSpec((1, d), lambda i, w, f: (0, 0))

    def window(s, e, j):
        def index(i, w, f):
            chunk = jnp.minimum(w[(i * n_sub + s) * N_EXPERTS + e] + j, n_win - 1)
            if j == COMBINE_WINDOWS - 1:
                chunk = jnp.where(f[i * n_sub + s] > 0, chunk, 0)
            return chunk, 0
        return index

    yb_specs = []
    for s in range(n_sub):
        for e in range(N_EXPERTS):
            for j in range(COMBINE_WINDOWS):
                yb_specs.append(pl.BlockSpec((COMBINE_WIN, d), window(s, e, j)))
    grid_spec = pltpu.PrefetchScalarGridSpec(
        num_scalar_prefetch=2,
        grid=(t // tm,),
        in_specs=[
            pl.BlockSpec((tm, d), lambda i, w, f: (i, 0)),
            pl.BlockSpec((tm, N_EXPERTS), lambda i, w, f: (i, 0)),
            vec, vec,
        ] + yb_specs,
        out_specs=pl.BlockSpec((tm, d), lambda i, w, f: (i, 0)),
        scratch_shapes=[pltpu.VMEM((tm, d), F32)],
    )
    return pl.pallas_call(
        _combine_kernel,
        grid_spec=grid_spec,
        out_shape=jax.ShapeDtypeStruct((t, d), F32),
        compiler_params=_params("arbitrary"),
        name="combine",
    )(win, far, x, pos, g2, fgain, *([yb] * len(yb_specs)))


def _routing_tables(rank, tot, sel):
    t = rank.shape[0]
    bsz = MOE_BLOCK
    n_blocks = (2 * t) // bsz + N_EXPERTS
    cum = tot[:, 0, :N_EXPERTS].astype(jnp.int32)
    counts = cum[-1]
    padded = ((counts + bsz - 1) // bsz) * bsz
    pad_end = jnp.cumsum(padded)
    start_pad = pad_end - padded
    selected = sel[:, :N_EXPERTS] > 0
    pos = jnp.where(selected, start_pad[None, :] + rank[:, :N_EXPERTS].astype(jnp.int32), -1)

    bstart = jnp.arange(n_blocks, dtype=jnp.int32) * bsz
    be = jnp.minimum(jnp.sum(bstart[:, None] >= pad_end[None, :], axis=1), N_EXPERTS - 1).astype(jnp.int32)
    r0 = bstart - start_pad[be]
    nv = jnp.clip(counts[be] - r0, 0, bsz).astype(jnp.int32)
    n_gather = bsz // MOE_GATHER_ROWS
    goff = jnp.arange(n_gather, dtype=jnp.int32) * MOE_GATHER_ROWS
    r0g = (r0[:, None] + goff[None, :]).reshape(-1)
    ng = jnp.clip(nv[:, None] - goff[None, :], 0, MOE_GATHER_ROWS).reshape(-1)
    beg = jnp.repeat(be, n_gather)
    chunk_end = cum[:, beg].T
    chunk_start = jnp.concatenate([jnp.zeros((1, N_EXPERTS), jnp.int32), cum[:-1]], axis=0)[:, beg].T
    clo = jnp.sum(chunk_end <= r0g[:, None], axis=1).astype(jnp.int32)
    chi = jnp.sum(chunk_start < (r0g + ng)[:, None], axis=1).astype(jnp.int32)
    beu = jnp.repeat(be, bsz // MOE_GATHER_UNIT)

    big = jnp.int32(2 ** 30)
    first = jnp.min(jnp.where(pos >= 0, pos, big).reshape(t // COMBINE_ROWS, COMBINE_ROWS, N_EXPERTS), axis=1)
    win2 = jnp.where(first == big, 0, first // COMBINE_WIN).astype(jnp.int32)
    last = jnp.max(pos.reshape(t // COMBINE_ROWS, COMBINE_ROWS, N_EXPERTS), axis=1)
    far = jnp.any((last >= 0) & (last // COMBINE_WIN >= win2 + (COMBINE_WINDOWS - 1)), axis=1).astype(jnp.int32)
    return pos, be, nv, beu, ng, clo, chi, win2.reshape(-1), far


def _rotary_tables(seq, dk):
    half = dk // 2
    inv = ROPE_BASE ** (-jnp.arange(half, dtype=F32) / half)
    ang_a = (jnp.arange(seq // ROT_BLOCK) * ROT_BLOCK).astype(F32)[:, None] * inv[None, :]
    ang_b = jnp.arange(ROT_BLOCK).astype(F32)[:, None] * inv[None, :]
    return jnp.cos(ang_a), jnp.sin(ang_a), jnp.cos(ang_b), jnp.sin(ang_b)


def _decay_tables(dk, dv):
    c = RET_CHUNK
    log_gamma = jnp.log1p(-jnp.exp2(-5.0 - jnp.arange(N_HEADS, dtype=F32)))
    idx = jnp.arange(c, dtype=F32)
    row_dec = jnp.exp(log_gamma[:, None] * (idx + 1.0 - c))
    causal = (idx[:, None] >= idx[None, :]).astype(F32)
    dmask = row_dec[:, :, None] * causal[None]
    k_dec = jnp.exp(log_gamma[:, None] * (c - 1.0 - idx)) * (dk ** -0.5)
    q_dec = jnp.exp(log_gamma[:, None] * (idx + 1.0))
    chunk_dec = jnp.exp(log_gamma * c)
    half = dk // 2
    kdec_tile = jnp.tile(jnp.repeat(k_dec.T, half, axis=1), (ROW_TILE // c, 1))
    qdec_tile = jnp.broadcast_to(q_dec[:, :, None], (N_HEADS, c, dv))
    return dmask, kdec_tile, qdec_tile, chunk_dec


def _pool_bands(tm):
    lag = jnp.arange(tm)[:, None] - jnp.arange(tm)[None, :]
    return jnp.stack([((lag >= 0) & (lag < w)) for w in POOL_WINDOWS]).astype(BF16)


def kernel(x, c, ada_w, ada_b, norm_gain, ret_w_in, ret_gn_gain, ret_w_out, ffn_w_gate, ffn_w_up, ffn_w_down,
           pool_w, pool_b, pool_scale, moe_router, moe_w_gate, moe_w_up, moe_w_down, final_norm_gain):
    bsz, seq, d = x.shape
    assert bsz == 1 and ada_w.shape[0] == 2
    xt = x.reshape(seq, d)
    qk_total = d
    v_total = ret_w_out.shape[1]
    dk = qk_total // N_HEADS
    dv = v_total // N_HEADS

    mod = _modulation(c, ada_w, ada_b)
    def mods(i):
        parts = [mod[i, :, k * d:(k + 1) * d] for k in range(N_MOD)]
        sh1, sc1, g1, sh2, sc2, g2 = parts
        return sh1, 1.0 + sc1, g1, sh2, 1.0 + sc2, g2

    sh1, sc1p, g1, sh2, sc2p, g2 = mods(0)
    dmask, kdec_tile, qdec_tile, chunk_dec = _decay_tables(dk, dv)
    proj, (w_out_b, wg_b, wu_b, wd_b) = _ret_proj(
        xt, norm_gain[0, 0][None], sc1p, sh1, ret_w_in[0].astype(BF16), _rotary_tables(seq, dk), kdec_tile,
        qk_total, v_total, [ret_w_out[0], ffn_w_gate[0], ffn_w_up[0], ffn_w_down[0]])
    x1 = _retention(proj, chunk_dec, dmask, qdec_tile, ret_gn_gain[0][None], xt, g1, w_out_b, qk_total, v_total)
    x2 = _dense_ffn(x1, norm_gain[0, 1][None], sc2p, sh2, g2, wg_b, wu_b, wd_b)

    sh1, sc1p, g1, sh2, sc2p, g2 = mods(1)
    band = _pool_bands(POOL_SUB_ROWS)
    wr = jnp.pad(moe_router[0], ((0, 0), (0, LANES - N_EXPERTS)))
    wr_hi = wr.astype(BF16)
    wr_lo = (wr - wr_hi.astype(F32)).astype(BF16)
    tri = (jnp.arange(RANK_ROWS)[:, None] >= jnp.arange(RANK_ROWS)[None, :]).astype(BF16)
    x3, h4, gates, sel, rank, tot = _pool_router(
        x2, norm_gain[1, 0][None], sc1p, sh1, g1, band,
        pool_w[0].astype(BF16), pool_b[0].reshape(1, d), pool_scale[0][None],
        norm_gain[1, 1][None], sc2p, sh2, jnp.concatenate([wr_hi, wr_lo], axis=1), tri)
    pos, be, nv, beu, ng, clo, chi, win, far = _routing_tables(rank, tot, sel)
    xb, gs = _gather(beu, ng, clo, chi, h4, pos.T, gates[:, :N_EXPERTS].T)
    yb = _experts(be, nv, xb, gs, moe_w_gate[0], moe_w_up[0], moe_w_down[0])
    out = _combine(win, far, x3, pos, g2, final_norm_gain[None], yb)
    return out.reshape(bsz, seq, d)
```

```python
import functools

import jax
import jax.numpy as jnp
from jax import lax
from jax.experimental import pallas as pl
from jax.experimental.pallas import tpu as pltpu

F32 = jnp.float32
BF16 = jnp.bfloat16

EPS = 1e-6
N_HEADS = 4
RET_CHUNK = 256
ROPE_BASE = 10000.0
ROT_BLOCK = 128
POOL_WINDOWS = (2, 4, 8, 16)
POOL_HALO = 16
N_EXPERTS = 8
N_MOD = 6

VMEM_LIMIT_BYTES = 56 * 1024 * 1024

ROW_TILE = 1024
MOD_COL_TILE = 1024
PROJ_COL_TILE = 3072
DENSE_FF_TILE = 512
RET_ROWS = 1024
POOL_ROWS = 1024
POOL_SUB_ROWS = 256
RANK_ROWS = 512
MOE_BLOCK = 2048
MOE_DOT_ROWS = 512
MOE_GATHER_UNIT = 1024
MOE_GATHER_ROWS = 256
MOE_FF_TILE = 512
GATHER_CHUNK = 512
COMBINE_TILE = 256
COMBINE_ROWS = 256
COMBINE_WIN = 128
COMBINE_WINDOWS = COMBINE_ROWS // COMBINE_WIN + 1
LANES = 128


def _silu(v):
    return v / (1.0 + jnp.exp(-v))


def _norm_mod(x, gain, scale1p, shift):
    ms = jnp.mean(x * x, axis=-1, keepdims=True)
    return (x * lax.rsqrt(ms + EPS)) * (gain * scale1p) + shift


def _params(*sem):
    return pltpu.CompilerParams(dimension_semantics=sem, vmem_limit_bytes=VMEM_LIMIT_BYTES)


def _resident(shape):
    nd = len(shape)
    return pl.BlockSpec(shape, lambda *_: (0,) * nd, pipeline_mode=pl.Buffered(1))


def _mod_kernel(c_ref, w_ref, b_ref, o_ref):
    ca = _silu(c_ref[...])
    o_ref[0] = jnp.sum(ca * w_ref[0], axis=0, keepdims=True) + b_ref[0]


def _modulation(c, ada_w, ada_b):
    depth, d, n = ada_w.shape
    tn = MOD_COL_TILE
    return pl.pallas_call(
        _mod_kernel,
        grid=(depth, n // tn),
        in_specs=[
            pl.BlockSpec((d, 1), lambda i, j: (0, 0)),
            pl.BlockSpec((1, d, tn), lambda i, j: (i, 0, j)),
            pl.BlockSpec((1, 1, tn), lambda i, j: (i, 0, j)),
        ],
        out_specs=pl.BlockSpec((1, 1, tn), lambda i, j: (i, 0, j)),
        out_shape=jax.ShapeDtypeStruct((depth, 1, n), F32),
        compiler_params=_params("arbitrary", "arbitrary"),
        name="mod",
    )(c.reshape(d, 1), ada_w, ada_b.reshape(depth, 1, n))


def _proj_kernel(x_ref, gain_ref, sc_ref, sh_ref, w_ref, cosa_ref, sina_ref, cosb_ref, sinb_ref, kdec_ref, *refs,
                 n_qk_chunks, n_v_chunks, n_col_tiles, dk, n_cast):
    cast_in = refs[:n_cast]
    o_ref = refs[n_cast]
    cast_out = refs[n_cast + 1:2 * n_cast + 1]
    h_ref, cos_ref, sin_ref = refs[2 * n_cast + 1:]
    j = pl.program_id(1)

    @pl.when(j == 0)
    def _():
        h_ref[...] = _norm_mod(x_ref[...], gain_ref[...], sc_ref[...], sh_ref[...]).astype(BF16)
        for src, dst in zip(cast_in, cast_out):
            dst[...] = src[...].astype(BF16)
        cb = cosb_ref[...]
        sb = sinb_ref[...]
        for a in range(cosa_ref.shape[0]):
            ca = cosa_ref[a:a + 1, :]
            sa = sina_ref[a:a + 1, :]
            cos_ref[a * ROT_BLOCK:(a + 1) * ROT_BLOCK, :] = ca * cb - sa * sb
            sin_ref[a * ROT_BLOCK:(a + 1) * ROT_BLOCK, :] = sa * cb + ca * sb

    half = dk // 2
    n_chunks = w_ref.shape[1] // dk

    def chunk(c, kind, head):
        acc = jnp.dot(h_ref[...], w_ref[:, c * dk:(c + 1) * dk], preferred_element_type=F32)
        if kind == "v":
            o_ref[:, c * dk:(c + 1) * dk] = acc.astype(BF16)
        elif kind == "g":
            o_ref[:, c * dk:(c + 1) * dk] = _silu(acc).astype(BF16)
        else:
            cos = cos_ref[...]
            sin = sin_ref[...]
            t1 = acc[:, :half]
            t2 = acc[:, half:]
            o1 = t1 * cos - t2 * sin
            o2 = t1 * sin + t2 * cos
            if kind == "k":
                s = kdec_ref[:, head * half:(head + 1) * half]
                o1 = o1 * s
                o2 = o2 * s
            o_ref[:, c * dk:c * dk + half] = o1.astype(BF16)
            o_ref[:, c * dk + half:(c + 1) * dk] = o2.astype(BF16)

    def kind_of(gc):
        if gc < n_qk_chunks:
            return "q", gc
        if gc < 2 * n_qk_chunks:
            return "k", gc - n_qk_chunks
        if gc < 2 * n_qk_chunks + n_v_chunks:
            return "v", 0
        return "g", 0

    for jj in range(n_col_tiles):
        @pl.when(j == jj)
        def _():
            for c in range(n_chunks):
                chunk(c, *kind_of(jj * n_chunks + c))


def _ret_proj(x, gain, sc1p, sh, w_in, rot, kdec, qk_total, v_total, to_cast):
    t, d = x.shape
    n = w_in.shape[1]
    tm, tn = ROW_TILE, PROJ_COL_TILE
    n_row_tiles = t // tm
    dk = qk_total // N_HEADS
    half = dk // 2
    cosa, sina, cosb, sinb = rot
    kern = functools.partial(_proj_kernel, n_qk_chunks=qk_total // dk, n_v_chunks=v_total // dk,
                             n_col_tiles=n // tn, dk=dk, n_cast=len(to_cast))
    vec = pl.BlockSpec((1, d), lambda i, j: (0, 0))
    rot_a = pl.BlockSpec((tm // ROT_BLOCK, half), lambda i, j: (i, 0))
    rot_b = pl.BlockSpec((ROT_BLOCK, half), lambda i, j: (0, 0))
    cast_specs = [pl.BlockSpec((w.shape[0] // n_row_tiles, w.shape[1]), lambda i, j: (i, 0)) for w in to_cast]
    outs = pl.pallas_call(
        kern,
        grid=(n_row_tiles, n // tn),
        in_specs=[
            pl.BlockSpec((tm, d), lambda i, j: (i, 0)),
            vec, vec, vec,
            pl.BlockSpec((d, tn), lambda i, j: (0, j)),
            rot_a, rot_a, rot_b, rot_b,
            pl.BlockSpec((tm, N_HEADS * half), lambda i, j: (0, 0)),
        ] + cast_specs,
        out_specs=[pl.BlockSpec((tm, tn), lambda i, j: (i, j))] + cast_specs,
        out_shape=[jax.ShapeDtypeStruct((t, n), BF16)] + [jax.ShapeDtypeStruct(w.shape, BF16) for w in to_cast],
        scratch_shapes=[pltpu.VMEM((tm, d), BF16), pltpu.VMEM((tm, half), F32), pltpu.VMEM((tm, half), F32)],
        compiler_params=_params("arbitrary", "arbitrary"),
        name="proj",
    )(x, gain, sc1p, sh, w_in, cosa, sina, cosb, sinb, kdec, *to_cast)
    return outs[0], outs[1:]


def _ret_kernel(cdec_ref, q_ref, k_ref, v_ref, sg_ref, dmask_ref, qdec_ref, gn_ref, x_ref, g_ref, wo_ref, o_ref,
                state_ref, y_ref, *, dk, dv):
    @pl.when(pl.program_id(0) == 0)
    def _():
        state_ref[...] = jnp.zeros_like(state_ref)

    rows_per_step = q_ref.shape[0]
    for n in range(rows_per_step // RET_CHUNK):
        rows = slice(n * RET_CHUNK, (n + 1) * RET_CHUNK)
        if n > 0:
            prev = slice((n - 1) * RET_CHUNK, n * RET_CHUNK)
            o_ref[prev, :] = x_ref[prev, :] + g_ref[...] * jnp.dot(y_ref[prev, :], wo_ref[...],
                                                                   preferred_element_type=F32)
        for h in range(N_HEADS):
            qcols = slice(h * dk, (h + 1) * dk)
            vcols = slice(h * dv, (h + 1) * dv)
            q = q_ref[rows, qcols]
            ks = k_ref[rows, qcols]
            v = v_ref[rows, vcols]
            s = lax.dot_general(q, ks, (((1,), (1,)), ((), ())), preferred_element_type=F32)
            a = (s * dmask_ref[h]).astype(BF16)
            intra = jnp.dot(a, v, preferred_element_type=F32)
            st = state_ref[h]
            cross = jnp.dot(q, st.astype(BF16), preferred_element_type=F32)
            o = intra + cross * qdec_ref[h]
            kv = lax.dot_general(ks, v, (((0,), (0,)), ((), ())), preferred_element_type=F32)
            state_ref[h] = st * cdec_ref[h] + kv
            mu = jnp.mean(o, axis=-1, keepdims=True)
            dlt = o - mu
            var = jnp.mean(dlt * dlt, axis=-1, keepdims=True)
            on = dlt * lax.rsqrt(var + EPS)
            y = on * gn_ref[:, vcols] * sg_ref[rows, vcols].astype(F32)
            y_ref[rows, vcols] = y.astype(BF16)
    last = slice(rows_per_step - RET_CHUNK, rows_per_step)
    o_ref[last, :] = x_ref[last, :] + g_ref[...] * jnp.dot(y_ref[last, :], wo_ref[...],
                                                           preferred_element_type=F32)


def _retention(proj, cdec, dmask, qdec, gn_gain, x, g1, w_out, qk_total, v_total):
    t, d = x.shape
    r = RET_ROWS
    dk = qk_total // N_HEADS
    dv = v_total // N_HEADS
    assert v_total == 2 * qk_total
    kern = functools.partial(_ret_kernel, dk=dk, dv=dv)
    return pl.pallas_call(
        kern,
        grid=(t // r,),
        in_specs=[
            pl.BlockSpec(memory_space=pltpu.SMEM),
            pl.BlockSpec((r, qk_total), lambda i: (i, 0)),
            pl.BlockSpec((r, qk_total), lambda i: (i, 1)),
            pl.BlockSpec((r, v_total), lambda i: (i, 1)),
            pl.BlockSpec((r, v_total), lambda i: (i, 2)),
            _resident((N_HEADS, RET_CHUNK, RET_CHUNK)),
            _resident((N_HEADS, RET_CHUNK, dv)),
            pl.BlockSpec((1, v_total), lambda i: (0, 0)),
            pl.BlockSpec((r, d), lambda i: (i, 0)),
            pl.BlockSpec((1, d), lambda i: (0, 0)),
            _resident((v_total, d)),
        ],
        out_specs=pl.BlockSpec((r, d), lambda i: (i, 0)),
        out_shape=jax.ShapeDtypeStruct((t, d), F32),
        scratch_shapes=[pltpu.VMEM((N_HEADS, dk, dv), F32), pltpu.VMEM((r, v_total), BF16)],
        compiler_params=_params("arbitrary"),
        name="ret",
    )(cdec, proj, proj, proj, proj, dmask, qdec, gn_gain, x, g1, w_out)


def _ffn_kernel(x_ref, gain_ref, sc_ref, sh_ref, g_ref, wg_ref, wu_ref, wd_ref, o_ref, *, ff_tile):
    x = x_ref[...]
    h = _norm_mod(x, gain_ref[...], sc_ref[...], sh_ref[...]).astype(BF16)
    d_ff = wg_ref.shape[1]
    acc = jnp.zeros(x.shape, F32)
    for lo in range(0, d_ff, ff_tile):
        hi = min(lo + ff_tile, d_ff)
        a = jnp.dot(h, wg_ref[:, lo:hi], preferred_element_type=F32)
        b = jnp.dot(h, wu_ref[:, lo:hi], preferred_element_type=F32)
        act = (_silu(a) * b).astype(BF16)
        acc = acc + jnp.dot(act, wd_ref[lo:hi, :], preferred_element_type=F32)
    o_ref[...] = x + g_ref[...] * acc


def _dense_ffn(x, gain, sc1p, sh, g2, wg, wu, wd):
    t, d = x.shape
    d_ff = wg.shape[1]
    tm = ROW_TILE
    vec = pl.BlockSpec((1, d), lambda i: (0, 0))
    return pl.pallas_call(
        functools.partial(_ffn_kernel, ff_tile=DENSE_FF_TILE),
        grid=(t // tm,),
        in_specs=[
            pl.BlockSpec((tm, d), lambda i: (i, 0)),
            vec, vec, vec, vec,
            _resident((d, d_ff)), _resident((d, d_ff)), _resident((d_ff, d)),
        ],
        out_specs=pl.BlockSpec((tm, d), lambda i: (i, 0)),
        out_shape=jax.ShapeDtypeStruct((t, d), F32),
        compiler_params=_params("arbitrary"),
        name="ffn",
    )(x, gain, sc1p, sh, g2, wg, wu, wd)


def _pool_kernel(x_ref, gain_ref, sc_ref, sh_ref, g_ref, band_ref, wp_ref, bp_ref, ps_ref, o_ref,
                 hbuf_ref):
    i = pl.program_id(0)
    tm = x_ref.shape[0]
    gw = wp_ref.shape[1]

    @pl.when(i == 0)
    def _():
        hbuf_ref[0:POOL_HALO, :] = jnp.zeros((POOL_HALO, hbuf_ref.shape[1]), F32)

    @pl.when(i > 0)
    def _():
        hbuf_ref[0:POOL_HALO, :] = hbuf_ref[tm:tm + POOL_HALO, :]

    hbuf_ref[POOL_HALO:, :] = _norm_mod(x_ref[...], gain_ref[...], sc_ref[...], sh_ref[...])

    def split(v):
        hi = v.astype(BF16)
        return hi, (v - hi.astype(F32)).astype(BF16)

    sub = band_ref.shape[1]
    for s in range(tm // sub):
        rows = slice(s * sub, (s + 1) * sub)
        lo = POOL_HALO + s * sub
        x = x_ref[rows, :]
        h = hbuf_ref[lo:lo + sub, :]
        h_hi, h_lo = split(h)
        t1 = (lax.broadcasted_iota(jnp.int32, (sub, 1), 0) + (i * tm + s * sub + 1)).astype(F32)
        first_rows = lax.broadcasted_iota(jnp.int32, (POOL_HALO, 1), 0)
        for g, w in enumerate(POOL_WINDOWS):
            cols = slice(g * gw, (g + 1) * gw)
            band = band_ref[g]
            win = (jnp.dot(band, h_hi[:, cols], preferred_element_type=F32)
                   + jnp.dot(band, h_lo[:, cols], preferred_element_type=F32))
            head = jnp.zeros((POOL_HALO, gw), F32)
            for k in range(1, w):
                prev_row = hbuf_ref[lo - k:lo - k + 1, cols]
                head = head + jnp.where(first_rows < w - k, prev_row, 0.0)
            win = jnp.concatenate([win[:POOL_HALO] + head, win[POOL_HALO:]], axis=0)
            pooled = win * (1.0 / jnp.minimum(t1, float(w))) - h[:, cols]
            y = jnp.dot(pooled.astype(BF16), wp_ref[g], preferred_element_type=F32) + bp_ref[:, cols]
            o_ref[rows, cols] = x[:, cols] + (g_ref[:, cols] * ps_ref[:, cols]) * y


def _route(x, gain_ref, sc_ref, sh_ref, w_ref, tri_ref, h_ref, gates_ref, sel_ref, rank_ref, tot_ref, carry_ref):
    h = _norm_mod(x, gain_ref[...], sc_ref[...], sh_ref[...])
    h_hi = h.astype(BF16)
    h_ref[...] = h_hi
    h_lo = (h - h_hi.astype(F32)).astype(BF16)
    w = w_ref[...]
    p_hi = jnp.dot(h_hi, w, preferred_element_type=F32)
    p_lo = jnp.dot(h_lo, w, preferred_element_type=F32)
    logits = p_hi[:, :LANES] + p_hi[:, LANES:] + p_lo[:, :LANES] + p_lo[:, LANES:]
    lane = lax.broadcasted_iota(jnp.int32, logits.shape, 1).astype(F32)
    neg = jnp.float32(-jnp.inf)
    lg = jnp.where(lane < float(N_EXPERTS), logits, neg)
    m0 = jnp.max(lg, axis=-1, keepdims=True)
    i0 = jnp.min(jnp.where(lg == m0, lane, float(LANES)), axis=-1, keepdims=True)
    lg1 = jnp.where(lane == i0, neg, lg)
    m1 = jnp.max(lg1, axis=-1, keepdims=True)
    i1 = jnp.min(jnp.where(lg1 == m1, lane, float(LANES)), axis=-1, keepdims=True)
    e1 = jnp.exp(m1 - m0)
    den = 1.0 + e1
    is0 = lane == i0
    is1 = lane == i1
    gates_ref[...] = jnp.where(is0, 1.0 / den, jnp.where(is1, e1 / den, 0.0))
    sel = jnp.where(is0 | is1, 1.0, 0.0).astype(BF16)
    sel_ref[...] = sel

    @pl.when(pl.program_id(0) == 0)
    def _():
        carry_ref[...] = jnp.zeros_like(carry_ref)

    carry = carry_ref[...]
    for b in range(sel.shape[0] // RANK_ROWS):
        rows = slice(b * RANK_ROWS, (b + 1) * RANK_ROWS)
        s = sel[rows, :]
        cum = jnp.dot(tri_ref[...], s, preferred_element_type=F32)
        rank_ref[rows, :] = cum - s.astype(F32) + carry
        carry = carry + cum[RANK_ROWS - 1:, :]
        tot_ref[b] = carry
    carry_ref[...] = carry


def _pool_router_kernel(x_ref, gain_ref, sc_ref, sh_ref, g_ref, band_ref, wp_ref, bp_ref, ps_ref,
                        gain2_ref, sc2_ref, sh2_ref, wr_ref, tri_ref,
                        o_ref, h_ref, gates_ref, sel_ref, rank_ref, tot_ref, hbuf_ref, carry_ref):
    _pool_kernel(x_ref, gain_ref, sc_ref, sh_ref, g_ref, band_ref, wp_ref, bp_ref, ps_ref, o_ref,
                 hbuf_ref)
    _route(o_ref[...], gain2_ref, sc2_ref, sh2_ref, wr_ref, tri_ref, h_ref, gates_ref, sel_ref, rank_ref, tot_ref,
           carry_ref)


def _pool_router(x, gain, sc1p, sh, g1, band, wp, bp, ps, gain2, sc2p, sh2, w_cat, tri):
    t, d = x.shape
    tm = POOL_ROWS
    n_sub = tm // RANK_ROWS
    vec = pl.BlockSpec((1, d), lambda i: (0, 0))
    rows = pl.BlockSpec((tm, d), lambda i: (i, 0))
    lanes = pl.BlockSpec((tm, LANES), lambda i: (i, 0))
    return pl.pallas_call(
        _pool_router_kernel,
        grid=(t // tm,),
        in_specs=[
            rows, vec, vec, vec, vec,
            pl.BlockSpec(band.shape, lambda i: (0, 0, 0)),
            pl.BlockSpec(wp.shape, lambda i: (0, 0, 0)),
            vec, vec,
            vec, vec, vec,
            pl.BlockSpec((d, 2 * LANES), lambda i: (0, 0)),
            pl.BlockSpec((RANK_ROWS, RANK_ROWS), lambda i: (0, 0)),
        ],
        out_specs=[rows, rows, lanes, lanes, lanes, pl.BlockSpec((n_sub, 1, LANES), lambda i: (i, 0, 0))],
        out_shape=[
            jax.ShapeDtypeStruct((t, d), F32),
            jax.ShapeDtypeStruct((t, d), BF16),
            jax.ShapeDtypeStruct((t, LANES), F32),
            jax.ShapeDtypeStruct((t, LANES), BF16),
            jax.ShapeDtypeStruct((t, LANES), F32),
            jax.ShapeDtypeStruct((t // RANK_ROWS, 1, LANES), F32),
        ],
        scratch_shapes=[pltpu.VMEM((POOL_HALO + tm, d), F32), pltpu.VMEM((1, LANES), F32)],
        compiler_params=_params("arbitrary"),
        name="pool_router",
    )(x, gain, sc1p, sh, g1, band, wp, bp, ps, gain2, sc2p, sh2, w_cat, tri)


def _gather_kernel(be_ref, ng_ref, clo_ref, chi_ref, h_ref, post_ref, gatest_ref, xb_ref, gs_ref, acc_ref):
    u = pl.program_id(0)
    e = be_ref[u]
    unit_rows = xb_ref.shape[0]
    n_groups = unit_rows // MOE_GATHER_ROWS
    last_chunk = h_ref.shape[0] // GATHER_CHUNK - 1
    g0 = u * n_groups

    @pl.when(ng_ref[g0] > 0)
    def _():
        acc_ref[...] = jnp.zeros_like(acc_ref)
        trips = chi_ref[g0] - clo_ref[g0]
        for q in range(1, n_groups):
            trips = jnp.maximum(trips, chi_ref[g0 + q] - clo_ref[g0 + q])

        def body(i, gsums):
            out = []
            for q in range(n_groups):
                rows = slice(q * MOE_GATHER_ROWS, (q + 1) * MOE_GATHER_ROWS)
                c = clo_ref[g0 + q] + i
                live = c < chi_ref[g0 + q]
                off = pl.multiple_of(jnp.minimum(c, last_chunk) * GATHER_CHUNK, GATHER_CHUNK)
                p = post_ref[pl.ds(e, 1), pl.ds(off, GATHER_CHUNK)]
                p = jnp.where(live, p, -2)
                row_pos = (lax.broadcasted_iota(jnp.int32, (MOE_GATHER_ROWS, GATHER_CHUNK), 0)
                           + (u * unit_rows + q * MOE_GATHER_ROWS))
                match = row_pos == p
                onehot = jnp.where(match, 1.0, 0.0).astype(BF16)
                acc_ref[rows, :] += jnp.dot(onehot, h_ref[pl.ds(off, GATHER_CHUNK), :],
                                            preferred_element_type=F32)
                gate = gatest_ref[pl.ds(e, 1), pl.ds(off, GATHER_CHUNK)]
                out.append(gsums[q] + jnp.sum(jnp.where(match, gate, 0.0), axis=1, keepdims=True))
            return tuple(out)

        zero = jnp.zeros((MOE_GATHER_ROWS, 1), F32)
        gsums = lax.fori_loop(0, trips, body, (zero,) * n_groups)
        xb_ref[...] = acc_ref[...].astype(BF16)
        for q in range(n_groups):
            gs_ref[q * MOE_GATHER_ROWS:(q + 1) * MOE_GATHER_ROWS, :] = gsums[q]

    @pl.when(ng_ref[g0] == 0)
    def _():
        xb_ref[...] = jnp.zeros_like(xb_ref)
        gs_ref[...] = jnp.zeros_like(gs_ref)


def _gather(beu, ng, clo, chi, h, post, gatest):
    t, d = h.shape
    n_units = beu.shape[0]
    grid_spec = pltpu.PrefetchScalarGridSpec(
        num_scalar_prefetch=4,
        grid=(n_units,),
        in_specs=[
            pl.BlockSpec((t, d), lambda u, *_: (0, 0), pipeline_mode=pl.Buffered(1)),
            pl.BlockSpec(post.shape, lambda u, *_: (0, 0), pipeline_mode=pl.Buffered(1)),
            pl.BlockSpec(gatest.shape, lambda u, *_: (0, 0), pipeline_mode=pl.Buffered(1)),
        ],
        out_specs=[
            pl.BlockSpec((MOE_GATHER_UNIT, d), lambda u, *_: (u, 0)),
            pl.BlockSpec((MOE_GATHER_UNIT, 1), lambda u, *_: (u, 0)),
        ],
        scratch_shapes=[pltpu.VMEM((MOE_GATHER_UNIT, d), F32)],
    )
    return pl.pallas_call(
        _gather_kernel,
        grid_spec=grid_spec,
        out_shape=[
            jax.ShapeDtypeStruct((n_units * MOE_GATHER_UNIT, d), BF16),
            jax.ShapeDtypeStruct((n_units * MOE_GATHER_UNIT, 1), F32),
        ],
        compiler_params=_params("arbitrary"),
        name="gather",
    )(beu, ng, clo, chi, h, post, gatest)


def _experts_kernel(be_ref, nv_ref, xb_ref, gs_ref, wg_ref, wu_ref, wd_ref, yb_ref,
                    acc_ref, wgb_ref, wub_ref, wdb_ref):
    b = pl.program_id(0)
    f = pl.program_id(1)
    nf = pl.num_programs(1)
    nv = nv_ref[b]
    block_rows = xb_ref.shape[0]

    @pl.when((b == 0) & (f == 0))
    def _():
        acc_ref[...] = jnp.zeros_like(acc_ref)

    def swiglu_rows(lo, n, cast):
        rows = slice(lo, lo + n)
        if cast:
            wgb_ref[...] = wg_ref[0].astype(BF16)
            wub_ref[...] = wu_ref[0].astype(BF16)
            wdb_ref[...] = wd_ref[0].astype(BF16)
        x = xb_ref[rows, :]
        a = jnp.dot(x, wgb_ref[...], preferred_element_type=F32)
        u = jnp.dot(x, wub_ref[...], preferred_element_type=F32)
        act = (_silu(a) * u).astype(BF16)
        prev = jnp.where(f > 0, acc_ref[rows, :], 0.0)
        new = prev + jnp.dot(act, wdb_ref[...], preferred_element_type=F32)
        acc_ref[rows, :] = new
        yb_ref[rows, :] = (new * gs_ref[rows, :]).astype(BF16)

    pair = 2 * MOE_DOT_ROWS
    half_group = MOE_DOT_ROWS // 2
    for p in range(block_rows // pair):
        lo = p * pair
        full = nv >= lo + pair

        @pl.when(full)
        def _():
            swiglu_rows(lo, pair, p == 0)

        for hh in range(2):
            glo = lo + hh * MOE_DOT_ROWS
            used = jnp.logical_not(full) & (nv > glo)
            tall = nv > glo + half_group
            first = p == 0 and hh == 0

            @pl.when(used & tall)
            def _():
                swiglu_rows(glo, MOE_DOT_ROWS, first)

            @pl.when(used & jnp.logical_not(tall))
            def _():
                swiglu_rows(glo, half_group, first)

    @pl.when(f == nf - 1)
    def _():
        for q in range(block_rows // half_group):
            @pl.when(nv <= q * half_group)
            def _():
                yb_ref[q * half_group:(q + 1) * half_group, :] = jnp.zeros((half_group, yb_ref.shape[1]), BF16)


def _experts(be, nv, xb, gs, wg, wu, wd):
    d = xb.shape[1]
    n_blocks = be.shape[0]
    d_ff = wg.shape[2]
    tf = MOE_FF_TILE
    nf = d_ff // tf
    bsz = MOE_BLOCK

    def ff_idx(f, nvr, b):
        return jnp.where(nvr[b] > 0, f, nf - 1)

    grid_spec = pltpu.PrefetchScalarGridSpec(
        num_scalar_prefetch=2,
        grid=(n_blocks, nf),
        in_specs=[
            pl.BlockSpec((bsz, d), lambda b, f, *_: (b, 0)),
            pl.BlockSpec((bsz, 1), lambda b, f, *_: (b, 0)),
            pl.BlockSpec((1, d, tf), lambda b, f, be_r, nv_r: (be_r[b], 0, ff_idx(f, nv_r, b))),
            pl.BlockSpec((1, d, tf), lambda b, f, be_r, nv_r: (be_r[b], 0, ff_idx(f, nv_r, b))),
            pl.BlockSpec((1, tf, d), lambda b, f, be_r, nv_r: (be_r[b], ff_idx(f, nv_r, b), 0)),
        ],
        out_specs=pl.BlockSpec((bsz, d), lambda b, f, *_: (b, 0)),
        scratch_shapes=[
            pltpu.VMEM((bsz, d), F32),
            pltpu.VMEM((d, tf), BF16), pltpu.VMEM((d, tf), BF16), pltpu.VMEM((tf, d), BF16),
        ],
    )
    return pl.pallas_call(
        _experts_kernel,
        grid_spec=grid_spec,
        out_shape=jax.ShapeDtypeStruct((n_blocks * bsz, d), BF16),
        compiler_params=_params("arbitrary", "arbitrary"),
        name="experts",
    )(be, nv, xb, gs, wg, wu, wd)


def _combine_kernel(win_ref, far_ref, x_ref, pos_ref, g_ref, fg_ref, *refs):
    yb_refs, o_ref, acc_ref = refs[:-2], refs[-2], refs[-1]
    i = pl.program_id(0)
    n_sub = x_ref.shape[0] // COMBINE_ROWS
    near = COMBINE_WINDOWS - 1
    for s in range(n_sub):
        rows = slice(s * COMBINE_ROWS, (s + 1) * COMBINE_ROWS)

        def onehot_dot(first_chunk, n_chunks):
            lane = lax.broadcasted_iota(jnp.int32, (COMBINE_ROWS, n_chunks * COMBINE_WIN), 1)
            onehots, windows = [], []
            for e in range(N_EXPERTS):
                k = (i * n_sub + s) * N_EXPERTS + e
                base = (win_ref[k] + first_chunk) * COMBINE_WIN
                rel = pos_ref[rows, e:e + 1] - base
                onehots.append(jnp.where(lane == rel, 1.0, 0.0).astype(BF16))
                w0 = COMBINE_WINDOWS * (s * N_EXPERTS + e) + first_chunk
                windows += [yb_refs[w0 + j][...] for j in range(n_chunks)]
            return jnp.dot(jnp.concatenate(onehots, axis=1), jnp.concatenate(windows, axis=0),
                           preferred_element_type=F32)

        acc_ref[rows, :] = onehot_dot(0, near)

        @pl.when(far_ref[i * n_sub + s] > 0)
        def _():
            acc_ref[rows, :] += onehot_dot(near, 1)

        xo = x_ref[rows, :] + g_ref[...] * acc_ref[rows, :]
        ms = jnp.mean(xo * xo, axis=-1, keepdims=True)
        o_ref[rows, :] = xo * lax.rsqrt(ms + EPS) * fg_ref[...]


def _combine(win, far, x, pos, g2, fgain, yb):
    t, d = x.shape
    tm = COMBINE_TILE
    n_sub = tm // COMBINE_ROWS
    n_win = yb.shape[0] // COMBINE_WIN
    vec = pl.BlockSpec((1, d), lambda i, w, f: (0, 0))

    def window(s, e, j):
        def index(i, w, f):
            chunk = jnp.minimum(w[(i * n_sub + s) * N_EXPERTS + e] + j, n_win - 1)
            if j == COMBINE_WINDOWS - 1:
                chunk = jnp.where(f[i * n_sub + s] > 0, chunk, 0)
            return chunk, 0
        return index

    yb_specs = []
    for s in range(n_sub):
        for e in range(N_EXPERTS):
            for j in range(COMBINE_WINDOWS):
                yb_specs.append(pl.BlockSpec((COMBINE_WIN, d), window(s, e, j)))
    grid_spec = pltpu.PrefetchScalarGridSpec(
        num_scalar_prefetch=2,
        grid=(t // tm,),
        in_specs=[
            pl.BlockSpec((tm, d), lambda i, w, f: (i, 0)),
            pl.BlockSpec((tm, N_EXPERTS), lambda i, w, f: (i, 0)),
            vec, vec,
        ] + yb_specs,
        out_specs=pl.BlockSpec((tm, d), lambda i, w, f: (i, 0)),
        scratch_shapes=[pltpu.VMEM((tm, d), F32)],
    )
    return pl.pallas_call(
        _combine_kernel,
        grid_spec=grid_spec,
        out_shape=jax.ShapeDtypeStruct((t, d), F32),
        compiler_params=_params("arbitrary"),
        name="combine",
    )(win, far, x, pos, g2, fgain, *([yb] * len(yb_specs)))


def _routing_tables(rank, tot, sel):
    t = rank.shape[0]
    bsz = MOE_BLOCK
    n_blocks = (2 * t) // bsz + N_EXPERTS
    cum = tot[:, 0, :N_EXPERTS].astype(jnp.int32)
    counts = cum[-1]
    padded = ((counts + bsz - 1) // bsz) * bsz
    pad_end = jnp.cumsum(padded)
    start_pad = pad_end - padded
    selected = sel[:, :N_EXPERTS] > 0
    pos = jnp.where(selected, start_pad[None, :] + rank[:, :N_EXPERTS].astype(jnp.int32), -1)

    bstart = jnp.arange(n_blocks, dtype=jnp.int32) * bsz
    be = jnp.minimum(jnp.sum(bstart[:, None] >= pad_end[None, :], axis=1), N_EXPERTS - 1).astype(jnp.int32)
    r0 = bstart - start_pad[be]
    nv = jnp.clip(counts[be] - r0, 0, bsz).astype(jnp.int32)
    n_gather = bsz // MOE_GATHER_ROWS
    goff = jnp.arange(n_gather, dtype=jnp.int32) * MOE_GATHER_ROWS
    r0g = (r0[:, None] + goff[None, :]).reshape(-1)
    ng = jnp.clip(nv[:, None] - goff[None, :], 0, MOE_GATHER_ROWS).reshape(-1)
    beg = jnp.repeat(be, n_gather)
    chunk_end = cum[:, beg].T
    chunk_start = jnp.concatenate([jnp.zeros((1, N_EXPERTS), jnp.int32), cum[:-1]], axis=0)[:, beg].T
    clo = jnp.sum(chunk_end <= r0g[:, None], axis=1).astype(jnp.int32)
    chi = jnp.sum(chunk_start < (r0g + ng)[:, None], axis=1).astype(jnp.int32)
    beu = jnp.repeat(be, bsz // MOE_GATHER_UNIT)

    big = jnp.int32(2 ** 30)
    first = jnp.min(jnp.where(pos >= 0, pos, big).reshape(t // COMBINE_ROWS, COMBINE_ROWS, N_EXPERTS), axis=1)
    win2 = jnp.where(first == big, 0, first // COMBINE_WIN).astype(jnp.int32)
    last = jnp.max(pos.reshape(t // COMBINE_ROWS, COMBINE_ROWS, N_EXPERTS), axis=1)
    far = jnp.any((last >= 0) & (last // COMBINE_WIN >= win2 + (COMBINE_WINDOWS - 1)), axis=1).astype(jnp.int32)
    return pos, be, nv, beu, ng, clo, chi, win2.reshape(-1), far


def _rotary_tables(seq, dk):
    half = dk // 2
    inv = ROPE_BASE ** (-jnp.arange(half, dtype=F32) / half)
    ang_a = (jnp.arange(seq // ROT_BLOCK) * ROT_BLOCK).astype(F32)[:, None] * inv[None, :]
    ang_b = jnp.arange(ROT_BLOCK).astype(F32)[:, None] * inv[None, :]
    return jnp.cos(ang_a), jnp.sin(ang_a), jnp.cos(ang_b), jnp.sin(ang_b)


def _decay_tables(dk, dv):
    c = RET_CHUNK
    log_gamma = jnp.log1p(-jnp.exp2(-5.0 - jnp.arange(N_HEADS, dtype=F32)))
    idx = jnp.arange(c, dtype=F32)
    row_dec = jnp.exp(log_gamma[:, None] * (idx + 1.0 - c))
    causal = (idx[:, None] >= idx[None, :]).astype(F32)
    dmask = row_dec[:, :, None] * causal[None]
    k_dec = jnp.exp(log_gamma[:, None] * (c - 1.0 - idx)) * (dk ** -0.5)
    q_dec = jnp.exp(log_gamma[:, None] * (idx + 1.0))
    chunk_dec = jnp.exp(log_gamma * c)
    half = dk // 2
    kdec_tile = jnp.tile(jnp.repeat(k_dec.T, half, axis=1), (ROW_TILE // c, 1))
    qdec_tile = jnp.broadcast_to(q_dec[:, :, None], (N_HEADS, c, dv))
    return dmask, kdec_tile, qdec_tile, chunk_dec


def _pool_bands(tm):
    lag = jnp.arange(tm)[:, None] - jnp.arange(tm)[None, :]
    return jnp.stack([((lag >= 0) & (lag < w)) for w in POOL_WINDOWS]).astype(BF16)


def kernel(x, c, ada_w, ada_b, norm_gain, ret_w_in, ret_gn_gain, ret_w_out, ffn_w_gate, ffn_w_up, ffn_w_down,
           pool_w, pool_b, pool_scale, moe_router, moe_w_gate, moe_w_up, moe_w_down, final_norm_gain):
    bsz, seq, d = x.shape
    assert bsz == 1 and ada_w.shape[0] == 2
    xt = x.reshape(seq, d)
    qk_total = d
    v_total = ret_w_out.shape[1]
    dk = qk_total // N_HEADS
    dv = v_total // N_HEADS

    mod = _modulation(c, ada_w, ada_b)
    def mods(i):
        parts = [mod[i, :, k * d:(k + 1) * d] for k in range(N_MOD)]
        sh1, sc1, g1, sh2, sc2, g2 = parts
        return sh1, 1.0 + sc1, g1, sh2, 1.0 + sc2, g2

    sh1, sc1p, g1, sh2, sc2p, g2 = mods(0)
    dmask, kdec_tile, qdec_tile, chunk_dec = _decay_tables(dk, dv)
    proj, (w_out_b, wg_b, wu_b, wd_b) = _ret_proj(
        xt, norm_gain[0, 0][None], sc1p, sh1, ret_w_in[0].astype(BF16), _rotary_tables(seq, dk), kdec_tile,
        qk_total, v_total, [ret_w_out[0], ffn_w_gate[0], ffn_w_up[0], ffn_w_down[0]])
    x1 = _retention(proj, chunk_dec, dmask, qdec_tile, ret_gn_gain[0][None], xt, g1, w_out_b, qk_total, v_total)
    x2 = _dense_ffn(x1, norm_gain[0, 1][None], sc2p, sh2, g2, wg_b, wu_b, wd_b)

    sh1, sc1p, g1, sh2, sc2p, g2 = mods(1)
    band = _pool_bands(POOL_SUB_ROWS)
    wr = jnp.pad(moe_router[0], ((0, 0), (0, LANES - N_EXPERTS)))
    wr_hi = wr.astype(BF16)
    wr_lo = (wr - wr_hi.astype(F32)).astype(BF16)
    tri = (jnp.arange(RANK_ROWS)[:, None] >= jnp.arange(RANK_ROWS)[None, :]).astype(BF16)
    x3, h4, gates, sel, rank, tot = _pool_router(
        x2, norm_gain[1, 0][None], sc1p, sh1, g1, band,
        pool_w[0].astype(BF16), pool_b[0].reshape(1, d), pool_scale[0][None],
        norm_gain[1, 1][None], sc2p, sh2, jnp.concatenate([wr_hi, wr_lo], axis=1), tri)
    pos, be, nv, beu, ng, clo, chi, win, far = _routing_tables(rank, tot, sel)
    xb, gs = _gather(beu, ng, clo, chi, h4, pos.T, gates[:, :N_EXPERTS].T)
    yb = _experts(be, nv, xb, gs, moe_w_gate[0], moe_w_up[0], moe_w_down[0])
    out = _combine(win, far, x3, pos, g2, final_norm_gain[None], yb)
    return out.reshape(bsz, seq, d)
```

```python
import functools

import jax
import jax.numpy as jnp
from jax import lax
from jax.experimental import pallas as pl
from jax.experimental.pallas import tpu as pltpu

F32 = jnp.float32
BF16 = jnp.bfloat16

EPS = 1e-6
N_HEADS = 4
RET_CHUNK = 256
ROPE_BASE = 10000.0
ROT_BLOCK = 128
POOL_WINDOWS = (2, 4, 8, 16)
POOL_HALO = 16
N_EXPERTS = 8
N_MOD = 6

VMEM_LIMIT_BYTES = 56 * 1024 * 1024

ROW_TILE = 1024
MOD_COL_TILE = 1024
PROJ_COL_TILE = 3072
DENSE_FF_TILE = 512
RET_ROWS = 1024
POOL_ROWS = 1024
POOL_SUB_ROWS = 256
RANK_ROWS = 256
MOE_BLOCK = 2048
MOE_DOT_ROWS = 512
MOE_GATHER_UNIT = 1024
MOE_GATHER_ROWS = 256
MOE_FF_TILE = 512
GATHER_CHUNK = 256
COMBINE_TILE = 256
COMBINE_ROWS = 256
COMBINE_WIN = 128
COMBINE_WINDOWS = COMBINE_ROWS // COMBINE_WIN + 1
LANES = 128


def _silu(v):
    return v / (1.0 + jnp.exp(-v))


def _norm_mod(x, gain, scale1p, shift):
    ms = jnp.mean(x * x, axis=-1, keepdims=True)
    return (x * lax.rsqrt(ms + EPS)) * (gain * scale1p) + shift


def _params(*sem):
    return pltpu.CompilerParams(dimension_semantics=sem, vmem_limit_bytes=VMEM_LIMIT_BYTES)


def _resident(shape):
    nd = len(shape)
    return pl.BlockSpec(shape, lambda *_: (0,) * nd, pipeline_mode=pl.Buffered(1))


def _mod_kernel(c_ref, w_ref, b_ref, o_ref):
    ca = _silu(c_ref[...])
    o_ref[0] = jnp.sum(ca * w_ref[0], axis=0, keepdims=True) + b_ref[0]


def _modulation(c, ada_w, ada_b):
    depth, d, n = ada_w.shape
    tn = MOD_COL_TILE
    return pl.pallas_call(
        _mod_kernel,
        grid=(depth, n // tn),
        in_specs=[
            pl.BlockSpec((d, 1), lambda i, j: (0, 0)),
            pl.BlockSpec((1, d, tn), lambda i, j: (i, 0, j)),
            pl.BlockSpec((1, 1, tn), lambda i, j: (i, 0, j)),
        ],
        out_specs=pl.BlockSpec((1, 1, tn), lambda i, j: (i, 0, j)),
        out_shape=jax.ShapeDtypeStruct((depth, 1, n), F32),
        compiler_params=_params("arbitrary", "arbitrary"),
        name="mod",
    )(c.reshape(d, 1), ada_w, ada_b.reshape(depth, 1, n))


def _proj_kernel(x_ref, gain_ref, sc_ref, sh_ref, w_ref, cosa_ref, sina_ref, cosb_ref, sinb_ref, kdec_ref, *refs,
                 n_qk_chunks, n_v_chunks, n_col_tiles, dk, n_cast):
    cast_in = refs[:n_cast]
    o_ref = refs[n_cast]
    cast_out = refs[n_cast + 1:2 * n_cast + 1]
    h_ref, cos_ref, sin_ref = refs[2 * n_cast + 1:]
    j = pl.program_id(1)

    @pl.when(j == 0)
    def _():
        h_ref[...] = _norm_mod(x_ref[...], gain_ref[...], sc_ref[...], sh_ref[...]).astype(BF16)
        for src, dst in zip(cast_in, cast_out):
            dst[...] = src[...].astype(BF16)
        cb = cosb_ref[...]
        sb = sinb_ref[...]
        for a in range(cosa_ref.shape[0]):
            ca = cosa_ref[a:a + 1, :]
            sa = sina_ref[a:a + 1, :]
            cos_ref[a * ROT_BLOCK:(a + 1) * ROT_BLOCK, :] = ca * cb - sa * sb
            sin_ref[a * ROT_BLOCK:(a + 1) * ROT_BLOCK, :] = sa * cb + ca * sb

    half = dk // 2
    n_chunks = w_ref.shape[1] // dk

    def chunk(c, kind, head):
        acc = jnp.dot(h_ref[...], w_ref[:, c * dk:(c + 1) * dk], preferred_element_type=F32)
        if kind == "v":
            o_ref[:, c * dk:(c + 1) * dk] = acc.astype(BF16)
        elif kind == "g":
            o_ref[:, c * dk:(c + 1) * dk] = _silu(acc).astype(BF16)
        else:
            cos = cos_ref[...]
            sin = sin_ref[...]
            t1 = acc[:, :half]
            t2 = acc[:, half:]
            o1 = t1 * cos - t2 * sin
            o2 = t1 * sin + t2 * cos
            if kind == "k":
                s = kdec_ref[:, head * half:(head + 1) * half]
                o1 = o1 * s
                o2 = o2 * s
            o_ref[:, c * dk:c * dk + half] = o1.astype(BF16)
            o_ref[:, c * dk + half:(c + 1) * dk] = o2.astype(BF16)

    def kind_of(gc):
        if gc < n_qk_chunks:
            return "q", gc
        if gc < 2 * n_qk_chunks:
            return "k", gc - n_qk_chunks
        if gc < 2 * n_qk_chunks + n_v_chunks:
            return "v", 0
        return "g", 0

    for jj in range(n_col_tiles):
        @pl.when(j == jj)
        def _():
            for c in range(n_chunks):
                chunk(c, *kind_of(jj * n_chunks + c))


def _ret_proj(x, gain, sc1p, sh, w_in, rot, kdec, qk_total, v_total, to_cast):
    t, d = x.shape
    n = w_in.shape[1]
    tm, tn = ROW_TILE, PROJ_COL_TILE
    n_row_tiles = t // tm
    dk = qk_total // N_HEADS
    half = dk // 2
    cosa, sina, cosb, sinb = rot
    kern = functools.partial(_proj_kernel, n_qk_chunks=qk_total // dk, n_v_chunks=v_total // dk,
                             n_col_tiles=n // tn, dk=dk, n_cast=len(to_cast))
    vec = pl.BlockSpec((1, d), lambda i, j: (0, 0))
    rot_a = pl.BlockSpec((tm // ROT_BLOCK, half), lambda i, j: (i, 0))
    rot_b = pl.BlockSpec((ROT_BLOCK, half), lambda i, j: (0, 0))
    cast_specs = [pl.BlockSpec((w.shape[0] // n_row_tiles, w.shape[1]), lambda i, j: (i, 0)) for w in to_cast]
    outs = pl.pallas_call(
        kern,
        grid=(n_row_tiles, n // tn),
        in_specs=[
            pl.BlockSpec((tm, d), lambda i, j: (i, 0)),
            vec, vec, vec,
            pl.BlockSpec((d, tn), lambda i, j: (0, j)),
            rot_a, rot_a, rot_b, rot_b,
            pl.BlockSpec((tm, N_HEADS * half), lambda i, j: (0, 0)),
        ] + cast_specs,
        out_specs=[pl.BlockSpec((tm, tn), lambda i, j: (i, j))] + cast_specs,
        out_shape=[jax.ShapeDtypeStruct((t, n), BF16)] + [jax.ShapeDtypeStruct(w.shape, BF16) for w in to_cast],
        scratch_shapes=[pltpu.VMEM((tm, d), BF16), pltpu.VMEM((tm, half), F32), pltpu.VMEM((tm, half), F32)],
        compiler_params=_params("arbitrary", "arbitrary"),
        name="proj",
    )(x, gain, sc1p, sh, w_in, cosa, sina, cosb, sinb, kdec, *to_cast)
    return outs[0], outs[1:]


def _ret_kernel(cdec_ref, q_ref, k_ref, v_ref, sg_ref, dmask_ref, qdec_ref, gn_ref, x_ref, g_ref, wo_ref, o_ref,
                state_ref, y_ref, *, dk, dv):
    @pl.when(pl.program_id(0) == 0)
    def _():
        state_ref[...] = jnp.zeros_like(state_ref)

    rows_per_step = q_ref.shape[0]
    for n in range(rows_per_step // RET_CHUNK):
        rows = slice(n * RET_CHUNK, (n + 1) * RET_CHUNK)
        if n > 0:
            prev = slice((n - 1) * RET_CHUNK, n * RET_CHUNK)
            o_ref[prev, :] = x_ref[prev, :] + g_ref[...] * jnp.dot(y_ref[prev, :], wo_ref[...],
                                                                   preferred_element_type=F32)
        for h in range(N_HEADS):
            qcols = slice(h * dk, (h + 1) * dk)
            vcols = slice(h * dv, (h + 1) * dv)
            q = q_ref[rows, qcols]
            ks = k_ref[rows, qcols]
            v = v_ref[rows, vcols]
            s = lax.dot_general(q, ks, (((1,), (1,)), ((), ())), preferred_element_type=F32)
            a = (s * dmask_ref[h]).astype(BF16)
            intra = jnp.dot(a, v, preferred_element_type=F32)
            st = state_ref[h]
            cross = jnp.dot(q, st.astype(BF16), preferred_element_type=F32)
            o = intra + cross * qdec_ref[h]
            kv = lax.dot_general(ks, v, (((0,), (0,)), ((), ())), preferred_element_type=F32)
            state_ref[h] = st * cdec_ref[h] + kv
            mu = jnp.mean(o, axis=-1, keepdims=True)
            dlt = o - mu
            var = jnp.mean(dlt * dlt, axis=-1, keepdims=True)
            on = dlt * lax.rsqrt(var + EPS)
            y = on * gn_ref[:, vcols] * sg_ref[rows, vcols].astype(F32)
            y_ref[rows, vcols] = y.astype(BF16)
    last = slice(rows_per_step - RET_CHUNK, rows_per_step)
    o_ref[last, :] = x_ref[last, :] + g_ref[...] * jnp.dot(y_ref[last, :], wo_ref[...],
                                                           preferred_element_type=F32)


def _retention(proj, cdec, dmask, qdec, gn_gain, x, g1, w_out, qk_total, v_total):
    t, d = x.shape
    r = RET_ROWS
    dk = qk_total // N_HEADS
    dv = v_total // N_HEADS
    assert v_total == 2 * qk_total
    kern = functools.partial(_ret_kernel, dk=dk, dv=dv)
    return pl.pallas_call(
        kern,
        grid=(t // r,),
        in_specs=[
            pl.BlockSpec(memory_space=pltpu.SMEM),
            pl.BlockSpec((r, qk_total), lambda i: (i, 0)),
            pl.BlockSpec((r, qk_total), lambda i: (i, 1)),
            pl.BlockSpec((r, v_total), lambda i: (i, 1)),
            pl.BlockSpec((r, v_total), lambda i: (i, 2)),
            _resident((N_HEADS, RET_CHUNK, RET_CHUNK)),
            _resident((N_HEADS, RET_CHUNK, dv)),
            pl.BlockSpec((1, v_total), lambda i: (0, 0)),
            pl.BlockSpec((r, d), lambda i: (i, 0)),
            pl.BlockSpec((1, d), lambda i: (0, 0)),
            _resident((v_total, d)),
        ],
        out_specs=pl.BlockSpec((r, d), lambda i: (i, 0)),
        out_shape=jax.ShapeDtypeStruct((t, d), F32),
        scratch_shapes=[pltpu.VMEM((N_HEADS, dk, dv), F32), pltpu.VMEM((r, v_total), BF16)],
        compiler_params=_params("arbitrary"),
        name="ret",
    )(cdec, proj, proj, proj, proj, dmask, qdec, gn_gain, x, g1, w_out)


def _ffn_kernel(x_ref, gain_ref, sc_ref, sh_ref, g_ref, wg_ref, wu_ref, wd_ref, o_ref, *, ff_tile):
    x = x_ref[...]
    h = _norm_mod(x, gain_ref[...], sc_ref[...], sh_ref[...]).astype(BF16)
    d_ff = wg_ref.shape[1]
    acc = jnp.zeros(x.shape, F32)
    for lo in range(0, d_ff, ff_tile):
        hi = min(lo + ff_tile, d_ff)
        a = jnp.dot(h, wg_ref[:, lo:hi], preferred_element_type=F32)
        b = jnp.dot(h, wu_ref[:, lo:hi], preferred_element_type=F32)
        act = (_silu(a) * b).astype(BF16)
        acc = acc + jnp.dot(act, wd_ref[lo:hi, :], preferred_element_type=F32)
    o_ref[...] = x + g_ref[...] * acc


def _dense_ffn(x, gain, sc1p, sh, g2, wg, wu, wd):
    t, d = x.shape
    d_ff = wg.shape[1]
    tm = ROW_TILE
    vec = pl.BlockSpec((1, d), lambda i: (0, 0))
    return pl.pallas_call(
        functools.partial(_ffn_kernel, ff_tile=DENSE_FF_TILE),
        grid=(t // tm,),
        in_specs=[
            pl.BlockSpec((tm, d), lambda i: (i, 0)),
            vec, vec, vec, vec,
            _resident((d, d_ff)), _resident((d, d_ff)), _resident((d_ff, d)),
        ],
        out_specs=pl.BlockSpec((tm, d), lambda i: (i, 0)),
        out_shape=jax.ShapeDtypeStruct((t, d), F32),
        compiler_params=_params("arbitrary"),
        name="ffn",
    )(x, gain, sc1p, sh, g2, wg, wu, wd)


def _pool_kernel(x_ref, gain_ref, sc_ref, sh_ref, g_ref, band_ref, wp_ref, bp_ref, ps_ref, o_ref,
                 hbuf_ref):
    i = pl.program_id(0)
    tm = x_ref.shape[0]
    gw = wp_ref.shape[1]

    @pl.when(i == 0)
    def _():
        hbuf_ref[0:POOL_HALO, :] = jnp.zeros((POOL_HALO, hbuf_ref.shape[1]), F32)

    @pl.when(i > 0)
    def _():
        hbuf_ref[0:POOL_HALO, :] = hbuf_ref[tm:tm + POOL_HALO, :]

    hbuf_ref[POOL_HALO:, :] = _norm_mod(x_ref[...], gain_ref[...], sc_ref[...], sh_ref[...])

    def split(v):
        hi = v.astype(BF16)
        return hi, (v - hi.astype(F32)).astype(BF16)

    sub = band_ref.shape[1]
    for s in range(tm // sub):
        rows = slice(s * sub, (s + 1) * sub)
        lo = POOL_HALO + s * sub
        x = x_ref[rows, :]
        h = hbuf_ref[lo:lo + sub, :]
        h_hi, h_lo = split(h)
        t1 = (lax.broadcasted_iota(jnp.int32, (sub, 1), 0) + (i * tm + s * sub + 1)).astype(F32)
        first_rows = lax.broadcasted_iota(jnp.int32, (POOL_HALO, 1), 0)
        for g, w in enumerate(POOL_WINDOWS):
            cols = slice(g * gw, (g + 1) * gw)
            band = band_ref[g]
            win = (jnp.dot(band, h_hi[:, cols], preferred_element_type=F32)
                   + jnp.dot(band, h_lo[:, cols], preferred_element_type=F32))
            head = jnp.zeros((POOL_HALO, gw), F32)
            for k in range(1, w):
                prev_row = hbuf_ref[lo - k:lo - k + 1, cols]
                head = head + jnp.where(first_rows < w - k, prev_row, 0.0)
            win = jnp.concatenate([win[:POOL_HALO] + head, win[POOL_HALO:]], axis=0)
            pooled = win * (1.0 / jnp.minimum(t1, float(w))) - h[:, cols]
            y = jnp.dot(pooled.astype(BF16), wp_ref[g], preferred_element_type=F32) + bp_ref[:, cols]
            o_ref[rows, cols] = x[:, cols] + (g_ref[:, cols] * ps_ref[:, cols]) * y


def _route(x, gain_ref, sc_ref, sh_ref, w_ref, tri_ref, h_ref, gates_ref, sel_ref, rank_ref, tot_ref, carry_ref):
    h = _norm_mod(x, gain_ref[...], sc_ref[...], sh_ref[...])
    h_hi = h.astype(BF16)
    h_ref[...] = h_hi
    h_lo = (h - h_hi.astype(F32)).astype(BF16)
    w = w_ref[...]
    p_hi = jnp.dot(h_hi, w, preferred_element_type=F32)
    p_lo = jnp.dot(h_lo, w, preferred_element_type=F32)
    logits = p_hi[:, :LANES] + p_hi[:, LANES:] + p_lo[:, :LANES] + p_lo[:, LANES:]
    lane = lax.broadcasted_iota(jnp.int32, logits.shape, 1).astype(F32)
    neg = jnp.float32(-jnp.inf)
    lg = jnp.where(lane < float(N_EXPERTS), logits, neg)
    m0 = jnp.max(lg, axis=-1, keepdims=True)
    i0 = jnp.min(jnp.where(lg == m0, lane, float(LANES)), axis=-1, keepdims=True)
    lg1 = jnp.where(lane == i0, neg, lg)
    m1 = jnp.max(lg1, axis=-1, keepdims=True)
    i1 = jnp.min(jnp.where(lg1 == m1, lane, float(LANES)), axis=-1, keepdims=True)
    e1 = jnp.exp(m1 - m0)
    den = 1.0 + e1
    is0 = lane == i0
    is1 = lane == i1
    gates_ref[...] = jnp.where(is0, 1.0 / den, jnp.where(is1, e1 / den, 0.0))
    sel = jnp.where(is0 | is1, 1.0, 0.0).astype(BF16)
    sel_ref[...] = sel

    @pl.when(pl.program_id(0) == 0)
    def _():
        carry_ref[...] = jnp.zeros_like(carry_ref)

    carry = carry_ref[...]
    for b in range(sel.shape[0] // RANK_ROWS):
        rows = slice(b * RANK_ROWS, (b + 1) * RANK_ROWS)
        s = sel[rows, :]
        cum = jnp.dot(tri_ref[...], s, preferred_element_type=F32)
        rank_ref[rows, :] = cum - s.astype(F32) + carry
        carry = carry + cum[RANK_ROWS - 1:, :]
        tot_ref[b] = carry
    carry_ref[...] = carry


def _pool_router_kernel(x_ref, gain_ref, sc_ref, sh_ref, g_ref, band_ref, wp_ref, bp_ref, ps_ref,
                        gain2_ref, sc2_ref, sh2_ref, wr_ref, tri_ref,
                        o_ref, h_ref, gates_ref, sel_ref, rank_ref, tot_ref, hbuf_ref, carry_ref):
    _pool_kernel(x_ref, gain_ref, sc_ref, sh_ref, g_ref, band_ref, wp_ref, bp_ref, ps_ref, o_ref,
                 hbuf_ref)
    _route(o_ref[...], gain2_ref, sc2_ref, sh2_ref, wr_ref, tri_ref, h_ref, gates_ref, sel_ref, rank_ref, tot_ref,
           carry_ref)


def _pool_router(x, gain, sc1p, sh, g1, band, wp, bp, ps, gain2, sc2p, sh2, w_cat, tri):
    t, d = x.shape
    tm = POOL_ROWS
    n_sub = tm // RANK_ROWS
    vec = pl.BlockSpec((1, d), lambda i: (0, 0))
    rows = pl.BlockSpec((tm, d), lambda i: (i, 0))
    lanes = pl.BlockSpec((tm, LANES), lambda i: (i, 0))
    return pl.pallas_call(
        _pool_router_kernel,
        grid=(t // tm,),
        in_specs=[
            rows, vec, vec, vec, vec,
            pl.BlockSpec(band.shape, lambda i: (0, 0, 0)),
            pl.BlockSpec(wp.shape, lambda i: (0, 0, 0)),
            vec, vec,
            vec, vec, vec,
            pl.BlockSpec((d, 2 * LANES), lambda i: (0, 0)),
            pl.BlockSpec((RANK_ROWS, RANK_ROWS), lambda i: (0, 0)),
        ],
        out_specs=[rows, rows, lanes, lanes, lanes, pl.BlockSpec((n_sub, 1, LANES), lambda i: (i, 0, 0))],
        out_shape=[
            jax.ShapeDtypeStruct((t, d), F32),
            jax.ShapeDtypeStruct((t, d), BF16),
            jax.ShapeDtypeStruct((t, LANES), F32),
            jax.ShapeDtypeStruct((t, LANES), BF16),
            jax.ShapeDtypeStruct((t, LANES), F32),
            jax.ShapeDtypeStruct((t // RANK_ROWS, 1, LANES), F32),
        ],
        scratch_shapes=[pltpu.VMEM((POOL_HALO + tm, d), F32), pltpu.VMEM((1, LANES), F32)],
        compiler_params=_params("arbitrary"),
        name="pool_router",
    )(x, gain, sc1p, sh, g1, band, wp, bp, ps, gain2, sc2p, sh2, w_cat, tri)


def _gather_kernel(be_ref, ng_ref, clo_ref, chi_ref, h_ref, post_ref, gatest_ref, xb_ref, gs_ref, acc_ref):
    u = pl.program_id(0)
    e = be_ref[u]
    unit_rows = xb_ref.shape[0]
    n_groups = unit_rows // MOE_GATHER_ROWS
    last_chunk = h_ref.shape[0] // GATHER_CHUNK - 1
    g0 = u * n_groups

    @pl.when(ng_ref[g0] > 0)
    def _():
        acc_ref[...] = jnp.zeros_like(acc_ref)
        trips = chi_ref[g0] - clo_ref[g0]
        for q in range(1, n_groups):
            trips = jnp.maximum(trips, chi_ref[g0 + q] - clo_ref[g0 + q])

        def body(i, gsums):
            out = []
            for q in range(n_groups):
                rows = slice(q * MOE_GATHER_ROWS, (q + 1) * MOE_GATHER_ROWS)
                c = clo_ref[g0 + q] + i
                live = c < chi_ref[g0 + q]
                off = pl.multiple_of(jnp.minimum(c, last_chunk) * GATHER_CHUNK, GATHER_CHUNK)
                p = post_ref[pl.ds(e, 1), pl.ds(off, GATHER_CHUNK)]
                p = jnp.where(live, p, -2)
                row_pos = (lax.broadcasted_iota(jnp.int32, (MOE_GATHER_ROWS, GATHER_CHUNK), 0)
                           + (u * unit_rows + q * MOE_GATHER_ROWS))
                match = row_pos == p
                onehot = jnp.where(match, 1.0, 0.0).astype(BF16)
                acc_ref[rows, :] += jnp.dot(onehot, h_ref[pl.ds(off, GATHER_CHUNK), :],
                                            preferred_element_type=F32)
                gate = gatest_ref[pl.ds(e, 1), pl.ds(off, GATHER_CHUNK)]
                out.append(gsums[q] + jnp.sum(jnp.where(match, gate, 0.0), axis=1, keepdims=True))
            return tuple(out)

        zero = jnp.zeros((MOE_GATHER_ROWS, 1), F32)
        gsums = lax.fori_loop(0, trips, body, (zero,) * n_groups)
        xb_ref[...] = acc_ref[...].astype(BF16)
        for q in range(n_groups):
            gs_ref[q * MOE_GATHER_ROWS:(q + 1) * MOE_GATHER_ROWS, :] = gsums[q]

    @pl.when(ng_ref[g0] == 0)
    def _():
        xb_ref[...] = jnp.zeros_like(xb_ref)
        gs_ref[...] = jnp.zeros_like(gs_ref)


def _gather(beu, ng, clo, chi, h, post, gatest):
    t, d = h.shape
    n_units = beu.shape[0]
    grid_spec = pltpu.PrefetchScalarGridSpec(
        num_scalar_prefetch=4,
        grid=(n_units,),
        in_specs=[
            pl.BlockSpec((t, d), lambda u, *_: (0, 0), pipeline_mode=pl.Buffered(1)),
            pl.BlockSpec(post.shape, lambda u, *_: (0, 0), pipeline_mode=pl.Buffered(1)),
            pl.BlockSpec(gatest.shape, lambda u, *_: (0, 0), pipeline_mode=pl.Buffered(1)),
        ],
        out_specs=[
            pl.BlockSpec((MOE_GATHER_UNIT, d), lambda u, *_: (u, 0)),
            pl.BlockSpec((MOE_GATHER_UNIT, 1), lambda u, *_: (u, 0)),
        ],
        scratch_shapes=[pltpu.VMEM((MOE_GATHER_UNIT, d), F32)],
    )
    return pl.pallas_call(
        _gather_kernel,
        grid_spec=grid_spec,
        out_shape=[
            jax.ShapeDtypeStruct((n_units * MOE_GATHER_UNIT, d), BF16),
            jax.ShapeDtypeStruct((n_units * MOE_GATHER_UNIT, 1), F32),
        ],
        compiler_params=_params("arbitrary"),
        name="gather",
    )(beu, ng, clo, chi, h, post, gatest)


def _experts_kernel(be_ref, nv_ref, xb_ref, gs_ref, wg_ref, wu_ref, wd_ref, yb_ref,
                    acc_ref, wgb_ref, wub_ref, wdb_ref):
    b = pl.program_id(0)
    f = pl.program_id(1)
    nf = pl.num_programs(1)
    nv = nv_ref[b]
    block_rows = xb_ref.shape[0]

    @pl.when((b == 0) & (f == 0))
    def _():
        acc_ref[...] = jnp.zeros_like(acc_ref)

    def swiglu_rows(lo, n, cast):
        rows = slice(lo, lo + n)
        if cast:
            wgb_ref[...] = wg_ref[0].astype(BF16)
            wub_ref[...] = wu_ref[0].astype(BF16)
            wdb_ref[...] = wd_ref[0].astype(BF16)
        x = xb_ref[rows, :]
        a = jnp.dot(x, wgb_ref[...], preferred_element_type=F32)
        u = jnp.dot(x, wub_ref[...], preferred_element_type=F32)
        act = (_silu(a) * u).astype(BF16)
        prev = jnp.where(f > 0, acc_ref[rows, :], 0.0)
        new = prev + jnp.dot(act, wdb_ref[...], preferred_element_type=F32)
        acc_ref[rows, :] = new
        yb_ref[rows, :] = (new * gs_ref[rows, :]).astype(BF16)

    pair = 2 * MOE_DOT_ROWS
    half_group = MOE_DOT_ROWS // 2
    for p in range(block_rows // pair):
        lo = p * pair
        full = nv >= lo + pair

        @pl.when(full)
        def _():
            swiglu_rows(lo, pair, p == 0)

        for hh in range(2):
            glo = lo + hh * MOE_DOT_ROWS
            used = jnp.logical_not(full) & (nv > glo)
            tall = nv > glo + half_group
            first = p == 0 and hh == 0

            @pl.when(used & tall)
            def _():
                swiglu_rows(glo, MOE_DOT_ROWS, first)

            @pl.when(used & jnp.logical_not(tall))
            def _():
                swiglu_rows(glo, half_group, first)

    @pl.when(f == nf - 1)
    def _():
        for q in range(block_rows // half_group):
            @pl.when(nv <= q * half_group)
            def _():
                yb_ref[q * half_group:(q + 1) * half_group, :] = jnp.zeros((half_group, yb_ref.shape[1]), BF16)


def _experts(be, nv, xb, gs, wg, wu, wd):
    d = xb.shape[1]
    n_blocks = be.shape[0]
    d_ff = wg.shape[2]
    tf = MOE_FF_TILE
    nf = d_ff // tf
    bsz = MOE_BLOCK

    def ff_idx(f, nvr, b):
        return jnp.where(nvr[b] > 0, f, nf - 1)

    grid_spec = pltpu.PrefetchScalarGridSpec(
        num_scalar_prefetch=2,
        grid=(n_blocks, nf),
        in_specs=[
            pl.BlockSpec((bsz, d), lambda b, f, *_: (b, 0)),
            pl.BlockSpec((bsz, 1), lambda b, f, *_: (b, 0)),
            pl.BlockSpec((1, d, tf), lambda b, f, be_r, nv_r: (be_r[b], 0, ff_idx(f, nv_r, b))),
            pl.BlockSpec((1, d, tf), lambda b, f, be_r, nv_r: (be_r[b], 0, ff_idx(f, nv_r, b))),
            pl.BlockSpec((1, tf, d), lambda b, f, be_r, nv_r: (be_r[b], ff_idx(f, nv_r, b), 0)),
        ],
        out_specs=pl.BlockSpec((bsz, d), lambda b, f, *_: (b, 0)),
        scratch_shapes=[
            pltpu.VMEM((bsz, d), F32),
            pltpu.VMEM((d, tf), BF16), pltpu.VMEM((d, tf), BF16), pltpu.VMEM((tf, d), BF16),
        ],
    )
    return pl.pallas_call(
        _experts_kernel,
        grid_spec=grid_spec,
        out_shape=jax.ShapeDtypeStruct((n_blocks * bsz, d), BF16),
        compiler_params=_params("arbitrary", "arbitrary"),
        name="experts",
    )(be, nv, xb, gs, wg, wu, wd)


def _combine_kernel(win_ref, far_ref, x_ref, pos_ref, g_ref, fg_ref, *refs):
    yb_refs, o_ref, acc_ref = refs[:-2], refs[-2], refs[-1]
    i = pl.program_id(0)
    n_sub = x_ref.shape[0] // COMBINE_ROWS
    near = COMBINE_WINDOWS - 1
    for s in range(n_sub):
        rows = slice(s * COMBINE_ROWS, (s + 1) * COMBINE_ROWS)

        def onehot_dot(first_chunk, n_chunks):
            lane = lax.broadcasted_iota(jnp.int32, (COMBINE_ROWS, n_chunks * COMBINE_WIN), 1)
            onehots, windows = [], []
            for e in range(N_EXPERTS):
                k = (i * n_sub + s) * N_EXPERTS + e
                base = (win_ref[k] + first_chunk) * COMBINE_WIN
                rel = pos_ref[rows, e:e + 1] - base
                onehots.append(jnp.where(lane == rel, 1.0, 0.0).astype(BF16))
                w0 = COMBINE_WINDOWS * (s * N_EXPERTS + e) + first_chunk
                windows += [yb_refs[w0 + j][...] for j in range(n_chunks)]
            return jnp.dot(jnp.concatenate(onehots, axis=1), jnp.concatenate(windows, axis=0),
                           preferred_element_type=F32)

        acc_ref[rows, :] = onehot_dot(0, near)

        @pl.when(far_ref[i * n_sub + s] > 0)
        def _():
            acc_ref[rows, :] += onehot_dot(near, 1)

        xo = x_ref[rows, :] + g_ref[...] * acc_ref[rows, :]
        ms = jnp.mean(xo * xo, axis=-1, keepdims=True)
        o_ref[rows, :] = xo * lax.rsqrt(ms + EPS) * fg_ref[...]


def _combine(win, far, x, pos, g2, fgain, yb):
    t, d = x.shape
    tm = COMBINE_TILE
    n_sub = tm // COMBINE_ROWS
    n_win = yb.shape[0] // COMBINE_WIN
    vec = pl.BlockSpec((1, d), lambda i, w, f: (0, 0))

    def window(s, e, j):
        def index(i, w, f):
            chunk = jnp.minimum(w[(i * n_sub + s) * N_EXPERTS + e] + j, n_win - 1)
            if j == COMBINE_WINDOWS - 1:
                chunk = jnp.where(f[i * n_sub + s] > 0, chunk, 0)
            return chunk, 0
        return index

    yb_specs = []
    for s in range(n_sub):
        for e in range(N_EXPERTS):
            for j in range(COMBINE_WINDOWS):
                yb_specs.append(pl.BlockSpec((COMBINE_WIN, d), window(s, e, j)))
    grid_spec = pltpu.PrefetchScalarGridSpec(
        num_scalar_prefetch=2,
        grid=(t // tm,),
        in_specs=[
            pl.BlockSpec((tm, d), lambda i, w, f: (i, 0)),
            pl.BlockSpec((tm, N_EXPERTS), lambda i, w, f: (i, 0)),
            vec, vec,
        ] + yb_specs,
        out_specs=pl.BlockSpec((tm, d), lambda i, w, f: (i, 0)),
        scratch_shapes=[pltpu.VMEM((tm, d), F32)],
    )
    return pl.pallas_call(
        _combine_kernel,
        grid_spec=grid_spec,
        out_shape=jax.ShapeDtypeStruct((t, d), F32),
        compiler_params=_params("arbitrary"),
        name="combine",
    )(win, far, x, pos, g2, fgain, *([yb] * len(yb_specs)))


def _routing_tables(rank, tot, sel):
    t = rank.shape[0]
    bsz = MOE_BLOCK
    n_blocks = (2 * t) // bsz + N_EXPERTS
    cum = tot[:, 0, :N_EXPERTS].astype(jnp.int32)
    counts = cum[-1]
    padded = ((counts + bsz - 1) // bsz) * bsz
    pad_end = jnp.cumsum(padded)
    start_pad = pad_end - padded
    selected = sel[:, :N_EXPERTS] > 0
    pos = jnp.where(selected, start_pad[None, :] + rank[:, :N_EXPERTS].astype(jnp.int32), -1)

    bstart = jnp.arange(n_blocks, dtype=jnp.int32) * bsz
    be = jnp.minimum(jnp.sum(bstart[:, None] >= pad_end[None, :], axis=1), N_EXPERTS - 1).astype(jnp.int32)
    r0 = bstart - start_pad[be]
    nv = jnp.clip(counts[be] - r0, 0, bsz).astype(jnp.int32)
    n_gather = bsz // MOE_GATHER_ROWS
    goff = jnp.arange(n_gather, dtype=jnp.int32) * MOE_GATHER_ROWS
    r0g = (r0[:, None] + goff[None, :]).reshape(-1)
    ng = jnp.clip(nv[:, None] - goff[None, :], 0, MOE_GATHER_ROWS).reshape(-1)
    beg = jnp.repeat(be, n_gather)
    chunk_end = cum[:, beg].T
    chunk_start = jnp.concatenate([jnp.zeros((1, N_EXPERTS), jnp.int32), cum[:-1]], axis=0)[:, beg].T
    clo = jnp.sum(chunk_end <= r0g[:, None], axis=1).astype(jnp.int32)
    chi = jnp.sum(chunk_start < (r0g + ng)[:, None], axis=1).astype(jnp.int32)
    beu = jnp.repeat(be, bsz // MOE_GATHER_UNIT)

    big = jnp.int32(2 ** 30)
    first = jnp.min(jnp.where(pos >= 0, pos, big).reshape(t // COMBINE_ROWS, COMBINE_ROWS, N_EXPERTS), axis=1)
    win2 = jnp.where(first == big, 0, first // COMBINE_WIN).astype(jnp.int32)
    last = jnp.max(pos.reshape(t // COMBINE_ROWS, COMBINE_ROWS, N_EXPERTS), axis=1)
    far = jnp.any((last >= 0) & (last // COMBINE_WIN >= win2 + (COMBINE_WINDOWS - 1)), axis=1).astype(jnp.int32)
    return pos, be, nv, beu, ng, clo, chi, win2.reshape(-1), far


def _rotary_tables(seq, dk):
    half = dk // 2
    inv = ROPE_BASE ** (-jnp.arange(half, dtype=F32) / half)
    ang_a = (jnp.arange(seq // ROT_BLOCK) * ROT_BLOCK).astype(F32)[:, None] * inv[None, :]
    ang_b = jnp.arange(ROT_BLOCK).astype(F32)[:, None] * inv[None, :]
    return jnp.cos(ang_a), jnp.sin(ang_a), jnp.cos(ang_b), jnp.sin(ang_b)


def _decay_tables(dk, dv):
    c = RET_CHUNK
    log_gamma = jnp.log1p(-jnp.exp2(-5.0 - jnp.arange(N_HEADS, dtype=F32)))
    idx = jnp.arange(c, dtype=F32)
    row_dec = jnp.exp(log_gamma[:, None] * (idx + 1.0 - c))
    causal = (idx[:, None] >= idx[None, :]).astype(F32)
    dmask = row_dec[:, :, None] * causal[None]
    k_dec = jnp.exp(log_gamma[:, None] * (c - 1.0 - idx)) * (dk ** -0.5)
    q_dec = jnp.exp(log_gamma[:, None] * (idx + 1.0))
    chunk_dec = jnp.exp(log_gamma * c)
    half = dk // 2
    kdec_tile = jnp.tile(jnp.repeat(k_dec.T, half, axis=1), (ROW_TILE // c, 1))
    qdec_tile = jnp.broadcast_to(q_dec[:, :, None], (N_HEADS, c, dv))
    return dmask, kdec_tile, qdec_tile, chunk_dec


def _pool_bands(tm):
    lag = jnp.arange(tm)[:, None] - jnp.arange(tm)[None, :]
    return jnp.stack([((lag >= 0) & (lag < w)) for w in POOL_WINDOWS]).astype(BF16)


def kernel(x, c, ada_w, ada_b, norm_gain, ret_w_in, ret_gn_gain, ret_w_out, ffn_w_gate, ffn_w_up, ffn_w_down,
           pool_w, pool_b, pool_scale, moe_router, moe_w_gate, moe_w_up, moe_w_down, final_norm_gain):
    bsz, seq, d = x.shape
    assert bsz == 1 and ada_w.shape[0] == 2
    xt = x.reshape(seq, d)
    qk_total = d
    v_total = ret_w_out.shape[1]
    dk = qk_total // N_HEADS
    dv = v_total // N_HEADS

    mod = _modulation(c, ada_w, ada_b)
    def mods(i):
        parts = [mod[i, :, k * d:(k + 1) * d] for k in range(N_MOD)]
        sh1, sc1, g1, sh2, sc2, g2 = parts
        return sh1, 1.0 + sc1, g1, sh2, 1.0 + sc2, g2

    sh1, sc1p, g1, sh2, sc2p, g2 = mods(0)
    dmask, kdec_tile, qdec_tile, chunk_dec = _decay_tables(dk, dv)
    proj, (w_out_b, wg_b, wu_b, wd_b) = _ret_proj(
        xt, norm_gain[0, 0][None], sc1p, sh1, ret_w_in[0].astype(BF16), _rotary_tables(seq, dk), kdec_tile,
        qk_total, v_total, [ret_w_out[0], ffn_w_gate[0], ffn_w_up[0], ffn_w_down[0]])
    x1 = _retention(proj, chunk_dec, dmask, qdec_tile, ret_gn_gain[0][None], xt, g1, w_out_b, qk_total, v_total)
    x2 = _dense_ffn(x1, norm_gain[0, 1][None], sc2p, sh2, g2, wg_b, wu_b, wd_b)

    sh1, sc1p, g1, sh2, sc2p, g2 = mods(1)
    band = _pool_bands(POOL_SUB_ROWS)
    wr = jnp.pad(moe_router[0], ((0, 0), (0, LANES - N_EXPERTS)))
    wr_hi = wr.astype(BF16)
    wr_lo = (wr - wr_hi.astype(F32)).astype(BF16)
    tri = (jnp.arange(RANK_ROWS)[:, None] >= jnp.arange(RANK_ROWS)[None, :]).astype(BF16)
    x3, h4, gates, sel, rank, tot = _pool_router(
        x2, norm_gain[1, 0][None], sc1p, sh1, g1, band,
        pool_w[0].astype(BF16), pool_b[0].reshape(1, d), pool_scale[0][None],
        norm_gain[1, 1][None], sc2p, sh2, jnp.concatenate([wr_hi, wr_lo], axis=1), tri)
    pos, be, nv, beu, ng, clo, chi, win, far = _routing_tables(rank, tot, sel)
    xb, gs = _gather(beu, ng, clo, chi, h4, pos.T, gates[:, :N_EXPERTS].T)
    yb = _experts(be, nv, xb, gs, moe_w_gate[0], moe_w_up[0], moe_w_down[0])
    out = _combine(win, far, x3, pos, g2, final_norm_gain[None], yb)
    return out.reshape(bsz, seq, d)
```
